```python
import math
import jax, jax.numpy as jnp
from jax import lax
import numpy as np

D_MODEL = 1024
BATCH = 8
SEQ = 2048
DEPTH = 4
DEC_BATCH = 128
DEC_SEQ = 4
PAST_LEN = 16384
PAGE_SIZE = 128

HEAD_DIM = 64
N_HEADS = D_MODEL // HEAD_DIM
N_GLA = (5 * N_HEADS) // 16
N_RET = (5 * N_HEADS) // 16
N_RWKV = N_HEADS - N_GLA - N_RET
MIX_WIDTH = N_HEADS * HEAD_DIM
RWKV_W = N_RWKV * HEAD_DIM
GLA_DK = HEAD_DIM // 2
GLA_K = N_GLA * GLA_DK
GLA_V = N_GLA * HEAD_DIM
RET_W = N_RET * HEAD_DIM
LORA_W = max(32, int(round(1.8 * D_MODEL ** 0.5 / 32)) * 32)
LORA_A = max(32, int(round(1.8 * D_MODEL ** 0.5 / 32)) * 32)
LORA_G = max(32, int(round(0.6 * D_MODEL ** 0.8 / 32)) * 32)
GLA_LORA = 16
GLA_GATE_NORM = 16.0
CHUNK = 64
D_FF = ((8 * D_MODEL // 3 + 127) // 128) * 128
CONV_W = 3
ALPHA = (2 * DEPTH) ** 0.25
BETA = (8 * DEPTH) ** -0.25
RWKV_GN_EPS = 64e-5
LN_EPS = 1e-5
RMS_EPS = 1e-6
ROPE_BASE = 10000.0

RWKV_SIZES = (RWKV_W, RWKV_W, RWKV_W, LORA_W, LORA_A, LORA_G)
RWKV_COLS = sum(RWKV_SIZES)
GLA_SIZES = (GLA_K, GLA_K, GLA_V, GLA_V, GLA_LORA)
GLA_COLS = sum(GLA_SIZES)
RET_SIZES = (RET_W, RET_W, RET_W, RET_W)
RET_COLS = sum(RET_SIZES)
IN_COLS = RWKV_COLS + GLA_COLS + RET_COLS

kernel_name = 'hybrid_rwkv7_gla_retnet_convffn_step'


def _split(a, sizes):
    return jnp.split(a, np.cumsum(sizes)[:-1].tolist(), axis=-1)


def _heads(a, h):
    return a.reshape(a.shape[:-1] + (h, a.shape[-1] // h))


def _layer_norm(x, g, b):
    xf = x.astype(jnp.float32)
    mu = xf.mean(-1, keepdims=True)
    var = jnp.mean(jnp.square(xf - mu), -1, keepdims=True)
    return ((xf - mu) * lax.rsqrt(var + LN_EPS) * g + b).astype(x.dtype)


def _rms_heads(o):
    return o * lax.rsqrt(jnp.mean(jnp.square(o), -1, keepdims=True) + RMS_EPS)


def _to_chunks(a, c):
    b, t, h, d = a.shape
    return a.reshape(b, t // c, c, h, d).transpose(1, 0, 3, 2, 4)


def _from_chunks(o):
    n, b, h, c, d = o.shape
    return o.transpose(1, 0, 3, 2, 4).reshape(b, n * c, h, d)


def _rwkv7_mix(p, shift0, s0, mu, w0, bw, a0, ba, bg, k_k, k_a, r_k, ln_w, ln_b):
    f32 = jnp.float32
    b_, t_, _ = p.shape
    prev = jnp.concatenate([shift0[:, None].astype(p.dtype), p[:, :-1]], axis=1)
    pm = p + (prev - p) * mu
    r, k, v, xw, xa, xg = _split(pm, RWKV_SIZES)
    w_log = -jax.nn.softplus(-(w0 + jnp.tanh(xw) @ bw).astype(f32)) - 0.5
    decay = jnp.exp(-jnp.exp(w_log))
    a = jax.nn.sigmoid((a0 + xa @ ba).astype(f32))
    g = (jax.nn.sigmoid(xg) @ bg).astype(f32)
    kk = _heads(k.astype(f32) * k_k, N_RWKV)
    kk = kk / jnp.maximum(jnp.sqrt(jnp.sum(kk * kk, -1, keepdims=True)), 1e-12)
    k = k.astype(f32) * (1.0 + (a - 1.0) * k_a)
    r_h, k_h, v_h, w_h, a_h = [_heads(z.astype(f32), N_RWKV) for z in (r, k, v, decay, a)]
    seq = [jnp.moveaxis(z, 1, 0) for z in (r_h, w_h, -kk, kk * a_h, k_h, v_h)]

    def step(s, inp):
        r_t, w_t, a_t, b_t, k_t, v_t = inp
        sa = jnp.einsum('bhvk,bhk->bhv', s, a_t)
        s = s * w_t[:, :, None, :] + sa[..., None] * b_t[:, :, None, :] + v_t[..., None] * k_t[:, :, None, :]
        return s, jnp.einsum('bhvk,bhk->bhv', s, r_t)

    s, y = lax.scan(step, s0.astype(f32), seq)
    y = jnp.moveaxis(y, 0, 1)
    ym = y.mean(-1, keepdims=True)
    yv = jnp.mean(jnp.square(y - ym), -1, keepdims=True)
    y = ((y - ym) * lax.rsqrt(yv + RWKV_GN_EPS)).reshape(b_, t_, RWKV_W) * ln_w + ln_b
    bonus = jnp.sum(r_h * k_h * r_k, -1, keepdims=True) * v_h
    y = (y + bonus.reshape(b_, t_, RWKV_W)) * g
    return y.astype(p.dtype), p[:, -1], s.astype(s0.dtype)


def _gla_chunked(q, k, v, log_g, s0):
    c = math.gcd(q.shape[1], CHUNK)
    q, k, v, log_g = (_to_chunks(z, c) for z in (q, k, v, log_g))
    b = jnp.cumsum(log_g, axis=-2)
    b_last = b[..., -1:, :]
    q_in = q * jnp.exp(b)
    k_in = k * jnp.exp(-b)
    k_st = k * jnp.exp(b_last - b)
    causal = jnp.tril(jnp.ones((c, c), bool))

    def step(s, inp):
        qi, ki, ks, vi, bl = inp
        att = jnp.where(causal, jnp.einsum('bhik,bhjk->bhij', qi, ki), 0.0)
        o = jnp.einsum('bhij,bhjv->bhiv', att, vi) + jnp.einsum('bhik,bhkv->bhiv', qi, s)
        s = s * jnp.exp(bl)[:, :, 0, :, None] + jnp.einsum('bhjk,bhjv->bhkv', ks, vi)
        return s, o

    s, o = lax.scan(step, s0, (q_in, k_in, k_st, v, b_last))
    return _from_chunks(o), s


def _gla_mix(p, s0, bgk, bgk_b, norm_w):
    f32 = jnp.float32
    b_, t_, _ = p.shape
    q, k, v, g, gk = _split(p, GLA_SIZES)
    log_g = jax.nn.log_sigmoid((gk @ bgk + bgk_b).astype(f32)) / GLA_GATE_NORM
    qh = _heads(q.astype(f32), N_GLA) * GLA_DK ** -0.5
    kh, vh, gh = (_heads(z.astype(f32), N_GLA) for z in (k, v, log_g))
    o, s = _gla_chunked(qh, kh, vh, gh, s0.astype(f32))
    o = (_rms_heads(o) * norm_w).reshape(b_, t_, GLA_V) * jax.nn.silu(g.astype(f32))
    return o.astype(p.dtype), s.astype(s0.dtype)


def _rotary(x, pos):
    half = x.shape[-1] // 2
    inv = 1.0 / (ROPE_BASE ** jnp.linspace(0.0, 1.0, half, dtype=jnp.float32))
    ang = pos.astype(jnp.float32)[:, None] * inv[None]
    cos, sin = jnp.cos(ang)[None, :, None], jnp.sin(ang)[None, :, None]
    x1, x2 = x[..., :half], x[..., half:]
    return jnp.concatenate([x1 * cos - x2 * sin, x1 * sin + x2 * cos], -1)


def _retention_chunked(q, k, v, log_gamma, s0):
    c = math.gcd(q.shape[1], CHUNK)
    q, k, v = (_to_chunks(z, c) for z in (q, k, v))
    i = jnp.arange(c, dtype=jnp.float32)
    diff = i[:, None] - i[None, :]
    causal = diff >= 0
    dmask = jnp.where(causal, jnp.exp(jnp.where(causal, diff, 0.0) * log_gamma[:, None, None]), 0.0)
    q_decay = jnp.exp((i + 1.0) * log_gamma[:, None])[..., None]
    k_decay = jnp.exp((c - 1.0 - i) * log_gamma[:, None])[..., None]
    chunk_decay = jnp.exp(c * log_gamma)[:, None, None]

    def step(s, inp):
        qi, ki, vi = inp
        att = jnp.einsum('bhik,bhjk->bhij', qi, ki) * dmask
        o = jnp.einsum('bhij,bhjv->bhiv', att, vi) + jnp.einsum('bhik,bhkv->bhiv', qi * q_decay, s)
        s = s * chunk_decay + jnp.einsum('bhjk,bhjv->bhkv', ki * k_decay, vi)
        return s, o

    s, o = lax.scan(step, s0, (q, k, v))
    return _from_chunks(o), s


def _ret_mix(p, s0, pos):
    f32 = jnp.float32
    b_, t_, _ = p.shape
    q, k, v, g = _split(p, RET_SIZES)
    qh = _rotary(_heads(q.astype(f32), N_RET), pos)
    kh = _rotary(_heads(k.astype(f32), N_RET), pos) * HEAD_DIM ** -0.5
    vh = _heads(v.astype(f32), N_RET)
    log_gamma = jnp.log(1.0 - jnp.exp2(-5.0 - jnp.arange(N_RET, dtype=f32)))
    o, s = _retention_chunked(qh, kh, vh, log_gamma, s0.astype(f32))
    o = _rms_heads(o).reshape(b_, t_, RET_W) * jax.nn.silu(g.astype(f32))
    return o.astype(p.dtype), s.astype(s0.dtype)


def _conv_ffn(x, conv0, w_up, conv_w, conv_b, w_down):
    t_ = x.shape[1]
    u = x @ w_up
    buf = jnp.concatenate([conv0.astype(u.dtype), u], axis=1)
    c = conv_b + sum(conv_w[j] * buf[:, j:j + t_] for j in range(CONV_W))
    a, b = jnp.split(c, 2, axis=-1)
    return (jax.nn.gelu(a) * b) @ w_down, buf[:, -(CONV_W - 1):]


def _trunk(x, pos, s_rwkv, s_shift, s_gla, s_ret, s_conv, params):
    (w_in, rwkv_mu, rwkv_w0, rwkv_bw, rwkv_a0, rwkv_ba, rwkv_bg, rwkv_kk, rwkv_ka, rwkv_rk,
     rwkv_lnw, rwkv_lnb, gla_bgk, gla_bgk_b, gla_norm_w, w_out, ln1_g, ln1_b, ln2_g, ln2_b,
     ffn_up, ffn_conv_w, ffn_conv_b, ffn_down) = params
    n_rw, n_sh, n_gl, n_rt, n_cv = [], [], [], [], []
    for l in range(DEPTH):
        p = x @ w_in[l]
        p_rw, p_gl, p_rt = _split(p, (RWKV_COLS, GLA_COLS, RET_COLS))
        y_a, sh, s_a = _rwkv7_mix(p_rw, s_shift[l], s_rwkv[l], rwkv_mu[l], rwkv_w0[l], rwkv_bw[l],
                                  rwkv_a0[l], rwkv_ba[l], rwkv_bg[l], rwkv_kk[l], rwkv_ka[l],
                                  rwkv_rk[l], rwkv_lnw[l], rwkv_lnb[l])
        y_b, s_b = _gla_mix(p_gl, s_gla[l], gla_bgk[l], gla_bgk_b[l], gla_norm_w[l])
        y_c, s_c = _ret_mix(p_rt, s_ret[l], pos)
        mix = jnp.concatenate([y_a, y_b, y_c], axis=-1) @ w_out[l]
        x = _layer_norm(ALPHA * x + mix, ln1_g[l], ln1_b[l])
        f, cv = _conv_ffn(x, s_conv[l], ffn_up[l], ffn_conv_w[l], ffn_conv_b[l], ffn_down[l])
        x = _layer_norm(ALPHA * x + f, ln2_g[l], ln2_b[l])
        n_rw.append(s_a); n_sh.append(sh); n_gl.append(s_b); n_rt.append(s_c); n_cv.append(cv)
    return x, (jnp.stack(n_rw), jnp.stack(n_sh), jnp.stack(n_gl), jnp.stack(n_rt), jnp.stack(n_cv))


def setup_inputs(seed: int = 0) -> dict:
    key = jax.random.key(seed)
    ks = iter(jax.random.split(key, 40))
    f32 = jnp.float32
    L = DEPTH

    def nrm(shape, s):
        return jax.random.normal(next(ks), shape, f32) * s

    def uni(shape, lo, hi):
        return jax.random.uniform(next(ks), shape, f32, lo, hi)

    return {
        'x_prompt': nrm((BATCH, SEQ, D_MODEL), 1.0),
        'x_sample': nrm((DEC_BATCH, DEC_SEQ, D_MODEL), 1.0),
        'state_rwkv': nrm((L, DEC_BATCH, N_RWKV, HEAD_DIM, HEAD_DIM), 0.3),
        'state_shift': nrm((L, DEC_BATCH, RWKV_COLS), 1.0),
        'state_gla': nrm((L, DEC_BATCH, N_GLA, GLA_DK, HEAD_DIM), 0.3),
        'state_ret': nrm((L, DEC_BATCH, N_RET, HEAD_DIM, HEAD_DIM), 0.3),
        'state_conv': nrm((L, DEC_BATCH, CONV_W - 1, 2 * D_FF), BETA),
        'w_in': nrm((L, D_MODEL, IN_COLS), D_MODEL ** -0.5),
        'rwkv_mu': uni((L, RWKV_COLS), 0.0, 1.0),
        'rwkv_w0': uni((L, RWKV_W), -6.0, 0.0),
        'rwkv_bw': nrm((L, LORA_W, RWKV_W), 0.1 * LORA_W ** -0.5),
        'rwkv_a0': nrm((L, RWKV_W), 0.1),
        'rwkv_ba': nrm((L, LORA_A, RWKV_W), 0.1 * LORA_A ** -0.5),
        'rwkv_bg': nrm((L, LORA_G, RWKV_W), LORA_G ** -0.5),
        'rwkv_kk': 0.85 + nrm((L, RWKV_W), 0.02),
        'rwkv_ka': 1.0 + nrm((L, RWKV_W), 0.02),
        'rwkv_rk': nrm((L, N_RWKV, HEAD_DIM), 0.1),
        'rwkv_lnw': 1.0 + nrm((L, RWKV_W), 0.02),
        'rwkv_lnb': nrm((L, RWKV_W), 0.02),
        'gla_bgk': nrm((L, GLA_LORA, GLA_K), GLA_LORA ** -0.5),
        'gla_bgk_b': nrm((L, GLA_K), 0.1),
        'gla_norm_w': 1.0 + nrm((L, HEAD_DIM), 0.02),
        'w_out': nrm((L, MIX_WIDTH, D_MODEL), BETA * MIX_WIDTH ** -0.5),
        'ln1_g': 1.0 + nrm((L, D_MODEL), 0.02),
        'ln1_b': nrm((L, D_MODEL), 0.02),
        'ln2_g': 1.0 + nrm((L, D_MODEL), 0.02),
        'ln2_b': nrm((L, D_MODEL), 0.02),
        'ffn_up': nrm((L, D_MODEL, 2 * D_FF), BETA * D_MODEL ** -0.5),
        'ffn_conv_w': nrm((L, CONV_W, 2 * D_FF), CONV_W ** -0.5),
        'ffn_conv_b': nrm((L, 2 * D_FF), 0.02),
        'ffn_down': nrm((L, D_FF, D_MODEL), BETA * D_FF ** -0.5),
    }


def reference(x_prompt, x_sample, state_rwkv, state_shift, state_gla, state_ret, state_conv,
              w_in, rwkv_mu, rwkv_w0, rwkv_bw, rwkv_a0, rwkv_ba, rwkv_bg, rwkv_kk, rwkv_ka,
              rwkv_rk, rwkv_lnw, rwkv_lnb, gla_bgk, gla_bgk_b, gla_norm_w, w_out,
              ln1_g, ln1_b, ln2_g, ln2_b, ffn_up, ffn_conv_w, ffn_conv_b, ffn_down):
    params = (w_in, rwkv_mu, rwkv_w0, rwkv_bw, rwkv_a0, rwkv_ba, rwkv_bg, rwkv_kk, rwkv_ka,
              rwkv_rk, rwkv_lnw, rwkv_lnb, gla_bgk, gla_bgk_b, gla_norm_w, w_out,
              ln1_g, ln1_b, ln2_g, ln2_b, ffn_up, ffn_conv_w, ffn_conv_b, ffn_down)
    bp = x_prompt.shape[0]

    def zeros_like_state(s):
        return jnp.zeros((s.shape[0], bp) + s.shape[2:], x_prompt.dtype)

    y_p, (rw_p, sh_p, gl_p, rt_p, cv_p) = _trunk(
        x_prompt, jnp.arange(x_prompt.shape[1]),
        zeros_like_state(state_rwkv), zeros_like_state(state_shift), zeros_like_state(state_gla),
        zeros_like_state(state_ret), zeros_like_state(state_conv), params)
    y_s, (rw_s, sh_s, gl_s, rt_s, cv_s) = _trunk(
        x_sample, PAST_LEN + jnp.arange(x_sample.shape[1]),
        state_rwkv, state_shift, state_gla, state_ret, state_conv, params)
    return (y_p, y_s, rw_p, sh_p, gl_p, rt_p, cv_p, rw_s, sh_s, gl_s, rt_s, cv_s)
```

```python
import functools
import math

import numpy as np
import jax
import jax.numpy as jnp
from jax import lax
from jax.experimental import pallas as pl
from jax.experimental.pallas import tpu as pltpu

F32 = jnp.float32
BF16 = jnp.bfloat16

D_MODEL = 1024
DEPTH = 4
PAST_LEN = 16384
HEAD_DIM = 64
N_HEADS = D_MODEL // HEAD_DIM
N_GLA = (5 * N_HEADS) // 16
N_RET = (5 * N_HEADS) // 16
N_RWKV = N_HEADS - N_GLA - N_RET
RWKV_W = N_RWKV * HEAD_DIM
GLA_DK = HEAD_DIM // 2
GLA_K = N_GLA * GLA_DK
GLA_V = N_GLA * HEAD_DIM
RET_W = N_RET * HEAD_DIM
LORA_W = 64
LORA_A = 64
LORA_G = 160
GLA_LORA = 16
GLA_GATE_NORM = 16.0
CHUNK = 64
D_FF = 2816
CONV_W = 3
ALPHA = (2 * DEPTH) ** 0.25
RWKV_GN_EPS = 64e-5
LN_EPS = 1e-5
RMS_EPS = 1e-6
ROPE_BASE = 10000.0
RWKV_COLS = 3 * RWKV_W + LORA_W + LORA_A + LORA_G
GLA_COLS = 2 * GLA_K + 2 * GLA_V + GLA_LORA
RET_COLS = 4 * RET_W

LANE = 128
HW = 384
RW_P = 1536
GQ_P = 256
GL_P = 2 * GQ_P + 2 * HW + LANE
RT_P = 4 * HW
NP = RW_P + GL_P + RT_P
MIX_P = 3 * HW
SLOTS = 8
SLOT0 = 2

VMEM_LIMIT = 56 * 1024 * 1024


_NN = (((1,), (0,)), ((), ()))
_NT = (((1,), (1,)), ((), ()))
_TN = (((0,), (0,)), ((), ()))


def _bf(x):
    return x.astype(BF16)


def _dg(a, b, dims):
    return lax.dot_general(a, b, dims, preferred_element_type=F32)


def _dot(a, b, dims=_NN):
    return _dg(_bf(a), _bf(b), dims)


def _split(x, n):
    parts = []
    r = x
    for i in range(n):
        h = _bf(r)
        parts.append(h)
        if i + 1 < n:
            r = r - h.astype(F32)
    return parts


def _dot3(a, b, dims=_NN):
    ah, al = _split(a, 2)
    bh, bl = _split(b, 2)
    return _dg(ah, bh, dims) + (_dg(ah, bl, dims) + _dg(al, bh, dims))


def _dot_sel(a, sel_bf, n=2, dims=_NN):
    out = None
    for h in _split(a, n):
        t = _dg(h, sel_bf, dims)
        out = t if out is None else out + t
    return out


def _sel_dot(sel_bf, b, n=3, dims=_NN):
    out = None
    for h in _split(b, n):
        t = _dg(sel_bf, h, dims)
        out = t if out is None else out + t
    return out


def _layer_norm(h, g, b):
    mu = jnp.mean(h, -1, keepdims=True)
    d = h - mu
    var = jnp.mean(d * d, -1, keepdims=True)
    return d * lax.rsqrt(var + LN_EPS) * g + b


def _swap_halves(x):
    pieces = []
    for i in range(x.shape[1] // LANE):
        p = x[:, i * LANE:(i + 1) * LANE]
        up = pltpu.roll(p, LANE - HEAD_DIM // 2, axis=1)
        dn = pltpu.roll(p, HEAD_DIM // 2, axis=1)
        lane = lax.broadcasted_iota(jnp.int32, p.shape, 1)
        pieces.append(jnp.where((lane % HEAD_DIM) < HEAD_DIM // 2, up, dn))
    return jnp.concatenate(pieces, axis=1)


def _mixer_kernel(x_ref, shift_ref, srw_ref, sgl_ref, srt_ref, cos_ref, sin_ref,
                  dmask_ref, qd_ref, kd_ref, cd_ref,
                  w1_ref, mu_ref, vec_ref, bw_ref, ba_ref, bg_ref, bgk_ref, bgkb_ref,
                  wout_ref, ln_ref,
                  x1_ref, shift_out_ref, srw_out, sgl_out, srt_out,
                  carry_ref, rw_r, rw_k, rw_v, rw_lw, rw_a, rw_b, rw_g, rw_bonus, rw_y,
                  gl_q, gl_k, gl_v, gl_lg, gl_gate, gl_o,
                  rt_q, rt_k, rt_v, rt_gate, rt_o,
                  *, rows, chunk, sample):
    R, C = rows, chunk
    n_chunks = R // C
    j = pl.program_id(1)

    @pl.when(j == 0)
    def _():
        srw_out[...] = srw_ref[...]
        sgl_out[...] = sgl_ref[...]
        srt_out[...] = srt_ref[...]

    x = x_ref[...].reshape(R, D_MODEL)
    xb = _bf(x)
    row = lax.broadcasted_iota(jnp.int32, (R, 1), 0)
    if sample:
        slot = row % SLOTS
        real = (slot >= SLOT0) & (slot < SLOT0 + 4)
    else:
        real = None

    def keep(v):
        return v if real is None else jnp.where(real, v, 0.0)

    hi_ = lax.broadcasted_iota(jnp.int32, (HW, HW), 0) // HEAD_DIM
    hj_ = lax.broadcasted_iota(jnp.int32, (HW, HW), 1) // HEAD_DIM
    head_sum = (hi_ == hj_).astype(BF16)

    p = _dg(xb, w1_ref[:, 0:RW_P], _NN)
    rolled = pltpu.roll(p, 1, axis=0)
    if sample:
        shift_out_ref[...] = p.reshape(shift_out_ref.shape)
        prev = jnp.where(slot == SLOT0, shift_ref[...].reshape(R, RW_P), rolled)
    else:
        first = jnp.where(j == 0, shift_ref[0, 0:1, :], carry_ref[SLOTS - 1:SLOTS, :])
        prev = jnp.where(row == 0, first, rolled)
        carry_ref[...] = p[R - SLOTS:R, :]
        shift_out_ref[0] = p[R - SLOTS:R, :]
    pm = p + (prev - p) * mu_ref[...]
    r = pm[:, 0:HW]
    k = pm[:, HW:2 * HW]
    v = pm[:, 2 * HW:3 * HW]
    wa = pm[:, 3 * HW:3 * HW + LANE]
    xg = pm[:, 3 * HW + LANE:RW_P]
    w0 = vec_ref[0:1, :]
    a0 = vec_ref[1:2, :]
    k_k = vec_ref[2:3, :]
    k_a = vec_ref[3:4, :]
    r_k = vec_ref[4:5, :]
    w_log = -jax.nn.softplus(-(w0 + _dg(_bf(jnp.tanh(wa)), bw_ref[...], _NN))) - 0.5
    log_decay = -jnp.exp(w_log)
    a = jax.nn.sigmoid(a0 + _dg(_bf(wa), ba_ref[...], _NN))
    g = _dg(_bf(jax.nn.sigmoid(xg)), bg_ref[...], _NN)
    kk = k * k_k
    kk = kk / jnp.maximum(jnp.sqrt(_dot_sel(kk * kk, head_sum)), 1e-12)
    k2 = k * (1.0 + (a - 1.0) * k_a)
    rw_r[...] = r
    rw_k[...] = keep(k2)
    rw_v[...] = keep(v)
    rw_lw[...] = keep(log_decay)
    rw_a[...] = keep(-kk)
    rw_b[...] = keep(kk * a)
    rw_g[...] = g
    rw_bonus[...] = _dot_sel(r * k2 * r_k, head_sum) * v

    p = _dg(xb, w1_ref[:, RW_P:RW_P + GL_P], _NN)
    gk = p[:, 2 * GQ_P + 2 * HW:GL_P]
    lg = jax.nn.log_sigmoid(_dg(_bf(gk), bgk_ref[...], _NN) + bgkb_ref[...]) / GLA_GATE_NORM
    gl_q[...] = p[:, 0:GQ_P] * GLA_DK ** -0.5
    gl_k[...] = keep(p[:, GQ_P:2 * GQ_P])
    gl_v[...] = p[:, 2 * GQ_P:2 * GQ_P + HW]
    gl_lg[...] = keep(lg)
    gl_gate[...] = jax.nn.silu(p[:, 2 * GQ_P + HW:2 * GQ_P + 2 * HW])

    p = _dg(xb, w1_ref[:, RW_P + GL_P:NP], _NN)
    cos = cos_ref[...]
    sin = sin_ref[...]
    q = p[:, 0:HW]
    k = p[:, HW:2 * HW]
    rt_q[...] = q * cos + _swap_halves(q) * sin
    rt_k[...] = keep((k * cos + _swap_halves(k) * sin) * HEAD_DIM ** -0.5)
    rt_v[...] = p[:, 2 * HW:3 * HW]
    rt_gate[...] = jax.nn.silu(p[:, 3 * HW:4 * HW])

    ii = lax.broadcasted_iota(jnp.int32, (C, C), 0)
    jj = lax.broadcasted_iota(jnp.int32, (C, C), 1)
    tri_incl = ii >= jj
    tri_strict = ii > jj
    tri_bf = tri_incl.astype(BF16)
    eye = (ii == jj).astype(F32)
    levels = []
    s = 1
    while s < C:
        levels.append(((ii // (2 * s)) == (jj // (2 * s))) & ((ii % (2 * s)) >= s) & ((jj % (2 * s)) < s))
        s *= 2
    ones_cv = jnp.ones((C, HEAD_DIM), BF16)

    def chunk_body(c, carry):
        r0 = pl.multiple_of(c * C, C)
        rs = pl.ds(r0, C)
        sb = c if sample else 0

        r = rw_r[rs, :]
        k = rw_k[rs, :]
        v = rw_v[rs, :]
        lw = rw_lw[rs, :]
        a = rw_a[rs, :]
        b = rw_b[rs, :]
        cum = _sel_dot(tri_bf, lw)
        cum_last = cum[C - 1:C, :]
        w_inv = jnp.exp(-cum)
        w_rem = jnp.exp(cum_last - cum)
        r_t = r * jnp.exp(cum)
        a_t = a * jnp.exp(cum - lw)
        b_t = b * w_inv
        k_t = k * w_inv
        b_h = b * w_rem
        k_h = k * w_rem
        w_c = jnp.exp(cum_last)
        ys = []
        for h in range(N_RWKV):
            hs = slice(h * HEAD_DIM, (h + 1) * HEAD_DIM)
            S = srw_out[sb, h]
            ar = jnp.concatenate([a_t[:, hs], r_t[:, hs]], axis=0)
            xb_ = _dot3(ar, b_t[:, hs], _NT)
            xk_ = _dot3(ar, k_t[:, hs], _NT)
            a_ab = jnp.where(tri_strict, xb_[0:C], 0.0)
            a_ak = jnp.where(tri_strict, xk_[0:C], 0.0)
            p_rb = jnp.where(tri_incl, xb_[C:2 * C], 0.0)
            p_rk = jnp.where(tri_incl, xk_[C:2 * C], 0.0)
            m = eye + jnp.where(levels[0], a_ab, 0.0)
            for lvl in levels[1:]:
                e = jnp.where(lvl, a_ab, 0.0)
                m = m + _dot3(m, _dot3(e, m))
            vh = v[:, hs]
            z = _dot3(a_t[:, hs], S, _NT) + _dot3(a_ak, vh)
            u = _dot3(m, z)
            ys.append(_dot3(r_t[:, hs], S, _NT) + _dot3(p_rb, u) + _dot3(p_rk, vh))
            srw_out[sb, h] = S * w_c[:, hs] + _dot3(u, b_h[:, hs], _TN) + _dot3(vh, k_h[:, hs], _TN)
        rw_y[rs, :] = jnp.concatenate(ys, axis=1)

        q = gl_q[rs, :]
        k = gl_k[rs, :]
        v = gl_v[rs, :]
        lg = gl_lg[rs, :]
        cum = _sel_dot(tri_bf, lg)
        cum_last = cum[C - 1:C, :]
        q_in = q * jnp.exp(cum)
        k_in = k * jnp.exp(-cum)
        k_st = k * jnp.exp(cum_last - cum)
        lg_parts = _split(lg, 3)
        g_col = jnp.exp(_dg(lg_parts[0], ones_cv, _TN) + _dg(lg_parts[1], ones_cv, _TN)
                        + _dg(lg_parts[2], ones_cv, _TN))
        os_ = []
        for h in range(N_GLA):
            ks = slice(h * GLA_DK, (h + 1) * GLA_DK)
            vs = slice(h * HEAD_DIM, (h + 1) * HEAD_DIM)
            S = sgl_out[sb, h]
            att = jnp.where(tri_incl, _dot(q_in[:, ks], k_in[:, ks], _NT), 0.0)
            os_.append(_dot(att, v[:, vs]) + _dot(q_in[:, ks], S))
            sgl_out[sb, h] = S * g_col[ks, :] + _dot(k_st[:, ks], v[:, vs], _TN)
        os_.append(jnp.zeros((C, HW - GLA_V), F32))
        gl_o[rs, :] = jnp.concatenate(os_, axis=1)

        q = rt_q[rs, :]
        k = rt_k[rs, :]
        v = rt_v[rs, :]
        os_ = []
        for h in range(N_RET):
            hs = slice(h * HEAD_DIM, (h + 1) * HEAD_DIM)
            S = srt_out[sb, h]
            att = _dot(q[:, hs], k[:, hs], _NT) * dmask_ref[h]
            os_.append(_dot(att, v[:, hs]) + _dot(q[:, hs] * qd_ref[h], S))
            srt_out[sb, h] = S * cd_ref[h, 0:1, :] + _dot(k[:, hs] * kd_ref[h], v[:, hs], _TN)
        os_.append(jnp.zeros((C, HW - RET_W), F32))
        rt_o[rs, :] = jnp.concatenate(os_, axis=1)
        return carry

    lax.fori_loop(0, n_chunks, chunk_body, 0)

    ln_w = vec_ref[5:6, :]
    ln_b = vec_ref[6:7, :]
    gn_w = vec_ref[7:8, :]
    inv_hd = 1.0 / HEAD_DIM
    y = rw_y[...]
    ym = _dot_sel(y, head_sum) * inv_hd
    d = y - ym
    yv = _dot_sel(d * d, head_sum) * inv_hd
    ya = (d * lax.rsqrt(yv + RWKV_GN_EPS) * ln_w + ln_b + rw_bonus[...]) * rw_g[...]
    o = gl_o[...]
    ob = o * lax.rsqrt(_dot_sel(o * o, head_sum) * inv_hd + RMS_EPS) * gn_w * gl_gate[...]
    o = rt_o[...]
    oc = o * lax.rsqrt(_dot_sel(o * o, head_sum) * inv_hd + RMS_EPS) * rt_gate[...]
    mix = _dg(_bf(jnp.concatenate([ya, ob, oc], axis=1)), wout_ref[...], _NN)
    out = _layer_norm(ALPHA * x + mix, ln_ref[0:1, :], ln_ref[1:2, :])
    x1_ref[...] = out.reshape(x1_ref.shape)


def _const_spec(shape, idx):
    nd = len(shape)
    return pl.BlockSpec(shape, lambda i, j: idx + (0,) * (nd - len(idx)), pipeline_mode=pl.Buffered(1))


def _mixer_call(l, x, shift_rows, s_rw, s_gl, s_rt, tabs, wts, *, bb, tt, chunk, sample):
    B, T, _ = x.shape
    R = bb * tt
    nb, nt = B // bb, T // tt
    cos, sin, dmask, qd, kd, cd = tabs
    (w1, mu, vec, bw, ba, bg, bgk, bgkb, wout, ln1) = wts
    C = chunk

    def lw(shape):
        return pl.BlockSpec((None,) + shape, lambda i, j: (l,) + (0,) * len(shape), pipeline_mode=pl.Buffered(1))

    in_specs = [
        pl.BlockSpec((bb, tt, D_MODEL), lambda i, j: (i, j, 0)),
        pl.BlockSpec((bb, SLOTS, RW_P), lambda i, j: (i, 0, 0)),
        pl.BlockSpec((bb, N_RWKV, HEAD_DIM, HEAD_DIM), lambda i, j: (i, 0, 0, 0)),
        pl.BlockSpec((bb, N_GLA, GLA_DK, HEAD_DIM), lambda i, j: (i, 0, 0, 0)),
        pl.BlockSpec((bb, N_RET, HEAD_DIM, HEAD_DIM), lambda i, j: (i, 0, 0, 0)),
        pl.BlockSpec((R, HW), lambda i, j: (j, 0)),
        pl.BlockSpec((R, HW), lambda i, j: (j, 0)),
        _const_spec((N_RET, C, C), ()),
        _const_spec((N_RET, C, HEAD_DIM), ()),
        _const_spec((N_RET, C, HEAD_DIM), ()),
        _const_spec((N_RET, SLOTS, HEAD_DIM), ()),
        lw((D_MODEL, NP)), lw((1, RW_P)), lw((8, HW)), lw((LANE, HW)), lw((LANE, HW)), lw((2 * LANE, HW)),
        lw((LANE, GQ_P)), lw((1, GQ_P)), lw((MIX_P, D_MODEL)), lw((2, D_MODEL)),
    ]
    out_specs = [
        pl.BlockSpec((bb, tt, D_MODEL), lambda i, j: (i, j, 0)),
        pl.BlockSpec((bb, SLOTS, RW_P), lambda i, j: (i, 0, 0)),
        pl.BlockSpec((bb, N_RWKV, HEAD_DIM, HEAD_DIM), lambda i, j: (i, 0, 0, 0)),
        pl.BlockSpec((bb, N_GLA, GLA_DK, HEAD_DIM), lambda i, j: (i, 0, 0, 0)),
        pl.BlockSpec((bb, N_RET, HEAD_DIM, HEAD_DIM), lambda i, j: (i, 0, 0, 0)),
    ]
    out_shape = [
        jax.ShapeDtypeStruct((B, T, D_MODEL), F32),
        jax.ShapeDtypeStruct((B, SLOTS, RW_P), F32),
        jax.ShapeDtypeStruct(s_rw.shape, F32),
        jax.ShapeDtypeStruct(s_gl.shape, F32),
        jax.ShapeDtypeStruct(s_rt.shape, F32),
    ]
    scratch = [pltpu.VMEM((SLOTS, RW_P), F32)]
    scratch += [pltpu.VMEM((R, HW), F32)] * 9
    scratch += [pltpu.VMEM((R, GQ_P), F32), pltpu.VMEM((R, GQ_P), F32), pltpu.VMEM((R, HW), F32),
                pltpu.VMEM((R, GQ_P), F32), pltpu.VMEM((R, HW), F32), pltpu.VMEM((R, HW), F32)]
    scratch += [pltpu.VMEM((R, HW), F32)] * 5
    return pl.pallas_call(
        functools.partial(_mixer_kernel, rows=R, chunk=C, sample=sample),
        grid=(nb, nt),
        in_specs=in_specs, out_specs=out_specs, out_shape=out_shape,
        scratch_shapes=scratch,
        compiler_params=pltpu.CompilerParams(dimension_semantics=("arbitrary", "arbitrary"),
                                             vmem_limit_bytes=VMEM_LIMIT),
        name="mixer_sample" if sample else "mixer_prompt",
    )(x, shift_rows, s_rw, s_gl, s_rt, cos, sin, dmask, qd, kd, cd,
      w1, mu, vec, bw, ba, bg, bgk, bgkb, wout, ln1)


FFN_CHUNK = 256


def _ffn_kernel(x_ref, conv_ref, wup_ref, cwb_ref, wdn_ref, ln_ref, y_ref, cv_out_ref, carry_ref,
                *, rows, sample):
    R = rows
    j = pl.program_id(1)
    x = x_ref[...].reshape(R, D_MODEL)
    xb = _bf(x)
    row = lax.broadcasted_iota(jnp.int32, (R, 1), 0)
    if sample:
        slot = row % SLOTS
        real = (slot >= SLOT0) & (slot < SLOT0 + 4)
    acc = jnp.zeros((R, D_MODEL), F32)
    for f in range(D_FF // FFN_CHUNK):
        halves = []
        for base in (0, D_FF):
            cs = slice(base + f * FFN_CHUNK, base + (f + 1) * FFN_CHUNK)
            u = _dg(xb, wup_ref[:, cs], _NN)
            if sample:
                u = jnp.where(real, u, 0.0) + conv_ref[:, :, cs].reshape(R, FFN_CHUNK)
                u1 = pltpu.roll(u, 1, axis=0)
                u2 = pltpu.roll(u, 2, axis=0)
                cv_out_ref[:, :, cs] = u.reshape(cv_out_ref.shape[0], SLOTS, FFN_CHUNK)
            else:
                prev = jnp.where(j == 0, conv_ref[0, :, cs], carry_ref[:, cs])
                u1 = jnp.where(row == 0, prev[SLOTS - 1:SLOTS], pltpu.roll(u, 1, axis=0))
                u2 = jnp.where(row == 0, prev[SLOTS - 2:SLOTS - 1],
                               jnp.where(row == 1, prev[SLOTS - 1:SLOTS], pltpu.roll(u, 2, axis=0)))
                carry_ref[:, cs] = u[R - SLOTS:R]
                cv_out_ref[0, :, cs] = u[R - SLOTS:R]
            halves.append(cwb_ref[3:4, cs] + (cwb_ref[0:1, cs] * u2 + cwb_ref[1:2, cs] * u1 + cwb_ref[2:3, cs] * u))
        hid = jax.nn.gelu(halves[0]) * halves[1]
        acc = acc + _dg(_bf(hid), wdn_ref[f * FFN_CHUNK:(f + 1) * FFN_CHUNK, :], _NN)
    out = _layer_norm(ALPHA * x + acc, ln_ref[0:1, :], ln_ref[1:2, :])
    y_ref[...] = out.reshape(y_ref.shape)


def _ffn_call(l, x, conv_rows, wts, *, bb, tt, sample):
    B, T, _ = x.shape
    R = bb * tt
    nb, nt = B // bb, T // tt
    wup, cwb, wdn, ln2 = wts

    def lw(shape):
        return pl.BlockSpec((None,) + shape, lambda i, j: (l,) + (0,) * len(shape), pipeline_mode=pl.Buffered(1))

    return pl.pallas_call(
        functools.partial(_ffn_kernel, rows=R, sample=sample),
        grid=(nb, nt),
        in_specs=[
            pl.BlockSpec((bb, tt, D_MODEL), lambda i, j: (i, j, 0)),
            pl.BlockSpec((bb, SLOTS, 2 * D_FF), lambda i, j: (i, 0, 0)),
            lw((D_MODEL, 2 * D_FF)), lw((8, 2 * D_FF)), lw((D_FF, D_MODEL)), lw((2, D_MODEL)),
        ],
        out_specs=[
            pl.BlockSpec((bb, tt, D_MODEL), lambda i, j: (i, j, 0)),
            pl.BlockSpec((bb, SLOTS, 2 * D_FF), lambda i, j: (i, 0, 0)),
        ],
        out_shape=[jax.ShapeDtypeStruct((B, T, D_MODEL), F32),
                   jax.ShapeDtypeStruct((B, SLOTS, 2 * D_FF), F32)],
        scratch_shapes=[pltpu.VMEM((SLOTS, 2 * D_FF), F32)],
        compiler_params=pltpu.CompilerParams(dimension_semantics=("arbitrary", "arbitrary"),
                                             vmem_limit_bytes=VMEM_LIMIT),
        name="ffn_sample" if sample else "ffn_prompt",
    )(x, conv_rows, wup, cwb, wdn, ln2)


def _pad_to(a, axis, n):
    pad = [(0, 0)] * a.ndim
    pad[axis] = (0, n - a.shape[axis])
    return jnp.pad(a, pad)


def _place(a, axis, segs, total):
    out = []
    pos = 0
    for src, w, dst in segs:
        if dst > pos:
            shp = list(a.shape)
            shp[axis] = dst - pos
            out.append(jnp.zeros(shp, a.dtype))
        out.append(lax.slice_in_dim(a, src, src + w, axis=axis))
        pos = dst + w
    if total > pos:
        shp = list(a.shape)
        shp[axis] = total - pos
        out.append(jnp.zeros(shp, a.dtype))
    return jnp.concatenate(out, axis=axis)


def _in_col_segments():
    g0 = RWKV_COLS
    t0 = RWKV_COLS + GLA_COLS
    segs = [(0, RWKV_COLS, 0)]
    segs += [(g0, GLA_K, RW_P), (g0 + GLA_K, GLA_K, RW_P + GQ_P),
             (g0 + 2 * GLA_K, GLA_V, RW_P + 2 * GQ_P), (g0 + 2 * GLA_K + GLA_V, GLA_V, RW_P + 2 * GQ_P + HW),
             (g0 + 2 * GLA_K + 2 * GLA_V, GLA_LORA, RW_P + 2 * GQ_P + 2 * HW)]
    segs += [(t0 + i * RET_W, RET_W, RW_P + GL_P + i * HW) for i in range(4)]
    return segs


def _prep_weights(w_in, rwkv_mu, rwkv_w0, rwkv_bw, rwkv_a0, rwkv_ba, rwkv_bg, rwkv_kk, rwkv_ka,
                  rwkv_rk, rwkv_lnw, rwkv_lnb, gla_bgk, gla_bgk_b, gla_norm_w, w_out,
                  ln1_g, ln1_b, ln2_g, ln2_b, ffn_up, ffn_conv_w, ffn_conv_b, ffn_down):
    L = w_in.shape[0]
    w1 = _bf(_place(w_in, 2, _in_col_segments(), NP))
    mu = _pad_to(rwkv_mu, 1, RW_P)[:, None, :]
    gnw = _pad_to(jnp.tile(gla_norm_w, (1, N_GLA)), 1, HW)
    vec = jnp.stack([rwkv_w0, rwkv_a0, rwkv_kk, rwkv_ka, rwkv_rk.reshape(L, RWKV_W), rwkv_lnw, rwkv_lnb, gnw], axis=1)
    bw = _bf(_pad_to(rwkv_bw, 1, LANE))
    ba = _bf(_place(rwkv_ba, 1, [(0, LORA_A, LORA_W)], LANE))
    bg = _bf(_pad_to(rwkv_bg, 1, 2 * LANE))
    bgk = _bf(_pad_to(_pad_to(gla_bgk, 1, LANE), 2, GQ_P))
    bgkb = _pad_to(gla_bgk_b, 1, GQ_P)[:, None, :]
    wout = _bf(_place(w_out, 1, [(0, RWKV_W + GLA_V, 0), (RWKV_W + GLA_V, RET_W, 2 * HW)], MIX_P))
    ln1 = jnp.stack([ln1_g, ln1_b], axis=1)
    mixer_w = (w1, mu, vec, bw, ba, bg, bgk, bgkb, wout, ln1)
    cwb = _pad_to(jnp.concatenate([ffn_conv_w, ffn_conv_b[:, None, :]], axis=1), 1, 8)
    ln2 = jnp.stack([ln2_g, ln2_b], axis=1)
    ffn_w = (_bf(ffn_up), cwb, _bf(ffn_down), ln2)
    return mixer_w, ffn_w


def _tables(pos, i_real, c_real, chunk, reps):
    half = HEAD_DIM // 2
    inv = 1.0 / (ROPE_BASE ** jnp.linspace(0.0, 1.0, half, dtype=F32))
    ang = pos.astype(F32)[:, None] * inv[None]
    cos, sin = jnp.cos(ang), jnp.sin(ang)
    cos = jnp.tile(jnp.concatenate([cos, cos], -1), (reps, HW // HEAD_DIM))
    sin = jnp.tile(jnp.concatenate([-sin, sin], -1), (reps, HW // HEAD_DIM))
    log_gamma = jnp.log(1.0 - jnp.exp2(-5.0 - jnp.arange(N_RET, dtype=F32)))
    i = i_real.astype(F32)
    diff = i[:, None] - i[None, :]
    causal = diff >= 0
    dmask = jnp.where(causal, jnp.exp(jnp.where(causal, diff, 0.0) * log_gamma[:, None, None]), 0.0)
    qd = jnp.exp((i + 1.0) * log_gamma[:, None])[..., None]
    kd = jnp.exp((c_real - 1.0 - i) * log_gamma[:, None])[..., None]
    cd = jnp.exp(c_real * log_gamma)[:, None, None]
    qd = jnp.broadcast_to(qd, (N_RET, chunk, HEAD_DIM))
    kd = jnp.broadcast_to(kd, (N_RET, chunk, HEAD_DIM))
    cd = jnp.broadcast_to(cd, (N_RET, SLOTS, HEAD_DIM))
    return cos, sin, dmask, qd, kd, cd


def _run_group(x, s_rw, s_sh, s_gl, s_rt, s_cv, mixer_w, ffn_w, *, sample, bb_mix, tt_mix, bb_ffn, tt_ffn, chunk):
    B, T, _ = x.shape
    if sample:
        pos = jnp.clip(jnp.arange(SLOTS) - SLOT0, 0, 3) + PAST_LEN
        tabs = _tables(pos, jnp.arange(SLOTS) - SLOT0, 4.0, chunk, bb_mix)
        shift_rows = _place(_pad_to(s_sh, 2, RW_P)[:, :, None, :], 2, [(0, 1, SLOT0)], SLOTS)
        conv_rows = _pad_to(s_cv, 2, SLOTS)
    else:
        tabs = _tables(jnp.arange(T), jnp.arange(chunk), float(chunk), chunk, 1)
        shift_rows = _pad_to(_pad_to(s_sh, 2, RW_P)[:, :, None, :], 2, SLOTS)
        conv_rows = _place(s_cv, 2, [(0, CONV_W - 1, SLOTS - (CONV_W - 1))], SLOTS)
    n_rw, n_sh, n_gl, n_rt, n_cv = [], [], [], [], []
    for l in range(DEPTH):
        x, sh, rw, gl, rt = _mixer_call(l, x, shift_rows[l], s_rw[l], s_gl[l], s_rt[l], tabs, mixer_w,
                                        bb=bb_mix, tt=tt_mix, chunk=chunk, sample=sample)
        x, cv = _ffn_call(l, x, conv_rows[l], ffn_w, bb=bb_ffn, tt=tt_ffn, sample=sample)
        if sample:
            n_sh.append(sh[:, SLOT0 + 3, :RWKV_COLS])
            n_cv.append(cv[:, SLOT0 + 2:SLOT0 + 4])
        else:
            n_sh.append(sh[:, SLOTS - 1, :RWKV_COLS])
            n_cv.append(cv[:, SLOTS - (CONV_W - 1):])
        n_rw.append(rw)
        n_gl.append(gl)
        n_rt.append(rt)
    return x, (jnp.stack(n_rw), jnp.stack(n_sh), jnp.stack(n_gl), jnp.stack(n_rt), jnp.stack(n_cv))


def kernel(x_prompt, x_sample, state_rwkv, state_shift, state_gla, state_ret, state_conv, w_in, rwkv_mu, rwkv_w0, rwkv_bw, rwkv_a0, rwkv_ba, rwkv_bg, rwkv_kk, rwkv_ka, rwkv_rk, rwkv_lnw, rwkv_lnb, gla_bgk, gla_bgk_b, gla_norm_w, w_out, ln1_g, ln1_b, ln2_g, ln2_b, ffn_up, ffn_conv_w, ffn_conv_b, ffn_down):
    mixer_w, ffn_w = _prep_weights(w_in, rwkv_mu, rwkv_w0, rwkv_bw, rwkv_a0, rwkv_ba, rwkv_bg, rwkv_kk, rwkv_ka,
                                   rwkv_rk, rwkv_lnw, rwkv_lnb, gla_bgk, gla_bgk_b, gla_norm_w, w_out,
                                   ln1_g, ln1_b, ln2_g, ln2_b, ffn_up, ffn_conv_w, ffn_conv_b, ffn_down)
    bp, tp, _ = x_prompt.shape
    bs, ts, _ = x_sample.shape

    def zeros_like_state(s):
        return jnp.zeros((s.shape[0], bp) + s.shape[2:], F32)

    tt = min(256, tp)
    chunk = math.gcd(tp, CHUNK)
    y_p, st_p = _run_group(
        x_prompt, zeros_like_state(state_rwkv), zeros_like_state(state_shift), zeros_like_state(state_gla),
        zeros_like_state(state_ret), zeros_like_state(state_conv), mixer_w, ffn_w,
        sample=False, bb_mix=1, tt_mix=tt, bb_ffn=1, tt_ffn=tt, chunk=chunk)

    x8 = _place(x_sample, 1, [(0, ts, SLOT0)], SLOTS)
    bb_mix = min(16, bs)
    bb_ffn = min(32, bs)
    y_s, st_s = _run_group(
        x8, state_rwkv, state_shift, state_gla, state_ret, state_conv, mixer_w, ffn_w,
        sample=True, bb_mix=bb_mix, tt_mix=SLOTS, bb_ffn=bb_ffn, tt_ffn=SLOTS, chunk=SLOTS)
    y_s = y_s[:, SLOT0:SLOT0 + ts]
    return (y_p, y_s) + st_p + st_s
```

```python
import functools
import math

import jax
import jax.numpy as jnp
from jax import lax
from jax.experimental import pallas as pl
from jax.experimental.pallas import tpu as pltpu

F32 = jnp.float32
BF16 = jnp.bfloat16

D_MODEL = 1024
DEPTH = 4
PAST_LEN = 16384
DEC_SEQ = 4
HEAD_DIM = 64
N_HEADS = D_MODEL // HEAD_DIM
N_GLA = (5 * N_HEADS) // 16
N_RET = (5 * N_HEADS) // 16
N_RWKV = N_HEADS - N_GLA - N_RET
RWKV_W = N_RWKV * HEAD_DIM
GLA_DK = HEAD_DIM // 2
GLA_K = N_GLA * GLA_DK
GLA_V = N_GLA * HEAD_DIM
RET_W = N_RET * HEAD_DIM
LORA_W = 64
LORA_A = 64
LORA_G = 160
GLA_LORA = 16
GLA_GATE_NORM = 16.0
CHUNK = 64
D_FF = 2816
CONV_W = 3
ALPHA = (2 * DEPTH) ** 0.25
RWKV_GN_EPS = 64e-5
LN_EPS = 1e-5
RMS_EPS = 1e-6
ROPE_BASE = 10000.0
RWKV_COLS = 3 * RWKV_W + LORA_W + LORA_A + LORA_G
GLA_COLS = 2 * GLA_K + 2 * GLA_V + GLA_LORA
RET_COLS = 4 * RET_W

LANE = 128
HW = 384
RW_P = 1536
GQ_P = 256
GL_P = 2 * GQ_P + 2 * HW + LANE
RT_P = 4 * HW
NP = RW_P + GL_P + RT_P
MIX_P = 3 * HW
SLOTS = 8
SLOT0 = 2

VMEM_LIMIT = 56 * 1024 * 1024


_NN = (((1,), (0,)), ((), ()))
_NT = (((1,), (1,)), ((), ()))
_TN = (((0,), (0,)), ((), ()))


def _bf(x):
    return x.astype(BF16)


def _dg(a, b, dims):
    return lax.dot_general(a, b, dims, preferred_element_type=F32)


def _dot(a, b, dims=_NN):
    return _dg(_bf(a), _bf(b), dims)


def _split(x, n):
    parts = []
    r = x
    for i in range(n):
        h = _bf(r)
        parts.append(h)
        if i + 1 < n:
            r = r - h.astype(F32)
    return parts


def _dot_sel(a, sel_bf, n=2, dims=_NN):
    out = None
    for h in _split(a, n):
        t = _dg(h, sel_bf, dims)
        out = t if out is None else out + t
    return out


def _sel_dot(sel_bf, b, n=3, dims=_NN):
    out = None
    for h in _split(b, n):
        t = _dg(sel_bf, h, dims)
        out = t if out is None else out + t
    return out


def _layer_norm(h, g, b):
    mu = jnp.mean(h, -1, keepdims=True)
    d = h - mu
    var = jnp.mean(d * d, -1, keepdims=True)
    return d * lax.rsqrt(var + LN_EPS) * g + b


def _swap_halves(x):
    pieces = []
    for i in range(x.shape[1] // LANE):
        p = x[:, i * LANE:(i + 1) * LANE]
        up = pltpu.roll(p, LANE - HEAD_DIM // 2, axis=1)
        dn = pltpu.roll(p, HEAD_DIM // 2, axis=1)
        lane = lax.broadcasted_iota(jnp.int32, p.shape, 1)
        pieces.append(jnp.where((lane % HEAD_DIM) < HEAD_DIM // 2, up, dn))
    return jnp.concatenate(pieces, axis=1)


def _mixer_kernel(x_ref, shift_ref, srw_ref, sgl_ref, srt_ref, cos_ref, sin_ref,
                  dmask_ref, qd_ref, kd_ref, cd_ref,
                  w1_ref, mu_ref, vec_ref, bw_ref, ba_ref, bg_ref, bgk_ref, bgkb_ref,
                  wout_ref, ln_ref,
                  x1_ref, shift_out_ref, srw_out, sgl_out, srt_out,
                  carry_ref, rw_r, rw_k, rw_v, rw_lw, rw_a, rw_b, rw_g, rw_bonus, rw_y,
                  gl_q, gl_k, gl_v, gl_lg, gl_gate, gl_o,
                  rt_q, rt_k, rt_v, rt_gate, rt_o,
                  *, rows, chunk, group, sample):
    R, C = rows, chunk
    G = group * C
    j = pl.program_id(1)

    @pl.when(j == 0)
    def _():
        srw_out[...] = srw_ref[...]
        sgl_out[...] = sgl_ref[...]
        srt_out[...] = srt_ref[...]

    x = x_ref[...].reshape(R, D_MODEL)
    xb = _bf(x)
    row = lax.broadcasted_iota(jnp.int32, (R, 1), 0)
    if sample:
        slot = row % SLOTS
        real = (slot >= SLOT0) & (slot < SLOT0 + DEC_SEQ)
    else:
        real = None

    def keep(v):
        return v if real is None else jnp.where(real, v, 0.0)

    hi_ = lax.broadcasted_iota(jnp.int32, (HW, HW), 0) // HEAD_DIM
    hj_ = lax.broadcasted_iota(jnp.int32, (HW, HW), 1) // HEAD_DIM
    head_sum = (hi_ == hj_).astype(BF16)

    p = _dg(xb, w1_ref[:, 0:RW_P], _NN)
    rolled = pltpu.roll(p, 1, axis=0)
    if sample:
        shift_out_ref[...] = p.reshape(shift_out_ref.shape)
        prev = jnp.where(slot == SLOT0, shift_ref[...].reshape(R, RW_P), rolled)
    else:
        first = jnp.where(j == 0, shift_ref[0, 0:1, :], carry_ref[SLOTS - 1:SLOTS, :])
        prev = jnp.where(row == 0, first, rolled)
        carry_ref[...] = p[R - SLOTS:R, :]
        shift_out_ref[0] = p[R - SLOTS:R, :]
    pm = p + (prev - p) * mu_ref[...]
    r = pm[:, 0:HW]
    k = pm[:, HW:2 * HW]
    v = pm[:, 2 * HW:3 * HW]
    wa = pm[:, 3 * HW:3 * HW + LANE]
    xg = pm[:, 3 * HW + LANE:RW_P]
    w0 = vec_ref[0:1, :]
    a0 = vec_ref[1:2, :]
    k_k = vec_ref[2:3, :]
    k_a = vec_ref[3:4, :]
    r_k = vec_ref[4:5, :]
    w_log = -jax.nn.softplus(-(w0 + _dg(_bf(jnp.tanh(wa)), bw_ref[...], _NN))) - 0.5
    log_decay = -jnp.exp(w_log)
    a = jax.nn.sigmoid(a0 + _dg(_bf(wa), ba_ref[...], _NN))
    g = _dg(_bf(jax.nn.sigmoid(xg)), bg_ref[...], _NN)
    kk = k * k_k
    kk = kk / jnp.maximum(jnp.sqrt(_dot_sel(kk * kk, head_sum)), 1e-12)
    k2 = k * (1.0 + (a - 1.0) * k_a)
    rw_r[...] = r
    rw_k[...] = keep(k2)
    rw_v[...] = keep(v)
    rw_lw[...] = keep(log_decay)
    rw_a[...] = keep(-kk)
    rw_b[...] = keep(kk * a)
    rw_g[...] = g
    rw_bonus[...] = _dot_sel(r * k2 * r_k, head_sum) * v

    p = _dg(xb, w1_ref[:, RW_P:RW_P + GL_P], _NN)
    gk = p[:, 2 * GQ_P + 2 * HW:GL_P]
    lg = jax.nn.log_sigmoid(_dg(_bf(gk), bgk_ref[...], _NN) + bgkb_ref[...]) / GLA_GATE_NORM
    gl_q[...] = p[:, 0:GQ_P] * GLA_DK ** -0.5
    gl_k[...] = keep(p[:, GQ_P:2 * GQ_P])
    gl_v[...] = p[:, 2 * GQ_P:2 * GQ_P + HW]
    gl_lg[...] = keep(lg)
    gl_gate[...] = jax.nn.silu(p[:, 2 * GQ_P + HW:2 * GQ_P + 2 * HW])

    p = _dg(xb, w1_ref[:, RW_P + GL_P:NP], _NN)
    cos = cos_ref[...]
    sin = sin_ref[...]
    q = p[:, 0:HW]
    k = p[:, HW:2 * HW]
    rt_q[...] = q * cos + _swap_halves(q) * sin
    rt_k[...] = keep((k * cos + _swap_halves(k) * sin) * HEAD_DIM ** -0.5)
    rt_v[...] = p[:, 2 * HW:3 * HW]
    rt_gate[...] = jax.nn.silu(p[:, 3 * HW:4 * HW])

    gi = lax.broadcasted_iota(jnp.int32, (G, G), 0)
    gj = lax.broadcasted_iota(jnp.int32, (G, G), 1)
    same_chunk = (gi // C) == (gj // C)
    cum_sel = (same_chunk & (gi >= gj)).astype(BF16)
    tot_sel = same_chunk.astype(BF16)
    ii = lax.broadcasted_iota(jnp.int32, (C, C), 0)
    jj = lax.broadcasted_iota(jnp.int32, (C, C), 1)
    tri_incl = ii >= jj
    tri_strict = ii > jj
    eye = (ii == jj).astype(F32)
    levels = []
    s = 1
    while s < C:
        levels.append(((ii // (2 * s)) == (jj // (2 * s))) & ((ii % (2 * s)) >= s) & ((jj % (2 * s)) < s))
        s *= 2
    ones_cv = jnp.ones((C, HEAD_DIM), BF16)
    chunks = range(group)

    def group_body(gidx, carry):
        g0 = pl.multiple_of(gidx * G, G)
        gs = pl.ds(g0, G)

        def cr(c):
            return slice(c * C, (c + 1) * C)

        def state_index(c):
            return gidx * group + c if sample else 0

        lw = rw_lw[gs, :]
        cum = _sel_dot(cum_sel, lw)
        tot = _sel_dot(tot_sel, lw)
        w_inv = jnp.exp(-cum)
        w_rem = jnp.exp(tot - cum)
        r_t = rw_r[gs, :] * jnp.exp(cum)
        a_t = rw_a[gs, :] * jnp.exp(cum - lw)
        b_t = rw_b[gs, :] * w_inv
        k_t = rw_k[gs, :] * w_inv
        b_h = rw_b[gs, :] * w_rem
        k_h = rw_k[gs, :] * w_rem
        w_c = jnp.exp(tot)
        v_rw = rw_v[gs, :]
        items = [(c, h) for c in chunks for h in range(N_RWKV)]

        def hs(h):
            return slice(h * HEAD_DIM, (h + 1) * HEAD_DIM)

        ar = [jnp.concatenate([a_t[cr(c), hs(h)], r_t[cr(c), hs(h)]], axis=0) for c, h in items]
        xb_ = [_dot(ar[i], b_t[cr(c), hs(h)], _NT) for i, (c, h) in enumerate(items)]
        xk_ = [_dot(ar[i], k_t[cr(c), hs(h)], _NT) for i, (c, h) in enumerate(items)]
        a_ab = [jnp.where(tri_strict, t[0:C], 0.0) for t in xb_]
        a_ak = [jnp.where(tri_strict, t[0:C], 0.0) for t in xk_]
        p_rb = [jnp.where(tri_incl, t[C:2 * C], 0.0) for t in xb_]
        p_rk = [jnp.where(tri_incl, t[C:2 * C], 0.0) for t in xk_]

        lg = gl_lg[gs, :]
        gcum = _sel_dot(cum_sel, lg)
        gtot = _sel_dot(tot_sel, lg)
        q_in = gl_q[gs, :] * jnp.exp(gcum)
        k_in = gl_k[gs, :] * jnp.exp(-gcum)
        k_st = gl_k[gs, :] * jnp.exp(gtot - gcum)
        v_gl = gl_v[gs, :]
        g_col = []
        for c in chunks:
            parts = _split(lg[cr(c), :], 3)
            g_col.append(jnp.exp(_dg(parts[0], ones_cv, _TN) + _dg(parts[1], ones_cv, _TN)
                                 + _dg(parts[2], ones_cv, _TN)))
        gitems = [(c, h) for c in chunks for h in range(N_GLA)]

        def ks(h):
            return slice(h * GLA_DK, (h + 1) * GLA_DK)

        g_att = [jnp.where(tri_incl, _dot(q_in[cr(c), ks(h)], k_in[cr(c), ks(h)], _NT), 0.0) for c, h in gitems]
        g_kv = [_dot(k_st[cr(c), ks(h)], v_gl[cr(c), hs(h)], _TN) for c, h in gitems]

        q_rt = rt_q[gs, :]
        k_rt = rt_k[gs, :]
        v_rt = rt_v[gs, :]
        ritems = [(c, h) for c in chunks for h in range(N_RET)]
        r_att = [_dot(q_rt[cr(c), hs(h)], k_rt[cr(c), hs(h)], _NT) * dmask_ref[h] for c, h in ritems]
        r_kv = [_dot(k_rt[cr(c), hs(h)] * kd_ref[h], v_rt[cr(c), hs(h)], _TN) for c, h in ritems]

        m = [eye + jnp.where(levels[0], t, 0.0) for t in a_ab]
        for lvl in levels[1:]:
            t_ = [_dot(jnp.where(lvl, a_ab[i], 0.0), m[i]) for i in range(len(items))]
            m = [m[i] + _dot(m[i], t_[i]) for i in range(len(items))]

        g_av = [_dot(g_att[i], v_gl[cr(c), hs(h)]) for i, (c, h) in enumerate(gitems)]
        r_av = [_dot(r_att[i], v_rt[cr(c), hs(h)]) for i, (c, h) in enumerate(ritems)]

        vv = [v_rw[cr(c), hs(h)] for c, h in items]
        akv = [_dot(a_ak[i], vv[i]) for i in range(len(items))]
        at2 = [_dot(m[i], a_t[cr(c), hs(h)]) for i, (c, h) in enumerate(items)]
        y0 = [_dot(p_rk[i], vv[i]) for i in range(len(items))]
        u0 = [_dot(m[i], akv[i]) for i in range(len(items))]
        gp = [_dot(at2[i], b_h[cr(c), hs(h)], _TN) for i, (c, h) in enumerate(items)]
        hh = [_dot(u0[i], b_h[cr(c), hs(h)], _TN) + _dot(vv[i], k_h[cr(c), hs(h)], _TN)
              for i, (c, h) in enumerate(items)]

        s0 = [None] * len(items)
        if sample:
            for i, (c, h) in enumerate(items):
                S = srw_out[state_index(c), h]
                s0[i] = S
                srw_out[state_index(c), h] = S * w_c[c * C:c * C + 1, hs(h)] + _dot(S, gp[i]) + hh[i]
        else:
            cur = [srw_out[0, h] for h in range(N_RWKV)]
            for c in chunks:
                for h in range(N_RWKV):
                    i = c * N_RWKV + h
                    s0[i] = cur[h]
                    cur[h] = cur[h] * w_c[c * C:c * C + 1, hs(h)] + _dot(cur[h], gp[i]) + hh[i]
            for h in range(N_RWKV):
                srw_out[0, h] = cur[h]

        gs0 = [None] * len(gitems)
        rs0 = [None] * len(ritems)
        if sample:
            for i, (c, h) in enumerate(gitems):
                S = sgl_out[state_index(c), h]
                gs0[i] = S
                sgl_out[state_index(c), h] = S * g_col[c][ks(h), :] + g_kv[i]
            for i, (c, h) in enumerate(ritems):
                S = srt_out[state_index(c), h]
                rs0[i] = S
                srt_out[state_index(c), h] = S * cd_ref[h, 0:1, :] + r_kv[i]
        else:
            for h in range(N_GLA):
                S = sgl_out[0, h]
                for c in chunks:
                    gs0[c * N_GLA + h] = S
                    S = S * g_col[c][ks(h), :] + g_kv[c * N_GLA + h]
                sgl_out[0, h] = S
            for h in range(N_RET):
                S = srt_out[0, h]
                for c in chunks:
                    rs0[c * N_RET + h] = S
                    S = S * cd_ref[h, 0:1, :] + r_kv[c * N_RET + h]
                srt_out[0, h] = S

        u = [_dot(at2[i], s0[i], _NT) + u0[i] for i in range(len(items))]
        g_o = [g_av[i] + _dot(q_in[cr(c), ks(h)], gs0[i]) for i, (c, h) in enumerate(gitems)]
        r_o = [r_av[i] + _dot(q_rt[cr(c), hs(h)] * qd_ref[h], rs0[i]) for i, (c, h) in enumerate(ritems)]
        y = [_dot(r_t[cr(c), hs(h)], s0[i], _NT) + _dot(p_rb[i], u[i]) + y0[i] for i, (c, h) in enumerate(items)]
        for c in chunks:
            rows_c = pl.ds(g0 + c * C, C)
            rw_y[rows_c, :] = jnp.concatenate(y[c * N_RWKV:(c + 1) * N_RWKV], axis=1)
            gl_o[rows_c, :] = jnp.concatenate(g_o[c * N_GLA:(c + 1) * N_GLA]
                                              + [jnp.zeros((C, HW - GLA_V), F32)], axis=1)
            rt_o[rows_c, :] = jnp.concatenate(r_o[c * N_RET:(c + 1) * N_RET]
                                              + [jnp.zeros((C, HW - RET_W), F32)], axis=1)
        return carry

    if R == G:
        group_body(0, 0)
    else:
        lax.fori_loop(0, R // G, group_body, 0)

    ln_w = vec_ref[5:6, :]
    ln_b = vec_ref[6:7, :]
    gn_w = vec_ref[7:8, :]
    inv_hd = 1.0 / HEAD_DIM
    y = rw_y[...]
    ym = _dot_sel(y, head_sum) * inv_hd
    d = y - ym
    yv = _dot_sel(d * d, head_sum) * inv_hd
    ya = (d * lax.rsqrt(yv + RWKV_GN_EPS) * ln_w + ln_b + rw_bonus[...]) * rw_g[...]
    o = gl_o[...]
    ob = o * lax.rsqrt(_dot_sel(o * o, head_sum) * inv_hd + RMS_EPS) * gn_w * gl_gate[...]
    o = rt_o[...]
    oc = o * lax.rsqrt(_dot_sel(o * o, head_sum) * inv_hd + RMS_EPS) * rt_gate[...]
    mix = _dg(_bf(jnp.concatenate([ya, ob, oc], axis=1)), wout_ref[...], _NN)
    out = _layer_norm(ALPHA * x + mix, ln_ref[0:1, :], ln_ref[1:2, :])
    x1_ref[...] = out.reshape(x1_ref.shape)


def _const_spec(shape):
    nd = len(shape)
    return pl.BlockSpec(shape, lambda i, j: (0,) * nd, pipeline_mode=pl.Buffered(1))


def _layer_spec(l, shape):
    return pl.BlockSpec((None,) + shape, lambda i, j: (l,) + (0,) * len(shape), pipeline_mode=pl.Buffered(1))


def _mixer_call(l, x, shift_rows, s_rw, s_gl, s_rt, tabs, wts, *, bb, tt, chunk, group, sample):
    B, T, _ = x.shape
    R = bb * tt
    nb, nt = B // bb, T // tt
    cos, sin, dmask, qd, kd, cd = tabs
    (w1, mu, vec, bw, ba, bg, bgk, bgkb, wout, ln1) = wts
    C = chunk
    lw = functools.partial(_layer_spec, l)

    in_specs = [
        pl.BlockSpec((bb, tt, D_MODEL), lambda i, j: (i, j, 0)),
        pl.BlockSpec((bb, SLOTS, RW_P), lambda i, j: (i, 0, 0)),
        pl.BlockSpec((bb, N_RWKV, HEAD_DIM, HEAD_DIM), lambda i, j: (i, 0, 0, 0)),
        pl.BlockSpec((bb, N_GLA, GLA_DK, HEAD_DIM), lambda i, j: (i, 0, 0, 0)),
        pl.BlockSpec((bb, N_RET, HEAD_DIM, HEAD_DIM), lambda i, j: (i, 0, 0, 0)),
        pl.BlockSpec((R, HW), lambda i, j: (j, 0)),
        pl.BlockSpec((R, HW), lambda i, j: (j, 0)),
        _const_spec((N_RET, C, C)),
        _const_spec((N_RET, C, HEAD_DIM)),
        _const_spec((N_RET, C, HEAD_DIM)),
        _const_spec((N_RET, SLOTS, HEAD_DIM)),
        lw((D_MODEL, NP)), lw((1, RW_P)), lw((8, HW)), lw((LANE, HW)), lw((LANE, HW)), lw((2 * LANE, HW)),
        lw((LANE, GQ_P)), lw((1, GQ_P)), lw((MIX_P, D_MODEL)), lw((2, D_MODEL)),
    ]
    out_specs = [
        pl.BlockSpec((bb, tt, D_MODEL), lambda i, j: (i, j, 0)),
        pl.BlockSpec((bb, SLOTS, RW_P), lambda i, j: (i, 0, 0)),
        pl.BlockSpec((bb, N_RWKV, HEAD_DIM, HEAD_DIM), lambda i, j: (i, 0, 0, 0)),
        pl.BlockSpec((bb, N_GLA, GLA_DK, HEAD_DIM), lambda i, j: (i, 0, 0, 0)),
        pl.BlockSpec((bb, N_RET, HEAD_DIM, HEAD_DIM), lambda i, j: (i, 0, 0, 0)),
    ]
    out_shape = [
        jax.ShapeDtypeStruct((B, T, D_MODEL), F32),
        jax.ShapeDtypeStruct((B, SLOTS, RW_P), F32),
        jax.ShapeDtypeStruct(s_rw.shape, F32),
        jax.ShapeDtypeStruct(s_gl.shape, F32),
        jax.ShapeDtypeStruct(s_rt.shape, F32),
    ]
    scratch = [pltpu.VMEM((SLOTS, RW_P), F32)]
    scratch += [pltpu.VMEM((R, HW), F32)] * 9
    scratch += [pltpu.VMEM((R, GQ_P), F32), pltpu.VMEM((R, GQ_P), F32), pltpu.VMEM((R, HW), F32),
                pltpu.VMEM((R, GQ_P), F32), pltpu.VMEM((R, HW), F32), pltpu.VMEM((R, HW), F32)]
    scratch += [pltpu.VMEM((R, HW), F32)] * 5
    return pl.pallas_call(
        functools.partial(_mixer_kernel, rows=R, chunk=C, group=group, sample=sample),
        grid=(nb, nt),
        in_specs=in_specs, out_specs=out_specs, out_shape=out_shape,
        scratch_shapes=scratch,
        compiler_params=pltpu.CompilerParams(dimension_semantics=("arbitrary", "arbitrary"),
                                             vmem_limit_bytes=VMEM_LIMIT),
        name="mixer_sample" if sample else "mixer_prompt",
    )(x, shift_rows, s_rw, s_gl, s_rt, cos, sin, dmask, qd, kd, cd,
      w1, mu, vec, bw, ba, bg, bgk, bgkb, wout, ln1)


FFN_CHUNK = 256


def _ffn_kernel(x_ref, conv_ref, wup_ref, cwb_ref, wdn_ref, ln_ref, y_ref, cv_out_ref, carry_ref,
                *, rows, sample):
    R = rows
    j = pl.program_id(1)
    x = x_ref[...].reshape(R, D_MODEL)
    xb = _bf(x)
    row = lax.broadcasted_iota(jnp.int32, (R, 1), 0)
    if sample:
        slot = row % SLOTS
        real = (slot >= SLOT0) & (slot < SLOT0 + DEC_SEQ)
    acc = jnp.zeros((R, D_MODEL), F32)
    for f in range(D_FF // FFN_CHUNK):
        halves = []
        for base in (0, D_FF):
            cs = slice(base + f * FFN_CHUNK, base + (f + 1) * FFN_CHUNK)
            u = _dg(xb, wup_ref[:, cs], _NN)
            if sample:
                u = jnp.where(real, u, 0.0) + conv_ref[:, :, cs].reshape(R, FFN_CHUNK)
                u1 = pltpu.roll(u, 1, axis=0)
                u2 = pltpu.roll(u, 2, axis=0)
                cv_out_ref[:, :, cs] = u.reshape(cv_out_ref.shape[0], SLOTS, FFN_CHUNK)
            else:
                prev = jnp.where(j == 0, conv_ref[0, :, cs], carry_ref[:, cs])
                u1 = jnp.where(row == 0, prev[SLOTS - 1:SLOTS], pltpu.roll(u, 1, axis=0))
                u2 = jnp.where(row == 0, prev[SLOTS - 2:SLOTS - 1],
                               jnp.where(row == 1, prev[SLOTS - 1:SLOTS], pltpu.roll(u, 2, axis=0)))
                carry_ref[:, cs] = u[R - SLOTS:R]
                cv_out_ref[0, :, cs] = u[R - SLOTS:R]
            halves.append(cwb_ref[3:4, cs] + (cwb_ref[0:1, cs] * u2 + cwb_ref[1:2, cs] * u1 + cwb_ref[2:3, cs] * u))
        hid = jax.nn.gelu(halves[0]) * halves[1]
        acc = acc + _dg(_bf(hid), wdn_ref[f * FFN_CHUNK:(f + 1) * FFN_CHUNK, :], _NN)
    out = _layer_norm(ALPHA * x + acc, ln_ref[0:1, :], ln_ref[1:2, :])
    y_ref[...] = out.reshape(y_ref.shape)


def _ffn_call(l, x, conv_rows, wts, *, bb, tt, sample):
    B, T, _ = x.shape
    R = bb * tt
    nb, nt = B // bb, T // tt
    wup, cwb, wdn, ln2 = wts
    lw = functools.partial(_layer_spec, l)
    return pl.pallas_call(
        functools.partial(_ffn_kernel, rows=R, sample=sample),
        grid=(nb, nt),
        in_specs=[
            pl.BlockSpec((bb, tt, D_MODEL), lambda i, j: (i, j, 0)),
            pl.BlockSpec((bb, SLOTS, 2 * D_FF), lambda i, j: (i, 0, 0)),
            lw((D_MODEL, 2 * D_FF)), lw((8, 2 * D_FF)), lw((D_FF, D_MODEL)), lw((2, D_MODEL)),
        ],
        out_specs=[
            pl.BlockSpec((bb, tt, D_MODEL), lambda i, j: (i, j, 0)),
            pl.BlockSpec((bb, SLOTS, 2 * D_FF), lambda i, j: (i, 0, 0)),
        ],
        out_shape=[jax.ShapeDtypeStruct((B, T, D_MODEL), F32),
                   jax.ShapeDtypeStruct((B, SLOTS, 2 * D_FF), F32)],
        scratch_shapes=[pltpu.VMEM((SLOTS, 2 * D_FF), F32)],
        compiler_params=pltpu.CompilerParams(dimension_semantics=("arbitrary", "arbitrary"),
                                             vmem_limit_bytes=VMEM_LIMIT),
        name="ffn_sample" if sample else "ffn_prompt",
    )(x, conv_rows, wup, cwb, wdn, ln2)


def _pad_to(a, axis, n):
    pad = [(0, 0)] * a.ndim
    pad[axis] = (0, n - a.shape[axis])
    return jnp.pad(a, pad)


def _place(a, axis, segs, total):
    out = []
    pos = 0
    for src, w, dst in segs:
        if dst > pos:
            shp = list(a.shape)
            shp[axis] = dst - pos
            out.append(jnp.zeros(shp, a.dtype))
        out.append(lax.slice_in_dim(a, src, src + w, axis=axis))
        pos = dst + w
    if total > pos:
        shp = list(a.shape)
        shp[axis] = total - pos
        out.append(jnp.zeros(shp, a.dtype))
    return jnp.concatenate(out, axis=axis)


def _in_col_segments():
    g0 = RWKV_COLS
    t0 = RWKV_COLS + GLA_COLS
    segs = [(0, RWKV_COLS, 0)]
    segs += [(g0, GLA_K, RW_P), (g0 + GLA_K, GLA_K, RW_P + GQ_P),
             (g0 + 2 * GLA_K, GLA_V, RW_P + 2 * GQ_P), (g0 + 2 * GLA_K + GLA_V, GLA_V, RW_P + 2 * GQ_P + HW),
             (g0 + 2 * GLA_K + 2 * GLA_V, GLA_LORA, RW_P + 2 * GQ_P + 2 * HW)]
    segs += [(t0 + i * RET_W, RET_W, RW_P + GL_P + i * HW) for i in range(4)]
    return segs


def _prep_weights(w_in, rwkv_mu, rwkv_w0, rwkv_bw, rwkv_a0, rwkv_ba, rwkv_bg, rwkv_kk, rwkv_ka,
                  rwkv_rk, rwkv_lnw, rwkv_lnb, gla_bgk, gla_bgk_b, gla_norm_w, w_out,
                  ln1_g, ln1_b, ln2_g, ln2_b, ffn_up, ffn_conv_w, ffn_conv_b, ffn_down):
    L = w_in.shape[0]
    w1 = _bf(_place(w_in, 2, _in_col_segments(), NP))
    mu = _pad_to(rwkv_mu, 1, RW_P)[:, None, :]
    gnw = _pad_to(jnp.tile(gla_norm_w, (1, N_GLA)), 1, HW)
    vec = jnp.stack([rwkv_w0, rwkv_a0, rwkv_kk, rwkv_ka, rwkv_rk.reshape(L, RWKV_W), rwkv_lnw, rwkv_lnb, gnw], axis=1)
    bw = _bf(_pad_to(rwkv_bw, 1, LANE))
    ba = _bf(_place(rwkv_ba, 1, [(0, LORA_A, LORA_W)], LANE))
    bg = _bf(_pad_to(rwkv_bg, 1, 2 * LANE))
    bgk = _bf(_pad_to(_pad_to(gla_bgk, 1, LANE), 2, GQ_P))
    bgkb = _pad_to(gla_bgk_b, 1, GQ_P)[:, None, :]
    wout = _bf(_place(w_out, 1, [(0, RWKV_W + GLA_V, 0), (RWKV_W + GLA_V, RET_W, 2 * HW)], MIX_P))
    ln1 = jnp.stack([ln1_g, ln1_b], axis=1)
    mixer_w = (w1, mu, vec, bw, ba, bg, bgk, bgkb, wout, ln1)
    cwb = _pad_to(jnp.concatenate([ffn_conv_w, ffn_conv_b[:, None, :]], axis=1), 1, 8)
    ln2 = jnp.stack([ln2_g, ln2_b], axis=1)
    ffn_w = (_bf(ffn_up), cwb, _bf(ffn_down), ln2)
    return mixer_w, ffn_w


def _tables(pos, i_real, c_real, chunk, reps):
    half = HEAD_DIM // 2
    inv = 1.0 / (ROPE_BASE ** jnp.linspace(0.0, 1.0, half, dtype=F32))
    ang = pos.astype(F32)[:, None] * inv[None]
    cos, sin = jnp.cos(ang), jnp.sin(ang)
    cos = jnp.tile(jnp.concatenate([cos, cos], -1), (reps, HW // HEAD_DIM))
    sin = jnp.tile(jnp.concatenate([-sin, sin], -1), (reps, HW // HEAD_DIM))
    log_gamma = jnp.log(1.0 - jnp.exp2(-5.0 - jnp.arange(N_RET, dtype=F32)))
    i = i_real.astype(F32)
    diff = i[:, None] - i[None, :]
    causal = diff >= 0
    dmask = jnp.where(causal, jnp.exp(jnp.where(causal, diff, 0.0) * log_gamma[:, None, None]), 0.0)
    qd = jnp.exp((i + 1.0) * log_gamma[:, None])[..., None]
    kd = jnp.exp((c_real - 1.0 - i) * log_gamma[:, None])[..., None]
    cd = jnp.exp(c_real * log_gamma)[:, None, None]
    qd = jnp.broadcast_to(qd, (N_RET, chunk, HEAD_DIM))
    kd = jnp.broadcast_to(kd, (N_RET, chunk, HEAD_DIM))
    cd = jnp.broadcast_to(cd, (N_RET, SLOTS, HEAD_DIM))
    return cos, sin, dmask, qd, kd, cd


def _run_group(x, s_rw, s_sh, s_gl, s_rt, s_cv, mixer_w, ffn_w, *, sample, bb_mix, tt_mix, bb_ffn, tt_ffn,
               chunk, group):
    B, T, _ = x.shape
    if sample:
        pos = jnp.clip(jnp.arange(SLOTS) - SLOT0, 0, DEC_SEQ - 1) + PAST_LEN
        tabs = _tables(pos, jnp.arange(SLOTS) - SLOT0, float(DEC_SEQ), chunk, bb_mix)
        shift_rows = _place(_pad_to(s_sh, 2, RW_P)[:, :, None, :], 2, [(0, 1, SLOT0)], SLOTS)
        conv_rows = _pad_to(s_cv, 2, SLOTS)
    else:
        tabs = _tables(jnp.arange(T), jnp.arange(chunk), float(chunk), chunk, 1)
        shift_rows = _pad_to(_pad_to(s_sh, 2, RW_P)[:, :, None, :], 2, SLOTS)
        conv_rows = _place(s_cv, 2, [(0, CONV_W - 1, SLOTS - (CONV_W - 1))], SLOTS)
    n_rw, n_sh, n_gl, n_rt, n_cv = [], [], [], [], []
    for l in range(DEPTH):
        x, sh, rw, gl, rt = _mixer_call(l, x, shift_rows[l], s_rw[l], s_gl[l], s_rt[l], tabs, mixer_w,
                                        bb=bb_mix, tt=tt_mix, chunk=chunk, group=group, sample=sample)
        x, cv = _ffn_call(l, x, conv_rows[l], ffn_w, bb=bb_ffn, tt=tt_ffn, sample=sample)
        if sample:
            n_sh.append(sh[:, SLOT0 + DEC_SEQ - 1, :RWKV_COLS])
            n_cv.append(cv[:, SLOT0 + DEC_SEQ - (CONV_W - 1):SLOT0 + DEC_SEQ])
        else:
            n_sh.append(sh[:, SLOTS - 1, :RWKV_COLS])
            n_cv.append(cv[:, SLOTS - (CONV_W - 1):])
        n_rw.append(rw)
        n_gl.append(gl)
        n_rt.append(rt)
    return x, (jnp.stack(n_rw), jnp.stack(n_sh), jnp.stack(n_gl), jnp.stack(n_rt), jnp.stack(n_cv))


def kernel(x_prompt, x_sample, state_rwkv, state_shift, state_gla, state_ret, state_conv, w_in, rwkv_mu, rwkv_w0, rwkv_bw, rwkv_a0, rwkv_ba, rwkv_bg, rwkv_kk, rwkv_ka, rwkv_rk, rwkv_lnw, rwkv_lnb, gla_bgk, gla_bgk_b, gla_norm_w, w_out, ln1_g, ln1_b, ln2_g, ln2_b, ffn_up, ffn_conv_w, ffn_conv_b, ffn_down):
    mixer_w, ffn_w = _prep_weights(w_in, rwkv_mu, rwkv_w0, rwkv_bw, rwkv_a0, rwkv_ba, rwkv_bg, rwkv_kk, rwkv_ka,
                                   rwkv_rk, rwkv_lnw, rwkv_lnb, gla_bgk, gla_bgk_b, gla_norm_w, w_out,
                                   ln1_g, ln1_b, ln2_g, ln2_b, ffn_up, ffn_conv_w, ffn_conv_b, ffn_down)
    bp, tp, _ = x_prompt.shape
    bs, ts, _ = x_sample.shape
    assert ts == DEC_SEQ

    def zeros_like_state(s):
        return jnp.zeros((s.shape[0], bp) + s.shape[2:], F32)

    tt = min(256, tp)
    chunk = math.gcd(tp, CHUNK)
    y_p, st_p = _run_group(
        x_prompt, zeros_like_state(state_rwkv), zeros_like_state(state_shift), zeros_like_state(state_gla),
        zeros_like_state(state_ret), zeros_like_state(state_conv), mixer_w, ffn_w,
        sample=False, bb_mix=1, tt_mix=tt, bb_ffn=1, tt_ffn=tt, chunk=chunk, group=tt // chunk)

    x8 = _place(x_sample, 1, [(0, ts, SLOT0)], SLOTS)
    bb_mix = min(16, bs)
    bb_ffn = min(32, bs)
    y_s, st_s = _run_group(
        x8, state_rwkv, state_shift, state_gla, state_ret, state_conv, mixer_w, ffn_w,
        sample=True, bb_mix=bb_mix, tt_mix=SLOTS, bb_ffn=bb_ffn, tt_ffn=SLOTS, chunk=SLOTS, group=min(4, bb_mix))
    y_s = y_s[:, SLOT0:SLOT0 + ts]
    return (y_p, y_s) + st_p + st_s
```

```python
import functools
import math

import jax
import jax.numpy as jnp
from jax import lax
from jax.experimental import pallas as pl
from jax.experimental.pallas import tpu as pltpu

F32 = jnp.float32
BF16 = jnp.bfloat16

D_MODEL = 1024
DEPTH = 4
PAST_LEN = 16384
DEC_SEQ = 4
HEAD_DIM = 64
N_HEADS = D_MODEL // HEAD_DIM
N_GLA = (5 * N_HEADS) // 16
N_RET = (5 * N_HEADS) // 16
N_RWKV = N_HEADS - N_GLA - N_RET
RWKV_W = N_RWKV * HEAD_DIM
GLA_DK = HEAD_DIM // 2
GLA_K = N_GLA * GLA_DK
GLA_V = N_GLA * HEAD_DIM
RET_W = N_RET * HEAD_DIM
LORA_W = 64
LORA_A = 64
LORA_G = 160
GLA_LORA = 16
GLA_GATE_NORM = 16.0
CHUNK = 64
D_FF = 2816
CONV_W = 3
ALPHA = (2 * DEPTH) ** 0.25
RWKV_GN_EPS = 64e-5
LN_EPS = 1e-5
RMS_EPS = 1e-6
ROPE_BASE = 10000.0
RWKV_COLS = 3 * RWKV_W + LORA_W + LORA_A + LORA_G
GLA_COLS = 2 * GLA_K + 2 * GLA_V + GLA_LORA
RET_COLS = 4 * RET_W

LANE = 128
HW = 384
RW_P = 1536
GQ_P = 256
GL_P = 2 * GQ_P + 2 * HW + LANE
RT_P = 4 * HW
NP = RW_P + GL_P + RT_P
MIX_P = 3 * HW
SLOTS = 8
SLOT0 = 2

VMEM_LIMIT = 56 * 1024 * 1024


_NN = (((1,), (0,)), ((), ()))
_NT = (((1,), (1,)), ((), ()))
_TN = (((0,), (0,)), ((), ()))


def _bf(x):
    return x.astype(BF16)


def _dg(a, b, dims):
    return lax.dot_general(a, b, dims, preferred_element_type=F32)


def _dot(a, b, dims=_NN):
    return _dg(_bf(a), _bf(b), dims)


def _split(x, n):
    parts = []
    r = x
    for i in range(n):
        h = _bf(r)
        parts.append(h)
        if i + 1 < n:
            r = r - h.astype(F32)
    return parts


def _dot_sel(a, sel_bf, n=2, dims=_NN):
    out = None
    for h in _split(a, n):
        t = _dg(h, sel_bf, dims)
        out = t if out is None else out + t
    return out


def _sel_dot(sel_bf, b, n=3, dims=_NN):
    out = None
    for h in _split(b, n):
        t = _dg(sel_bf, h, dims)
        out = t if out is None else out + t
    return out


def _layer_norm(h, g, b):
    mu = jnp.mean(h, -1, keepdims=True)
    d = h - mu
    var = jnp.mean(d * d, -1, keepdims=True)
    return d * lax.rsqrt(var + LN_EPS) * g + b


def _swap_halves(x):
    pieces = []
    for i in range(x.shape[1] // LANE):
        p = x[:, i * LANE:(i + 1) * LANE]
        up = pltpu.roll(p, LANE - HEAD_DIM // 2, axis=1)
        dn = pltpu.roll(p, HEAD_DIM // 2, axis=1)
        lane = lax.broadcasted_iota(jnp.int32, p.shape, 1)
        pieces.append(jnp.where((lane % HEAD_DIM) < HEAD_DIM // 2, up, dn))
    return jnp.concatenate(pieces, axis=1)


def _mixer_kernel(x_ref, shift_ref, srw_ref, sgl_ref, srt_ref, cos_ref, sin_ref,
                  dmask_ref, qd_ref, kd_ref, cd_ref,
                  w1_ref, mu_ref, vec_ref, bw_ref, ba_ref, bg_ref, bgk_ref, bgkb_ref,
                  wout_ref, ln_ref,
                  x1_ref, shift_out_ref, srw_out, sgl_out, srt_out,
                  carry_ref, rw_r, rw_k, rw_v, rw_lw, rw_a, rw_b, rw_g, rw_bonus, rw_y,
                  gl_q, gl_k, gl_v, gl_lg, gl_gate, gl_o,
                  rt_q, rt_k, rt_v, rt_gate, rt_o,
                  *, rows, chunk, group, sample):
    R, C = rows, chunk
    G = group * C
    j = pl.program_id(1)

    @pl.when(j == 0)
    def _():
        srw_out[...] = srw_ref[...]
        sgl_out[...] = sgl_ref[...]
        srt_out[...] = srt_ref[...]

    x = x_ref[...].reshape(R, D_MODEL)
    xb = _bf(x)
    row = lax.broadcasted_iota(jnp.int32, (R, 1), 0)
    if sample:
        slot = row % SLOTS
        real = (slot >= SLOT0) & (slot < SLOT0 + DEC_SEQ)
    else:
        real = None

    def keep(v):
        return v if real is None else jnp.where(real, v, 0.0)

    hi_ = lax.broadcasted_iota(jnp.int32, (2 * LANE, 2 * LANE), 0) // HEAD_DIM
    hj_ = lax.broadcasted_iota(jnp.int32, (2 * LANE, 2 * LANE), 1) // HEAD_DIM
    head_ones = (hi_ == hj_).astype(BF16)

    def head_sum(v):
        return jnp.concatenate([_dg(_bf(v[:, 0:2 * LANE]), head_ones, _NN),
                                _dg(_bf(v[:, 2 * LANE:HW]), head_ones[0:LANE, 0:LANE], _NN)], axis=1)

    p = _dg(xb, w1_ref[:, 0:RW_P], _NN)
    rolled = pltpu.roll(p, 1, axis=0)
    if sample:
        shift_out_ref[...] = p.reshape(shift_out_ref.shape)
        prev = jnp.where(slot == SLOT0, shift_ref[...].reshape(R, RW_P), rolled)
    else:
        first = jnp.where(j == 0, shift_ref[0, 0:1, :], carry_ref[SLOTS - 1:SLOTS, :])
        prev = jnp.where(row == 0, first, rolled)
        carry_ref[...] = p[R - SLOTS:R, :]
        shift_out_ref[0] = p[R - SLOTS:R, :]
    pm = p + (prev - p) * mu_ref[...]
    r = pm[:, 0:HW]
    k = pm[:, HW:2 * HW]
    v = pm[:, 2 * HW:3 * HW]
    wa = pm[:, 3 * HW:3 * HW + LANE]
    xg = pm[:, 3 * HW + LANE:RW_P]
    w0 = vec_ref[0:1, :]
    a0 = vec_ref[1:2, :]
    k_k = vec_ref[2:3, :]
    k_a = vec_ref[3:4, :]
    r_k = vec_ref[4:5, :]
    w_log = -jax.nn.softplus(-(w0 + _dg(_bf(jnp.tanh(wa)), bw_ref[...], _NN))) - 0.5
    log_decay = -jnp.exp(w_log)
    a = jax.nn.sigmoid(a0 + _dg(_bf(wa), ba_ref[...], _NN))
    g = _dg(_bf(jax.nn.sigmoid(xg)), bg_ref[...], _NN)
    kk = k * k_k
    kk = kk / jnp.maximum(jnp.sqrt(head_sum(kk * kk)), 1e-12)
    k2 = k * (1.0 + (a - 1.0) * k_a)
    rw_r[...] = r
    rw_k[...] = keep(k2)
    rw_v[...] = keep(v)
    rw_lw[...] = keep(log_decay)
    rw_a[...] = keep(-kk)
    rw_b[...] = keep(kk * a)
    rw_g[...] = g
    rw_bonus[...] = head_sum(r * k2 * r_k) * v

    p = _dg(xb, w1_ref[:, RW_P:RW_P + GL_P], _NN)
    gk = p[:, 2 * GQ_P + 2 * HW:GL_P]
    lg = jax.nn.log_sigmoid(_dg(_bf(gk), bgk_ref[...], _NN) + bgkb_ref[...]) / GLA_GATE_NORM
    gl_q[...] = p[:, 0:GQ_P] * GLA_DK ** -0.5
    gl_k[...] = keep(p[:, GQ_P:2 * GQ_P])
    gl_v[...] = p[:, 2 * GQ_P:2 * GQ_P + HW]
    gl_lg[...] = keep(lg)
    gl_gate[...] = jax.nn.silu(p[:, 2 * GQ_P + HW:2 * GQ_P + 2 * HW])

    p = _dg(xb, w1_ref[:, RW_P + GL_P:NP], _NN)
    cos = cos_ref[...]
    sin = sin_ref[...]
    q = p[:, 0:HW]
    k = p[:, HW:2 * HW]
    rt_q[...] = q * cos + _swap_halves(q) * sin
    rt_k[...] = keep((k * cos + _swap_halves(k) * sin) * HEAD_DIM ** -0.5)
    rt_v[...] = p[:, 2 * HW:3 * HW]
    rt_gate[...] = jax.nn.silu(p[:, 3 * HW:4 * HW])

    gi = lax.broadcasted_iota(jnp.int32, (G, G), 0)
    gj = lax.broadcasted_iota(jnp.int32, (G, G), 1)
    same_chunk = (gi // C) == (gj // C)
    cum_sel = (same_chunk & (gi >= gj)).astype(BF16)

    def chunk_cumsum(v):
        cum = _sel_dot(cum_sel, v, n=2)
        tot = jnp.concatenate([jnp.broadcast_to(cum[(c + 1) * C - 1:(c + 1) * C, :], (C, v.shape[1]))
                               for c in range(group)], axis=0)
        return cum, tot
    ii = lax.broadcasted_iota(jnp.int32, (C, C), 0)
    jj = lax.broadcasted_iota(jnp.int32, (C, C), 1)
    tri_incl = ii >= jj
    tri_strict = ii > jj
    eye = (ii == jj).astype(F32)
    levels = []
    s = 1
    while s < C:
        levels.append(((ii // (2 * s)) == (jj // (2 * s))) & ((ii % (2 * s)) >= s) & ((jj % (2 * s)) < s))
        s *= 2
    ones_cv = jnp.ones((C, HEAD_DIM), BF16)
    chunks = range(group)

    def group_body(gidx, carry):
        g0 = pl.multiple_of(gidx * G, G)
        gs = pl.ds(g0, G)

        def cr(c):
            return slice(c * C, (c + 1) * C)

        def state_index(c):
            return gidx * group + c if sample else 0

        lw = rw_lw[gs, :]
        cum, tot = chunk_cumsum(lw)
        w_inv = jnp.exp(-cum)
        w_rem = jnp.exp(tot - cum)
        r_t = rw_r[gs, :] * jnp.exp(cum)
        a_t = rw_a[gs, :] * jnp.exp(cum - lw)
        b_t = rw_b[gs, :] * w_inv
        k_t = rw_k[gs, :] * w_inv
        b_h = rw_b[gs, :] * w_rem
        k_h = rw_k[gs, :] * w_rem
        w_c = jnp.exp(tot)
        v_rw = rw_v[gs, :]
        items = [(c, h) for c in chunks for h in range(N_RWKV)]

        def hs(h):
            return slice(h * HEAD_DIM, (h + 1) * HEAD_DIM)

        ar = [jnp.concatenate([a_t[cr(c), hs(h)], r_t[cr(c), hs(h)]], axis=0) for c, h in items]
        xb_ = [_dot(ar[i], b_t[cr(c), hs(h)], _NT) for i, (c, h) in enumerate(items)]
        xk_ = [_dot(ar[i], k_t[cr(c), hs(h)], _NT) for i, (c, h) in enumerate(items)]
        a_ab = [jnp.where(tri_strict, t[0:C], 0.0) for t in xb_]
        a_ak = [jnp.where(tri_strict, t[0:C], 0.0) for t in xk_]
        p_rb = [jnp.where(tri_incl, t[C:2 * C], 0.0) for t in xb_]
        p_rk = [jnp.where(tri_incl, t[C:2 * C], 0.0) for t in xk_]

        lg = gl_lg[gs, :]
        gcum, gtot = chunk_cumsum(lg)
        q_in = gl_q[gs, :] * jnp.exp(gcum)
        k_in = gl_k[gs, :] * jnp.exp(-gcum)
        k_st = gl_k[gs, :] * jnp.exp(gtot - gcum)
        v_gl = gl_v[gs, :]
        g_col = []
        for c in chunks:
            parts = _split(lg[cr(c), :], 3)
            g_col.append(jnp.exp(_dg(parts[0], ones_cv, _TN) + _dg(parts[1], ones_cv, _TN)
                                 + _dg(parts[2], ones_cv, _TN)))
        gitems = [(c, h) for c in chunks for h in range(N_GLA)]

        def ks(h):
            return slice(h * GLA_DK, (h + 1) * GLA_DK)

        g_att = [jnp.where(tri_incl, _dot(q_in[cr(c), ks(h)], k_in[cr(c), ks(h)], _NT), 0.0) for c, h in gitems]
        g_kv = [_dot(k_st[cr(c), ks(h)], v_gl[cr(c), hs(h)], _TN) for c, h in gitems]

        q_rt = rt_q[gs, :]
        k_rt = rt_k[gs, :]
        v_rt = rt_v[gs, :]
        ritems = [(c, h) for c in chunks for h in range(N_RET)]
        r_att = [_dot(q_rt[cr(c), hs(h)], k_rt[cr(c), hs(h)], _NT) * dmask_ref[h] for c, h in ritems]
        r_kv = [_dot(k_rt[cr(c), hs(h)] * kd_ref[h], v_rt[cr(c), hs(h)], _TN) for c, h in ritems]

        m = [eye + jnp.where(levels[0], t, 0.0) for t in a_ab]
        for lvl in levels[1:]:
            t_ = [_dot(jnp.where(lvl, a_ab[i], 0.0), m[i]) for i in range(len(items))]
            m = [m[i] + _dot(m[i], t_[i]) for i in range(len(items))]

        g_av = [_dot(g_att[i], v_gl[cr(c), hs(h)]) for i, (c, h) in enumerate(gitems)]
        r_av = [_dot(r_att[i], v_rt[cr(c), hs(h)]) for i, (c, h) in enumerate(ritems)]

        vv = [v_rw[cr(c), hs(h)] for c, h in items]
        akv = [_dot(a_ak[i], vv[i]) for i in range(len(items))]
        at2 = [_dot(m[i], a_t[cr(c), hs(h)]) for i, (c, h) in enumerate(items)]
        y0 = [_dot(p_rk[i], vv[i]) for i in range(len(items))]
        u0 = [_dot(m[i], akv[i]) for i in range(len(items))]
        gp = [_dot(at2[i], b_h[cr(c), hs(h)], _TN) for i, (c, h) in enumerate(items)]
        hh = [_dot(u0[i], b_h[cr(c), hs(h)], _TN) + _dot(vv[i], k_h[cr(c), hs(h)], _TN)
              for i, (c, h) in enumerate(items)]

        s0 = [None] * len(items)
        if sample:
            for i, (c, h) in enumerate(items):
                S = srw_out[state_index(c), h]
                s0[i] = S
                srw_out[state_index(c), h] = S * w_c[c * C:c * C + 1, hs(h)] + _dot(S, gp[i]) + hh[i]
        else:
            cur = [srw_out[0, h] for h in range(N_RWKV)]
            for c in chunks:
                for h in range(N_RWKV):
                    i = c * N_RWKV + h
                    s0[i] = cur[h]
                    cur[h] = cur[h] * w_c[c * C:c * C + 1, hs(h)] + _dot(cur[h], gp[i]) + hh[i]
            for h in range(N_RWKV):
                srw_out[0, h] = cur[h]

        gs0 = [None] * len(gitems)
        rs0 = [None] * len(ritems)
        if sample:
            for i, (c, h) in enumerate(gitems):
                S = sgl_out[state_index(c), h]
                gs0[i] = S
                sgl_out[state_index(c), h] = S * g_col[c][ks(h), :] + g_kv[i]
            for i, (c, h) in enumerate(ritems):
                S = srt_out[state_index(c), h]
                rs0[i] = S
                srt_out[state_index(c), h] = S * cd_ref[h, 0:1, :] + r_kv[i]
        else:
            for h in range(N_GLA):
                S = sgl_out[0, h]
                for c in chunks:
                    gs0[c * N_GLA + h] = S
                    S = S * g_col[c][ks(h), :] + g_kv[c * N_GLA + h]
                sgl_out[0, h] = S
            for h in range(N_RET):
                S = srt_out[0, h]
                for c in chunks:
                    rs0[c * N_RET + h] = S
                    S = S * cd_ref[h, 0:1, :] + r_kv[c * N_RET + h]
                srt_out[0, h] = S

        u = [_dot(at2[i], s0[i], _NT) + u0[i] for i in range(len(items))]
        g_o = [g_av[i] + _dot(q_in[cr(c), ks(h)], gs0[i]) for i, (c, h) in enumerate(gitems)]
        r_o = [r_av[i] + _dot(q_rt[cr(c), hs(h)] * qd_ref[h], rs0[i]) for i, (c, h) in enumerate(ritems)]
        y = [_dot(r_t[cr(c), hs(h)], s0[i], _NT) + _dot(p_rb[i], u[i]) + y0[i] for i, (c, h) in enumerate(items)]
        for c in chunks:
            rows_c = pl.ds(g0 + c * C, C)
            rw_y[rows_c, :] = jnp.concatenate(y[c * N_RWKV:(c + 1) * N_RWKV], axis=1)
            gl_o[rows_c, :] = jnp.concatenate(g_o[c * N_GLA:(c + 1) * N_GLA]
                                              + [jnp.zeros((C, HW - GLA_V), F32)], axis=1)
            rt_o[rows_c, :] = jnp.concatenate(r_o[c * N_RET:(c + 1) * N_RET]
                                              + [jnp.zeros((C, HW - RET_W), F32)], axis=1)
        return carry

    if R == G:
        group_body(0, 0)
    else:
        lax.fori_loop(0, R // G, group_body, 0)

    ln_w = vec_ref[5:6, :]
    ln_b = vec_ref[6:7, :]
    gn_w = vec_ref[7:8, :]
    inv_hd = 1.0 / HEAD_DIM
    y = rw_y[...]
    ym = head_sum(y) * inv_hd
    d = y - ym
    yv = head_sum(d * d) * inv_hd
    ya = (d * lax.rsqrt(yv + RWKV_GN_EPS) * ln_w + ln_b + rw_bonus[...]) * rw_g[...]
    o = gl_o[...]
    ob = o * lax.rsqrt(head_sum(o * o) * inv_hd + RMS_EPS) * gn_w * gl_gate[...]
    o = rt_o[...]
    oc = o * lax.rsqrt(head_sum(o * o) * inv_hd + RMS_EPS) * rt_gate[...]
    mix = _dg(_bf(jnp.concatenate([ya, ob, oc], axis=1)), wout_ref[...], _NN)
    out = _layer_norm(ALPHA * x + mix, ln_ref[0:1, :], ln_ref[1:2, :])
    x1_ref[...] = out.reshape(x1_ref.shape)


def _const_spec(shape):
    nd = len(shape)
    return pl.BlockSpec(shape, lambda i, j: (0,) * nd, pipeline_mode=pl.Buffered(1))


def _layer_spec(l, shape):
    return pl.BlockSpec((None,) + shape, lambda i, j: (l,) + (0,) * len(shape), pipeline_mode=pl.Buffered(1))


def _mixer_call(l, x, shift_rows, s_rw, s_gl, s_rt, tabs, wts, *, bb, tt, chunk, group, sample):
    B, T, _ = x.shape
    R = bb * tt
    nb, nt = B // bb, T // tt
    cos, sin, dmask, qd, kd, cd = tabs
    (w1, mu, vec, bw, ba, bg, bgk, bgkb, wout, ln1) = wts
    C = chunk
    lw = functools.partial(_layer_spec, l)

    in_specs = [
        pl.BlockSpec((bb, tt, D_MODEL), lambda i, j: (i, j, 0)),
        pl.BlockSpec((None, bb, SLOTS, RW_P), lambda i, j: (l, i, 0, 0)),
        pl.BlockSpec((None, bb, N_RWKV, HEAD_DIM, HEAD_DIM), lambda i, j: (l, i, 0, 0, 0)),
        pl.BlockSpec((None, bb, N_GLA, GLA_DK, HEAD_DIM), lambda i, j: (l, i, 0, 0, 0)),
        pl.BlockSpec((None, bb, N_RET, HEAD_DIM, HEAD_DIM), lambda i, j: (l, i, 0, 0, 0)),
        pl.BlockSpec((R, HW), lambda i, j: (j, 0)),
        pl.BlockSpec((R, HW), lambda i, j: (j, 0)),
        _const_spec((N_RET, C, C)),
        _const_spec((N_RET, C, HEAD_DIM)),
        _const_spec((N_RET, C, HEAD_DIM)),
        _const_spec((N_RET, SLOTS, HEAD_DIM)),
        lw((D_MODEL, NP)), lw((1, RW_P)), lw((8, HW)), lw((LANE, HW)), lw((LANE, HW)), lw((2 * LANE, HW)),
        lw((LANE, GQ_P)), lw((1, GQ_P)), lw((MIX_P, D_MODEL)), lw((2, D_MODEL)),
    ]
    out_specs = [
        pl.BlockSpec((bb, tt, D_MODEL), lambda i, j: (i, j, 0)),
        pl.BlockSpec((bb, SLOTS, RW_P), lambda i, j: (i, 0, 0)),
        pl.BlockSpec((bb, N_RWKV, HEAD_DIM, HEAD_DIM), lambda i, j: (i, 0, 0, 0)),
        pl.BlockSpec((bb, N_GLA, GLA_DK, HEAD_DIM), lambda i, j: (i, 0, 0, 0)),
        pl.BlockSpec((bb, N_RET, HEAD_DIM, HEAD_DIM), lambda i, j: (i, 0, 0, 0)),
    ]
    out_shape = [
        jax.ShapeDtypeStruct((B, T, D_MODEL), F32),
        jax.ShapeDtypeStruct((B, SLOTS, RW_P), F32),
        jax.ShapeDtypeStruct(s_rw.shape[1:], F32),
        jax.ShapeDtypeStruct(s_gl.shape[1:], F32),
        jax.ShapeDtypeStruct(s_rt.shape[1:], F32),
    ]
    scratch = [pltpu.VMEM((SLOTS, RW_P), F32)]
    scratch += [pltpu.VMEM((R, HW), F32)] * 9
    scratch += [pltpu.VMEM((R, GQ_P), F32), pltpu.VMEM((R, GQ_P), F32), pltpu.VMEM((R, HW), F32),
                pltpu.VMEM((R, GQ_P), F32), pltpu.VMEM((R, HW), F32), pltpu.VMEM((R, HW), F32)]
    scratch += [pltpu.VMEM((R, HW), F32)] * 5
    return pl.pallas_call(
        functools.partial(_mixer_kernel, rows=R, chunk=C, group=group, sample=sample),
        grid=(nb, nt),
        in_specs=in_specs, out_specs=out_specs, out_shape=out_shape,
        scratch_shapes=scratch,
        compiler_params=pltpu.CompilerParams(dimension_semantics=("arbitrary", "arbitrary"),
                                             vmem_limit_bytes=VMEM_LIMIT),
        name="mixer_sample" if sample else "mixer_prompt",
    )(x, shift_rows, s_rw, s_gl, s_rt, cos, sin, dmask, qd, kd, cd,
      w1, mu, vec, bw, ba, bg, bgk, bgkb, wout, ln1)


FFN_CHUNK = 256


def _ffn_kernel(x_ref, conv_ref, wup_ref, cwb_ref, wdn_ref, ln_ref, y_ref, cv_out_ref, carry_ref,
                *, rows, sample):
    R = rows
    j = pl.program_id(1)
    x = x_ref[...].reshape(R, D_MODEL)
    xb = _bf(x)
    row = lax.broadcasted_iota(jnp.int32, (R, 1), 0)
    if sample:
        slot = row % SLOTS
        real = (slot >= SLOT0) & (slot < SLOT0 + DEC_SEQ)
    row8 = lax.broadcasted_iota(jnp.int32, (SLOTS, 1), 0)
    n_f = D_FF // FFN_CHUNK

    def cols(f, base):
        return slice(base + f * FFN_CHUNK, base + (f + 1) * FFN_CHUNK)

    def up(f):
        return [_dg(xb, wup_ref[:, cols(f, base)], _NN) for base in (0, D_FF)]

    def conv(u, cs):
        if sample:
            u = jnp.where(real, u, 0.0) + conv_ref[:, :, cs].reshape(R, FFN_CHUNK)
            u1 = pltpu.roll(u, 1, axis=0)
            u2 = pltpu.roll(u, 2, axis=0)
            cv_out_ref[:, :, cs] = u.reshape(cv_out_ref.shape[0], SLOTS, FFN_CHUNK)
        else:
            prev = jnp.where(j == 0, conv_ref[0, :, cs], carry_ref[:, cs])
            u1 = pltpu.roll(u, 1, axis=0)
            u2 = pltpu.roll(u, 2, axis=0)
            h1 = jnp.where(row8 == 0, prev[SLOTS - 1:SLOTS], u1[0:SLOTS])
            h2 = jnp.where(row8 == 0, prev[SLOTS - 2:SLOTS - 1],
                           jnp.where(row8 == 1, prev[SLOTS - 1:SLOTS], u2[0:SLOTS]))
            u1 = jnp.concatenate([h1, u1[SLOTS:]], axis=0)
            u2 = jnp.concatenate([h2, u2[SLOTS:]], axis=0)
            carry_ref[:, cs] = u[R - SLOTS:R]
            cv_out_ref[0, :, cs] = u[R - SLOTS:R]
        return cwb_ref[3:4, cs] + (cwb_ref[0:1, cs] * u2 + cwb_ref[1:2, cs] * u1 + cwb_ref[2:3, cs] * u)

    acc = jnp.zeros((R, D_MODEL), F32)
    u_next = up(0)
    for f in range(n_f):
        u_cur = u_next
        if f + 1 < n_f:
            u_next = up(f + 1)
        hid = jax.nn.gelu(conv(u_cur[0], cols(f, 0))) * conv(u_cur[1], cols(f, D_FF))
        acc = acc + _dg(_bf(hid), wdn_ref[f * FFN_CHUNK:(f + 1) * FFN_CHUNK, :], _NN)
    out = _layer_norm(ALPHA * x + acc, ln_ref[0:1, :], ln_ref[1:2, :])
    y_ref[...] = out.reshape(y_ref.shape)


def _ffn_call(l, x, conv_rows, wts, *, bb, tt, sample):
    B, T, _ = x.shape
    R = bb * tt
    nb, nt = B // bb, T // tt
    wup, cwb, wdn, ln2 = wts
    lw = functools.partial(_layer_spec, l)
    return pl.pallas_call(
        functools.partial(_ffn_kernel, rows=R, sample=sample),
        grid=(nb, nt),
        in_specs=[
            pl.BlockSpec((bb, tt, D_MODEL), lambda i, j: (i, j, 0)),
            pl.BlockSpec((None, bb, SLOTS, 2 * D_FF), lambda i, j: (l, i, 0, 0)),
            lw((D_MODEL, 2 * D_FF)), lw((8, 2 * D_FF)), lw((D_FF, D_MODEL)), lw((2, D_MODEL)),
        ],
        out_specs=[
            pl.BlockSpec((bb, tt, D_MODEL), lambda i, j: (i, j, 0)),
            pl.BlockSpec((bb, SLOTS, 2 * D_FF), lambda i, j: (i, 0, 0)),
        ],
        out_shape=[jax.ShapeDtypeStruct((B, T, D_MODEL), F32),
                   jax.ShapeDtypeStruct((B, SLOTS, 2 * D_FF), F32)],
        scratch_shapes=[pltpu.VMEM((SLOTS, 2 * D_FF), F32)],
        compiler_params=pltpu.CompilerParams(dimension_semantics=("arbitrary", "arbitrary"),
                                             vmem_limit_bytes=VMEM_LIMIT),
        name="ffn_sample" if sample else "ffn_prompt",
    )(x, conv_rows, wup, cwb, wdn, ln2)


def _pad_to(a, axis, n):
    pad = [(0, 0)] * a.ndim
    pad[axis] = (0, n - a.shape[axis])
    return jnp.pad(a, pad)


def _place(a, axis, segs, total):
    out = []
    pos = 0
    for src, w, dst in segs:
        if dst > pos:
            shp = list(a.shape)
            shp[axis] = dst - pos
            out.append(jnp.zeros(shp, a.dtype))
        out.append(lax.slice_in_dim(a, src, src + w, axis=axis))
        pos = dst + w
    if total > pos:
        shp = list(a.shape)
        shp[axis] = total - pos
        out.append(jnp.zeros(shp, a.dtype))
    return jnp.concatenate(out, axis=axis)


def _in_col_segments():
    g0 = RWKV_COLS
    t0 = RWKV_COLS + GLA_COLS
    segs = [(0, RWKV_COLS, 0)]
    segs += [(g0, GLA_K, RW_P), (g0 + GLA_K, GLA_K, RW_P + GQ_P),
             (g0 + 2 * GLA_K, GLA_V, RW_P + 2 * GQ_P), (g0 + 2 * GLA_K + GLA_V, GLA_V, RW_P + 2 * GQ_P + HW),
             (g0 + 2 * GLA_K + 2 * GLA_V, GLA_LORA, RW_P + 2 * GQ_P + 2 * HW)]
    segs += [(t0 + i * RET_W, RET_W, RW_P + GL_P + i * HW) for i in range(4)]
    return segs


def _prep_weights(w_in, rwkv_mu, rwkv_w0, rwkv_bw, rwkv_a0, rwkv_ba, rwkv_bg, rwkv_kk, rwkv_ka,
                  rwkv_rk, rwkv_lnw, rwkv_lnb, gla_bgk, gla_bgk_b, gla_norm_w, w_out,
                  ln1_g, ln1_b, ln2_g, ln2_b, ffn_up, ffn_conv_w, ffn_conv_b, ffn_down):
    L = w_in.shape[0]
    w1 = _place(_bf(w_in), 2, _in_col_segments(), NP)
    mu = _pad_to(rwkv_mu, 1, RW_P)[:, None, :]
    gnw = _pad_to(jnp.tile(gla_norm_w, (1, N_GLA)), 1, HW)
    vec = jnp.stack([rwkv_w0, rwkv_a0, rwkv_kk, rwkv_ka, rwkv_rk.reshape(L, RWKV_W), rwkv_lnw, rwkv_lnb, gnw], axis=1)
    bw = _bf(_pad_to(rwkv_bw, 1, LANE))
    ba = _bf(_place(rwkv_ba, 1, [(0, LORA_A, LORA_W)], LANE))
    bg = _bf(_pad_to(rwkv_bg, 1, 2 * LANE))
    bgk = _bf(_pad_to(_pad_to(gla_bgk, 1, LANE), 2, GQ_P))
    bgkb = _pad_to(gla_bgk_b, 1, GQ_P)[:, None, :]
    wout = _place(_bf(w_out), 1, [(0, RWKV_W + GLA_V, 0), (RWKV_W + GLA_V, RET_W, 2 * HW)], MIX_P)
    ln1 = jnp.stack([ln1_g, ln1_b], axis=1)
    mixer_w = (w1, mu, vec, bw, ba, bg, bgk, bgkb, wout, ln1)
    cwb = _pad_to(jnp.concatenate([ffn_conv_w, ffn_conv_b[:, None, :]], axis=1), 1, 8)
    ln2 = jnp.stack([ln2_g, ln2_b], axis=1)
    ffn_w = (_bf(ffn_up), cwb, _bf(ffn_down), ln2)
    return mixer_w, ffn_w


def _tables(pos, i_real, c_real, chunk, reps):
    half = HEAD_DIM // 2
    inv = 1.0 / (ROPE_BASE ** jnp.linspace(0.0, 1.0, half, dtype=F32))
    ang = pos.astype(F32)[:, None] * inv[None]
    cos, sin = jnp.cos(ang), jnp.sin(ang)
    cos = jnp.tile(jnp.concatenate([cos, cos], -1), (reps, HW // HEAD_DIM))
    sin = jnp.tile(jnp.concatenate([-sin, sin], -1), (reps, HW // HEAD_DIM))
    log_gamma = jnp.log(1.0 - jnp.exp2(-5.0 - jnp.arange(N_RET, dtype=F32)))
    i = i_real.astype(F32)
    diff = i[:, None] - i[None, :]
    causal = diff >= 0
    dmask = jnp.where(causal, jnp.exp(jnp.where(causal, diff, 0.0) * log_gamma[:, None, None]), 0.0)
    qd = jnp.exp((i + 1.0) * log_gamma[:, None])[..., None]
    kd = jnp.exp((c_real - 1.0 - i) * log_gamma[:, None])[..., None]
    cd = jnp.exp(c_real * log_gamma)[:, None, None]
    qd = jnp.broadcast_to(qd, (N_RET, chunk, HEAD_DIM))
    kd = jnp.broadcast_to(kd, (N_RET, chunk, HEAD_DIM))
    cd = jnp.broadcast_to(cd, (N_RET, SLOTS, HEAD_DIM))
    return cos, sin, dmask, qd, kd, cd


def _run_group(x, s_rw, s_sh, s_gl, s_rt, s_cv, mixer_w, ffn_w, *, sample, bb_mix, tt_mix, bb_ffn, tt_ffn,
               chunk, group):
    B, T, _ = x.shape
    if sample:
        pos = jnp.clip(jnp.arange(SLOTS) - SLOT0, 0, DEC_SEQ - 1) + PAST_LEN
        tabs = _tables(pos, jnp.arange(SLOTS) - SLOT0, float(DEC_SEQ), chunk, bb_mix)
        shift_rows = _place(_pad_to(s_sh, 2, RW_P)[:, :, None, :], 2, [(0, 1, SLOT0)], SLOTS)
        conv_rows = _pad_to(s_cv, 2, SLOTS)
    else:
        tabs = _tables(jnp.arange(T), jnp.arange(chunk), float(chunk), chunk, 1)
        shift_rows = _pad_to(_pad_to(s_sh, 2, RW_P)[:, :, None, :], 2, SLOTS)
        conv_rows = _place(s_cv, 2, [(0, CONV_W - 1, SLOTS - (CONV_W - 1))], SLOTS)
    n_rw, n_sh, n_gl, n_rt, n_cv = [], [], [], [], []
    for l in range(DEPTH):
        x, sh, rw, gl, rt = _mixer_call(l, x, shift_rows, s_rw, s_gl, s_rt, tabs, mixer_w,
                                        bb=bb_mix, tt=tt_mix, chunk=chunk, group=group, sample=sample)
        x, cv = _ffn_call(l, x, conv_rows, ffn_w, bb=bb_ffn, tt=tt_ffn, sample=sample)
        if sample:
            n_sh.append(sh[:, SLOT0 + DEC_SEQ - 1, :RWKV_COLS])
            n_cv.append(cv[:, SLOT0 + DEC_SEQ - (CONV_W - 1):SLOT0 + DEC_SEQ])
        else:
            n_sh.append(sh[:, SLOTS - 1, :RWKV_COLS])
            n_cv.append(cv[:, SLOTS - (CONV_W - 1):])
        n_rw.append(rw)
        n_gl.append(gl)
        n_rt.append(rt)
    return x, (jnp.stack(n_rw), jnp.stack(n_sh), jnp.stack(n_gl), jnp.stack(n_rt), jnp.stack(n_cv))


def kernel(x_prompt, x_sample, state_rwkv, state_shift, state_gla, state_ret, state_conv, w_in, rwkv_mu, rwkv_w0, rwkv_bw, rwkv_a0, rwkv_ba, rwkv_bg, rwkv_kk, rwkv_ka, rwkv_rk, rwkv_lnw, rwkv_lnb, gla_bgk, gla_bgk_b, gla_norm_w, w_out, ln1_g, ln1_b, ln2_g, ln2_b, ffn_up, ffn_conv_w, ffn_conv_b, ffn_down):
    mixer_w, ffn_w = _prep_weights(w_in, rwkv_mu, rwkv_w0, rwkv_bw, rwkv_a0, rwkv_ba, rwkv_bg, rwkv_kk, rwkv_ka,
                                   rwkv_rk, rwkv_lnw, rwkv_lnb, gla_bgk, gla_bgk_b, gla_norm_w, w_out,
                                   ln1_g, ln1_b, ln2_g, ln2_b, ffn_up, ffn_conv_w, ffn_conv_b, ffn_down)
    bp, tp, _ = x_prompt.shape
    bs, ts, _ = x_sample.shape
    assert ts == DEC_SEQ

    def zeros_like_state(s):
        return jnp.zeros((s.shape[0], bp) + s.shape[2:], F32)

    tt = min(256, tp)
    chunk = math.gcd(tp, CHUNK)
    y_p, st_p = _run_group(
        x_prompt, zeros_like_state(state_rwkv), zeros_like_state(state_shift), zeros_like_state(state_gla),
        zeros_like_state(state_ret), zeros_like_state(state_conv), mixer_w, ffn_w,
        sample=False, bb_mix=1, tt_mix=tt, bb_ffn=1, tt_ffn=tt, chunk=chunk, group=tt // chunk)

    x8 = _place(x_sample, 1, [(0, ts, SLOT0)], SLOTS)
    bb_mix = min(16, bs)
    bb_ffn = min(32, bs)
    y_s, st_s = _run_group(
        x8, state_rwkv, state_shift, state_gla, state_ret, state_conv, mixer_w, ffn_w,
        sample=True, bb_mix=bb_mix, tt_mix=SLOTS, bb_ffn=bb_ffn, tt_ffn=SLOTS, chunk=SLOTS, group=min(4, bb_mix))
    y_s = y_s[:, SLOT0:SLOT0 + ts]
    return (y_p, y_s) + st_p + st_s
```

```python
import functools
import math

import jax
import jax.numpy as jnp
from jax import lax
from jax.experimental import pallas as pl
from jax.experimental.pallas import tpu as pltpu

F32 = jnp.float32
BF16 = jnp.bfloat16

D_MODEL = 1024
DEPTH = 4
PAST_LEN = 16384
DEC_SEQ = 4
HEAD_DIM = 64
N_HEADS = D_MODEL // HEAD_DIM
N_GLA = (5 * N_HEADS) // 16
N_RET = (5 * N_HEADS) // 16
N_RWKV = N_HEADS - N_GLA - N_RET
RWKV_W = N_RWKV * HEAD_DIM
GLA_DK = HEAD_DIM // 2
GLA_K = N_GLA * GLA_DK
GLA_V = N_GLA * HEAD_DIM
RET_W = N_RET * HEAD_DIM
LORA_W = 64
LORA_A = 64
LORA_G = 160
GLA_LORA = 16
GLA_GATE_NORM = 16.0
CHUNK = 64
D_FF = 2816
CONV_W = 3
ALPHA = (2 * DEPTH) ** 0.25
RWKV_GN_EPS = 64e-5
LN_EPS = 1e-5
RMS_EPS = 1e-6
ROPE_BASE = 10000.0
RWKV_COLS = 3 * RWKV_W + LORA_W + LORA_A + LORA_G
GLA_COLS = 2 * GLA_K + 2 * GLA_V + GLA_LORA
RET_COLS = 4 * RET_W

LANE = 128
HW = 384
RW_P = 1536
GQ_P = 256
GL_P = 2 * GQ_P + 2 * HW + LANE
RT_P = 4 * HW
NP = RW_P + GL_P + RT_P
MIX_P = 3 * HW
SLOTS = 8
SLOT0 = 2

VMEM_LIMIT = 56 * 1024 * 1024


_NN = (((1,), (0,)), ((), ()))
_NT = (((1,), (1,)), ((), ()))
_TN = (((0,), (0,)), ((), ()))


def _bf(x):
    return x.astype(BF16)


def _dg(a, b, dims):
    return lax.dot_general(a, b, dims, preferred_element_type=F32)


def _dot(a, b, dims=_NN):
    return _dg(_bf(a), _bf(b), dims)


def _split(x, n):
    parts = []
    r = x
    for i in range(n):
        h = _bf(r)
        parts.append(h)
        if i + 1 < n:
            r = r - h.astype(F32)
    return parts


def _dot_sel(a, sel_bf, n=2, dims=_NN):
    out = None
    for h in _split(a, n):
        t = _dg(h, sel_bf, dims)
        out = t if out is None else out + t
    return out


def _sel_dot(sel_bf, b, n=3, dims=_NN):
    out = None
    for h in _split(b, n):
        t = _dg(sel_bf, h, dims)
        out = t if out is None else out + t
    return out


def _layer_norm(h, g, b):
    mu = jnp.mean(h, -1, keepdims=True)
    d = h - mu
    var = jnp.mean(d * d, -1, keepdims=True)
    return d * lax.rsqrt(var + LN_EPS) * g + b


def _swap_halves(x):
    pieces = []
    for i in range(x.shape[1] // LANE):
        p = x[:, i * LANE:(i + 1) * LANE]
        up = pltpu.roll(p, LANE - HEAD_DIM // 2, axis=1)
        dn = pltpu.roll(p, HEAD_DIM // 2, axis=1)
        lane = lax.broadcasted_iota(jnp.int32, p.shape, 1)
        pieces.append(jnp.where((lane % HEAD_DIM) < HEAD_DIM // 2, up, dn))
    return jnp.concatenate(pieces, axis=1)


def _mixer_kernel(x_ref, shift_ref, srw_ref, sgl_ref, srt_ref, cos_ref, sin_ref,
                  dmask_ref, qd_ref, kd_ref, cd_ref,
                  w1_ref, mu_ref, vec_ref, bw_ref, ba_ref, bg_ref, bgk_ref, bgkb_ref,
                  wout_ref, ln_ref,
                  x1_ref, shift_out_ref, srw_out, sgl_out, srt_out,
                  carry_ref, rw_r, rw_k, rw_v, rw_lw, rw_a, rw_b, rw_g, rw_bonus, rw_y,
                  gl_q, gl_k, gl_v, gl_lg, gl_gate, gl_o,
                  rt_q, rt_k, rt_v, rt_gate, rt_o,
                  *, rows, chunk, group, sample):
    R, C = rows, chunk
    G = group * C
    j = pl.program_id(1)

    @pl.when(j == 0)
    def _():
        srw_out[...] = srw_ref[...]
        sgl_out[...] = sgl_ref[...]
        srt_out[...] = srt_ref[...]

    x = x_ref[...].reshape(R, D_MODEL)
    xb = _bf(x)
    row = lax.broadcasted_iota(jnp.int32, (R, 1), 0)
    if sample:
        slot = row % SLOTS
        real = (slot >= SLOT0) & (slot < SLOT0 + DEC_SEQ)
    else:
        real = None

    def keep(v):
        return v if real is None else jnp.where(real, v, 0.0)

    hi_ = lax.broadcasted_iota(jnp.int32, (2 * LANE, 2 * LANE), 0) // HEAD_DIM
    hj_ = lax.broadcasted_iota(jnp.int32, (2 * LANE, 2 * LANE), 1) // HEAD_DIM
    head_ones = (hi_ == hj_).astype(BF16)

    def head_sum(v):
        return jnp.concatenate([_dg(_bf(v[:, 0:2 * LANE]), head_ones, _NN),
                                _dg(_bf(v[:, 2 * LANE:HW]), head_ones[0:LANE, 0:LANE], _NN)], axis=1)

    p = _dg(xb, w1_ref[:, 0:RW_P], _NN)
    rolled = pltpu.roll(p, 1, axis=0)
    if sample:
        shift_out_ref[...] = p.reshape(shift_out_ref.shape)
        prev = jnp.where(slot == SLOT0, shift_ref[...].reshape(R, RW_P), rolled)
    else:
        first = jnp.where(j == 0, shift_ref[0, 0:1, :], carry_ref[SLOTS - 1:SLOTS, :])
        prev = jnp.where(row == 0, first, rolled)
        carry_ref[...] = p[R - SLOTS:R, :]
        shift_out_ref[0] = p[R - SLOTS:R, :]
    pm = p + (prev - p) * mu_ref[...]
    r = pm[:, 0:HW]
    k = pm[:, HW:2 * HW]
    v = pm[:, 2 * HW:3 * HW]
    wa = pm[:, 3 * HW:3 * HW + LANE]
    xg = pm[:, 3 * HW + LANE:RW_P]
    w0 = vec_ref[0:1, :]
    a0 = vec_ref[1:2, :]
    k_k = vec_ref[2:3, :]
    k_a = vec_ref[3:4, :]
    r_k = vec_ref[4:5, :]
    w_log = -jax.nn.softplus(-(w0 + _dg(_bf(jnp.tanh(wa)), bw_ref[...], _NN))) - 0.5
    log_decay = -jnp.exp(w_log)
    a = jax.nn.sigmoid(a0 + _dg(_bf(wa), ba_ref[...], _NN))
    g = _dg(_bf(jax.nn.sigmoid(xg)), bg_ref[...], _NN)
    kk = k * k_k
    kk = kk / jnp.maximum(jnp.sqrt(head_sum(kk * kk)), 1e-12)
    k2 = k * (1.0 + (a - 1.0) * k_a)
    rw_r[...] = r
    rw_k[...] = keep(k2)
    rw_v[...] = keep(v)
    rw_lw[...] = keep(log_decay)
    rw_a[...] = keep(-kk)
    rw_b[...] = keep(kk * a)
    rw_g[...] = g
    rw_bonus[...] = head_sum(r * k2 * r_k) * v

    p = _dg(xb, w1_ref[:, RW_P:RW_P + GL_P], _NN)
    gk = p[:, 2 * GQ_P + 2 * HW:GL_P]
    lg = jax.nn.log_sigmoid(_dg(_bf(gk), bgk_ref[...], _NN) + bgkb_ref[...]) / GLA_GATE_NORM
    gl_q[...] = p[:, 0:GQ_P] * GLA_DK ** -0.5
    gl_k[...] = keep(p[:, GQ_P:2 * GQ_P])
    gl_v[...] = p[:, 2 * GQ_P:2 * GQ_P + HW]
    gl_lg[...] = keep(lg)
    gl_gate[...] = jax.nn.silu(p[:, 2 * GQ_P + HW:2 * GQ_P + 2 * HW])

    p = _dg(xb, w1_ref[:, RW_P + GL_P:NP], _NN)
    cos = cos_ref[...]
    sin = sin_ref[...]
    q = p[:, 0:HW]
    k = p[:, HW:2 * HW]
    rt_q[...] = q * cos + _swap_halves(q) * sin
    rt_k[...] = keep((k * cos + _swap_halves(k) * sin) * HEAD_DIM ** -0.5)
    rt_v[...] = p[:, 2 * HW:3 * HW]
    rt_gate[...] = jax.nn.silu(p[:, 3 * HW:4 * HW])

    gi = lax.broadcasted_iota(jnp.int32, (G, G), 0)
    gj = lax.broadcasted_iota(jnp.int32, (G, G), 1)
    same_chunk = (gi // C) == (gj // C)
    cum_sel = (same_chunk & (gi >= gj)).astype(BF16)

    def chunk_cumsum(v):
        cum = _sel_dot(cum_sel, v, n=2)
        tot = jnp.concatenate([jnp.broadcast_to(cum[(c + 1) * C - 1:(c + 1) * C, :], (C, v.shape[1]))
                               for c in range(group)], axis=0)
        return cum, tot
    ii = lax.broadcasted_iota(jnp.int32, (C, C), 0)
    jj = lax.broadcasted_iota(jnp.int32, (C, C), 1)
    tri_incl = ii >= jj
    tri_strict = ii > jj
    eye = (ii == jj).astype(F32)
    levels = []
    s = 1
    while s < C:
        levels.append(((ii // (2 * s)) == (jj // (2 * s))) & ((ii % (2 * s)) >= s) & ((jj % (2 * s)) < s))
        s *= 2
    ones_cv = jnp.ones((C, HEAD_DIM), BF16)
    chunks = range(group)

    def group_body(gidx, carry):
        g0 = pl.multiple_of(gidx * G, G)
        gs = pl.ds(g0, G)

        def cr(c):
            return slice(c * C, (c + 1) * C)

        def state_index(c):
            return gidx * group + c if sample else 0

        lw = rw_lw[gs, :]
        cum, tot = chunk_cumsum(lw)
        w_inv = jnp.exp(-cum)
        w_rem = jnp.exp(tot - cum)
        r_t = rw_r[gs, :] * jnp.exp(cum)
        a_t = rw_a[gs, :] * jnp.exp(cum - lw)
        b_t = rw_b[gs, :] * w_inv
        k_t = rw_k[gs, :] * w_inv
        b_h = rw_b[gs, :] * w_rem
        k_h = rw_k[gs, :] * w_rem
        w_c = jnp.exp(tot)
        v_rw = rw_v[gs, :]
        items = [(c, h) for c in chunks for h in range(N_RWKV)]

        def hs(h):
            return slice(h * HEAD_DIM, (h + 1) * HEAD_DIM)

        ar = [jnp.concatenate([a_t[cr(c), hs(h)], r_t[cr(c), hs(h)]], axis=0) for c, h in items]
        xb_ = [_dot(ar[i], b_t[cr(c), hs(h)], _NT) for i, (c, h) in enumerate(items)]
        xk_ = [_dot(ar[i], k_t[cr(c), hs(h)], _NT) for i, (c, h) in enumerate(items)]
        a_ab = [jnp.where(tri_strict, t[0:C], 0.0) for t in xb_]
        a_ak = [jnp.where(tri_strict, t[0:C], 0.0) for t in xk_]
        p_rb = [jnp.where(tri_incl, t[C:2 * C], 0.0) for t in xb_]
        p_rk = [jnp.where(tri_incl, t[C:2 * C], 0.0) for t in xk_]

        lg = gl_lg[gs, :]
        gcum, gtot = chunk_cumsum(lg)
        q_in = gl_q[gs, :] * jnp.exp(gcum)
        k_in = gl_k[gs, :] * jnp.exp(-gcum)
        k_st = gl_k[gs, :] * jnp.exp(gtot - gcum)
        v_gl = gl_v[gs, :]
        g_col = []
        for c in chunks:
            parts = _split(lg[cr(c), :], 3)
            g_col.append(jnp.exp(_dg(parts[0], ones_cv, _TN) + _dg(parts[1], ones_cv, _TN)
                                 + _dg(parts[2], ones_cv, _TN)))
        gitems = [(c, h) for c in chunks for h in range(N_GLA)]

        def ks(h):
            return slice(h * GLA_DK, (h + 1) * GLA_DK)

        g_att = [jnp.where(tri_incl, _dot(q_in[cr(c), ks(h)], k_in[cr(c), ks(h)], _NT), 0.0) for c, h in gitems]
        g_kv = [_dot(k_st[cr(c), ks(h)], v_gl[cr(c), hs(h)], _TN) for c, h in gitems]

        q_rt = rt_q[gs, :]
        k_rt = rt_k[gs, :]
        v_rt = rt_v[gs, :]
        ritems = [(c, h) for c in chunks for h in range(N_RET)]
        r_att = [_dot(q_rt[cr(c), hs(h)], k_rt[cr(c), hs(h)], _NT) * dmask_ref[h] for c, h in ritems]
        r_kv = [_dot(k_rt[cr(c), hs(h)] * kd_ref[h], v_rt[cr(c), hs(h)], _TN) for c, h in ritems]

        m = [eye + jnp.where(levels[0], t, 0.0) for t in a_ab]
        for lvl in levels[1:]:
            t_ = [_dot(jnp.where(lvl, a_ab[i], 0.0), m[i]) for i in range(len(items))]
            m = [m[i] + _dot(m[i], t_[i]) for i in range(len(items))]

        g_av = [_dot(g_att[i], v_gl[cr(c), hs(h)]) for i, (c, h) in enumerate(gitems)]
        r_av = [_dot(r_att[i], v_rt[cr(c), hs(h)]) for i, (c, h) in enumerate(ritems)]

        vv = [v_rw[cr(c), hs(h)] for c, h in items]
        akv = [_dot(a_ak[i], vv[i]) for i in range(len(items))]
        at2 = [_dot(m[i], a_t[cr(c), hs(h)]) for i, (c, h) in enumerate(items)]
        y0 = [_dot(p_rk[i], vv[i]) for i in range(len(items))]
        u0 = [_dot(m[i], akv[i]) for i in range(len(items))]
        gp = [_dot(at2[i], b_h[cr(c), hs(h)], _TN) for i, (c, h) in enumerate(items)]
        hh = [_dot(u0[i], b_h[cr(c), hs(h)], _TN) + _dot(vv[i], k_h[cr(c), hs(h)], _TN)
              for i, (c, h) in enumerate(items)]

        s0 = [None] * len(items)
        if sample:
            for i, (c, h) in enumerate(items):
                S = srw_out[state_index(c), h]
                s0[i] = S
                srw_out[state_index(c), h] = S * w_c[c * C:c * C + 1, hs(h)] + _dot(S, gp[i]) + hh[i]
        else:
            cur = [srw_out[0, h] for h in range(N_RWKV)]
            for c in chunks:
                for h in range(N_RWKV):
                    i = c * N_RWKV + h
                    s0[i] = cur[h]
                    cur[h] = cur[h] * w_c[c * C:c * C + 1, hs(h)] + _dot(cur[h], gp[i]) + hh[i]
            for h in range(N_RWKV):
                srw_out[0, h] = cur[h]

        gs0 = [None] * len(gitems)
        rs0 = [None] * len(ritems)
        if sample:
            for i, (c, h) in enumerate(gitems):
                S = sgl_out[state_index(c), h]
                gs0[i] = S
                sgl_out[state_index(c), h] = S * g_col[c][ks(h), :] + g_kv[i]
            for i, (c, h) in enumerate(ritems):
                S = srt_out[state_index(c), h]
                rs0[i] = S
                srt_out[state_index(c), h] = S * cd_ref[h, 0:1, :] + r_kv[i]
        else:
            for h in range(N_GLA):
                S = sgl_out[0, h]
                for c in chunks:
                    gs0[c * N_GLA + h] = S
                    S = S * g_col[c][ks(h), :] + g_kv[c * N_GLA + h]
                sgl_out[0, h] = S
            for h in range(N_RET):
                S = srt_out[0, h]
                for c in chunks:
                    rs0[c * N_RET + h] = S
                    S = S * cd_ref[h, 0:1, :] + r_kv[c * N_RET + h]
                srt_out[0, h] = S

        u = [_dot(at2[i], s0[i], _NT) + u0[i] for i in range(len(items))]
        g_o = [g_av[i] + _dot(q_in[cr(c), ks(h)], gs0[i]) for i, (c, h) in enumerate(gitems)]
        r_o = [r_av[i] + _dot(q_rt[cr(c), hs(h)] * qd_ref[h], rs0[i]) for i, (c, h) in enumerate(ritems)]
        y = [_dot(r_t[cr(c), hs(h)], s0[i], _NT) + _dot(p_rb[i], u[i]) + y0[i] for i, (c, h) in enumerate(items)]
        for c in chunks:
            rows_c = pl.ds(g0 + c * C, C)
            rw_y[rows_c, :] = jnp.concatenate(y[c * N_RWKV:(c + 1) * N_RWKV], axis=1)
            gl_o[rows_c, :] = jnp.concatenate(g_o[c * N_GLA:(c + 1) * N_GLA]
                                              + [jnp.zeros((C, HW - GLA_V), F32)], axis=1)
            rt_o[rows_c, :] = jnp.concatenate(r_o[c * N_RET:(c + 1) * N_RET]
                                              + [jnp.zeros((C, HW - RET_W), F32)], axis=1)
        return carry

    if R == G:
        group_body(0, 0)
    else:
        lax.fori_loop(0, R // G, group_body, 0)

    ln_w = vec_ref[5:6, :]
    ln_b = vec_ref[6:7, :]
    gn_w = vec_ref[7:8, :]
    inv_hd = 1.0 / HEAD_DIM
    y = rw_y[...]
    ym = head_sum(y) * inv_hd
    d = y - ym
    yv = head_sum(d * d) * inv_hd
    ya = (d * lax.rsqrt(yv + RWKV_GN_EPS) * ln_w + ln_b + rw_bonus[...]) * rw_g[...]
    o = gl_o[...]
    ob = o * lax.rsqrt(head_sum(o * o) * inv_hd + RMS_EPS) * gn_w * gl_gate[...]
    o = rt_o[...]
    oc = o * lax.rsqrt(head_sum(o * o) * inv_hd + RMS_EPS) * rt_gate[...]
    mix = _dg(_bf(jnp.concatenate([ya, ob, oc], axis=1)), wout_ref[...], _NN)
    out = _layer_norm(ALPHA * x + mix, ln_ref[0:1, :], ln_ref[1:2, :])
    x1_ref[...] = out.reshape(x1_ref.shape)


def _const_spec(shape):
    nd = len(shape)
    return pl.BlockSpec(shape, lambda i, j: (0,) * nd, pipeline_mode=pl.Buffered(1))


def _layer_spec(l, shape):
    return pl.BlockSpec((None,) + shape, lambda i, j: (l,) + (0,) * len(shape), pipeline_mode=pl.Buffered(1))


def _mixer_call(l, x, shift_rows, s_rw, s_gl, s_rt, tabs, wts, *, bb, tt, chunk, group, sample):
    B, T, _ = x.shape
    R = bb * tt
    nb, nt = B // bb, T // tt
    cos, sin, dmask, qd, kd, cd = tabs
    (w1, mu, vec, bw, ba, bg, bgk, bgkb, wout, ln1) = wts
    C = chunk
    lw = functools.partial(_layer_spec, l)

    in_specs = [
        pl.BlockSpec((bb, tt, D_MODEL), lambda i, j: (i, j, 0)),
        pl.BlockSpec((None, bb, SLOTS, RW_P), lambda i, j: (l, i, 0, 0)),
        pl.BlockSpec((None, bb, N_RWKV, HEAD_DIM, HEAD_DIM), lambda i, j: (l, i, 0, 0, 0)),
        pl.BlockSpec((None, bb, N_GLA, GLA_DK, HEAD_DIM), lambda i, j: (l, i, 0, 0, 0)),
        pl.BlockSpec((None, bb, N_RET, HEAD_DIM, HEAD_DIM), lambda i, j: (l, i, 0, 0, 0)),
        pl.BlockSpec((R, HW), lambda i, j: (j, 0)),
        pl.BlockSpec((R, HW), lambda i, j: (j, 0)),
        _const_spec((N_RET, C, C)),
        _const_spec((N_RET, C, HEAD_DIM)),
        _const_spec((N_RET, C, HEAD_DIM)),
        _const_spec((N_RET, SLOTS, HEAD_DIM)),
        lw((D_MODEL, NP)), lw((1, RW_P)), lw((8, HW)), lw((LANE, HW)), lw((LANE, HW)), lw((2 * LANE, HW)),
        lw((LANE, GQ_P)), lw((1, GQ_P)), lw((MIX_P, D_MODEL)), lw((2, D_MODEL)),
    ]
    out_specs = [
        pl.BlockSpec((bb, tt, D_MODEL), lambda i, j: (i, j, 0)),
        pl.BlockSpec((bb, SLOTS, RW_P), lambda i, j: (i, 0, 0)),
        pl.BlockSpec((bb, N_RWKV, HEAD_DIM, HEAD_DIM), lambda i, j: (i, 0, 0, 0)),
        pl.BlockSpec((bb, N_GLA, GLA_DK, HEAD_DIM), lambda i, j: (i, 0, 0, 0)),
        pl.BlockSpec((bb, N_RET, HEAD_DIM, HEAD_DIM), lambda i, j: (i, 0, 0, 0)),
    ]
    out_shape = [
        jax.ShapeDtypeStruct((B, T, D_MODEL), F32),
        jax.ShapeDtypeStruct((B, SLOTS, RW_P), F32),
        jax.ShapeDtypeStruct(s_rw.shape[1:], F32),
        jax.ShapeDtypeStruct(s_gl.shape[1:], F32),
        jax.ShapeDtypeStruct(s_rt.shape[1:], F32),
    ]
    scratch = [pltpu.VMEM((SLOTS, RW_P), F32)]
    scratch += [pltpu.VMEM((R, HW), F32)] * 9
    scratch += [pltpu.VMEM((R, GQ_P), F32), pltpu.VMEM((R, GQ_P), F32), pltpu.VMEM((R, HW), F32),
                pltpu.VMEM((R, GQ_P), F32), pltpu.VMEM((R, HW), F32), pltpu.VMEM((R, HW), F32)]
    scratch += [pltpu.VMEM((R, HW), F32)] * 5
    return pl.pallas_call(
        functools.partial(_mixer_kernel, rows=R, chunk=C, group=group, sample=sample),
        grid=(nb, nt),
        in_specs=in_specs, out_specs=out_specs, out_shape=out_shape,
        scratch_shapes=scratch,
        compiler_params=pltpu.CompilerParams(dimension_semantics=("arbitrary", "arbitrary"),
                                             vmem_limit_bytes=VMEM_LIMIT),
        name="mixer_sample" if sample else "mixer_prompt",
    )(x, shift_rows, s_rw, s_gl, s_rt, cos, sin, dmask, qd, kd, cd,
      w1, mu, vec, bw, ba, bg, bgk, bgkb, wout, ln1)


FFN_CHUNK = 256
FFN_AHEAD = 3


def _ffn_kernel(x_ref, conv_ref, wup_ref, cwb_ref, wdn_ref, ln_ref, y_ref, cv_out_ref, carry_ref,
                *, rows, sample):
    R = rows
    j = pl.program_id(1)
    x = x_ref[...].reshape(R, D_MODEL)
    xb = _bf(x)
    row = lax.broadcasted_iota(jnp.int32, (R, 1), 0)
    if sample:
        slot = row % SLOTS
        real = (slot >= SLOT0) & (slot < SLOT0 + DEC_SEQ)
    row8 = lax.broadcasted_iota(jnp.int32, (SLOTS, 1), 0)
    n_f = D_FF // FFN_CHUNK

    def cols(f, base):
        return slice(base + f * FFN_CHUNK, base + (f + 1) * FFN_CHUNK)

    def up(f):
        return [_dg(xb, wup_ref[:, cols(f, base)], _NN) for base in (0, D_FF)]

    def conv(u, cs):
        if sample:
            u = jnp.where(real, u, 0.0) + conv_ref[:, :, cs].reshape(R, FFN_CHUNK)
            u1 = pltpu.roll(u, 1, axis=0)
            u2 = pltpu.roll(u, 2, axis=0)
            cv_out_ref[:, :, cs] = u.reshape(cv_out_ref.shape[0], SLOTS, FFN_CHUNK)
        else:
            prev = jnp.where(j == 0, conv_ref[0, :, cs], carry_ref[:, cs])
            u1 = pltpu.roll(u, 1, axis=0)
            u2 = pltpu.roll(u, 2, axis=0)
            h1 = jnp.where(row8 == 0, prev[SLOTS - 1:SLOTS], u1[0:SLOTS])
            h2 = jnp.where(row8 == 0, prev[SLOTS - 2:SLOTS - 1],
                           jnp.where(row8 == 1, prev[SLOTS - 1:SLOTS], u2[0:SLOTS]))
            u1 = jnp.concatenate([h1, u1[SLOTS:]], axis=0)
            u2 = jnp.concatenate([h2, u2[SLOTS:]], axis=0)
            carry_ref[:, cs] = u[R - SLOTS:R]
            cv_out_ref[0, :, cs] = u[R - SLOTS:R]
        return cwb_ref[3:4, cs] + (cwb_ref[0:1, cs] * u2 + cwb_ref[1:2, cs] * u1 + cwb_ref[2:3, cs] * u)

    acc = jnp.zeros((R, D_MODEL), F32)
    u_queue = [up(f) for f in range(min(FFN_AHEAD, n_f))]
    for f in range(n_f):
        if f + FFN_AHEAD < n_f:
            u_queue.append(up(f + FFN_AHEAD))
        u_cur = u_queue.pop(0)
        hid = jax.nn.gelu(conv(u_cur[0], cols(f, 0))) * conv(u_cur[1], cols(f, D_FF))
        acc = acc + _dg(_bf(hid), wdn_ref[f * FFN_CHUNK:(f + 1) * FFN_CHUNK, :], _NN)
    out = _layer_norm(ALPHA * x + acc, ln_ref[0:1, :], ln_ref[1:2, :])
    y_ref[...] = out.reshape(y_ref.shape)


def _ffn_call(l, x, conv_rows, wts, *, bb, tt, sample):
    B, T, _ = x.shape
    R = bb * tt
    nb, nt = B // bb, T // tt
    wup, cwb, wdn, ln2 = wts
    lw = functools.partial(_layer_spec, l)
    return pl.pallas_call(
        functools.partial(_ffn_kernel, rows=R, sample=sample),
        grid=(nb, nt),
        in_specs=[
            pl.BlockSpec((bb, tt, D_MODEL), lambda i, j: (i, j, 0)),
            pl.BlockSpec((None, bb, SLOTS, 2 * D_FF), lambda i, j: (l, i, 0, 0)),
            lw((D_MODEL, 2 * D_FF)), lw((8, 2 * D_FF)), lw((D_FF, D_MODEL)), lw((2, D_MODEL)),
        ],
        out_specs=[
            pl.BlockSpec((bb, tt, D_MODEL), lambda i, j: (i, j, 0)),
            pl.BlockSpec((bb, SLOTS, 2 * D_FF), lambda i, j: (i, 0, 0)),
        ],
        out_shape=[jax.ShapeDtypeStruct((B, T, D_MODEL), F32),
                   jax.ShapeDtypeStruct((B, SLOTS, 2 * D_FF), F32)],
        scratch_shapes=[pltpu.VMEM((SLOTS, 2 * D_FF), F32)],
        compiler_params=pltpu.CompilerParams(dimension_semantics=("arbitrary", "arbitrary"),
                                             vmem_limit_bytes=VMEM_LIMIT),
        name="ffn_sample" if sample else "ffn_prompt",
    )(x, conv_rows, wup, cwb, wdn, ln2)


def _pad_to(a, axis, n):
    pad = [(0, 0)] * a.ndim
    pad[axis] = (0, n - a.shape[axis])
    return jnp.pad(a, pad)


def _place(a, axis, segs, total):
    out = []
    pos = 0
    for src, w, dst in segs:
        if dst > pos:
            shp = list(a.shape)
            shp[axis] = dst - pos
            out.append(jnp.zeros(shp, a.dtype))
        out.append(lax.slice_in_dim(a, src, src + w, axis=axis))
        pos = dst + w
    if total > pos:
        shp = list(a.shape)
        shp[axis] = total - pos
        out.append(jnp.zeros(shp, a.dtype))
    return jnp.concatenate(out, axis=axis)


def _in_col_segments():
    g0 = RWKV_COLS
    t0 = RWKV_COLS + GLA_COLS
    segs = [(0, RWKV_COLS, 0)]
    segs += [(g0, GLA_K, RW_P), (g0 + GLA_K, GLA_K, RW_P + GQ_P),
             (g0 + 2 * GLA_K, GLA_V, RW_P + 2 * GQ_P), (g0 + 2 * GLA_K + GLA_V, GLA_V, RW_P + 2 * GQ_P + HW),
             (g0 + 2 * GLA_K + 2 * GLA_V, GLA_LORA, RW_P + 2 * GQ_P + 2 * HW)]
    segs += [(t0 + i * RET_W, RET_W, RW_P + GL_P + i * HW) for i in range(4)]
    return segs


def _prep_weights(w_in, rwkv_mu, rwkv_w0, rwkv_bw, rwkv_a0, rwkv_ba, rwkv_bg, rwkv_kk, rwkv_ka,
                  rwkv_rk, rwkv_lnw, rwkv_lnb, gla_bgk, gla_bgk_b, gla_norm_w, w_out,
                  ln1_g, ln1_b, ln2_g, ln2_b, ffn_up, ffn_conv_w, ffn_conv_b, ffn_down):
    L = w_in.shape[0]
    w1 = _place(_bf(w_in), 2, _in_col_segments(), NP)
    mu = _pad_to(rwkv_mu, 1, RW_P)[:, None, :]
    gnw = _pad_to(jnp.tile(gla_norm_w, (1, N_GLA)), 1, HW)
    vec = jnp.stack([rwkv_w0, rwkv_a0, rwkv_kk, rwkv_ka, rwkv_rk.reshape(L, RWKV_W), rwkv_lnw, rwkv_lnb, gnw], axis=1)
    bw = _bf(_pad_to(rwkv_bw, 1, LANE))
    ba = _bf(_place(rwkv_ba, 1, [(0, LORA_A, LORA_W)], LANE))
    bg = _bf(_pad_to(rwkv_bg, 1, 2 * LANE))
    bgk = _bf(_pad_to(_pad_to(gla_bgk, 1, LANE), 2, GQ_P))
    bgkb = _pad_to(gla_bgk_b, 1, GQ_P)[:, None, :]
    wout = _place(_bf(w_out), 1, [(0, RWKV_W + GLA_V, 0), (RWKV_W + GLA_V, RET_W, 2 * HW)], MIX_P)
    ln1 = jnp.stack([ln1_g, ln1_b], axis=1)
    mixer_w = (w1, mu, vec, bw, ba, bg, bgk, bgkb, wout, ln1)
    cwb = _pad_to(jnp.concatenate([ffn_conv_w, ffn_conv_b[:, None, :]], axis=1), 1, 8)
    ln2 = jnp.stack([ln2_g, ln2_b], axis=1)
    ffn_w = (_bf(ffn_up), cwb, _bf(ffn_down), ln2)
    return mixer_w, ffn_w


def _tables(pos, i_real, c_real, chunk, reps):
    half = HEAD_DIM // 2
    inv = 1.0 / (ROPE_BASE ** jnp.linspace(0.0, 1.0, half, dtype=F32))
    ang = pos.astype(F32)[:, None] * inv[None]
    cos, sin = jnp.cos(ang), jnp.sin(ang)
    cos = jnp.tile(jnp.concatenate([cos, cos], -1), (reps, HW // HEAD_DIM))
    sin = jnp.tile(jnp.concatenate([-sin, sin], -1), (reps, HW // HEAD_DIM))
    log_gamma = jnp.log(1.0 - jnp.exp2(-5.0 - jnp.arange(N_RET, dtype=F32)))
    i = i_real.astype(F32)
    diff = i[:, None] - i[None, :]
    causal = diff >= 0
    dmask = jnp.where(causal, jnp.exp(jnp.where(causal, diff, 0.0) * log_gamma[:, None, None]), 0.0)
    qd = jnp.exp((i + 1.0) * log_gamma[:, None])[..., None]
    kd = jnp.exp((c_real - 1.0 - i) * log_gamma[:, None])[..., None]
    cd = jnp.exp(c_real * log_gamma)[:, None, None]
    qd = jnp.broadcast_to(qd, (N_RET, chunk, HEAD_DIM))
    kd = jnp.broadcast_to(kd, (N_RET, chunk, HEAD_DIM))
    cd = jnp.broadcast_to(cd, (N_RET, SLOTS, HEAD_DIM))
    return cos, sin, dmask, qd, kd, cd


def _run_group(x, s_rw, s_sh, s_gl, s_rt, s_cv, mixer_w, ffn_w, *, sample, bb_mix, tt_mix, bb_ffn, tt_ffn,
               chunk, group):
    B, T, _ = x.shape
    if sample:
        pos = jnp.clip(jnp.arange(SLOTS) - SLOT0, 0, DEC_SEQ - 1) + PAST_LEN
        tabs = _tables(pos, jnp.arange(SLOTS) - SLOT0, float(DEC_SEQ), chunk, bb_mix)
        shift_rows = _place(_pad_to(s_sh, 2, RW_P)[:, :, None, :], 2, [(0, 1, SLOT0)], SLOTS)
        conv_rows = _pad_to(s_cv, 2, SLOTS)
    else:
        tabs = _tables(jnp.arange(T), jnp.arange(chunk), float(chunk), chunk, 1)
        shift_rows = _pad_to(_pad_to(s_sh, 2, RW_P)[:, :, None, :], 2, SLOTS)
        conv_rows = _place(s_cv, 2, [(0, CONV_W - 1, SLOTS - (CONV_W - 1))], SLOTS)
    n_rw, n_sh, n_gl, n_rt, n_cv = [], [], [], [], []
    for l in range(DEPTH):
        x, sh, rw, gl, rt = _mixer_call(l, x, shift_rows, s_rw, s_gl, s_rt, tabs, mixer_w,
                                        bb=bb_mix, tt=tt_mix, chunk=chunk, group=group, sample=sample)
        x, cv = _ffn_call(l, x, conv_rows, ffn_w, bb=bb_ffn, tt=tt_ffn, sample=sample)
        if sample:
            n_sh.append(sh[:, SLOT0 + DEC_SEQ - 1, :RWKV_COLS])
            n_cv.append(cv[:, SLOT0 + DEC_SEQ - (CONV_W - 1):SLOT0 + DEC_SEQ])
        else:
            n_sh.append(sh[:, SLOTS - 1, :RWKV_COLS])
            n_cv.append(cv[:, SLOTS - (CONV_W - 1):])
        n_rw.append(rw)
        n_gl.append(gl)
        n_rt.append(rt)
    return x, (jnp.stack(n_rw), jnp.stack(n_sh), jnp.stack(n_gl), jnp.stack(n_rt), jnp.stack(n_cv))


def kernel(x_prompt, x_sample, state_rwkv, state_shift, state_gla, state_ret, state_conv, w_in, rwkv_mu, rwkv_w0, rwkv_bw, rwkv_a0, rwkv_ba, rwkv_bg, rwkv_kk, rwkv_ka, rwkv_rk, rwkv_lnw, rwkv_lnb, gla_bgk, gla_bgk_b, gla_norm_w, w_out, ln1_g, ln1_b, ln2_g, ln2_b, ffn_up, ffn_conv_w, ffn_conv_b, ffn_down):
    mixer_w, ffn_w = _prep_weights(w_in, rwkv_mu, rwkv_w0, rwkv_bw, rwkv_a0, rwkv_ba, rwkv_bg, rwkv_kk, rwkv_ka,
                                   rwkv_rk, rwkv_lnw, rwkv_lnb, gla_bgk, gla_bgk_b, gla_norm_w, w_out,
                                   ln1_g, ln1_b, ln2_g, ln2_b, ffn_up, ffn_conv_w, ffn_conv_b, ffn_down)
    bp, tp, _ = x_prompt.shape
    bs, ts, _ = x_sample.shape
    assert ts == DEC_SEQ

    def zeros_like_state(s):
        return jnp.zeros((s.shape[0], bp) + s.shape[2:], F32)

    tt = min(256, tp)
    chunk = math.gcd(tp, CHUNK)
    y_p, st_p = _run_group(
        x_prompt, zeros_like_state(state_rwkv), zeros_like_state(state_shift), zeros_like_state(state_gla),
        zeros_like_state(state_ret), zeros_like_state(state_conv), mixer_w, ffn_w,
        sample=False, bb_mix=1, tt_mix=tt, bb_ffn=1, tt_ffn=tt, chunk=chunk, group=tt // chunk)

    x8 = _place(x_sample, 1, [(0, ts, SLOT0)], SLOTS)
    bb_mix = min(16, bs)
    bb_ffn = min(32, bs)
    y_s, st_s = _run_group(
        x8, state_rwkv, state_shift, state_gla, state_ret, state_conv, mixer_w, ffn_w,
        sample=True, bb_mix=bb_mix, tt_mix=SLOTS, bb_ffn=bb_ffn, tt_ffn=SLOTS, chunk=SLOTS, group=min(4, bb_mix))
    y_s = y_s[:, SLOT0:SLOT0 + ts]
    return (y_p, y_s) + st_p + st_s
```

```python
import functools
import math

import jax
import jax.numpy as jnp
from jax import lax
from jax.experimental import pallas as pl
from jax.experimental.pallas import tpu as pltpu

F32 = jnp.float32
BF16 = jnp.bfloat16

D_MODEL = 1024
DEPTH = 4
PAST_LEN = 16384
DEC_SEQ = 4
HEAD_DIM = 64
N_HEADS = D_MODEL // HEAD_DIM
N_GLA = (5 * N_HEADS) // 16
N_RET = (5 * N_HEADS) // 16
N_RWKV = N_HEADS - N_GLA - N_RET
RWKV_W = N_RWKV * HEAD_DIM
GLA_DK = HEAD_DIM // 2
GLA_K = N_GLA * GLA_DK
GLA_V = N_GLA * HEAD_DIM
RET_W = N_RET * HEAD_DIM
LORA_W = 64
LORA_A = 64
LORA_G = 160
GLA_LORA = 16
GLA_GATE_NORM = 16.0
CHUNK = 64
D_FF = 2816
CONV_W = 3
ALPHA = (2 * DEPTH) ** 0.25
RWKV_GN_EPS = 64e-5
LN_EPS = 1e-5
RMS_EPS = 1e-6
ROPE_BASE = 10000.0
RWKV_COLS = 3 * RWKV_W + LORA_W + LORA_A + LORA_G
GLA_COLS = 2 * GLA_K + 2 * GLA_V + GLA_LORA
RET_COLS = 4 * RET_W

LANE = 128
HW = 384
RW_P = 1536
GQ_P = 256
GL_P = 2 * GQ_P + 2 * HW + LANE
RT_P = 4 * HW
NP = RW_P + GL_P + RT_P
MIX_P = 3 * HW
SLOTS = 8
SLOT0 = 2

VMEM_LIMIT = 56 * 1024 * 1024


_NN = (((1,), (0,)), ((), ()))
_NT = (((1,), (1,)), ((), ()))
_TN = (((0,), (0,)), ((), ()))


def _bf(x):
    return x.astype(BF16)


def _dg(a, b, dims):
    return lax.dot_general(a, b, dims, preferred_element_type=F32)


def _dot(a, b, dims=_NN):
    return _dg(_bf(a), _bf(b), dims)


def _split(x, n):
    parts = []
    r = x
    for i in range(n):
        h = _bf(r)
        parts.append(h)
        if i + 1 < n:
            r = r - h.astype(F32)
    return parts


def _dot_sel(a, sel_bf, n=2, dims=_NN):
    out = None
    for h in _split(a, n):
        t = _dg(h, sel_bf, dims)
        out = t if out is None else out + t
    return out


def _sel_dot(sel_bf, b, n=3, dims=_NN):
    out = None
    for h in _split(b, n):
        t = _dg(sel_bf, h, dims)
        out = t if out is None else out + t
    return out


def _layer_norm(h, g, b):
    mu = jnp.mean(h, -1, keepdims=True)
    d = h - mu
    var = jnp.mean(d * d, -1, keepdims=True)
    return d * lax.rsqrt(var + LN_EPS) * g + b


def _swap_halves(x):
    pieces = []
    for i in range(x.shape[1] // LANE):
        p = x[:, i * LANE:(i + 1) * LANE]
        up = pltpu.roll(p, LANE - HEAD_DIM // 2, axis=1)
        dn = pltpu.roll(p, HEAD_DIM // 2, axis=1)
        lane = lax.broadcasted_iota(jnp.int32, p.shape, 1)
        pieces.append(jnp.where((lane % HEAD_DIM) < HEAD_DIM // 2, up, dn))
    return jnp.concatenate(pieces, axis=1)


def _mixer_kernel(x_ref, shift_ref, srw_ref, sgl_ref, srt_ref, cos_ref, sin_ref,
                  dmask_ref, qd_ref, kd_ref, cd_ref,
                  w1_ref, mu_ref, vec_ref, bw_ref, ba_ref, bg_ref, bgk_ref, bgkb_ref,
                  wout_ref, ln_ref,
                  x1_ref, shift_out_ref, srw_out, sgl_out, srt_out,
                  carry_ref, rw_r, rw_k, rw_v, rw_lw, rw_a, rw_b, rw_g, rw_bonus, rw_y,
                  gl_q, gl_k, gl_v, gl_lg, gl_gate, gl_o,
                  rt_q, rt_k, rt_v, rt_gate, rt_o,
                  *, rows, chunk, group, sample):
    R, C = rows, chunk
    G = group * C
    j = pl.program_id(1)

    @pl.when(j == 0)
    def _():
        srw_out[...] = srw_ref[...]
        sgl_out[...] = sgl_ref[...]
        srt_out[...] = srt_ref[...]

    x = x_ref[...].reshape(R, D_MODEL)
    xb = _bf(x)
    row = lax.broadcasted_iota(jnp.int32, (R, 1), 0)
    if sample:
        slot = row % SLOTS
        real = (slot >= SLOT0) & (slot < SLOT0 + DEC_SEQ)
    else:
        real = None

    def keep(v):
        return v if real is None else jnp.where(real, v, 0.0)

    hi_ = lax.broadcasted_iota(jnp.int32, (2 * LANE, 2 * LANE), 0) // HEAD_DIM
    hj_ = lax.broadcasted_iota(jnp.int32, (2 * LANE, 2 * LANE), 1) // HEAD_DIM
    head_ones = (hi_ == hj_).astype(BF16)

    def head_sum(v):
        return jnp.concatenate([_dg(_bf(v[:, 0:2 * LANE]), head_ones, _NN),
                                _dg(_bf(v[:, 2 * LANE:HW]), head_ones[0:LANE, 0:LANE], _NN)], axis=1)

    p = _dg(xb, w1_ref[:, 0:RW_P], _NN)
    rolled = pltpu.roll(p, 1, axis=0)
    if sample:
        shift_out_ref[...] = p.reshape(shift_out_ref.shape)
        prev = jnp.where(slot == SLOT0, shift_ref[...].reshape(R, RW_P), rolled)
    else:
        first = jnp.where(j == 0, shift_ref[0, 0:1, :], carry_ref[SLOTS - 1:SLOTS, :])
        prev = jnp.where(row == 0, first, rolled)
        carry_ref[...] = p[R - SLOTS:R, :]
        shift_out_ref[0] = p[R - SLOTS:R, :]
    pm = p + (prev - p) * mu_ref[...]
    r = pm[:, 0:HW]
    k = pm[:, HW:2 * HW]
    v = pm[:, 2 * HW:3 * HW]
    wa = pm[:, 3 * HW:3 * HW + LANE]
    xg = pm[:, 3 * HW + LANE:RW_P]
    w0 = vec_ref[0:1, :]
    a0 = vec_ref[1:2, :]
    k_k = vec_ref[2:3, :]
    k_a = vec_ref[3:4, :]
    r_k = vec_ref[4:5, :]
    w_log = -jax.nn.softplus(-(w0 + _dg(_bf(jnp.tanh(wa)), bw_ref[...], _NN))) - 0.5
    log_decay = -jnp.exp(w_log)
    a = jax.nn.sigmoid(a0 + _dg(_bf(wa), ba_ref[...], _NN))
    g = _dg(_bf(jax.nn.sigmoid(xg)), bg_ref[...], _NN)
    kk = k * k_k
    kk = kk / jnp.maximum(jnp.sqrt(head_sum(kk * kk)), 1e-12)
    k2 = k * (1.0 + (a - 1.0) * k_a)
    rw_r[...] = r
    rw_k[...] = keep(k2)
    rw_v[...] = keep(v)
    rw_lw[...] = keep(log_decay)
    rw_a[...] = keep(-kk)
    rw_b[...] = keep(kk * a)
    rw_g[...] = g
    rw_bonus[...] = head_sum(r * k2 * r_k) * v

    p = _dg(xb, w1_ref[:, RW_P:RW_P + GL_P], _NN)
    gk = p[:, 2 * GQ_P + 2 * HW:GL_P]
    lg = jax.nn.log_sigmoid(_dg(_bf(gk), bgk_ref[...], _NN) + bgkb_ref[...]) / GLA_GATE_NORM
    gl_q[...] = p[:, 0:GQ_P] * GLA_DK ** -0.5
    gl_k[...] = keep(p[:, GQ_P:2 * GQ_P])
    gl_v[...] = p[:, 2 * GQ_P:2 * GQ_P + HW]
    gl_lg[...] = keep(lg)
    gl_gate[...] = jax.nn.silu(p[:, 2 * GQ_P + HW:2 * GQ_P + 2 * HW])

    p = _dg(xb, w1_ref[:, RW_P + GL_P:NP], _NN)
    cos = cos_ref[...]
    sin = sin_ref[...]
    q = p[:, 0:HW]
    k = p[:, HW:2 * HW]
    rt_q[...] = q * cos + _swap_halves(q) * sin
    rt_k[...] = keep((k * cos + _swap_halves(k) * sin) * HEAD_DIM ** -0.5)
    rt_v[...] = p[:, 2 * HW:3 * HW]
    rt_gate[...] = jax.nn.silu(p[:, 3 * HW:4 * HW])

    gi = lax.broadcasted_iota(jnp.int32, (G, G), 0)
    gj = lax.broadcasted_iota(jnp.int32, (G, G), 1)
    same_chunk = (gi // C) == (gj // C)
    cum_sel = (same_chunk & (gi >= gj)).astype(BF16)

    def chunk_cumsum(v):
        cum = _sel_dot(cum_sel, v, n=2)
        tot = jnp.concatenate([jnp.broadcast_to(cum[(c + 1) * C - 1:(c + 1) * C, :], (C, v.shape[1]))
                               for c in range(group)], axis=0)
        return cum, tot
    ii = lax.broadcasted_iota(jnp.int32, (C, C), 0)
    jj = lax.broadcasted_iota(jnp.int32, (C, C), 1)
    tri_incl = ii >= jj
    tri_strict = ii > jj
    eye = (ii == jj).astype(F32)
    levels = []
    s = 1
    while s < C:
        levels.append(((ii // (2 * s)) == (jj // (2 * s))) & ((ii % (2 * s)) >= s) & ((jj % (2 * s)) < s))
        s *= 2
    ones_cv = jnp.ones((C, HEAD_DIM), BF16)
    chunks = range(group)

    def group_body(gidx, carry):
        g0 = pl.multiple_of(gidx * G, G)
        gs = pl.ds(g0, G)

        def cr(c):
            return slice(c * C, (c + 1) * C)

        def state_index(c):
            return gidx * group + c if sample else 0

        lw = rw_lw[gs, :]
        cum, tot = chunk_cumsum(lw)
        w_inv = jnp.exp(-cum)
        w_rem = jnp.exp(tot - cum)
        r_t = rw_r[gs, :] * jnp.exp(cum)
        a_t = rw_a[gs, :] * jnp.exp(cum - lw)
        b_t = rw_b[gs, :] * w_inv
        k_t = rw_k[gs, :] * w_inv
        b_h = rw_b[gs, :] * w_rem
        k_h = rw_k[gs, :] * w_rem
        w_c = jnp.exp(tot)
        v_rw = rw_v[gs, :]
        items = [(c, h) for c in chunks for h in range(N_RWKV)]

        def hs(h):
            return slice(h * HEAD_DIM, (h + 1) * HEAD_DIM)

        ar = [jnp.concatenate([a_t[cr(c), hs(h)], r_t[cr(c), hs(h)]], axis=0) for c, h in items]
        xb_ = [_dot(ar[i], b_t[cr(c), hs(h)], _NT) for i, (c, h) in enumerate(items)]
        xk_ = [_dot(ar[i], k_t[cr(c), hs(h)], _NT) for i, (c, h) in enumerate(items)]
        a_ab = [jnp.where(tri_strict, t[0:C], 0.0) for t in xb_]
        a_ak = [jnp.where(tri_strict, t[0:C], 0.0) for t in xk_]
        p_rb = [jnp.where(tri_incl, t[C:2 * C], 0.0) for t in xb_]
        p_rk = [jnp.where(tri_incl, t[C:2 * C], 0.0) for t in xk_]

        lg = gl_lg[gs, :]
        gcum, gtot = chunk_cumsum(lg)
        q_in = gl_q[gs, :] * jnp.exp(gcum)
        k_in = gl_k[gs, :] * jnp.exp(-gcum)
        k_st = gl_k[gs, :] * jnp.exp(gtot - gcum)
        v_gl = gl_v[gs, :]
        g_col = []
        for c in chunks:
            parts = _split(lg[cr(c), :], 3)
            g_col.append(jnp.exp(_dg(parts[0], ones_cv, _TN) + _dg(parts[1], ones_cv, _TN)
                                 + _dg(parts[2], ones_cv, _TN)))
        gitems = [(c, h) for c in chunks for h in range(N_GLA)]

        def ks(h):
            return slice(h * GLA_DK, (h + 1) * GLA_DK)

        g_att = [jnp.where(tri_incl, _dot(q_in[cr(c), ks(h)], k_in[cr(c), ks(h)], _NT), 0.0) for c, h in gitems]
        g_kv = [_dot(k_st[cr(c), ks(h)], v_gl[cr(c), hs(h)], _TN) for c, h in gitems]

        q_rt = rt_q[gs, :]
        k_rt = rt_k[gs, :]
        v_rt = rt_v[gs, :]
        ritems = [(c, h) for c in chunks for h in range(N_RET)]
        r_att = [_dot(q_rt[cr(c), hs(h)], k_rt[cr(c), hs(h)], _NT) * dmask_ref[h] for c, h in ritems]
        r_kv = [_dot(k_rt[cr(c), hs(h)] * kd_ref[h], v_rt[cr(c), hs(h)], _TN) for c, h in ritems]

        m = [eye + jnp.where(levels[0], t, 0.0) for t in a_ab]
        for lvl in levels[1:]:
            t_ = [_dot(jnp.where(lvl, a_ab[i], 0.0), m[i]) for i in range(len(items))]
            m = [m[i] + _dot(m[i], t_[i]) for i in range(len(items))]

        g_av = [_dot(g_att[i], v_gl[cr(c), hs(h)]) for i, (c, h) in enumerate(gitems)]
        r_av = [_dot(r_att[i], v_rt[cr(c), hs(h)]) for i, (c, h) in enumerate(ritems)]

        vv = [v_rw[cr(c), hs(h)] for c, h in items]
        akv = [_dot(a_ak[i], vv[i]) for i in range(len(items))]
        at2 = [_dot(m[i], a_t[cr(c), hs(h)]) for i, (c, h) in enumerate(items)]
        y0 = [_dot(p_rk[i], vv[i]) for i in range(len(items))]
        u0 = [_dot(m[i], akv[i]) for i in range(len(items))]
        gp = [_dot(at2[i], b_h[cr(c), hs(h)], _TN) for i, (c, h) in enumerate(items)]
        hh = [_dot(u0[i], b_h[cr(c), hs(h)], _TN) + _dot(vv[i], k_h[cr(c), hs(h)], _TN)
              for i, (c, h) in enumerate(items)]

        s0 = [None] * len(items)
        if sample:
            for i, (c, h) in enumerate(items):
                S = srw_out[state_index(c), h]
                s0[i] = S
                srw_out[state_index(c), h] = S * w_c[c * C:c * C + 1, hs(h)] + _dot(S, gp[i]) + hh[i]
        else:
            cur = [srw_out[0, h] for h in range(N_RWKV)]
            for c in chunks:
                for h in range(N_RWKV):
                    i = c * N_RWKV + h
                    s0[i] = cur[h]
                    cur[h] = cur[h] * w_c[c * C:c * C + 1, hs(h)] + _dot(cur[h], gp[i]) + hh[i]
            for h in range(N_RWKV):
                srw_out[0, h] = cur[h]

        gs0 = [None] * len(gitems)
        rs0 = [None] * len(ritems)
        if sample:
            for i, (c, h) in enumerate(gitems):
                S = sgl_out[state_index(c), h]
                gs0[i] = S
                sgl_out[state_index(c), h] = S * g_col[c][ks(h), :] + g_kv[i]
            for i, (c, h) in enumerate(ritems):
                S = srt_out[state_index(c), h]
                rs0[i] = S
                srt_out[state_index(c), h] = S * cd_ref[h, 0:1, :] + r_kv[i]
        else:
            for h in range(N_GLA):
                S = sgl_out[0, h]
                for c in chunks:
                    gs0[c * N_GLA + h] = S
                    S = S * g_col[c][ks(h), :] + g_kv[c * N_GLA + h]
                sgl_out[0, h] = S
            for h in range(N_RET):
                S = srt_out[0, h]
                for c in chunks:
                    rs0[c * N_RET + h] = S
                    S = S * cd_ref[h, 0:1, :] + r_kv[c * N_RET + h]
                srt_out[0, h] = S

        u = [_dot(at2[i], s0[i], _NT) + u0[i] for i in range(len(items))]
        g_o = [g_av[i] + _dot(q_in[cr(c), ks(h)], gs0[i]) for i, (c, h) in enumerate(gitems)]
        r_o = [r_av[i] + _dot(q_rt[cr(c), hs(h)] * qd_ref[h], rs0[i]) for i, (c, h) in enumerate(ritems)]
        y = [_dot(r_t[cr(c), hs(h)], s0[i], _NT) + _dot(p_rb[i], u[i]) + y0[i] for i, (c, h) in enumerate(items)]
        for c in chunks:
            rows_c = pl.ds(g0 + c * C, C)
            rw_y[rows_c, :] = jnp.concatenate(y[c * N_RWKV:(c + 1) * N_RWKV], axis=1)
            gl_o[rows_c, :] = jnp.concatenate(g_o[c * N_GLA:(c + 1) * N_GLA]
                                              + [jnp.zeros((C, HW - GLA_V), F32)], axis=1)
            rt_o[rows_c, :] = jnp.concatenate(r_o[c * N_RET:(c + 1) * N_RET]
                                              + [jnp.zeros((C, HW - RET_W), F32)], axis=1)
        return carry

    if R == G:
        group_body(0, 0)
    else:
        lax.fori_loop(0, R // G, group_body, 0)

    ln_w = vec_ref[5:6, :]
    ln_b = vec_ref[6:7, :]
    gn_w = vec_ref[7:8, :]
    inv_hd = 1.0 / HEAD_DIM
    y = rw_y[...]
    ym = head_sum(y) * inv_hd
    d = y - ym
    yv = head_sum(d * d) * inv_hd
    ya = (d * lax.rsqrt(yv + RWKV_GN_EPS) * ln_w + ln_b + rw_bonus[...]) * rw_g[...]
    o = gl_o[...]
    ob = o * lax.rsqrt(head_sum(o * o) * inv_hd + RMS_EPS) * gn_w * gl_gate[...]
    o = rt_o[...]
    oc = o * lax.rsqrt(head_sum(o * o) * inv_hd + RMS_EPS) * rt_gate[...]
    mix = _dg(_bf(jnp.concatenate([ya, ob, oc], axis=1)), wout_ref[...], _NN)
    out = _layer_norm(ALPHA * x + mix, ln_ref[0:1, :], ln_ref[1:2, :])
    x1_ref[...] = out.reshape(x1_ref.shape)


def _const_spec(shape):
    nd = len(shape)
    return pl.BlockSpec(shape, lambda i, j: (0,) * nd, pipeline_mode=pl.Buffered(1))


def _layer_spec(l, shape):
    return pl.BlockSpec((None,) + shape, lambda i, j: (l,) + (0,) * len(shape), pipeline_mode=pl.Buffered(1))


def _mixer_call(l, x, shift_rows, s_rw, s_gl, s_rt, tabs, wts, *, bb, tt, chunk, group, sample):
    B, T, _ = x.shape
    R = bb * tt
    nb, nt = B // bb, T // tt
    cos, sin, dmask, qd, kd, cd = tabs
    (w1, mu, vec, bw, ba, bg, bgk, bgkb, wout, ln1) = wts
    C = chunk
    lw = functools.partial(_layer_spec, l)

    in_specs = [
        pl.BlockSpec((bb, tt, D_MODEL), lambda i, j: (i, j, 0)),
        pl.BlockSpec((None, bb, SLOTS, RW_P), lambda i, j: (l, i, 0, 0)),
        pl.BlockSpec((None, bb, N_RWKV, HEAD_DIM, HEAD_DIM), lambda i, j: (l, i, 0, 0, 0)),
        pl.BlockSpec((None, bb, N_GLA, GLA_DK, HEAD_DIM), lambda i, j: (l, i, 0, 0, 0)),
        pl.BlockSpec((None, bb, N_RET, HEAD_DIM, HEAD_DIM), lambda i, j: (l, i, 0, 0, 0)),
        pl.BlockSpec((R, HW), lambda i, j: (j, 0)),
        pl.BlockSpec((R, HW), lambda i, j: (j, 0)),
        _const_spec((N_RET, C, C)),
        _const_spec((N_RET, C, HEAD_DIM)),
        _const_spec((N_RET, C, HEAD_DIM)),
        _const_spec((N_RET, SLOTS, HEAD_DIM)),
        lw((D_MODEL, NP)), lw((1, RW_P)), lw((8, HW)), lw((LANE, HW)), lw((LANE, HW)), lw((2 * LANE, HW)),
        lw((LANE, GQ_P)), lw((1, GQ_P)), lw((MIX_P, D_MODEL)), lw((2, D_MODEL)),
    ]
    out_specs = [
        pl.BlockSpec((bb, tt, D_MODEL), lambda i, j: (i, j, 0)),
        pl.BlockSpec((bb, SLOTS, RW_P), lambda i, j: (i, 0, 0)),
        pl.BlockSpec((bb, N_RWKV, HEAD_DIM, HEAD_DIM), lambda i, j: (i, 0, 0, 0)),
        pl.BlockSpec((bb, N_GLA, GLA_DK, HEAD_DIM), lambda i, j: (i, 0, 0, 0)),
        pl.BlockSpec((bb, N_RET, HEAD_DIM, HEAD_DIM), lambda i, j: (i, 0, 0, 0)),
    ]
    out_shape = [
        jax.ShapeDtypeStruct((B, T, D_MODEL), F32),
        jax.ShapeDtypeStruct((B, SLOTS, RW_P), F32),
        jax.ShapeDtypeStruct(s_rw.shape[1:], F32),
        jax.ShapeDtypeStruct(s_gl.shape[1:], F32),
        jax.ShapeDtypeStruct(s_rt.shape[1:], F32),
    ]
    scratch = [pltpu.VMEM((SLOTS, RW_P), F32)]
    scratch += [pltpu.VMEM((R, HW), F32)] * 9
    scratch += [pltpu.VMEM((R, GQ_P), F32), pltpu.VMEM((R, GQ_P), F32), pltpu.VMEM((R, HW), F32),
                pltpu.VMEM((R, GQ_P), F32), pltpu.VMEM((R, HW), F32), pltpu.VMEM((R, HW), F32)]
    scratch += [pltpu.VMEM((R, HW), F32)] * 5
    return pl.pallas_call(
        functools.partial(_mixer_kernel, rows=R, chunk=C, group=group, sample=sample),
        grid=(nb, nt),
        in_specs=in_specs, out_specs=out_specs, out_shape=out_shape,
        scratch_shapes=scratch,
        compiler_params=pltpu.CompilerParams(dimension_semantics=("arbitrary", "arbitrary"),
                                             vmem_limit_bytes=VMEM_LIMIT),
        name="mixer_sample" if sample else "mixer_prompt",
    )(x, shift_rows, s_rw, s_gl, s_rt, cos, sin, dmask, qd, kd, cd,
      w1, mu, vec, bw, ba, bg, bgk, bgkb, wout, ln1)


FFN_CHUNK = 256
FFN_AHEAD = 3


def _ffn_kernel(x_ref, conv_ref, wup_ref, cwb_ref, wdn_ref, ln_ref, y_ref, cv_out_ref, carry_ref,
                *, rows, sample):
    R = rows
    j = pl.program_id(1)
    x = x_ref[...].reshape(R, D_MODEL)
    xb = _bf(x)
    row = lax.broadcasted_iota(jnp.int32, (R, 1), 0)
    if sample:
        slot = row % SLOTS
        real = (slot >= SLOT0) & (slot < SLOT0 + DEC_SEQ)
    row8 = lax.broadcasted_iota(jnp.int32, (SLOTS, 1), 0)
    n_f = D_FF // FFN_CHUNK

    def cols(f, base):
        return slice(base + f * FFN_CHUNK, base + (f + 1) * FFN_CHUNK)

    def up(f):
        return [_dg(xb, wup_ref[:, cols(f, base)], _NN) for base in (0, D_FF)]

    def conv(u, cs):
        if sample:
            u = jnp.where(real, u, 0.0) + conv_ref[:, :, cs].reshape(R, FFN_CHUNK)
            u1 = pltpu.roll(u, 1, axis=0)
            u2 = pltpu.roll(u, 2, axis=0)
            cv_out_ref[:, :, cs] = u.reshape(cv_out_ref.shape[0], SLOTS, FFN_CHUNK)
        else:
            prev = jnp.where(j == 0, conv_ref[0, :, cs], carry_ref[:, cs])
            u1 = pltpu.roll(u, 1, axis=0)
            u2 = pltpu.roll(u, 2, axis=0)
            h1 = jnp.where(row8 == 0, prev[SLOTS - 1:SLOTS], u1[0:SLOTS])
            h2 = jnp.where(row8 == 0, prev[SLOTS - 2:SLOTS - 1],
                           jnp.where(row8 == 1, prev[SLOTS - 1:SLOTS], u2[0:SLOTS]))
            u1 = jnp.concatenate([h1, u1[SLOTS:]], axis=0)
            u2 = jnp.concatenate([h2, u2[SLOTS:]], axis=0)
            carry_ref[:, cs] = u[R - SLOTS:R]
            cv_out_ref[0, :, cs] = u[R - SLOTS:R]
        return cwb_ref[3:4, cs] + (cwb_ref[0:1, cs] * u2 + cwb_ref[1:2, cs] * u1 + cwb_ref[2:3, cs] * u)

    acc = jnp.zeros((R, D_MODEL), F32)
    u_queue = [up(f) for f in range(min(FFN_AHEAD, n_f))]
    for f in range(n_f):
        if f + FFN_AHEAD < n_f:
            u_queue.append(up(f + FFN_AHEAD))
        u_cur = u_queue.pop(0)
        hid = jax.nn.gelu(conv(u_cur[0], cols(f, 0))) * conv(u_cur[1], cols(f, D_FF))
        acc = acc + _dg(_bf(hid), wdn_ref[f * FFN_CHUNK:(f + 1) * FFN_CHUNK, :], _NN)
    out = _layer_norm(ALPHA * x + acc, ln_ref[0:1, :], ln_ref[1:2, :])
    y_ref[...] = out.reshape(y_ref.shape)


def _ffn_call(l, x, conv_rows, wts, *, bb, tt, sample):
    B, T, _ = x.shape
    R = bb * tt
    nb, nt = B // bb, T // tt
    wup, cwb, wdn, ln2 = wts
    lw = functools.partial(_layer_spec, l)
    return pl.pallas_call(
        functools.partial(_ffn_kernel, rows=R, sample=sample),
        grid=(nb, nt),
        in_specs=[
            pl.BlockSpec((bb, tt, D_MODEL), lambda i, j: (i, j, 0)),
            pl.BlockSpec((None, bb, SLOTS, 2 * D_FF), lambda i, j: (l, i, 0, 0)),
            lw((D_MODEL, 2 * D_FF)), lw((8, 2 * D_FF)), lw((D_FF, D_MODEL)), lw((2, D_MODEL)),
        ],
        out_specs=[
            pl.BlockSpec((bb, tt, D_MODEL), lambda i, j: (i, j, 0)),
            pl.BlockSpec((bb, SLOTS, 2 * D_FF), lambda i, j: (i, 0, 0)),
        ],
        out_shape=[jax.ShapeDtypeStruct((B, T, D_MODEL), F32),
                   jax.ShapeDtypeStruct((B, SLOTS, 2 * D_FF), F32)],
        scratch_shapes=[pltpu.VMEM((SLOTS, 2 * D_FF), F32)],
        compiler_params=pltpu.CompilerParams(dimension_semantics=("arbitrary", "arbitrary"),
                                             vmem_limit_bytes=VMEM_LIMIT),
        name="ffn_sample" if sample else "ffn_prompt",
    )(x, conv_rows, wup, cwb, wdn, ln2)


N_STEPS = N_RWKV + N_GLA + N_RET
G0 = RWKV_COLS
T0 = RWKV_COLS + GLA_COLS


def _smix_kernel(x_ref, shift_ref, srw_ref, sgl_ref, srt_ref, w1t_ref, mu_ref, vec_ref,
                 bwt_ref, bat_ref, bgt_ref, bgkt_ref, gb_ref, gn_ref, rot_ref, gam_ref, wout_ref, ln_ref,
                 x1_ref, shift_out_ref, srw_out, sgl_out, srt_out,
                 rw_r, rw_k, rw_v, rw_w, rw_a, rw_b, rw_g, rw_bonus, rw_y,
                 gl_q, gl_k, gl_v, gl_g, gl_gate, gl_o, rt_q, rt_k, rt_v, rt_gate, rt_o, *, nb):
    s = pl.program_id(0)
    M = DEC_SEQ * nb

    def ts(t):
        return slice(t * nb, (t + 1) * nb)

    def tile_t(c):
        return jnp.concatenate([c] * DEC_SEQ, axis=1)

    def hrows(h):
        return slice(h * HEAD_DIM, (h + 1) * HEAD_DIM)

    @pl.when(s == 0)
    def _():
        xb = _bf(x_ref[...])

        def proj(r0, r1):
            return _dg(w1t_ref[r0:r1, :], xb, _NT)

        p = proj(0, RWKV_COLS)
        shift_out_ref[...] = p[:, (DEC_SEQ - 1) * nb:]
        prev = jnp.concatenate([shift_ref[...], p[:, :(DEC_SEQ - 1) * nb]], axis=1)
        pm = p + (prev - p) * tile_t(mu_ref[...])
        r = pm[0:RWKV_W]
        k = pm[RWKV_W:2 * RWKV_W]
        v = pm[2 * RWKV_W:3 * RWKV_W]
        xw = pm[3 * RWKV_W:3 * RWKV_W + LORA_W]
        xa = pm[3 * RWKV_W + LORA_W:3 * RWKV_W + LORA_W + LORA_A]
        xg = pm[3 * RWKV_W + LORA_W + LORA_A:RWKV_COLS]
        w0, a0, k_k, k_a, r_k = (tile_t(vec_ref[i]) for i in range(5))
        w_log = -jax.nn.softplus(-(w0 + _dg(bwt_ref[...], _bf(jnp.tanh(xw)), _NN))) - 0.5
        a = jax.nn.sigmoid(a0 + _dg(bat_ref[...], _bf(xa), _NN))
        kk = k * k_k
        k2 = k * (1.0 + (a - 1.0) * k_a)
        rk2 = r * k2 * r_k
        for h in range(N_RWKV):
            hs = hrows(h)
            kh = kk[hs]
            kh = kh / jnp.maximum(jnp.sqrt(jnp.sum(kh * kh, axis=0, keepdims=True)), 1e-12)
            rw_a[hs, :] = -kh
            rw_b[hs, :] = kh * a[hs]
            rw_bonus[hs, :] = jnp.sum(rk2[hs], axis=0, keepdims=True) * v[hs]
        rw_r[...] = r
        rw_k[...] = k2
        rw_v[...] = v
        rw_w[...] = jnp.exp(-jnp.exp(w_log))
        rw_g[...] = _dg(bgt_ref[...], _bf(jax.nn.sigmoid(xg)), _NN)

        p = proj(G0, T0)
        gk = p[2 * GLA_K + 2 * GLA_V:GLA_COLS]
        lg = jax.nn.log_sigmoid(_dg(bgkt_ref[...], _bf(gk), _NN) + tile_t(gb_ref[...])) / GLA_GATE_NORM
        gl_q[...] = p[0:GLA_K] * GLA_DK ** -0.5
        gl_k[...] = p[GLA_K:2 * GLA_K]
        gl_v[...] = p[2 * GLA_K:2 * GLA_K + GLA_V]
        gl_g[...] = jnp.exp(lg)
        gl_gate[...] = jax.nn.silu(p[2 * GLA_K + GLA_V:2 * GLA_K + 2 * GLA_V])

        p = proj(T0, T0 + RET_COLS)
        cos = jnp.concatenate([rot_ref[0, t] for t in range(DEC_SEQ)], axis=1)
        sin = jnp.concatenate([rot_ref[1, t] for t in range(DEC_SEQ)], axis=1)
        half = HEAD_DIM // 2

        def rot(xh):
            x1, x2 = xh[0:half], xh[half:HEAD_DIM]
            return jnp.concatenate([x1 * cos - x2 * sin, x1 * sin + x2 * cos], axis=0)

        for h in range(N_RET):
            hs = hrows(h)
            rt_q[hs, :] = rot(p[hs])
            rt_k[hs, :] = rot(p[RET_W + h * HEAD_DIM:RET_W + (h + 1) * HEAD_DIM]) * HEAD_DIM ** -0.5
        rt_v[...] = p[2 * RET_W:3 * RET_W]
        rt_gate[...] = jax.nn.silu(p[3 * RET_W:4 * RET_W])

    @pl.when(s < N_RWKV)
    def _():
        r0 = pl.multiple_of(s * HEAD_DIM, HEAD_DIM)
        hs = pl.ds(r0, HEAD_DIM)

        def v_group(i, carry):
            v0 = pl.multiple_of(i * 8, 8)
            vt = [rw_v[pl.ds(r0 + v0, 8), ts(t)] for t in range(DEC_SEQ)]
            ys = [[] for _ in range(DEC_SEQ)]
            for j in range(8):
                S = srw_ref[v0 + j]
                for t in range(DEC_SEQ):
                    sa = jnp.sum(S * rw_a[hs, ts(t)], axis=0, keepdims=True)
                    S = S * rw_w[hs, ts(t)] + sa * rw_b[hs, ts(t)] + vt[t][j:j + 1, :] * rw_k[hs, ts(t)]
                    ys[t].append(jnp.sum(S * rw_r[hs, ts(t)], axis=0, keepdims=True))
                srw_out[v0 + j] = S
            for t in range(DEC_SEQ):
                rw_y[pl.ds(r0 + v0, 8), ts(t)] = jnp.concatenate(ys[t], axis=0)
            return carry

        lax.fori_loop(0, HEAD_DIM // 8, v_group, 0)

    def kv_head(s_in, s_out, q_ref, k_ref, v_ref, o_ref, decay_rows, k0, v0, nk):
        for t in range(DEC_SEQ):
            src = s_in if t == 0 else s_out
            v_t = v_ref[pl.ds(v0, HEAD_DIM), ts(t)]

            def k_group(i, o, t=t, src=src, v_t=v_t):
                kg = pl.multiple_of(i * 8, 8)
                q8 = q_ref[pl.ds(k0 + kg, 8), ts(t)]
                k8 = k_ref[pl.ds(k0 + kg, 8), ts(t)]
                d8 = decay_rows(kg, t)
                for j in range(8):
                    S = src[kg + j] * d8[j:j + 1, :] + k8[j:j + 1, :] * v_t
                    s_out[kg + j] = S
                    o = o + q8[j:j + 1, :] * S
                return o

            o_ref[pl.ds(v0, HEAD_DIM), ts(t)] = lax.fori_loop(0, nk // 8, k_group, jnp.zeros((HEAD_DIM, nb), F32))

    @pl.when((s >= N_RWKV) & (s < N_RWKV + N_GLA))
    def _():
        h = s - N_RWKV
        k0 = pl.multiple_of(h * GLA_DK, GLA_DK)
        v0 = pl.multiple_of(h * HEAD_DIM, HEAD_DIM)
        kv_head(sgl_ref, sgl_out, gl_q, gl_k, gl_v, gl_o,
                lambda kg, t: gl_g[pl.ds(k0 + kg, 8), ts(t)], k0, v0, GLA_DK)

    @pl.when(s >= N_RWKV + N_GLA)
    def _():
        h = s - (N_RWKV + N_GLA)
        v0 = pl.multiple_of(h * HEAD_DIM, HEAD_DIM)
        gamma = gam_ref[h]
        kv_head(srt_ref, srt_out, rt_q, rt_k, rt_v, rt_o, lambda kg, t: gamma, v0, v0, HEAD_DIM)

    @pl.when(s == N_STEPS - 1)
    def _():
        ln_w, ln_b = tile_t(vec_ref[5]), tile_t(vec_ref[6])
        gn_w = tile_t(gn_ref[...])
        parts = []
        for h in range(N_RWKV):
            hs = hrows(h)
            y = rw_y[hs, :]
            d = y - jnp.mean(y, axis=0, keepdims=True)
            yv = jnp.mean(d * d, axis=0, keepdims=True)
            parts.append((d * lax.rsqrt(yv + RWKV_GN_EPS) * ln_w[hs] + ln_b[hs] + rw_bonus[hs, :]) * rw_g[hs, :])
        for h in range(N_GLA):
            hs = hrows(h)
            o = gl_o[hs, :]
            parts.append(o * lax.rsqrt(jnp.mean(o * o, axis=0, keepdims=True) + RMS_EPS) * gn_w[hs] * gl_gate[hs, :])
        for h in range(N_RET):
            hs = hrows(h)
            o = rt_o[hs, :]
            parts.append(o * lax.rsqrt(jnp.mean(o * o, axis=0, keepdims=True) + RMS_EPS) * rt_gate[hs, :])
        mix_t = _bf(jnp.concatenate(parts, axis=0))
        mix = _dg(mix_t, wout_ref[...], _TN)
        x1_ref[...] = _layer_norm(ALPHA * x_ref[...] + mix, ln_ref[0:1, :], ln_ref[1:2, :])


def _smix_call(l, x, shift_t, s_rw, s_gl, s_rt, wts):
    M = x.shape[0]
    nb = M // DEC_SEQ
    (w1t, mu, vec, bwt, bat, bgt, bgkt, gb, gn, rot, gam, wout, ln1) = wts

    def once(shape, idx=()):
        nd = len(shape)
        return pl.BlockSpec(shape, lambda s: idx + (0,) * (nd - len(idx)), pipeline_mode=pl.Buffered(1))

    def layer(shape):
        return pl.BlockSpec((None,) + shape, lambda s: (l,) + (0,) * len(shape), pipeline_mode=pl.Buffered(1))

    def head_in(shape, first, n):
        return pl.BlockSpec((None, None) + shape,
                            lambda s: (l, jnp.clip(s - first, 0, n - 1)) + (0,) * len(shape))

    def head_out(shape, first, n):
        return pl.BlockSpec((None,) + shape, lambda s: (jnp.clip(s - first, 0, n - 1),) + (0,) * len(shape))

    rw_blk = (HEAD_DIM, HEAD_DIM, nb)
    gl_blk = (GLA_DK, HEAD_DIM, nb)
    in_specs = [
        once((M, D_MODEL)), layer((RWKV_COLS, nb)),
        head_in(rw_blk, 0, N_RWKV), head_in(gl_blk, N_RWKV, N_GLA), head_in(rw_blk, N_RWKV + N_GLA, N_RET),
        layer((RWKV_COLS + GLA_COLS + RET_COLS, D_MODEL)), layer((RWKV_COLS, nb)), layer((7, RWKV_W, nb)),
        layer((RWKV_W, LORA_W)), layer((RWKV_W, LORA_A)), layer((RWKV_W, LORA_G)), layer((GLA_K, GLA_LORA)),
        layer((GLA_K, nb)), layer((GLA_V, nb)), once((2, DEC_SEQ, HEAD_DIM // 2, nb)), once((N_RET, 8, nb)),
        layer((D_MODEL, D_MODEL)), layer((2, D_MODEL)),
    ]
    out_specs = [
        once((M, D_MODEL)), once((RWKV_COLS, nb)),
        head_out(rw_blk, 0, N_RWKV), head_out(gl_blk, N_RWKV, N_GLA), head_out(rw_blk, N_RWKV + N_GLA, N_RET),
    ]
    out_shape = [
        jax.ShapeDtypeStruct((M, D_MODEL), F32), jax.ShapeDtypeStruct((RWKV_COLS, nb), F32),
        jax.ShapeDtypeStruct((N_RWKV,) + rw_blk, F32), jax.ShapeDtypeStruct((N_GLA,) + gl_blk, F32),
        jax.ShapeDtypeStruct((N_RET,) + rw_blk, F32),
    ]
    scratch = ([pltpu.VMEM((RWKV_W, M), F32)] * 9
               + [pltpu.VMEM((GLA_K, M), F32), pltpu.VMEM((GLA_K, M), F32), pltpu.VMEM((GLA_V, M), F32),
                  pltpu.VMEM((GLA_K, M), F32), pltpu.VMEM((GLA_V, M), F32), pltpu.VMEM((GLA_V, M), F32)]
               + [pltpu.VMEM((RET_W, M), F32)] * 5)
    return pl.pallas_call(
        functools.partial(_smix_kernel, nb=nb),
        grid=(N_STEPS,),
        in_specs=in_specs, out_specs=out_specs, out_shape=out_shape, scratch_shapes=scratch,
        compiler_params=pltpu.CompilerParams(dimension_semantics=("arbitrary",), vmem_limit_bytes=VMEM_LIMIT),
        name="mixer_sample",
    )(x, shift_t, s_rw, s_gl, s_rt, w1t, mu, vec, bwt, bat, bgt, bgkt, gb, gn, rot, gam, wout, ln1)


def _sffn_kernel(x_ref, conv_ref, wup_ref, cwb_ref, wdn_ref, ln_ref, y_ref, cv_out_ref, *, nb):
    M = DEC_SEQ * nb
    U = 2 * D_FF
    x = x_ref[...]
    xb = _bf(x)
    n_f = D_FF // FFN_CHUNK

    def cols(f, base):
        return slice(base + f * FFN_CHUNK, base + (f + 1) * FFN_CHUNK)

    def up(f):
        return [_dg(xb, wup_ref[:, cols(f, base)], _NN) for base in (0, D_FF)]

    def conv(u, cs):
        c0 = conv_ref[:, cs]
        c1 = conv_ref[:, slice(U + cs.start, U + cs.stop)]
        u1 = jnp.concatenate([c1, u[0:M - nb]], axis=0)
        u2 = jnp.concatenate([c0, c1, u[0:M - 2 * nb]], axis=0)
        cv_out_ref[:, cs] = u[M - 2 * nb:M - nb]
        cv_out_ref[:, slice(U + cs.start, U + cs.stop)] = u[M - nb:M]
        return cwb_ref[3:4, cs] + (cwb_ref[0:1, cs] * u2 + cwb_ref[1:2, cs] * u1 + cwb_ref[2:3, cs] * u)

    acc = jnp.zeros((M, D_MODEL), F32)
    u_queue = [up(f) for f in range(min(FFN_AHEAD, n_f))]
    for f in range(n_f):
        if f + FFN_AHEAD < n_f:
            u_queue.append(up(f + FFN_AHEAD))
        u_cur = u_queue.pop(0)
        hid = jax.nn.gelu(conv(u_cur[0], cols(f, 0))) * conv(u_cur[1], cols(f, D_FF))
        acc = acc + _dg(_bf(hid), wdn_ref[f * FFN_CHUNK:(f + 1) * FFN_CHUNK, :], _NN)
    y_ref[...] = _layer_norm(ALPHA * x + acc, ln_ref[0:1, :], ln_ref[1:2, :])


def _sffn_call(l, x, conv2, wts):
    M = x.shape[0]
    nb = M // DEC_SEQ
    wup, cwb, wdn, ln2 = wts

    def layer(shape):
        return pl.BlockSpec((None,) + shape, lambda i: (l,) + (0,) * len(shape), pipeline_mode=pl.Buffered(1))

    return pl.pallas_call(
        functools.partial(_sffn_kernel, nb=nb),
        grid=(1,),
        in_specs=[pl.BlockSpec((M, D_MODEL), lambda i: (0, 0)), layer((nb, 4 * D_FF)),
                  layer((D_MODEL, 2 * D_FF)), layer((8, 2 * D_FF)), layer((D_FF, D_MODEL)), layer((2, D_MODEL))],
        out_specs=[pl.BlockSpec((M, D_MODEL), lambda i: (0, 0)), pl.BlockSpec((nb, 4 * D_FF), lambda i: (0, 0))],
        out_shape=[jax.ShapeDtypeStruct((M, D_MODEL), F32), jax.ShapeDtypeStruct((nb, 4 * D_FF), F32)],
        compiler_params=pltpu.CompilerParams(dimension_semantics=("arbitrary",), vmem_limit_bytes=VMEM_LIMIT),
        name="ffn_sample",
    )(x, conv2, wup, cwb, wdn, ln2)


def _run_sample(x_sample, s_rw, s_sh, s_gl, s_rt, s_cv, p, ffn_w):
    bs, ts_, _ = x_sample.shape
    L = DEPTH

    def lanes(a):
        return jnp.broadcast_to(a[..., None], a.shape + (bs,))

    half = HEAD_DIM // 2
    inv = 1.0 / (ROPE_BASE ** jnp.linspace(0.0, 1.0, half, dtype=F32))
    ang = (PAST_LEN + jnp.arange(ts_)).astype(F32)[:, None] * inv[None]
    rot = lanes(jnp.stack([jnp.cos(ang), jnp.sin(ang)]))
    log_gamma = jnp.log(1.0 - jnp.exp2(-5.0 - jnp.arange(N_RET, dtype=F32)))
    gam = jnp.broadcast_to(jnp.exp(log_gamma)[:, None, None], (N_RET, 8, bs))
    vec = lanes(jnp.stack([p["rwkv_w0"], p["rwkv_a0"], p["rwkv_kk"], p["rwkv_ka"],
                           p["rwkv_rk"].reshape(L, RWKV_W), p["rwkv_lnw"], p["rwkv_lnb"]], axis=1))
    wts = (_bf(jnp.swapaxes(p["w_in"], 1, 2)), lanes(p["rwkv_mu"]), vec,
           _bf(jnp.swapaxes(p["rwkv_bw"], 1, 2)), _bf(jnp.swapaxes(p["rwkv_ba"], 1, 2)),
           _bf(jnp.swapaxes(p["rwkv_bg"], 1, 2)), _bf(jnp.swapaxes(p["gla_bgk"], 1, 2)),
           lanes(p["gla_bgk_b"]), lanes(jnp.tile(p["gla_norm_w"], (1, N_GLA))), rot, gam,
           _bf(p["w_out"]), jnp.stack([p["ln1_g"], p["ln1_b"]], axis=1))
    rw_t = jnp.transpose(s_rw, (0, 2, 3, 4, 1))
    gl_t = jnp.transpose(s_gl, (0, 2, 3, 4, 1))
    rt_t = jnp.transpose(s_rt, (0, 2, 3, 4, 1))
    sh_t = jnp.swapaxes(s_sh, 1, 2)
    cv2 = s_cv.reshape(L, bs, (CONV_W - 1) * 2 * D_FF)
    x = jnp.swapaxes(x_sample, 0, 1).reshape(ts_ * bs, D_MODEL)
    n_rw, n_sh, n_gl, n_rt, n_cv = [], [], [], [], []
    for l in range(L):
        x, sh, rw, gl, rt = _smix_call(l, x, sh_t, rw_t, gl_t, rt_t, wts)
        x, cv = _sffn_call(l, x, cv2, ffn_w)
        n_sh.append(sh)
        n_rw.append(rw)
        n_gl.append(gl)
        n_rt.append(rt)
        n_cv.append(cv)
    y = jnp.swapaxes(x.reshape(ts_, bs, D_MODEL), 0, 1)
    back = (0, 4, 1, 2, 3)
    return y, (jnp.transpose(jnp.stack(n_rw), back), jnp.swapaxes(jnp.stack(n_sh), 1, 2),
               jnp.transpose(jnp.stack(n_gl), back), jnp.transpose(jnp.stack(n_rt), back),
               jnp.stack(n_cv).reshape(L, bs, CONV_W - 1, 2 * D_FF))


def _pad_to(a, axis, n):
    pad = [(0, 0)] * a.ndim
    pad[axis] = (0, n - a.shape[axis])
    return jnp.pad(a, pad)


def _place(a, axis, segs, total):
    out = []
    pos = 0
    for src, w, dst in segs:
        if dst > pos:
            shp = list(a.shape)
            shp[axis] = dst - pos
            out.append(jnp.zeros(shp, a.dtype))
        out.append(lax.slice_in_dim(a, src, src + w, axis=axis))
        pos = dst + w
    if total > pos:
        shp = list(a.shape)
        shp[axis] = total - pos
        out.append(jnp.zeros(shp, a.dtype))
    return jnp.concatenate(out, axis=axis)


def _in_col_segments():
    g0 = RWKV_COLS
    t0 = RWKV_COLS + GLA_COLS
    segs = [(0, RWKV_COLS, 0)]
    segs += [(g0, GLA_K, RW_P), (g0 + GLA_K, GLA_K, RW_P + GQ_P),
             (g0 + 2 * GLA_K, GLA_V, RW_P + 2 * GQ_P), (g0 + 2 * GLA_K + GLA_V, GLA_V, RW_P + 2 * GQ_P + HW),
             (g0 + 2 * GLA_K + 2 * GLA_V, GLA_LORA, RW_P + 2 * GQ_P + 2 * HW)]
    segs += [(t0 + i * RET_W, RET_W, RW_P + GL_P + i * HW) for i in range(4)]
    return segs


def _prep_weights(w_in, rwkv_mu, rwkv_w0, rwkv_bw, rwkv_a0, rwkv_ba, rwkv_bg, rwkv_kk, rwkv_ka,
                  rwkv_rk, rwkv_lnw, rwkv_lnb, gla_bgk, gla_bgk_b, gla_norm_w, w_out,
                  ln1_g, ln1_b, ln2_g, ln2_b, ffn_up, ffn_conv_w, ffn_conv_b, ffn_down):
    L = w_in.shape[0]
    w1 = _place(_bf(w_in), 2, _in_col_segments(), NP)
    mu = _pad_to(rwkv_mu, 1, RW_P)[:, None, :]
    gnw = _pad_to(jnp.tile(gla_norm_w, (1, N_GLA)), 1, HW)
    vec = jnp.stack([rwkv_w0, rwkv_a0, rwkv_kk, rwkv_ka, rwkv_rk.reshape(L, RWKV_W), rwkv_lnw, rwkv_lnb, gnw], axis=1)
    bw = _bf(_pad_to(rwkv_bw, 1, LANE))
    ba = _bf(_place(rwkv_ba, 1, [(0, LORA_A, LORA_W)], LANE))
    bg = _bf(_pad_to(rwkv_bg, 1, 2 * LANE))
    bgk = _bf(_pad_to(_pad_to(gla_bgk, 1, LANE), 2, GQ_P))
    bgkb = _pad_to(gla_bgk_b, 1, GQ_P)[:, None, :]
    wout = _place(_bf(w_out), 1, [(0, RWKV_W + GLA_V, 0), (RWKV_W + GLA_V, RET_W, 2 * HW)], MIX_P)
    ln1 = jnp.stack([ln1_g, ln1_b], axis=1)
    mixer_w = (w1, mu, vec, bw, ba, bg, bgk, bgkb, wout, ln1)
    cwb = _pad_to(jnp.concatenate([ffn_conv_w, ffn_conv_b[:, None, :]], axis=1), 1, 8)
    ln2 = jnp.stack([ln2_g, ln2_b], axis=1)
    ffn_w = (_bf(ffn_up), cwb, _bf(ffn_down), ln2)
    return mixer_w, ffn_w


def _tables(pos, i_real, c_real, chunk, reps):
    half = HEAD_DIM // 2
    inv = 1.0 / (ROPE_BASE ** jnp.linspace(0.0, 1.0, half, dtype=F32))
    ang = pos.astype(F32)[:, None] * inv[None]
    cos, sin = jnp.cos(ang), jnp.sin(ang)
    cos = jnp.tile(jnp.concatenate([cos, cos], -1), (reps, HW // HEAD_DIM))
    sin = jnp.tile(jnp.concatenate([-sin, sin], -1), (reps, HW // HEAD_DIM))
    log_gamma = jnp.log(1.0 - jnp.exp2(-5.0 - jnp.arange(N_RET, dtype=F32)))
    i = i_real.astype(F32)
    diff = i[:, None] - i[None, :]
    causal = diff >= 0
    dmask = jnp.where(causal, jnp.exp(jnp.where(causal, diff, 0.0) * log_gamma[:, None, None]), 0.0)
    qd = jnp.exp((i + 1.0) * log_gamma[:, None])[..., None]
    kd = jnp.exp((c_real - 1.0 - i) * log_gamma[:, None])[..., None]
    cd = jnp.exp(c_real * log_gamma)[:, None, None]
    qd = jnp.broadcast_to(qd, (N_RET, chunk, HEAD_DIM))
    kd = jnp.broadcast_to(kd, (N_RET, chunk, HEAD_DIM))
    cd = jnp.broadcast_to(cd, (N_RET, SLOTS, HEAD_DIM))
    return cos, sin, dmask, qd, kd, cd


def _run_group(x, s_rw, s_sh, s_gl, s_rt, s_cv, mixer_w, ffn_w, *, sample, bb_mix, tt_mix, bb_ffn, tt_ffn,
               chunk, group):
    B, T, _ = x.shape
    if sample:
        pos = jnp.clip(jnp.arange(SLOTS) - SLOT0, 0, DEC_SEQ - 1) + PAST_LEN
        tabs = _tables(pos, jnp.arange(SLOTS) - SLOT0, float(DEC_SEQ), chunk, bb_mix)
        shift_rows = _place(_pad_to(s_sh, 2, RW_P)[:, :, None, :], 2, [(0, 1, SLOT0)], SLOTS)
        conv_rows = _pad_to(s_cv, 2, SLOTS)
    else:
        tabs = _tables(jnp.arange(T), jnp.arange(chunk), float(chunk), chunk, 1)
        shift_rows = _pad_to(_pad_to(s_sh, 2, RW_P)[:, :, None, :], 2, SLOTS)
        conv_rows = _place(s_cv, 2, [(0, CONV_W - 1, SLOTS - (CONV_W - 1))], SLOTS)
    n_rw, n_sh, n_gl, n_rt, n_cv = [], [], [], [], []
    for l in range(DEPTH):
        x, sh, rw, gl, rt = _mixer_call(l, x, shift_rows, s_rw, s_gl, s_rt, tabs, mixer_w,
                                        bb=bb_mix, tt=tt_mix, chunk=chunk, group=group, sample=sample)
        x, cv = _ffn_call(l, x, conv_rows, ffn_w, bb=bb_ffn, tt=tt_ffn, sample=sample)
        if sample:
            n_sh.append(sh[:, SLOT0 + DEC_SEQ - 1, :RWKV_COLS])
            n_cv.append(cv[:, SLOT0 + DEC_SEQ - (CONV_W - 1):SLOT0 + DEC_SEQ])
        else:
            n_sh.append(sh[:, SLOTS - 1, :RWKV_COLS])
            n_cv.append(cv[:, SLOTS - (CONV_W - 1):])
        n_rw.append(rw)
        n_gl.append(gl)
        n_rt.append(rt)
    return x, (jnp.stack(n_rw), jnp.stack(n_sh), jnp.stack(n_gl), jnp.stack(n_rt), jnp.stack(n_cv))


def kernel(x_prompt, x_sample, state_rwkv, state_shift, state_gla, state_ret, state_conv, w_in, rwkv_mu, rwkv_w0, rwkv_bw, rwkv_a0, rwkv_ba, rwkv_bg, rwkv_kk, rwkv_ka, rwkv_rk, rwkv_lnw, rwkv_lnb, gla_bgk, gla_bgk_b, gla_norm_w, w_out, ln1_g, ln1_b, ln2_g, ln2_b, ffn_up, ffn_conv_w, ffn_conv_b, ffn_down):
    mixer_w, ffn_w = _prep_weights(w_in, rwkv_mu, rwkv_w0, rwkv_bw, rwkv_a0, rwkv_ba, rwkv_bg, rwkv_kk, rwkv_ka,
                                   rwkv_rk, rwkv_lnw, rwkv_lnb, gla_bgk, gla_bgk_b, gla_norm_w, w_out,
                                   ln1_g, ln1_b, ln2_g, ln2_b, ffn_up, ffn_conv_w, ffn_conv_b, ffn_down)
    bp, tp, _ = x_prompt.shape
    bs, ts, _ = x_sample.shape
    assert ts == DEC_SEQ

    def zeros_like_state(s):
        return jnp.zeros((s.shape[0], bp) + s.shape[2:], F32)

    tt = min(256, tp)
    chunk = math.gcd(tp, CHUNK)
    y_p, st_p = _run_group(
        x_prompt, zeros_like_state(state_rwkv), zeros_like_state(state_shift), zeros_like_state(state_gla),
        zeros_like_state(state_ret), zeros_like_state(state_conv), mixer_w, ffn_w,
        sample=False, bb_mix=1, tt_mix=tt, bb_ffn=1, tt_ffn=tt, chunk=chunk, group=tt // chunk)

    raw = dict(w_in=w_in, rwkv_mu=rwkv_mu, rwkv_w0=rwkv_w0, rwkv_bw=rwkv_bw, rwkv_a0=rwkv_a0, rwkv_ba=rwkv_ba,
               rwkv_bg=rwkv_bg, rwkv_kk=rwkv_kk, rwkv_ka=rwkv_ka, rwkv_rk=rwkv_rk, rwkv_lnw=rwkv_lnw,
               rwkv_lnb=rwkv_lnb, gla_bgk=gla_bgk, gla_bgk_b=gla_bgk_b, gla_norm_w=gla_norm_w, w_out=w_out,
               ln1_g=ln1_g, ln1_b=ln1_b)
    y_s, st_s = _run_sample(x_sample, state_rwkv, state_shift, state_gla, state_ret, state_conv, raw, ffn_w)
    return (y_p, y_s) + st_p + st_s
```

```python
import functools
import math

import jax
import jax.numpy as jnp
from jax import lax
from jax.experimental import pallas as pl
from jax.experimental.pallas import tpu as pltpu

F32 = jnp.float32
BF16 = jnp.bfloat16

D_MODEL = 1024
DEPTH = 4
PAST_LEN = 16384
DEC_SEQ = 4
HEAD_DIM = 64
N_HEADS = D_MODEL // HEAD_DIM
N_GLA = (5 * N_HEADS) // 16
N_RET = (5 * N_HEADS) // 16
N_RWKV = N_HEADS - N_GLA - N_RET
RWKV_W = N_RWKV * HEAD_DIM
GLA_DK = HEAD_DIM // 2
GLA_K = N_GLA * GLA_DK
GLA_V = N_GLA * HEAD_DIM
RET_W = N_RET * HEAD_DIM
LORA_W = 64
LORA_A = 64
LORA_G = 160
GLA_LORA = 16
GLA_GATE_NORM = 16.0
CHUNK = 64
D_FF = 2816
CONV_W = 3
ALPHA = (2 * DEPTH) ** 0.25
RWKV_GN_EPS = 64e-5
LN_EPS = 1e-5
RMS_EPS = 1e-6
ROPE_BASE = 10000.0
RWKV_COLS = 3 * RWKV_W + LORA_W + LORA_A + LORA_G
GLA_COLS = 2 * GLA_K + 2 * GLA_V + GLA_LORA
RET_COLS = 4 * RET_W

LANE = 128
HW = 384
RW_P = 1536
GQ_P = 256
GL_P = 2 * GQ_P + 2 * HW + LANE
RT_P = 4 * HW
NP = RW_P + GL_P + RT_P
MIX_P = 3 * HW
SLOTS = 8

VMEM_LIMIT = 56 * 1024 * 1024


_NN = (((1,), (0,)), ((), ()))
_NT = (((1,), (1,)), ((), ()))
_TN = (((0,), (0,)), ((), ()))


def _bf(x):
    return x.astype(BF16)


def _dg(a, b, dims):
    return lax.dot_general(a, b, dims, preferred_element_type=F32)


def _dot(a, b, dims=_NN):
    return _dg(_bf(a), _bf(b), dims)


def _split(x, n):
    parts = []
    r = x
    for i in range(n):
        h = _bf(r)
        parts.append(h)
        if i + 1 < n:
            r = r - h.astype(F32)
    return parts


def _dot_sel(a, sel_bf, n=2, dims=_NN):
    out = None
    for h in _split(a, n):
        t = _dg(h, sel_bf, dims)
        out = t if out is None else out + t
    return out


def _sel_dot(sel_bf, b, n=3, dims=_NN):
    out = None
    for h in _split(b, n):
        t = _dg(sel_bf, h, dims)
        out = t if out is None else out + t
    return out


def _layer_norm(h, g, b):
    mu = jnp.mean(h, -1, keepdims=True)
    d = h - mu
    var = jnp.mean(d * d, -1, keepdims=True)
    return d * lax.rsqrt(var + LN_EPS) * g + b


def _swap_halves(x):
    pieces = []
    for i in range(x.shape[1] // LANE):
        p = x[:, i * LANE:(i + 1) * LANE]
        up = pltpu.roll(p, LANE - HEAD_DIM // 2, axis=1)
        dn = pltpu.roll(p, HEAD_DIM // 2, axis=1)
        lane = lax.broadcasted_iota(jnp.int32, p.shape, 1)
        pieces.append(jnp.where((lane % HEAD_DIM) < HEAD_DIM // 2, up, dn))
    return jnp.concatenate(pieces, axis=1)


def _mixer_kernel(x_ref, shift_ref, srw_ref, sgl_ref, srt_ref, cos_ref, sin_ref,
                  dmask_ref, qd_ref, kd_ref, cd_ref,
                  w1_ref, mu_ref, vec_ref, bw_ref, ba_ref, bg_ref, bgk_ref, bgkb_ref,
                  wout_ref, ln_ref,
                  x1_ref, shift_out_ref, srw_out, sgl_out, srt_out,
                  carry_ref, rw_r, rw_k, rw_v, rw_lw, rw_a, rw_b, rw_g, rw_bonus, rw_y,
                  gl_q, gl_k, gl_v, gl_lg, gl_gate, gl_o,
                  rt_q, rt_k, rt_v, rt_gate, rt_o,
                  *, rows, chunk, group):
    R, C = rows, chunk
    G = group * C
    j = pl.program_id(1)

    @pl.when(j == 0)
    def _():
        srw_out[...] = srw_ref[...]
        sgl_out[...] = sgl_ref[...]
        srt_out[...] = srt_ref[...]

    x = x_ref[0]
    xb = _bf(x)
    row = lax.broadcasted_iota(jnp.int32, (R, 1), 0)
    hi_ = lax.broadcasted_iota(jnp.int32, (2 * LANE, 2 * LANE), 0) // HEAD_DIM
    hj_ = lax.broadcasted_iota(jnp.int32, (2 * LANE, 2 * LANE), 1) // HEAD_DIM
    head_ones = (hi_ == hj_).astype(BF16)

    def head_sum(v):
        return jnp.concatenate([_dg(_bf(v[:, 0:2 * LANE]), head_ones, _NN),
                                _dg(_bf(v[:, 2 * LANE:HW]), head_ones[0:LANE, 0:LANE], _NN)], axis=1)

    p = _dg(xb, w1_ref[:, 0:RW_P], _NN)
    first = jnp.where(j == 0, shift_ref[0, 0:1, :], carry_ref[SLOTS - 1:SLOTS, :])
    prev = jnp.where(row == 0, first, pltpu.roll(p, 1, axis=0))
    carry_ref[...] = p[R - SLOTS:R, :]
    shift_out_ref[0] = p[R - SLOTS:R, :]
    pm = p + (prev - p) * mu_ref[...]
    r = pm[:, 0:HW]
    k = pm[:, HW:2 * HW]
    v = pm[:, 2 * HW:3 * HW]
    wa = pm[:, 3 * HW:3 * HW + LANE]
    xg = pm[:, 3 * HW + LANE:RW_P]
    w0 = vec_ref[0:1, :]
    a0 = vec_ref[1:2, :]
    k_k = vec_ref[2:3, :]
    k_a = vec_ref[3:4, :]
    r_k = vec_ref[4:5, :]
    w_log = -jax.nn.softplus(-(w0 + _dg(_bf(jnp.tanh(wa)), bw_ref[...], _NN))) - 0.5
    log_decay = -jnp.exp(w_log)
    a = jax.nn.sigmoid(a0 + _dg(_bf(wa), ba_ref[...], _NN))
    g = _dg(_bf(jax.nn.sigmoid(xg)), bg_ref[...], _NN)
    kk = k * k_k
    kk = kk / jnp.maximum(jnp.sqrt(head_sum(kk * kk)), 1e-12)
    k2 = k * (1.0 + (a - 1.0) * k_a)
    rw_r[...] = r
    rw_k[...] = k2
    rw_v[...] = v
    rw_lw[...] = log_decay
    rw_a[...] = -kk
    rw_b[...] = kk * a
    rw_g[...] = g
    rw_bonus[...] = head_sum(r * k2 * r_k) * v

    p = _dg(xb, w1_ref[:, RW_P:RW_P + GL_P], _NN)
    gk = p[:, 2 * GQ_P + 2 * HW:GL_P]
    lg = jax.nn.log_sigmoid(_dg(_bf(gk), bgk_ref[...], _NN) + bgkb_ref[...]) / GLA_GATE_NORM
    gl_q[...] = p[:, 0:GQ_P] * GLA_DK ** -0.5
    gl_k[...] = p[:, GQ_P:2 * GQ_P]
    gl_v[...] = p[:, 2 * GQ_P:2 * GQ_P + HW]
    gl_lg[...] = lg
    gl_gate[...] = jax.nn.silu(p[:, 2 * GQ_P + HW:2 * GQ_P + 2 * HW])

    p = _dg(xb, w1_ref[:, RW_P + GL_P:NP], _NN)
    cos = cos_ref[...]
    sin = sin_ref[...]
    q = p[:, 0:HW]
    k = p[:, HW:2 * HW]
    rt_q[...] = q * cos + _swap_halves(q) * sin
    rt_k[...] = (k * cos + _swap_halves(k) * sin) * HEAD_DIM ** -0.5
    rt_v[...] = p[:, 2 * HW:3 * HW]
    rt_gate[...] = jax.nn.silu(p[:, 3 * HW:4 * HW])

    gi = lax.broadcasted_iota(jnp.int32, (G, G), 0)
    gj = lax.broadcasted_iota(jnp.int32, (G, G), 1)
    same_chunk = (gi // C) == (gj // C)
    cum_sel = (same_chunk & (gi >= gj)).astype(BF16)

    def chunk_cumsum(v):
        cum = _sel_dot(cum_sel, v, n=2)
        tot = jnp.concatenate([jnp.broadcast_to(cum[(c + 1) * C - 1:(c + 1) * C, :], (C, v.shape[1]))
                               for c in range(group)], axis=0)
        return cum, tot
    ii = lax.broadcasted_iota(jnp.int32, (C, C), 0)
    jj = lax.broadcasted_iota(jnp.int32, (C, C), 1)
    tri_incl = ii >= jj
    tri_strict = ii > jj
    eye = (ii == jj).astype(F32)
    levels = []
    s = 1
    while s < C:
        levels.append(((ii // (2 * s)) == (jj // (2 * s))) & ((ii % (2 * s)) >= s) & ((jj % (2 * s)) < s))
        s *= 2
    ones_cv = jnp.ones((C, HEAD_DIM), BF16)
    chunks = range(group)

    def group_body(gidx, carry):
        g0 = pl.multiple_of(gidx * G, G)
        gs = pl.ds(g0, G)

        def cr(c):
            return slice(c * C, (c + 1) * C)

        lw = rw_lw[gs, :]
        cum, tot = chunk_cumsum(lw)
        w_inv = jnp.exp(-cum)
        w_rem = jnp.exp(tot - cum)
        r_t = rw_r[gs, :] * jnp.exp(cum)
        a_t = rw_a[gs, :] * jnp.exp(cum - lw)
        b_t = rw_b[gs, :] * w_inv
        k_t = rw_k[gs, :] * w_inv
        b_h = rw_b[gs, :] * w_rem
        k_h = rw_k[gs, :] * w_rem
        w_c = jnp.exp(tot)
        v_rw = rw_v[gs, :]
        items = [(c, h) for c in chunks for h in range(N_RWKV)]

        def hs(h):
            return slice(h * HEAD_DIM, (h + 1) * HEAD_DIM)

        ar = [jnp.concatenate([a_t[cr(c), hs(h)], r_t[cr(c), hs(h)]], axis=0) for c, h in items]
        xb_ = [_dot(ar[i], b_t[cr(c), hs(h)], _NT) for i, (c, h) in enumerate(items)]
        xk_ = [_dot(ar[i], k_t[cr(c), hs(h)], _NT) for i, (c, h) in enumerate(items)]
        a_ab = [jnp.where(tri_strict, t[0:C], 0.0) for t in xb_]
        a_ak = [jnp.where(tri_strict, t[0:C], 0.0) for t in xk_]
        p_rb = [jnp.where(tri_incl, t[C:2 * C], 0.0) for t in xb_]
        p_rk = [jnp.where(tri_incl, t[C:2 * C], 0.0) for t in xk_]

        lg = gl_lg[gs, :]
        gcum, gtot = chunk_cumsum(lg)
        q_in = gl_q[gs, :] * jnp.exp(gcum)
        k_in = gl_k[gs, :] * jnp.exp(-gcum)
        k_st = gl_k[gs, :] * jnp.exp(gtot - gcum)
        v_gl = gl_v[gs, :]
        g_col = []
        for c in chunks:
            parts = _split(lg[cr(c), :], 3)
            g_col.append(jnp.exp(_dg(parts[0], ones_cv, _TN) + _dg(parts[1], ones_cv, _TN)
                                 + _dg(parts[2], ones_cv, _TN)))
        gitems = [(c, h) for c in chunks for h in range(N_GLA)]

        def ks(h):
            return slice(h * GLA_DK, (h + 1) * GLA_DK)

        g_att = [jnp.where(tri_incl, _dot(q_in[cr(c), ks(h)], k_in[cr(c), ks(h)], _NT), 0.0) for c, h in gitems]
        g_kv = [_dot(k_st[cr(c), ks(h)], v_gl[cr(c), hs(h)], _TN) for c, h in gitems]

        q_rt = rt_q[gs, :]
        k_rt = rt_k[gs, :]
        v_rt = rt_v[gs, :]
        ritems = [(c, h) for c in chunks for h in range(N_RET)]
        r_att = [_dot(q_rt[cr(c), hs(h)], k_rt[cr(c), hs(h)], _NT) * dmask_ref[h] for c, h in ritems]
        r_kv = [_dot(k_rt[cr(c), hs(h)] * kd_ref[h], v_rt[cr(c), hs(h)], _TN) for c, h in ritems]

        m = [eye + jnp.where(levels[0], t, 0.0) for t in a_ab]
        for lvl in levels[1:]:
            t_ = [_dot(jnp.where(lvl, a_ab[i], 0.0), m[i]) for i in range(len(items))]
            m = [m[i] + _dot(m[i], t_[i]) for i in range(len(items))]

        g_av = [_dot(g_att[i], v_gl[cr(c), hs(h)]) for i, (c, h) in enumerate(gitems)]
        r_av = [_dot(r_att[i], v_rt[cr(c), hs(h)]) for i, (c, h) in enumerate(ritems)]

        vv = [v_rw[cr(c), hs(h)] for c, h in items]
        akv = [_dot(a_ak[i], vv[i]) for i in range(len(items))]
        at2 = [_dot(m[i], a_t[cr(c), hs(h)]) for i, (c, h) in enumerate(items)]
        y0 = [_dot(p_rk[i], vv[i]) for i in range(len(items))]
        u0 = [_dot(m[i], akv[i]) for i in range(len(items))]
        gp = [_dot(at2[i], b_h[cr(c), hs(h)], _TN) for i, (c, h) in enumerate(items)]
        hh = [_dot(jnp.concatenate([u0[i], vv[i]], axis=0),
                   jnp.concatenate([b_h[cr(c), hs(h)], k_h[cr(c), hs(h)]], axis=0), _TN)
              for i, (c, h) in enumerate(items)]

        s0 = [None] * len(items)
        cur = [srw_out[0, h] for h in range(N_RWKV)]
        for c in chunks:
            for h in range(N_RWKV):
                i = c * N_RWKV + h
                s0[i] = cur[h]
                cur[h] = cur[h] * w_c[c * C:c * C + 1, hs(h)] + _dot(cur[h], gp[i]) + hh[i]
        for h in range(N_RWKV):
            srw_out[0, h] = cur[h]

        gs0 = [None] * len(gitems)
        rs0 = [None] * len(ritems)
        for h in range(N_GLA):
            S = sgl_out[0, h]
            for c in chunks:
                gs0[c * N_GLA + h] = S
                S = S * g_col[c][ks(h), :] + g_kv[c * N_GLA + h]
            sgl_out[0, h] = S
        for h in range(N_RET):
            S = srt_out[0, h]
            for c in chunks:
                rs0[c * N_RET + h] = S
                S = S * cd_ref[h, 0:1, :] + r_kv[c * N_RET + h]
            srt_out[0, h] = S

        us = [_dot(jnp.concatenate([at2[i], r_t[cr(c), hs(h)]], axis=0), s0[i], _NT)
              for i, (c, h) in enumerate(items)]
        g_o = [g_av[i] + _dot(q_in[cr(c), ks(h)], gs0[i]) for i, (c, h) in enumerate(gitems)]
        r_o = [r_av[i] + _dot(q_rt[cr(c), hs(h)] * qd_ref[h], rs0[i]) for i, (c, h) in enumerate(ritems)]
        y = [us[i][C:2 * C] + _dot(p_rb[i], us[i][0:C] + u0[i]) + y0[i] for i in range(len(items))]
        for c in chunks:
            rows_c = pl.ds(g0 + c * C, C)
            rw_y[rows_c, :] = jnp.concatenate(y[c * N_RWKV:(c + 1) * N_RWKV], axis=1)
            gl_o[rows_c, :] = jnp.concatenate(g_o[c * N_GLA:(c + 1) * N_GLA]
                                              + [jnp.zeros((C, HW - GLA_V), F32)], axis=1)
            rt_o[rows_c, :] = jnp.concatenate(r_o[c * N_RET:(c + 1) * N_RET]
                                              + [jnp.zeros((C, HW - RET_W), F32)], axis=1)
        return carry

    if R == G:
        group_body(0, 0)
    else:
        lax.fori_loop(0, R // G, group_body, 0)

    ln_w = vec_ref[5:6, :]
    ln_b = vec_ref[6:7, :]
    gn_w = vec_ref[7:8, :]
    inv_hd = 1.0 / HEAD_DIM
    y = rw_y[...]
    ym = head_sum(y) * inv_hd
    d = y - ym
    yv = head_sum(d * d) * inv_hd
    ya = (d * lax.rsqrt(yv + RWKV_GN_EPS) * ln_w + ln_b + rw_bonus[...]) * rw_g[...]
    o = gl_o[...]
    ob = o * lax.rsqrt(head_sum(o * o) * inv_hd + RMS_EPS) * gn_w * gl_gate[...]
    o = rt_o[...]
    oc = o * lax.rsqrt(head_sum(o * o) * inv_hd + RMS_EPS) * rt_gate[...]
    mix = _dg(_bf(jnp.concatenate([ya, ob, oc], axis=1)), wout_ref[...], _NN)
    out = _layer_norm(ALPHA * x + mix, ln_ref[0:1, :], ln_ref[1:2, :])
    x1_ref[0] = out


def _const_spec(shape):
    nd = len(shape)
    return pl.BlockSpec(shape, lambda i, j: (0,) * nd, pipeline_mode=pl.Buffered(1))


def _layer_spec(l, shape):
    return pl.BlockSpec((None,) + shape, lambda i, j: (l,) + (0,) * len(shape), pipeline_mode=pl.Buffered(1))


def _mixer_call(l, x, shift_rows, s_rw, s_gl, s_rt, tabs, wts, *, tt, chunk):
    B, T, _ = x.shape
    bb, R = 1, tt
    nb, nt = B, T // tt
    group = tt // chunk
    cos, sin, dmask, qd, kd, cd = tabs
    (w1, mu, vec, bw, ba, bg, bgk, bgkb, wout, ln1) = wts
    C = chunk
    lw = functools.partial(_layer_spec, l)

    in_specs = [
        pl.BlockSpec((bb, tt, D_MODEL), lambda i, j: (i, j, 0)),
        pl.BlockSpec((None, bb, SLOTS, RW_P), lambda i, j: (l, i, 0, 0)),
        pl.BlockSpec((None, bb, N_RWKV, HEAD_DIM, HEAD_DIM), lambda i, j: (l, i, 0, 0, 0)),
        pl.BlockSpec((None, bb, N_GLA, GLA_DK, HEAD_DIM), lambda i, j: (l, i, 0, 0, 0)),
        pl.BlockSpec((None, bb, N_RET, HEAD_DIM, HEAD_DIM), lambda i, j: (l, i, 0, 0, 0)),
        pl.BlockSpec((R, HW), lambda i, j: (j, 0)),
        pl.BlockSpec((R, HW), lambda i, j: (j, 0)),
        _const_spec((N_RET, C, C)),
        _const_spec((N_RET, C, HEAD_DIM)),
        _const_spec((N_RET, C, HEAD_DIM)),
        _const_spec((N_RET, SLOTS, HEAD_DIM)),
        lw((D_MODEL, NP)), lw((1, RW_P)), lw((8, HW)), lw((LANE, HW)), lw((LANE, HW)), lw((2 * LANE, HW)),
        lw((LANE, GQ_P)), lw((1, GQ_P)), lw((MIX_P, D_MODEL)), lw((2, D_MODEL)),
    ]
    out_specs = [
        pl.BlockSpec((bb, tt, D_MODEL), lambda i, j: (i, j, 0)),
        pl.BlockSpec((bb, SLOTS, RW_P), lambda i, j: (i, 0, 0)),
        pl.BlockSpec((bb, N_RWKV, HEAD_DIM, HEAD_DIM), lambda i, j: (i, 0, 0, 0)),
        pl.BlockSpec((bb, N_GLA, GLA_DK, HEAD_DIM), lambda i, j: (i, 0, 0, 0)),
        pl.BlockSpec((bb, N_RET, HEAD_DIM, HEAD_DIM), lambda i, j: (i, 0, 0, 0)),
    ]
    out_shape = [
        jax.ShapeDtypeStruct((B, T, D_MODEL), F32),
        jax.ShapeDtypeStruct((B, SLOTS, RW_P), F32),
        jax.ShapeDtypeStruct(s_rw.shape[1:], F32),
        jax.ShapeDtypeStruct(s_gl.shape[1:], F32),
        jax.ShapeDtypeStruct(s_rt.shape[1:], F32),
    ]
    scratch = [pltpu.VMEM((SLOTS, RW_P), F32)]
    scratch += [pltpu.VMEM((R, HW), F32)] * 9
    scratch += [pltpu.VMEM((R, GQ_P), F32), pltpu.VMEM((R, GQ_P), F32), pltpu.VMEM((R, HW), F32),
                pltpu.VMEM((R, GQ_P), F32), pltpu.VMEM((R, HW), F32), pltpu.VMEM((R, HW), F32)]
    scratch += [pltpu.VMEM((R, HW), F32)] * 5
    return pl.pallas_call(
        functools.partial(_mixer_kernel, rows=R, chunk=C, group=group),
        grid=(nb, nt),
        in_specs=in_specs, out_specs=out_specs, out_shape=out_shape,
        scratch_shapes=scratch,
        compiler_params=pltpu.CompilerParams(dimension_semantics=("arbitrary", "arbitrary"),
                                             vmem_limit_bytes=VMEM_LIMIT),
        name="mixer_prompt",
    )(x, shift_rows, s_rw, s_gl, s_rt, cos, sin, dmask, qd, kd, cd,
      w1, mu, vec, bw, ba, bg, bgk, bgkb, wout, ln1)


FFN_CHUNK = 256
FFN_AHEAD = 3


def _ffn_kernel(x_ref, conv_ref, wup_ref, cwb_ref, wdn_ref, ln_ref, y_ref, cv_out_ref, carry_ref, *, rows):
    R = rows
    j = pl.program_id(1)
    x = x_ref[0]
    xb = _bf(x)
    row8 = lax.broadcasted_iota(jnp.int32, (SLOTS, 1), 0)
    n_f = D_FF // FFN_CHUNK

    def cols(f, base):
        return slice(base + f * FFN_CHUNK, base + (f + 1) * FFN_CHUNK)

    def up(f):
        return [_dg(xb, wup_ref[:, cols(f, base)], _NN) for base in (0, D_FF)]

    def conv(u, cs):
        prev = jnp.where(j == 0, conv_ref[0, :, cs], carry_ref[:, cs])
        u1 = pltpu.roll(u, 1, axis=0)
        u2 = pltpu.roll(u, 2, axis=0)
        h1 = jnp.where(row8 == 0, prev[SLOTS - 1:SLOTS], u1[0:SLOTS])
        h2 = jnp.where(row8 == 0, prev[SLOTS - 2:SLOTS - 1],
                       jnp.where(row8 == 1, prev[SLOTS - 1:SLOTS], u2[0:SLOTS]))
        u1 = jnp.concatenate([h1, u1[SLOTS:]], axis=0)
        u2 = jnp.concatenate([h2, u2[SLOTS:]], axis=0)
        carry_ref[:, cs] = u[R - SLOTS:R]
        cv_out_ref[0, :, cs] = u[R - SLOTS:R]
        return cwb_ref[3:4, cs] + (cwb_ref[0:1, cs] * u2 + cwb_ref[1:2, cs] * u1 + cwb_ref[2:3, cs] * u)

    acc = jnp.zeros((R, D_MODEL), F32)
    u_queue = [up(f) for f in range(min(FFN_AHEAD, n_f))]
    for f in range(n_f):
        if f + FFN_AHEAD < n_f:
            u_queue.append(up(f + FFN_AHEAD))
        u_cur = u_queue.pop(0)
        hid = jax.nn.gelu(conv(u_cur[0], cols(f, 0))) * conv(u_cur[1], cols(f, D_FF))
        acc = acc + _dg(_bf(hid), wdn_ref[f * FFN_CHUNK:(f + 1) * FFN_CHUNK, :], _NN)
    y_ref[0] = _layer_norm(ALPHA * x + acc, ln_ref[0:1, :], ln_ref[1:2, :])


def _ffn_call(l, x, conv_rows, wts, *, tt):
    B, T, _ = x.shape
    bb, nb, nt = 1, B, T // tt
    wup, cwb, wdn, ln2 = wts
    lw = functools.partial(_layer_spec, l)
    return pl.pallas_call(
        functools.partial(_ffn_kernel, rows=tt),
        grid=(nb, nt),
        in_specs=[
            pl.BlockSpec((bb, tt, D_MODEL), lambda i, j: (i, j, 0)),
            pl.BlockSpec((None, bb, SLOTS, 2 * D_FF), lambda i, j: (l, i, 0, 0)),
            lw((D_MODEL, 2 * D_FF)), lw((8, 2 * D_FF)), lw((D_FF, D_MODEL)), lw((2, D_MODEL)),
        ],
        out_specs=[
            pl.BlockSpec((bb, tt, D_MODEL), lambda i, j: (i, j, 0)),
            pl.BlockSpec((bb, SLOTS, 2 * D_FF), lambda i, j: (i, 0, 0)),
        ],
        out_shape=[jax.ShapeDtypeStruct((B, T, D_MODEL), F32),
                   jax.ShapeDtypeStruct((B, SLOTS, 2 * D_FF), F32)],
        scratch_shapes=[pltpu.VMEM((SLOTS, 2 * D_FF), F32)],
        compiler_params=pltpu.CompilerParams(dimension_semantics=("arbitrary", "arbitrary"),
                                             vmem_limit_bytes=VMEM_LIMIT),
        name="ffn_prompt",
    )(x, conv_rows, wup, cwb, wdn, ln2)


N_STEPS = N_RWKV + N_GLA + N_RET
G0 = RWKV_COLS
T0 = RWKV_COLS + GLA_COLS


def _smix_kernel(x_ref, shift_ref, srw_ref, sgl_ref, srt_ref, w1t_ref, mu_ref, vec_ref,
                 bwt_ref, bat_ref, bgt_ref, bgkt_ref, gb_ref, gn_ref, rot_ref, gam_ref, wout_ref, ln_ref,
                 x1_ref, shift_out_ref, srw_out, sgl_out, srt_out,
                 rw_r, rw_k, rw_v, rw_w, rw_a, rw_b, rw_g, rw_bonus, rw_y,
                 gl_q, gl_k, gl_v, gl_g, gl_gate, gl_o, rt_q, rt_k, rt_v, rt_gate, rt_o, *, nb):
    s = pl.program_id(0)
    M = DEC_SEQ * nb

    def ts(t):
        return slice(t * nb, (t + 1) * nb)

    def tile_t(c):
        return jnp.concatenate([c] * DEC_SEQ, axis=1)

    def hrows(h):
        return slice(h * HEAD_DIM, (h + 1) * HEAD_DIM)

    @pl.when(s == 0)
    def _():
        xb = _bf(x_ref[...])

        def proj(r0, r1):
            return _dg(w1t_ref[r0:r1, :], xb, _NT)

        p = proj(0, RWKV_COLS)
        shift_out_ref[...] = p[:, (DEC_SEQ - 1) * nb:]
        prev = jnp.concatenate([shift_ref[...], p[:, :(DEC_SEQ - 1) * nb]], axis=1)
        pm = p + (prev - p) * tile_t(mu_ref[...])
        r = pm[0:RWKV_W]
        k = pm[RWKV_W:2 * RWKV_W]
        v = pm[2 * RWKV_W:3 * RWKV_W]
        xw = pm[3 * RWKV_W:3 * RWKV_W + LORA_W]
        xa = pm[3 * RWKV_W + LORA_W:3 * RWKV_W + LORA_W + LORA_A]
        xg = pm[3 * RWKV_W + LORA_W + LORA_A:RWKV_COLS]
        w0, a0, k_k, k_a, r_k = (tile_t(vec_ref[i]) for i in range(5))
        w_log = -jax.nn.softplus(-(w0 + _dg(bwt_ref[...], _bf(jnp.tanh(xw)), _NN))) - 0.5
        a = jax.nn.sigmoid(a0 + _dg(bat_ref[...], _bf(xa), _NN))
        kk = k * k_k
        k2 = k * (1.0 + (a - 1.0) * k_a)
        rk2 = r * k2 * r_k
        for h in range(N_RWKV):
            hs = hrows(h)
            kh = kk[hs]
            kh = kh / jnp.maximum(jnp.sqrt(jnp.sum(kh * kh, axis=0, keepdims=True)), 1e-12)
            rw_a[hs, :] = -kh
            rw_b[hs, :] = kh * a[hs]
            rw_bonus[hs, :] = jnp.sum(rk2[hs], axis=0, keepdims=True) * v[hs]
        rw_r[...] = r
        rw_k[...] = k2
        rw_v[...] = v
        rw_w[...] = jnp.exp(-jnp.exp(w_log))
        rw_g[...] = _dg(bgt_ref[...], _bf(jax.nn.sigmoid(xg)), _NN)

        p = proj(G0, T0)
        gk = p[2 * GLA_K + 2 * GLA_V:GLA_COLS]
        lg = jax.nn.log_sigmoid(_dg(bgkt_ref[...], _bf(gk), _NN) + tile_t(gb_ref[...])) / GLA_GATE_NORM
        gl_q[...] = p[0:GLA_K] * GLA_DK ** -0.5
        gl_k[...] = p[GLA_K:2 * GLA_K]
        gl_v[...] = p[2 * GLA_K:2 * GLA_K + GLA_V]
        gl_g[...] = jnp.exp(lg)
        gl_gate[...] = jax.nn.silu(p[2 * GLA_K + GLA_V:2 * GLA_K + 2 * GLA_V])

        p = proj(T0, T0 + RET_COLS)
        cos = jnp.concatenate([rot_ref[0, t] for t in range(DEC_SEQ)], axis=1)
        sin = jnp.concatenate([rot_ref[1, t] for t in range(DEC_SEQ)], axis=1)
        half = HEAD_DIM // 2

        def rot(xh):
            x1, x2 = xh[0:half], xh[half:HEAD_DIM]
            return jnp.concatenate([x1 * cos - x2 * sin, x1 * sin + x2 * cos], axis=0)

        for h in range(N_RET):
            hs = hrows(h)
            rt_q[hs, :] = rot(p[hs])
            rt_k[hs, :] = rot(p[RET_W + h * HEAD_DIM:RET_W + (h + 1) * HEAD_DIM]) * HEAD_DIM ** -0.5
        rt_v[...] = p[2 * RET_W:3 * RET_W]
        rt_gate[...] = jax.nn.silu(p[3 * RET_W:4 * RET_W])

    @pl.when(s < N_RWKV)
    def _():
        r0 = pl.multiple_of(s * HEAD_DIM, HEAD_DIM)
        hs = pl.ds(r0, HEAD_DIM)

        def v_group(i, carry):
            v0 = pl.multiple_of(i * 8, 8)
            vt = [rw_v[pl.ds(r0 + v0, 8), ts(t)] for t in range(DEC_SEQ)]
            ys = [[] for _ in range(DEC_SEQ)]
            for j in range(8):
                S = srw_ref[v0 + j]
                for t in range(DEC_SEQ):
                    sa = jnp.sum(S * rw_a[hs, ts(t)], axis=0, keepdims=True)
                    S = S * rw_w[hs, ts(t)] + sa * rw_b[hs, ts(t)] + vt[t][j:j + 1, :] * rw_k[hs, ts(t)]
                    ys[t].append(jnp.sum(S * rw_r[hs, ts(t)], axis=0, keepdims=True))
                srw_out[v0 + j] = S
            for t in range(DEC_SEQ):
                rw_y[pl.ds(r0 + v0, 8), ts(t)] = jnp.concatenate(ys[t], axis=0)
            return carry

        lax.fori_loop(0, HEAD_DIM // 8, v_group, 0)

    def kv_head(s_in, s_out, q_ref, k_ref, v_ref, o_ref, decay_rows, k0, v0, nk):
        for t in range(DEC_SEQ):
            src = s_in if t == 0 else s_out
            v_t = v_ref[pl.ds(v0, HEAD_DIM), ts(t)]

            def k_group(i, o, t=t, src=src, v_t=v_t):
                kg = pl.multiple_of(i * 8, 8)
                q8 = q_ref[pl.ds(k0 + kg, 8), ts(t)]
                k8 = k_ref[pl.ds(k0 + kg, 8), ts(t)]
                d8 = decay_rows(kg, t)
                for j in range(8):
                    S = src[kg + j] * d8[j:j + 1, :] + k8[j:j + 1, :] * v_t
                    s_out[kg + j] = S
                    o = o + q8[j:j + 1, :] * S
                return o

            o_ref[pl.ds(v0, HEAD_DIM), ts(t)] = lax.fori_loop(0, nk // 8, k_group, jnp.zeros((HEAD_DIM, nb), F32))

    @pl.when((s >= N_RWKV) & (s < N_RWKV + N_GLA))
    def _():
        h = s - N_RWKV
        k0 = pl.multiple_of(h * GLA_DK, GLA_DK)
        v0 = pl.multiple_of(h * HEAD_DIM, HEAD_DIM)
        kv_head(sgl_ref, sgl_out, gl_q, gl_k, gl_v, gl_o,
                lambda kg, t: gl_g[pl.ds(k0 + kg, 8), ts(t)], k0, v0, GLA_DK)

    @pl.when(s >= N_RWKV + N_GLA)
    def _():
        h = s - (N_RWKV + N_GLA)
        v0 = pl.multiple_of(h * HEAD_DIM, HEAD_DIM)
        gamma = gam_ref[h]
        kv_head(srt_ref, srt_out, rt_q, rt_k, rt_v, rt_o, lambda kg, t: gamma, v0, v0, HEAD_DIM)

    @pl.when(s == N_STEPS - 1)
    def _():
        ln_w, ln_b = tile_t(vec_ref[5]), tile_t(vec_ref[6])
        gn_w = tile_t(gn_ref[...])
        parts = []
        for h in range(N_RWKV):
            hs = hrows(h)
            y = rw_y[hs, :]
            d = y - jnp.mean(y, axis=0, keepdims=True)
            yv = jnp.mean(d * d, axis=0, keepdims=True)
            parts.append((d * lax.rsqrt(yv + RWKV_GN_EPS) * ln_w[hs] + ln_b[hs] + rw_bonus[hs, :]) * rw_g[hs, :])
        for h in range(N_GLA):
            hs = hrows(h)
            o = gl_o[hs, :]
            parts.append(o * lax.rsqrt(jnp.mean(o * o, axis=0, keepdims=True) + RMS_EPS) * gn_w[hs] * gl_gate[hs, :])
        for h in range(N_RET):
            hs = hrows(h)
            o = rt_o[hs, :]
            parts.append(o * lax.rsqrt(jnp.mean(o * o, axis=0, keepdims=True) + RMS_EPS) * rt_gate[hs, :])
        mix_t = _bf(jnp.concatenate(parts, axis=0))
        mix = _dg(mix_t, wout_ref[...], _TN)
        x1_ref[...] = _layer_norm(ALPHA * x_ref[...] + mix, ln_ref[0:1, :], ln_ref[1:2, :])


def _smix_call(l, x, shift_t, s_rw, s_gl, s_rt, wts):
    M = x.shape[0]
    nb = M // DEC_SEQ
    (w1t, mu, vec, bwt, bat, bgt, bgkt, gb, gn, rot, gam, wout, ln1) = wts

    def once(shape, idx=()):
        nd = len(shape)
        return pl.BlockSpec(shape, lambda s: idx + (0,) * (nd - len(idx)), pipeline_mode=pl.Buffered(1))

    def layer(shape):
        return pl.BlockSpec((None,) + shape, lambda s: (l,) + (0,) * len(shape), pipeline_mode=pl.Buffered(1))

    def head_in(shape, first, n):
        return pl.BlockSpec((None, None) + shape,
                            lambda s: (l, jnp.clip(s - first, 0, n - 1)) + (0,) * len(shape))

    def head_out(shape, first, n):
        return pl.BlockSpec((None,) + shape, lambda s: (jnp.clip(s - first, 0, n - 1),) + (0,) * len(shape))

    rw_blk = (HEAD_DIM, HEAD_DIM, nb)
    gl_blk = (GLA_DK, HEAD_DIM, nb)
    in_specs = [
        once((M, D_MODEL)), layer((RWKV_COLS, nb)),
        head_in(rw_blk, 0, N_RWKV), head_in(gl_blk, N_RWKV, N_GLA), head_in(rw_blk, N_RWKV + N_GLA, N_RET),
        layer((RWKV_COLS + GLA_COLS + RET_COLS, D_MODEL)), layer((RWKV_COLS, nb)), layer((7, RWKV_W, nb)),
        layer((RWKV_W, LORA_W)), layer((RWKV_W, LORA_A)), layer((RWKV_W, LORA_G)), layer((GLA_K, GLA_LORA)),
        layer((GLA_K, nb)), layer((GLA_V, nb)), once((2, DEC_SEQ, HEAD_DIM // 2, nb)), once((N_RET, 8, nb)),
        layer((D_MODEL, D_MODEL)), layer((2, D_MODEL)),
    ]
    out_specs = [
        once((M, D_MODEL)), once((RWKV_COLS, nb)),
        head_out(rw_blk, 0, N_RWKV), head_out(gl_blk, N_RWKV, N_GLA), head_out(rw_blk, N_RWKV + N_GLA, N_RET),
    ]
    out_shape = [
        jax.ShapeDtypeStruct((M, D_MODEL), F32), jax.ShapeDtypeStruct((RWKV_COLS, nb), F32),
        jax.ShapeDtypeStruct((N_RWKV,) + rw_blk, F32), jax.ShapeDtypeStruct((N_GLA,) + gl_blk, F32),
        jax.ShapeDtypeStruct((N_RET,) + rw_blk, F32),
    ]
    scratch = ([pltpu.VMEM((RWKV_W, M), F32)] * 9
               + [pltpu.VMEM((GLA_K, M), F32), pltpu.VMEM((GLA_K, M), F32), pltpu.VMEM((GLA_V, M), F32),
                  pltpu.VMEM((GLA_K, M), F32), pltpu.VMEM((GLA_V, M), F32), pltpu.VMEM((GLA_V, M), F32)]
               + [pltpu.VMEM((RET_W, M), F32)] * 5)
    return pl.pallas_call(
        functools.partial(_smix_kernel, nb=nb),
        grid=(N_STEPS,),
        in_specs=in_specs, out_specs=out_specs, out_shape=out_shape, scratch_shapes=scratch,
        compiler_params=pltpu.CompilerParams(dimension_semantics=("arbitrary",), vmem_limit_bytes=VMEM_LIMIT),
        name="mixer_sample",
    )(x, shift_t, s_rw, s_gl, s_rt, w1t, mu, vec, bwt, bat, bgt, bgkt, gb, gn, rot, gam, wout, ln1)


def _sffn_kernel(x_ref, conv_ref, wup_ref, cwb_ref, wdn_ref, ln_ref, y_ref, cv_out_ref, *, nb):
    M = DEC_SEQ * nb
    x = x_ref[...]
    xb = _bf(x)
    n_f = D_FF // FFN_CHUNK

    def cols(f, base):
        return slice(base + f * FFN_CHUNK, base + (f + 1) * FFN_CHUNK)

    def up(f):
        return [_dg(xb, wup_ref[:, cols(f, base)], _NN) for base in (0, D_FF)]

    def conv(u, cs):
        tiles = range(cs.start // LANE, cs.stop // LANE)
        c0 = jnp.concatenate([conv_ref[:, 2 * k * LANE:(2 * k + 1) * LANE] for k in tiles], axis=1)
        c1 = jnp.concatenate([conv_ref[:, (2 * k + 1) * LANE:(2 * k + 2) * LANE] for k in tiles], axis=1)
        u1 = jnp.concatenate([c1, u[0:M - nb]], axis=0)
        u2 = jnp.concatenate([c0, c1, u[0:M - 2 * nb]], axis=0)
        for n, k in enumerate(tiles):
            cv_out_ref[:, 2 * k * LANE:(2 * k + 1) * LANE] = u[M - 2 * nb:M - nb, n * LANE:(n + 1) * LANE]
            cv_out_ref[:, (2 * k + 1) * LANE:(2 * k + 2) * LANE] = u[M - nb:M, n * LANE:(n + 1) * LANE]
        return cwb_ref[3:4, cs] + (cwb_ref[0:1, cs] * u2 + cwb_ref[1:2, cs] * u1 + cwb_ref[2:3, cs] * u)

    acc = jnp.zeros((M, D_MODEL), F32)
    u_queue = [up(f) for f in range(min(FFN_AHEAD, n_f))]
    for f in range(n_f):
        if f + FFN_AHEAD < n_f:
            u_queue.append(up(f + FFN_AHEAD))
        u_cur = u_queue.pop(0)
        hid = jax.nn.gelu(conv(u_cur[0], cols(f, 0))) * conv(u_cur[1], cols(f, D_FF))
        acc = acc + _dg(_bf(hid), wdn_ref[f * FFN_CHUNK:(f + 1) * FFN_CHUNK, :], _NN)
    y_ref[...] = _layer_norm(ALPHA * x + acc, ln_ref[0:1, :], ln_ref[1:2, :])


def _sffn_call(l, x, conv2, wts):
    M = x.shape[0]
    nb = M // DEC_SEQ
    wup, cwb, wdn, ln2 = wts

    def layer(shape):
        return pl.BlockSpec((None,) + shape, lambda i: (l,) + (0,) * len(shape), pipeline_mode=pl.Buffered(1))

    return pl.pallas_call(
        functools.partial(_sffn_kernel, nb=nb),
        grid=(1,),
        in_specs=[pl.BlockSpec((M, D_MODEL), lambda i: (0, 0)), layer((nb, 4 * D_FF)),
                  layer((D_MODEL, 2 * D_FF)), layer((8, 2 * D_FF)), layer((D_FF, D_MODEL)), layer((2, D_MODEL))],
        out_specs=[pl.BlockSpec((M, D_MODEL), lambda i: (0, 0)), pl.BlockSpec((nb, 4 * D_FF), lambda i: (0, 0))],
        out_shape=[jax.ShapeDtypeStruct((M, D_MODEL), F32), jax.ShapeDtypeStruct((nb, 4 * D_FF), F32)],
        compiler_params=pltpu.CompilerParams(dimension_semantics=("arbitrary",), vmem_limit_bytes=VMEM_LIMIT),
        name="ffn_sample",
    )(x, conv2, wup, cwb, wdn, ln2)


def _run_sample(x_sample, s_rw, s_sh, s_gl, s_rt, s_cv, p, ffn_w):
    bs, ts_, _ = x_sample.shape
    L = DEPTH

    def lanes(a):
        return jnp.broadcast_to(a[..., None], a.shape + (bs,))

    half = HEAD_DIM // 2
    inv = 1.0 / (ROPE_BASE ** jnp.linspace(0.0, 1.0, half, dtype=F32))
    ang = (PAST_LEN + jnp.arange(ts_)).astype(F32)[:, None] * inv[None]
    rot = lanes(jnp.stack([jnp.cos(ang), jnp.sin(ang)]))
    log_gamma = jnp.log(1.0 - jnp.exp2(-5.0 - jnp.arange(N_RET, dtype=F32)))
    gam = jnp.broadcast_to(jnp.exp(log_gamma)[:, None, None], (N_RET, 8, bs))
    vec = lanes(jnp.stack([p["rwkv_w0"], p["rwkv_a0"], p["rwkv_kk"], p["rwkv_ka"],
                           p["rwkv_rk"].reshape(L, RWKV_W), p["rwkv_lnw"], p["rwkv_lnb"]], axis=1))
    wts = (_bf(jnp.swapaxes(p["w_in"], 1, 2)), lanes(p["rwkv_mu"]), vec,
           _bf(jnp.swapaxes(p["rwkv_bw"], 1, 2)), _bf(jnp.swapaxes(p["rwkv_ba"], 1, 2)),
           _bf(jnp.swapaxes(p["rwkv_bg"], 1, 2)), _bf(jnp.swapaxes(p["gla_bgk"], 1, 2)),
           lanes(p["gla_bgk_b"]), lanes(jnp.tile(p["gla_norm_w"], (1, N_GLA))), rot, gam,
           _bf(p["w_out"]), jnp.stack([p["ln1_g"], p["ln1_b"]], axis=1))
    rw_t = jnp.transpose(s_rw, (0, 2, 3, 4, 1))
    gl_t = jnp.transpose(s_gl, (0, 2, 3, 4, 1))
    rt_t = jnp.transpose(s_rt, (0, 2, 3, 4, 1))
    sh_t = jnp.swapaxes(s_sh, 1, 2)
    n_tiles = 2 * D_FF // LANE
    cv2 = jnp.transpose(s_cv.reshape(L, bs, CONV_W - 1, n_tiles, LANE), (0, 1, 3, 2, 4)).reshape(L, bs, -1)
    x = jnp.swapaxes(x_sample, 0, 1).reshape(ts_ * bs, D_MODEL)
    n_rw, n_sh, n_gl, n_rt, n_cv = [], [], [], [], []
    for l in range(L):
        x, sh, rw, gl, rt = _smix_call(l, x, sh_t, rw_t, gl_t, rt_t, wts)
        x, cv = _sffn_call(l, x, cv2, ffn_w)
        n_sh.append(sh)
        n_rw.append(rw)
        n_gl.append(gl)
        n_rt.append(rt)
        n_cv.append(cv)
    y = jnp.swapaxes(x.reshape(ts_, bs, D_MODEL), 0, 1)
    back = (0, 4, 1, 2, 3)
    return y, (jnp.transpose(jnp.stack(n_rw), back), jnp.swapaxes(jnp.stack(n_sh), 1, 2),
               jnp.transpose(jnp.stack(n_gl), back), jnp.transpose(jnp.stack(n_rt), back),
               jnp.transpose(jnp.stack(n_cv).reshape(L, bs, n_tiles, CONV_W - 1, LANE),
                             (0, 1, 3, 2, 4)).reshape(L, bs, CONV_W - 1, 2 * D_FF))


def _pad_to(a, axis, n):
    pad = [(0, 0)] * a.ndim
    pad[axis] = (0, n - a.shape[axis])
    return jnp.pad(a, pad)


def _place(a, axis, segs, total):
    out = []
    pos = 0
    for src, w, dst in segs:
        if dst > pos:
            shp = list(a.shape)
            shp[axis] = dst - pos
            out.append(jnp.zeros(shp, a.dtype))
        out.append(lax.slice_in_dim(a, src, src + w, axis=axis))
        pos = dst + w
    if total > pos:
        shp = list(a.shape)
        shp[axis] = total - pos
        out.append(jnp.zeros(shp, a.dtype))
    return jnp.concatenate(out, axis=axis)


def _in_col_segments():
    g0 = RWKV_COLS
    t0 = RWKV_COLS + GLA_COLS
    segs = [(0, RWKV_COLS, 0)]
    segs += [(g0, GLA_K, RW_P), (g0 + GLA_K, GLA_K, RW_P + GQ_P),
             (g0 + 2 * GLA_K, GLA_V, RW_P + 2 * GQ_P), (g0 + 2 * GLA_K + GLA_V, GLA_V, RW_P + 2 * GQ_P + HW),
             (g0 + 2 * GLA_K + 2 * GLA_V, GLA_LORA, RW_P + 2 * GQ_P + 2 * HW)]
    segs += [(t0 + i * RET_W, RET_W, RW_P + GL_P + i * HW) for i in range(4)]
    return segs


def _prep_weights(w_in, rwkv_mu, rwkv_w0, rwkv_bw, rwkv_a0, rwkv_ba, rwkv_bg, rwkv_kk, rwkv_ka,
                  rwkv_rk, rwkv_lnw, rwkv_lnb, gla_bgk, gla_bgk_b, gla_norm_w, w_out,
                  ln1_g, ln1_b, ln2_g, ln2_b, ffn_up, ffn_conv_w, ffn_conv_b, ffn_down):
    L = w_in.shape[0]
    w1 = _place(_bf(w_in), 2, _in_col_segments(), NP)
    mu = _pad_to(rwkv_mu, 1, RW_P)[:, None, :]
    gnw = _pad_to(jnp.tile(gla_norm_w, (1, N_GLA)), 1, HW)
    vec = jnp.stack([rwkv_w0, rwkv_a0, rwkv_kk, rwkv_ka, rwkv_rk.reshape(L, RWKV_W), rwkv_lnw, rwkv_lnb, gnw], axis=1)
    bw = _bf(_pad_to(rwkv_bw, 1, LANE))
    ba = _bf(_place(rwkv_ba, 1, [(0, LORA_A, LORA_W)], LANE))
    bg = _bf(_pad_to(rwkv_bg, 1, 2 * LANE))
    bgk = _bf(_pad_to(_pad_to(gla_bgk, 1, LANE), 2, GQ_P))
    bgkb = _pad_to(gla_bgk_b, 1, GQ_P)[:, None, :]
    wout = _place(_bf(w_out), 1, [(0, RWKV_W + GLA_V, 0), (RWKV_W + GLA_V, RET_W, 2 * HW)], MIX_P)
    ln1 = jnp.stack([ln1_g, ln1_b], axis=1)
    mixer_w = (w1, mu, vec, bw, ba, bg, bgk, bgkb, wout, ln1)
    cwb = _pad_to(jnp.concatenate([ffn_conv_w, ffn_conv_b[:, None, :]], axis=1), 1, 8)
    ln2 = jnp.stack([ln2_g, ln2_b], axis=1)
    ffn_w = (_bf(ffn_up), cwb, _bf(ffn_down), ln2)
    return mixer_w, ffn_w


def _tables(n_pos, chunk):
    half = HEAD_DIM // 2
    inv = 1.0 / (ROPE_BASE ** jnp.linspace(0.0, 1.0, half, dtype=F32))
    ang = jnp.arange(n_pos).astype(F32)[:, None] * inv[None]
    cos, sin = jnp.cos(ang), jnp.sin(ang)
    cos = jnp.tile(jnp.concatenate([cos, cos], -1), (1, HW // HEAD_DIM))
    sin = jnp.tile(jnp.concatenate([-sin, sin], -1), (1, HW // HEAD_DIM))
    log_gamma = jnp.log(1.0 - jnp.exp2(-5.0 - jnp.arange(N_RET, dtype=F32)))
    i = jnp.arange(chunk, dtype=F32)
    diff = i[:, None] - i[None, :]
    causal = diff >= 0
    dmask = jnp.where(causal, jnp.exp(jnp.where(causal, diff, 0.0) * log_gamma[:, None, None]), 0.0)
    qd = jnp.exp((i + 1.0) * log_gamma[:, None])[..., None]
    kd = jnp.exp((chunk - 1.0 - i) * log_gamma[:, None])[..., None]
    cd = jnp.exp(chunk * log_gamma)[:, None, None]
    qd = jnp.broadcast_to(qd, (N_RET, chunk, HEAD_DIM))
    kd = jnp.broadcast_to(kd, (N_RET, chunk, HEAD_DIM))
    cd = jnp.broadcast_to(cd, (N_RET, SLOTS, HEAD_DIM))
    return cos, sin, dmask, qd, kd, cd


def _run_prompt(x, s_rw, s_sh, s_gl, s_rt, s_cv, mixer_w, ffn_w, *, tt, chunk):
    T = x.shape[1]
    tabs = _tables(T, chunk)
    shift_rows = _pad_to(_pad_to(s_sh, 2, RW_P)[:, :, None, :], 2, SLOTS)
    conv_rows = _place(s_cv, 2, [(0, CONV_W - 1, SLOTS - (CONV_W - 1))], SLOTS)
    n_rw, n_sh, n_gl, n_rt, n_cv = [], [], [], [], []
    for l in range(DEPTH):
        x, sh, rw, gl, rt = _mixer_call(l, x, shift_rows, s_rw, s_gl, s_rt, tabs, mixer_w, tt=tt, chunk=chunk)
        x, cv = _ffn_call(l, x, conv_rows, ffn_w, tt=tt)
        n_sh.append(sh[:, SLOTS - 1, :RWKV_COLS])
        n_cv.append(cv[:, SLOTS - (CONV_W - 1):])
        n_rw.append(rw)
        n_gl.append(gl)
        n_rt.append(rt)
    return x, (jnp.stack(n_rw), jnp.stack(n_sh), jnp.stack(n_gl), jnp.stack(n_rt), jnp.stack(n_cv))


def kernel(x_prompt, x_sample, state_rwkv, state_shift, state_gla, state_ret, state_conv, w_in, rwkv_mu, rwkv_w0, rwkv_bw, rwkv_a0, rwkv_ba, rwkv_bg, rwkv_kk, rwkv_ka, rwkv_rk, rwkv_lnw, rwkv_lnb, gla_bgk, gla_bgk_b, gla_norm_w, w_out, ln1_g, ln1_b, ln2_g, ln2_b, ffn_up, ffn_conv_w, ffn_conv_b, ffn_down):
    mixer_w, ffn_w = _prep_weights(w_in, rwkv_mu, rwkv_w0, rwkv_bw, rwkv_a0, rwkv_ba, rwkv_bg, rwkv_kk, rwkv_ka,
                                   rwkv_rk, rwkv_lnw, rwkv_lnb, gla_bgk, gla_bgk_b, gla_norm_w, w_out,
                                   ln1_g, ln1_b, ln2_g, ln2_b, ffn_up, ffn_conv_w, ffn_conv_b, ffn_down)
    bp, tp, _ = x_prompt.shape
    bs, ts, _ = x_sample.shape
    assert ts == DEC_SEQ

    def zeros_like_state(s):
        return jnp.zeros((s.shape[0], bp) + s.shape[2:], F32)

    tt = min(256, tp)
    chunk = math.gcd(tp, CHUNK)
    y_p, st_p = _run_prompt(
        x_prompt, zeros_like_state(state_rwkv), zeros_like_state(state_shift), zeros_like_state(state_gla),
        zeros_like_state(state_ret), zeros_like_state(state_conv), mixer_w, ffn_w, tt=tt, chunk=chunk)

    raw = dict(w_in=w_in, rwkv_mu=rwkv_mu, rwkv_w0=rwkv_w0, rwkv_bw=rwkv_bw, rwkv_a0=rwkv_a0, rwkv_ba=rwkv_ba,
               rwkv_bg=rwkv_bg, rwkv_kk=rwkv_kk, rwkv_ka=rwkv_ka, rwkv_rk=rwkv_rk, rwkv_lnw=rwkv_lnw,
               rwkv_lnb=rwkv_lnb, gla_bgk=gla_bgk, gla_bgk_b=gla_bgk_b, gla_norm_w=gla_norm_w, w_out=w_out,
               ln1_g=ln1_g, ln1_b=ln1_b)
    y_s, st_s = _run_sample(x_sample, state_rwkv, state_shift, state_gla, state_ret, state_conv, raw, ffn_w)
    return (y_p, y_s) + st_p + st_s
```

```python
import functools
import math

import jax
import jax.numpy as jnp
from jax import lax
from jax.experimental import pallas as pl
from jax.experimental.pallas import tpu as pltpu

F32 = jnp.float32
BF16 = jnp.bfloat16

D_MODEL = 1024
DEPTH = 4
PAST_LEN = 16384
DEC_SEQ = 4
HEAD_DIM = 64
N_HEADS = D_MODEL // HEAD_DIM
N_GLA = (5 * N_HEADS) // 16
N_RET = (5 * N_HEADS) // 16
N_RWKV = N_HEADS - N_GLA - N_RET
RWKV_W = N_RWKV * HEAD_DIM
GLA_DK = HEAD_DIM // 2
GLA_K = N_GLA * GLA_DK
GLA_V = N_GLA * HEAD_DIM
RET_W = N_RET * HEAD_DIM
LORA_W = 64
LORA_A = 64
LORA_G = 160
GLA_LORA = 16
GLA_GATE_NORM = 16.0
CHUNK = 64
D_FF = 2816
CONV_W = 3
ALPHA = (2 * DEPTH) ** 0.25
RWKV_GN_EPS = 64e-5
LN_EPS = 1e-5
RMS_EPS = 1e-6
ROPE_BASE = 10000.0
RWKV_COLS = 3 * RWKV_W + LORA_W + LORA_A + LORA_G
GLA_COLS = 2 * GLA_K + 2 * GLA_V + GLA_LORA
RET_COLS = 4 * RET_W

LANE = 128
HW = 384
RW_P = 1536
GQ_P = 256
GL_P = 2 * GQ_P + 2 * HW + LANE
RT_P = 4 * HW
NP = RW_P + GL_P + RT_P
MIX_P = 3 * HW
SLOTS = 8

VMEM_LIMIT = 56 * 1024 * 1024


_NN = (((1,), (0,)), ((), ()))
_NT = (((1,), (1,)), ((), ()))
_TN = (((0,), (0,)), ((), ()))


def _bf(x):
    return x.astype(BF16)


def _dg(a, b, dims):
    return lax.dot_general(a, b, dims, preferred_element_type=F32)


def _dot(a, b, dims=_NN):
    return _dg(_bf(a), _bf(b), dims)


def _split(x, n):
    parts = []
    r = x
    for i in range(n):
        h = _bf(r)
        parts.append(h)
        if i + 1 < n:
            r = r - h.astype(F32)
    return parts


def _dot_sel(a, sel_bf, n=2, dims=_NN):
    out = None
    for h in _split(a, n):
        t = _dg(h, sel_bf, dims)
        out = t if out is None else out + t
    return out


def _sel_dot(sel_bf, b, n=3, dims=_NN):
    out = None
    for h in _split(b, n):
        t = _dg(sel_bf, h, dims)
        out = t if out is None else out + t
    return out


def _layer_norm(h, g, b):
    mu = jnp.mean(h, -1, keepdims=True)
    d = h - mu
    var = jnp.mean(d * d, -1, keepdims=True)
    return d * lax.rsqrt(var + LN_EPS) * g + b


def _swap_halves(x):
    pieces = []
    for i in range(x.shape[1] // LANE):
        p = x[:, i * LANE:(i + 1) * LANE]
        up = pltpu.roll(p, LANE - HEAD_DIM // 2, axis=1)
        dn = pltpu.roll(p, HEAD_DIM // 2, axis=1)
        lane = lax.broadcasted_iota(jnp.int32, p.shape, 1)
        pieces.append(jnp.where((lane % HEAD_DIM) < HEAD_DIM // 2, up, dn))
    return jnp.concatenate(pieces, axis=1)


def _mixer_kernel(x_ref, shift_ref, srw_ref, sgl_ref, srt_ref, cos_ref, sin_ref,
                  dmask_ref, qd_ref, kd_ref, cd_ref,
                  w1_ref, mu_ref, vec_ref, bw_ref, ba_ref, bg_ref, bgk_ref, bgkb_ref,
                  wout_ref, ln_ref,
                  x1_ref, shift_out_ref, srw_out, sgl_out, srt_out,
                  carry_ref, rw_r, rw_k, rw_v, rw_lw, rw_a, rw_b, rw_g, rw_bonus, rw_y,
                  gl_q, gl_k, gl_v, gl_lg, gl_gate, gl_o,
                  rt_q, rt_k, rt_v, rt_gate, rt_o,
                  *, rows, chunk, group):
    R, C = rows, chunk
    G = group * C
    j = pl.program_id(1)

    @pl.when(j == 0)
    def _():
        srw_out[...] = srw_ref[...]
        sgl_out[...] = sgl_ref[...]
        srt_out[...] = srt_ref[...]

    x = x_ref[0]
    xb = _bf(x)
    row = lax.broadcasted_iota(jnp.int32, (R, 1), 0)
    hi_ = lax.broadcasted_iota(jnp.int32, (2 * LANE, 2 * LANE), 0) // HEAD_DIM
    hj_ = lax.broadcasted_iota(jnp.int32, (2 * LANE, 2 * LANE), 1) // HEAD_DIM
    head_ones = (hi_ == hj_).astype(BF16)

    def head_sum(v):
        return jnp.concatenate([_dg(_bf(v[:, 0:2 * LANE]), head_ones, _NN),
                                _dg(_bf(v[:, 2 * LANE:HW]), head_ones[0:LANE, 0:LANE], _NN)], axis=1)

    p = _dg(xb, w1_ref[:, 0:RW_P], _NN)
    first = jnp.where(j == 0, shift_ref[0, 0:1, :], carry_ref[SLOTS - 1:SLOTS, :])
    prev = jnp.where(row == 0, first, pltpu.roll(p, 1, axis=0))
    carry_ref[...] = p[R - SLOTS:R, :]
    shift_out_ref[0] = p[R - SLOTS:R, :]
    pm = p + (prev - p) * mu_ref[...]
    r = pm[:, 0:HW]
    k = pm[:, HW:2 * HW]
    v = pm[:, 2 * HW:3 * HW]
    wa = pm[:, 3 * HW:3 * HW + LANE]
    xg = pm[:, 3 * HW + LANE:RW_P]
    w0 = vec_ref[0:1, :]
    a0 = vec_ref[1:2, :]
    k_k = vec_ref[2:3, :]
    k_a = vec_ref[3:4, :]
    r_k = vec_ref[4:5, :]
    w_log = -jax.nn.softplus(-(w0 + _dg(_bf(jnp.tanh(wa)), bw_ref[...], _NN))) - 0.5
    log_decay = -jnp.exp(w_log)
    a = jax.nn.sigmoid(a0 + _dg(_bf(wa), ba_ref[...], _NN))
    g = _dg(_bf(jax.nn.sigmoid(xg)), bg_ref[...], _NN)
    kk = k * k_k
    kk = kk / jnp.maximum(jnp.sqrt(head_sum(kk * kk)), 1e-12)
    k2 = k * (1.0 + (a - 1.0) * k_a)
    rw_r[...] = r
    rw_k[...] = k2
    rw_v[...] = v
    rw_lw[...] = log_decay
    rw_a[...] = -kk
    rw_b[...] = kk * a
    rw_g[...] = g
    rw_bonus[...] = head_sum(r * k2 * r_k) * v

    p = _dg(xb, w1_ref[:, RW_P:RW_P + GL_P], _NN)
    gk = p[:, 2 * GQ_P + 2 * HW:GL_P]
    lg = jax.nn.log_sigmoid(_dg(_bf(gk), bgk_ref[...], _NN) + bgkb_ref[...]) / GLA_GATE_NORM
    gl_q[...] = p[:, 0:GQ_P] * GLA_DK ** -0.5
    gl_k[...] = p[:, GQ_P:2 * GQ_P]
    gl_v[...] = p[:, 2 * GQ_P:2 * GQ_P + HW]
    gl_lg[...] = lg
    gl_gate[...] = jax.nn.silu(p[:, 2 * GQ_P + HW:2 * GQ_P + 2 * HW])

    p = _dg(xb, w1_ref[:, RW_P + GL_P:NP], _NN)
    cos = cos_ref[...]
    sin = sin_ref[...]
    q = p[:, 0:HW]
    k = p[:, HW:2 * HW]
    rt_q[...] = q * cos + _swap_halves(q) * sin
    rt_k[...] = (k * cos + _swap_halves(k) * sin) * HEAD_DIM ** -0.5
    rt_v[...] = p[:, 2 * HW:3 * HW]
    rt_gate[...] = jax.nn.silu(p[:, 3 * HW:4 * HW])

    gi = lax.broadcasted_iota(jnp.int32, (G, G), 0)
    gj = lax.broadcasted_iota(jnp.int32, (G, G), 1)
    same_chunk = (gi // C) == (gj // C)
    cum_sel = (same_chunk & (gi >= gj)).astype(BF16)

    def chunk_cumsum(v):
        cum = _sel_dot(cum_sel, v, n=2)
        tot = jnp.concatenate([jnp.broadcast_to(cum[(c + 1) * C - 1:(c + 1) * C, :], (C, v.shape[1]))
                               for c in range(group)], axis=0)
        return cum, tot
    ii = lax.broadcasted_iota(jnp.int32, (C, C), 0)
    jj = lax.broadcasted_iota(jnp.int32, (C, C), 1)
    tri_incl = ii >= jj
    tri_strict = ii > jj
    eye = (ii == jj).astype(F32)
    levels = []
    s = 1
    while s < C:
        levels.append(((ii // (2 * s)) == (jj // (2 * s))) & ((ii % (2 * s)) >= s) & ((jj % (2 * s)) < s))
        s *= 2
    ones_cv = jnp.ones((C, HEAD_DIM), BF16)
    chunks = range(group)

    def group_body(gidx, carry):
        g0 = pl.multiple_of(gidx * G, G)
        gs = pl.ds(g0, G)

        def cr(c):
            return slice(c * C, (c + 1) * C)

        lw = rw_lw[gs, :]
        cum, tot = chunk_cumsum(lw)
        w_inv = jnp.exp(-cum)
        w_rem = jnp.exp(tot - cum)
        r_t = rw_r[gs, :] * jnp.exp(cum)
        a_t = rw_a[gs, :] * jnp.exp(cum - lw)
        b_t = rw_b[gs, :] * w_inv
        k_t = rw_k[gs, :] * w_inv
        b_h = rw_b[gs, :] * w_rem
        k_h = rw_k[gs, :] * w_rem
        w_c = jnp.exp(tot)
        v_rw = rw_v[gs, :]
        items = [(c, h) for c in chunks for h in range(N_RWKV)]

        def hs(h):
            return slice(h * HEAD_DIM, (h + 1) * HEAD_DIM)

        ar = [jnp.concatenate([a_t[cr(c), hs(h)], r_t[cr(c), hs(h)]], axis=0) for c, h in items]
        xb_ = [_dot(ar[i], b_t[cr(c), hs(h)], _NT) for i, (c, h) in enumerate(items)]
        xk_ = [_dot(ar[i], k_t[cr(c), hs(h)], _NT) for i, (c, h) in enumerate(items)]
        a_ab = [jnp.where(tri_strict, t[0:C], 0.0) for t in xb_]
        a_ak = [jnp.where(tri_strict, t[0:C], 0.0) for t in xk_]
        p_rb = [jnp.where(tri_incl, t[C:2 * C], 0.0) for t in xb_]
        p_rk = [jnp.where(tri_incl, t[C:2 * C], 0.0) for t in xk_]

        lg = gl_lg[gs, :]
        gcum, gtot = chunk_cumsum(lg)
        q_in = gl_q[gs, :] * jnp.exp(gcum)
        k_in = gl_k[gs, :] * jnp.exp(-gcum)
        k_st = gl_k[gs, :] * jnp.exp(gtot - gcum)
        v_gl = gl_v[gs, :]
        g_col = []
        for c in chunks:
            parts = _split(lg[cr(c), :], 3)
            g_col.append(jnp.exp(_dg(parts[0], ones_cv, _TN) + _dg(parts[1], ones_cv, _TN)
                                 + _dg(parts[2], ones_cv, _TN)))
        gitems = [(c, h) for c in chunks for h in range(N_GLA)]

        def ks(h):
            return slice(h * GLA_DK, (h + 1) * GLA_DK)

        g_att = [jnp.where(tri_incl, _dot(q_in[cr(c), ks(h)], k_in[cr(c), ks(h)], _NT), 0.0) for c, h in gitems]
        g_kv = [_dot(k_st[cr(c), ks(h)], v_gl[cr(c), hs(h)], _TN) for c, h in gitems]

        q_rt = rt_q[gs, :]
        k_rt = rt_k[gs, :]
        v_rt = rt_v[gs, :]
        ritems = [(c, h) for c in chunks for h in range(N_RET)]
        r_att = [_dot(q_rt[cr(c), hs(h)], k_rt[cr(c), hs(h)], _NT) * dmask_ref[h] for c, h in ritems]
        r_kv = [_dot(k_rt[cr(c), hs(h)] * kd_ref[h], v_rt[cr(c), hs(h)], _TN) for c, h in ritems]

        m = [eye + jnp.where(levels[0], t, 0.0) for t in a_ab]
        for lvl in levels[1:]:
            t_ = [_dot(jnp.where(lvl, a_ab[i], 0.0), m[i]) for i in range(len(items))]
            m = [m[i] + _dot(m[i], t_[i]) for i in range(len(items))]

        g_av = [_dot(g_att[i], v_gl[cr(c), hs(h)]) for i, (c, h) in enumerate(gitems)]
        r_av = [_dot(r_att[i], v_rt[cr(c), hs(h)]) for i, (c, h) in enumerate(ritems)]

        vv = [v_rw[cr(c), hs(h)] for c, h in items]
        akv = [_dot(a_ak[i], vv[i]) for i in range(len(items))]
        at2 = [_dot(m[i], a_t[cr(c), hs(h)]) for i, (c, h) in enumerate(items)]
        y0 = [_dot(p_rk[i], vv[i]) for i in range(len(items))]
        u0 = [_dot(m[i], akv[i]) for i in range(len(items))]
        gp = [_dot(at2[i], b_h[cr(c), hs(h)], _TN) for i, (c, h) in enumerate(items)]
        hh = [_dot(jnp.concatenate([u0[i], vv[i]], axis=0),
                   jnp.concatenate([b_h[cr(c), hs(h)], k_h[cr(c), hs(h)]], axis=0), _TN)
              for i, (c, h) in enumerate(items)]

        s0 = [None] * len(items)
        cur = [srw_out[0, h] for h in range(N_RWKV)]
        for c in chunks:
            for h in range(N_RWKV):
                i = c * N_RWKV + h
                s0[i] = cur[h]
                cur[h] = cur[h] * w_c[c * C:c * C + 1, hs(h)] + _dot(cur[h], gp[i]) + hh[i]
        for h in range(N_RWKV):
            srw_out[0, h] = cur[h]

        gs0 = [None] * len(gitems)
        rs0 = [None] * len(ritems)
        for h in range(N_GLA):
            S = sgl_out[0, h]
            for c in chunks:
                gs0[c * N_GLA + h] = S
                S = S * g_col[c][ks(h), :] + g_kv[c * N_GLA + h]
            sgl_out[0, h] = S
        for h in range(N_RET):
            S = srt_out[0, h]
            for c in chunks:
                rs0[c * N_RET + h] = S
                S = S * cd_ref[h, 0:1, :] + r_kv[c * N_RET + h]
            srt_out[0, h] = S

        us = [_dot(jnp.concatenate([at2[i], r_t[cr(c), hs(h)]], axis=0), s0[i], _NT)
              for i, (c, h) in enumerate(items)]
        g_o = [g_av[i] + _dot(q_in[cr(c), ks(h)], gs0[i]) for i, (c, h) in enumerate(gitems)]
        r_o = [r_av[i] + _dot(q_rt[cr(c), hs(h)] * qd_ref[h], rs0[i]) for i, (c, h) in enumerate(ritems)]
        y = [us[i][C:2 * C] + _dot(p_rb[i], us[i][0:C] + u0[i]) + y0[i] for i in range(len(items))]
        for c in chunks:
            rows_c = pl.ds(g0 + c * C, C)
            rw_y[rows_c, :] = jnp.concatenate(y[c * N_RWKV:(c + 1) * N_RWKV], axis=1)
            gl_o[rows_c, :] = jnp.concatenate(g_o[c * N_GLA:(c + 1) * N_GLA]
                                              + [jnp.zeros((C, HW - GLA_V), F32)], axis=1)
            rt_o[rows_c, :] = jnp.concatenate(r_o[c * N_RET:(c + 1) * N_RET]
                                              + [jnp.zeros((C, HW - RET_W), F32)], axis=1)
        return carry

    if R == G:
        group_body(0, 0)
    else:
        lax.fori_loop(0, R // G, group_body, 0)

    ln_w = vec_ref[5:6, :]
    ln_b = vec_ref[6:7, :]
    gn_w = vec_ref[7:8, :]
    inv_hd = 1.0 / HEAD_DIM
    y = rw_y[...]
    ym = head_sum(y) * inv_hd
    d = y - ym
    yv = head_sum(d * d) * inv_hd
    ya = (d * lax.rsqrt(yv + RWKV_GN_EPS) * ln_w + ln_b + rw_bonus[...]) * rw_g[...]
    o = gl_o[...]
    ob = o * lax.rsqrt(head_sum(o * o) * inv_hd + RMS_EPS) * gn_w * gl_gate[...]
    o = rt_o[...]
    oc = o * lax.rsqrt(head_sum(o * o) * inv_hd + RMS_EPS) * rt_gate[...]
    mix = _dg(_bf(jnp.concatenate([ya, ob, oc], axis=1)), wout_ref[...], _NN)
    out = _layer_norm(ALPHA * x + mix, ln_ref[0:1, :], ln_ref[1:2, :])
    x1_ref[0] = out


def _const_spec(shape):
    nd = len(shape)
    return pl.BlockSpec(shape, lambda i, j: (0,) * nd, pipeline_mode=pl.Buffered(1))


def _layer_spec(l, shape):
    return pl.BlockSpec((None,) + shape, lambda i, j: (l,) + (0,) * len(shape), pipeline_mode=pl.Buffered(1))


def _mixer_call(l, x, shift_rows, s_rw, s_gl, s_rt, tabs, wts, *, tt, chunk):
    B, T, _ = x.shape
    bb, R = 1, tt
    nb, nt = B, T // tt
    group = tt // chunk
    cos, sin, dmask, qd, kd, cd = tabs
    (w1, mu, vec, bw, ba, bg, bgk, bgkb, wout, ln1) = wts
    C = chunk
    lw = functools.partial(_layer_spec, l)

    in_specs = [
        pl.BlockSpec((bb, tt, D_MODEL), lambda i, j: (i, j, 0)),
        pl.BlockSpec((None, bb, SLOTS, RW_P), lambda i, j: (l, i, 0, 0)),
        pl.BlockSpec((None, bb, N_RWKV, HEAD_DIM, HEAD_DIM), lambda i, j: (l, i, 0, 0, 0)),
        pl.BlockSpec((None, bb, N_GLA, GLA_DK, HEAD_DIM), lambda i, j: (l, i, 0, 0, 0)),
        pl.BlockSpec((None, bb, N_RET, HEAD_DIM, HEAD_DIM), lambda i, j: (l, i, 0, 0, 0)),
        pl.BlockSpec((R, HW), lambda i, j: (j, 0)),
        pl.BlockSpec((R, HW), lambda i, j: (j, 0)),
        _const_spec((N_RET, C, C)),
        _const_spec((N_RET, C, HEAD_DIM)),
        _const_spec((N_RET, C, HEAD_DIM)),
        _const_spec((N_RET, SLOTS, HEAD_DIM)),
        lw((D_MODEL, NP)), lw((1, RW_P)), lw((8, HW)), lw((LANE, HW)), lw((LANE, HW)), lw((2 * LANE, HW)),
        lw((LANE, GQ_P)), lw((1, GQ_P)), lw((MIX_P, D_MODEL)), lw((2, D_MODEL)),
    ]
    out_specs = [
        pl.BlockSpec((bb, tt, D_MODEL), lambda i, j: (i, j, 0)),
        pl.BlockSpec((bb, SLOTS, RW_P), lambda i, j: (i, 0, 0)),
        pl.BlockSpec((bb, N_RWKV, HEAD_DIM, HEAD_DIM), lambda i, j: (i, 0, 0, 0)),
        pl.BlockSpec((bb, N_GLA, GLA_DK, HEAD_DIM), lambda i, j: (i, 0, 0, 0)),
        pl.BlockSpec((bb, N_RET, HEAD_DIM, HEAD_DIM), lambda i, j: (i, 0, 0, 0)),
    ]
    out_shape = [
        jax.ShapeDtypeStruct((B, T, D_MODEL), F32),
        jax.ShapeDtypeStruct((B, SLOTS, RW_P), F32),
        jax.ShapeDtypeStruct(s_rw.shape[1:], F32),
        jax.ShapeDtypeStruct(s_gl.shape[1:], F32),
        jax.ShapeDtypeStruct(s_rt.shape[1:], F32),
    ]
    scratch = [pltpu.VMEM((SLOTS, RW_P), F32)]
    scratch += [pltpu.VMEM((R, HW), F32)] * 9
    scratch += [pltpu.VMEM((R, GQ_P), F32), pltpu.VMEM((R, GQ_P), F32), pltpu.VMEM((R, HW), F32),
                pltpu.VMEM((R, GQ_P), F32), pltpu.VMEM((R, HW), F32), pltpu.VMEM((R, HW), F32)]
    scratch += [pltpu.VMEM((R, HW), F32)] * 5
    return pl.pallas_call(
        functools.partial(_mixer_kernel, rows=R, chunk=C, group=group),
        grid=(nb, nt),
        in_specs=in_specs, out_specs=out_specs, out_shape=out_shape,
        scratch_shapes=scratch,
        compiler_params=pltpu.CompilerParams(dimension_semantics=("arbitrary", "arbitrary"),
                                             vmem_limit_bytes=VMEM_LIMIT),
        name="mixer_prompt",
    )(x, shift_rows, s_rw, s_gl, s_rt, cos, sin, dmask, qd, kd, cd,
      w1, mu, vec, bw, ba, bg, bgk, bgkb, wout, ln1)


FFN_CHUNK = 256
FFN_AHEAD = 3


def _ffn_kernel(x_ref, conv_ref, wup_ref, cwb_ref, wdn_ref, ln_ref, y_ref, cv_out_ref, carry_ref, *, rows):
    R = rows
    j = pl.program_id(1)
    x = x_ref[0]
    xb = _bf(x)
    row8 = lax.broadcasted_iota(jnp.int32, (SLOTS, 1), 0)
    n_f = D_FF // FFN_CHUNK

    def cols(f, base):
        return slice(base + f * FFN_CHUNK, base + (f + 1) * FFN_CHUNK)

    def up(f):
        return [_dg(xb, wup_ref[:, cols(f, base)], _NN) for base in (0, D_FF)]

    def conv(u, cs):
        prev = jnp.where(j == 0, conv_ref[0, :, cs], carry_ref[:, cs])
        u1 = pltpu.roll(u, 1, axis=0)
        u2 = pltpu.roll(u, 2, axis=0)
        h1 = jnp.where(row8 == 0, prev[SLOTS - 1:SLOTS], u1[0:SLOTS])
        h2 = jnp.where(row8 == 0, prev[SLOTS - 2:SLOTS - 1],
                       jnp.where(row8 == 1, prev[SLOTS - 1:SLOTS], u2[0:SLOTS]))
        u1 = jnp.concatenate([h1, u1[SLOTS:]], axis=0)
        u2 = jnp.concatenate([h2, u2[SLOTS:]], axis=0)
        carry_ref[:, cs] = u[R - SLOTS:R]
        cv_out_ref[0, :, cs] = u[R - SLOTS:R]
        return cwb_ref[3:4, cs] + (cwb_ref[0:1, cs] * u2 + cwb_ref[1:2, cs] * u1 + cwb_ref[2:3, cs] * u)

    acc = jnp.zeros((R, D_MODEL), F32)
    u_queue = [up(f) for f in range(min(FFN_AHEAD, n_f))]
    for f in range(n_f):
        if f + FFN_AHEAD < n_f:
            u_queue.append(up(f + FFN_AHEAD))
        u_cur = u_queue.pop(0)
        hid = jax.nn.gelu(conv(u_cur[0], cols(f, 0))) * conv(u_cur[1], cols(f, D_FF))
        acc = acc + _dg(_bf(hid), wdn_ref[f * FFN_CHUNK:(f + 1) * FFN_CHUNK, :], _NN)
    y_ref[0] = _layer_norm(ALPHA * x + acc, ln_ref[0:1, :], ln_ref[1:2, :])


def _ffn_call(l, x, conv_rows, wts, *, tt):
    B, T, _ = x.shape
    bb, nb, nt = 1, B, T // tt
    wup, cwb, wdn, ln2 = wts
    lw = functools.partial(_layer_spec, l)
    return pl.pallas_call(
        functools.partial(_ffn_kernel, rows=tt),
        grid=(nb, nt),
        in_specs=[
            pl.BlockSpec((bb, tt, D_MODEL), lambda i, j: (i, j, 0)),
            pl.BlockSpec((None, bb, SLOTS, 2 * D_FF), lambda i, j: (l, i, 0, 0)),
            lw((D_MODEL, 2 * D_FF)), lw((8, 2 * D_FF)), lw((D_FF, D_MODEL)), lw((2, D_MODEL)),
        ],
        out_specs=[
            pl.BlockSpec((bb, tt, D_MODEL), lambda i, j: (i, j, 0)),
            pl.BlockSpec((bb, SLOTS, 2 * D_FF), lambda i, j: (i, 0, 0)),
        ],
        out_shape=[jax.ShapeDtypeStruct((B, T, D_MODEL), F32),
                   jax.ShapeDtypeStruct((B, SLOTS, 2 * D_FF), F32)],
        scratch_shapes=[pltpu.VMEM((SLOTS, 2 * D_FF), F32)],
        compiler_params=pltpu.CompilerParams(dimension_semantics=("arbitrary", "arbitrary"),
                                             vmem_limit_bytes=VMEM_LIMIT),
        name="ffn_prompt",
    )(x, conv_rows, wup, cwb, wdn, ln2)


N_STEPS = N_RWKV + N_GLA + N_RET
G0 = RWKV_COLS
T0 = RWKV_COLS + GLA_COLS


def _smix_kernel(x_ref, shift_ref, srw_ref, sgl_ref, srt_ref, w1t_ref, mu_ref, vec_ref,
                 bwt_ref, bat_ref, bgt_ref, bgkt_ref, gb_ref, gn_ref, rot_ref, gam_ref, wout_ref, ln_ref,
                 x1_ref, shift_out_ref, srw_out, sgl_out, srt_out,
                 rw_r, rw_k, rw_v, rw_w, rw_a, rw_b, rw_g, rw_bonus, rw_y,
                 gl_q, gl_k, gl_v, gl_g, gl_gate, gl_o, rt_q, rt_k, rt_v, rt_gate, rt_o, *, nb):
    s = pl.program_id(0)
    M = DEC_SEQ * nb

    def ts(t):
        return slice(t * nb, (t + 1) * nb)

    def tile_t(c):
        return jnp.concatenate([c] * DEC_SEQ, axis=1)

    def hrows(h):
        return slice(h * HEAD_DIM, (h + 1) * HEAD_DIM)

    @pl.when(s == 0)
    def _():
        xb = _bf(x_ref[...])

        def proj(r0, r1):
            return _dg(w1t_ref[r0:r1, :], xb, _NT)

        p = proj(0, RWKV_COLS)
        shift_out_ref[...] = p[:, (DEC_SEQ - 1) * nb:]
        prev = jnp.concatenate([shift_ref[...], p[:, :(DEC_SEQ - 1) * nb]], axis=1)
        pm = p + (prev - p) * tile_t(mu_ref[...])
        r = pm[0:RWKV_W]
        k = pm[RWKV_W:2 * RWKV_W]
        v = pm[2 * RWKV_W:3 * RWKV_W]
        xw = pm[3 * RWKV_W:3 * RWKV_W + LORA_W]
        xa = pm[3 * RWKV_W + LORA_W:3 * RWKV_W + LORA_W + LORA_A]
        xg = pm[3 * RWKV_W + LORA_W + LORA_A:RWKV_COLS]
        w0, a0, k_k, k_a, r_k = (tile_t(vec_ref[i]) for i in range(5))
        w_log = -jax.nn.softplus(-(w0 + _dg(bwt_ref[...], _bf(jnp.tanh(xw)), _NN))) - 0.5
        a = jax.nn.sigmoid(a0 + _dg(bat_ref[...], _bf(xa), _NN))
        kk = k * k_k
        k2 = k * (1.0 + (a - 1.0) * k_a)
        rk2 = r * k2 * r_k
        for h in range(N_RWKV):
            hs = hrows(h)
            kh = kk[hs]
            kh = kh / jnp.maximum(jnp.sqrt(jnp.sum(kh * kh, axis=0, keepdims=True)), 1e-12)
            rw_a[hs, :] = -kh
            rw_b[hs, :] = kh * a[hs]
            rw_bonus[hs, :] = jnp.sum(rk2[hs], axis=0, keepdims=True) * v[hs]
        rw_r[...] = r
        rw_k[...] = k2
        rw_v[...] = v
        rw_w[...] = jnp.exp(-jnp.exp(w_log))
        rw_g[...] = _dg(bgt_ref[...], _bf(jax.nn.sigmoid(xg)), _NN)

        p = proj(G0, T0)
        gk = p[2 * GLA_K + 2 * GLA_V:GLA_COLS]
        lg = jax.nn.log_sigmoid(_dg(bgkt_ref[...], _bf(gk), _NN) + tile_t(gb_ref[...])) / GLA_GATE_NORM
        gl_q[...] = p[0:GLA_K] * GLA_DK ** -0.5
        gl_k[...] = p[GLA_K:2 * GLA_K]
        gl_v[...] = p[2 * GLA_K:2 * GLA_K + GLA_V]
        gl_g[...] = jnp.exp(lg)
        gl_gate[...] = jax.nn.silu(p[2 * GLA_K + GLA_V:2 * GLA_K + 2 * GLA_V])

        p = proj(T0, T0 + RET_COLS)
        cos = jnp.concatenate([rot_ref[0, t] for t in range(DEC_SEQ)], axis=1)
        sin = jnp.concatenate([rot_ref[1, t] for t in range(DEC_SEQ)], axis=1)
        half = HEAD_DIM // 2

        def rot(xh):
            x1, x2 = xh[0:half], xh[half:HEAD_DIM]
            return jnp.concatenate([x1 * cos - x2 * sin, x1 * sin + x2 * cos], axis=0)

        for h in range(N_RET):
            hs = hrows(h)
            rt_q[hs, :] = rot(p[hs])
            rt_k[hs, :] = rot(p[RET_W + h * HEAD_DIM:RET_W + (h + 1) * HEAD_DIM]) * HEAD_DIM ** -0.5
        rt_v[...] = p[2 * RET_W:3 * RET_W]
        rt_gate[...] = jax.nn.silu(p[3 * RET_W:4 * RET_W])

    @pl.when(s < N_RWKV)
    def _():
        r0 = pl.multiple_of(s * HEAD_DIM, HEAD_DIM)
        hs = pl.ds(r0, HEAD_DIM)

        def v_group(i, carry):
            v0 = pl.multiple_of(i * 8, 8)
            vt = [rw_v[pl.ds(r0 + v0, 8), ts(t)] for t in range(DEC_SEQ)]
            ys = [[] for _ in range(DEC_SEQ)]
            for j in range(8):
                S = srw_ref[v0 + j]
                for t in range(DEC_SEQ):
                    sa = jnp.sum(S * rw_a[hs, ts(t)], axis=0, keepdims=True)
                    S = S * rw_w[hs, ts(t)] + sa * rw_b[hs, ts(t)] + vt[t][j:j + 1, :] * rw_k[hs, ts(t)]
                    ys[t].append(jnp.sum(S * rw_r[hs, ts(t)], axis=0, keepdims=True))
                srw_out[v0 + j] = S
            for t in range(DEC_SEQ):
                rw_y[pl.ds(r0 + v0, 8), ts(t)] = jnp.concatenate(ys[t], axis=0)
            return carry

        lax.fori_loop(0, HEAD_DIM // 8, v_group, 0)

    def kv_head(s_in, s_out, q_ref, k_ref, v_ref, o_ref, decay_rows, k0, v0, nk):
        for t in range(DEC_SEQ):
            src = s_in if t == 0 else s_out
            v_t = v_ref[pl.ds(v0, HEAD_DIM), ts(t)]

            def k_group(i, o, t=t, src=src, v_t=v_t):
                kg = pl.multiple_of(i * 8, 8)
                q8 = q_ref[pl.ds(k0 + kg, 8), ts(t)]
                k8 = k_ref[pl.ds(k0 + kg, 8), ts(t)]
                d8 = decay_rows(kg, t)
                for j in range(8):
                    S = src[kg + j] * d8[j:j + 1, :] + k8[j:j + 1, :] * v_t
                    s_out[kg + j] = S
                    o = o + q8[j:j + 1, :] * S
                return o

            o_ref[pl.ds(v0, HEAD_DIM), ts(t)] = lax.fori_loop(0, nk // 8, k_group, jnp.zeros((HEAD_DIM, nb), F32))

    @pl.when((s >= N_RWKV) & (s < N_RWKV + N_GLA))
    def _():
        h = s - N_RWKV
        k0 = pl.multiple_of(h * GLA_DK, GLA_DK)
        v0 = pl.multiple_of(h * HEAD_DIM, HEAD_DIM)
        kv_head(sgl_ref, sgl_out, gl_q, gl_k, gl_v, gl_o,
                lambda kg, t: gl_g[pl.ds(k0 + kg, 8), ts(t)], k0, v0, GLA_DK)

    @pl.when(s >= N_RWKV + N_GLA)
    def _():
        h = s - (N_RWKV + N_GLA)
        v0 = pl.multiple_of(h * HEAD_DIM, HEAD_DIM)
        gamma = gam_ref[h]
        kv_head(srt_ref, srt_out, rt_q, rt_k, rt_v, rt_o, lambda kg, t: gamma, v0, v0, HEAD_DIM)

    @pl.when(s == N_STEPS - 1)
    def _():
        ln_w, ln_b = tile_t(vec_ref[5]), tile_t(vec_ref[6])
        gn_w = tile_t(gn_ref[...])
        parts = []
        for h in range(N_RWKV):
            hs = hrows(h)
            y = rw_y[hs, :]
            d = y - jnp.mean(y, axis=0, keepdims=True)
            yv = jnp.mean(d * d, axis=0, keepdims=True)
            parts.append((d * lax.rsqrt(yv + RWKV_GN_EPS) * ln_w[hs] + ln_b[hs] + rw_bonus[hs, :]) * rw_g[hs, :])
        for h in range(N_GLA):
            hs = hrows(h)
            o = gl_o[hs, :]
            parts.append(o * lax.rsqrt(jnp.mean(o * o, axis=0, keepdims=True) + RMS_EPS) * gn_w[hs] * gl_gate[hs, :])
        for h in range(N_RET):
            hs = hrows(h)
            o = rt_o[hs, :]
            parts.append(o * lax.rsqrt(jnp.mean(o * o, axis=0, keepdims=True) + RMS_EPS) * rt_gate[hs, :])
        mix_t = _bf(jnp.concatenate(parts, axis=0))
        mix = _dg(mix_t, wout_ref[...], _TN)
        x1_ref[...] = _layer_norm(ALPHA * x_ref[...] + mix, ln_ref[0:1, :], ln_ref[1:2, :])


def _smix_call(l, x, shift_t, s_rw, s_gl, s_rt, wts):
    M = x.shape[0]
    nb = M // DEC_SEQ
    (w1t, mu, vec, bwt, bat, bgt, bgkt, gb, gn, rot, gam, wout, ln1) = wts

    def once(shape, idx=()):
        nd = len(shape)
        return pl.BlockSpec(shape, lambda s: idx + (0,) * (nd - len(idx)), pipeline_mode=pl.Buffered(1))

    def layer(shape):
        return pl.BlockSpec((None,) + shape, lambda s: (l,) + (0,) * len(shape), pipeline_mode=pl.Buffered(1))

    def head_in(shape, first, n):
        return pl.BlockSpec((None, None) + shape,
                            lambda s: (l, jnp.clip(s - first, 0, n - 1)) + (0,) * len(shape))

    def head_out(shape, first, n):
        return pl.BlockSpec((None,) + shape, lambda s: (jnp.clip(s - first, 0, n - 1),) + (0,) * len(shape))

    rw_blk = (HEAD_DIM, HEAD_DIM, nb)
    gl_blk = (GLA_DK, HEAD_DIM, nb)
    in_specs = [
        once((M, D_MODEL)), layer((RWKV_COLS, nb)),
        head_in(rw_blk, 0, N_RWKV), head_in(gl_blk, N_RWKV, N_GLA), head_in(rw_blk, N_RWKV + N_GLA, N_RET),
        layer((RWKV_COLS + GLA_COLS + RET_COLS, D_MODEL)), layer((RWKV_COLS, nb)), layer((7, RWKV_W, nb)),
        layer((RWKV_W, LORA_W)), layer((RWKV_W, LORA_A)), layer((RWKV_W, LORA_G)), layer((GLA_K, GLA_LORA)),
        layer((GLA_K, nb)), layer((GLA_V, nb)), once((2, DEC_SEQ, HEAD_DIM // 2, nb)), once((N_RET, 8, nb)),
        layer((D_MODEL, D_MODEL)), layer((2, D_MODEL)),
    ]
    out_specs = [
        once((M, D_MODEL)), once((RWKV_COLS, nb)),
        head_out(rw_blk, 0, N_RWKV), head_out(gl_blk, N_RWKV, N_GLA), head_out(rw_blk, N_RWKV + N_GLA, N_RET),
    ]
    out_shape = [
        jax.ShapeDtypeStruct((M, D_MODEL), F32), jax.ShapeDtypeStruct((RWKV_COLS, nb), F32),
        jax.ShapeDtypeStruct((N_RWKV,) + rw_blk, F32), jax.ShapeDtypeStruct((N_GLA,) + gl_blk, F32),
        jax.ShapeDtypeStruct((N_RET,) + rw_blk, F32),
    ]
    scratch = ([pltpu.VMEM((RWKV_W, M), F32)] * 9
               + [pltpu.VMEM((GLA_K, M), F32), pltpu.VMEM((GLA_K, M), F32), pltpu.VMEM((GLA_V, M), F32),
                  pltpu.VMEM((GLA_K, M), F32), pltpu.VMEM((GLA_V, M), F32), pltpu.VMEM((GLA_V, M), F32)]
               + [pltpu.VMEM((RET_W, M), F32)] * 5)
    return pl.pallas_call(
        functools.partial(_smix_kernel, nb=nb),
        grid=(N_STEPS,),
        in_specs=in_specs, out_specs=out_specs, out_shape=out_shape, scratch_shapes=scratch,
        compiler_params=pltpu.CompilerParams(dimension_semantics=("arbitrary",), vmem_limit_bytes=VMEM_LIMIT),
        name="mixer_sample",
    )(x, shift_t, s_rw, s_gl, s_rt, w1t, mu, vec, bwt, bat, bgt, bgkt, gb, gn, rot, gam, wout, ln1)


def _sffn_kernel(x_ref, conv_ref, wup_ref, cwb_ref, wdn_ref, ln_ref, y_ref, cv_out_ref, *, nb):
    M = DEC_SEQ * nb
    x = x_ref[...]
    xb = _bf(x)
    n_f = D_FF // FFN_CHUNK

    def cols(f, base):
        return slice(base + f * FFN_CHUNK, base + (f + 1) * FFN_CHUNK)

    def up(f):
        return [_dg(xb, wup_ref[:, cols(f, base)], _NN) for base in (0, D_FF)]

    def conv(u, cs):
        c0 = conv_ref[:, 0, cs]
        c1 = conv_ref[:, 1, cs]
        u1 = jnp.concatenate([c1, u[0:M - nb]], axis=0)
        u2 = jnp.concatenate([c0, c1, u[0:M - 2 * nb]], axis=0)
        cv_out_ref[:, 0, cs] = u[M - 2 * nb:M - nb]
        cv_out_ref[:, 1, cs] = u[M - nb:M]
        return cwb_ref[3:4, cs] + (cwb_ref[0:1, cs] * u2 + cwb_ref[1:2, cs] * u1 + cwb_ref[2:3, cs] * u)

    acc = jnp.zeros((M, D_MODEL), F32)
    u_queue = [up(f) for f in range(min(FFN_AHEAD, n_f))]
    for f in range(n_f):
        if f + FFN_AHEAD < n_f:
            u_queue.append(up(f + FFN_AHEAD))
        u_cur = u_queue.pop(0)
        hid = jax.nn.gelu(conv(u_cur[0], cols(f, 0))) * conv(u_cur[1], cols(f, D_FF))
        acc = acc + _dg(_bf(hid), wdn_ref[f * FFN_CHUNK:(f + 1) * FFN_CHUNK, :], _NN)
    y_ref[...] = _layer_norm(ALPHA * x + acc, ln_ref[0:1, :], ln_ref[1:2, :])


def _sffn_call(l, x, conv2, wts):
    M = x.shape[0]
    nb = M // DEC_SEQ
    wup, cwb, wdn, ln2 = wts

    def layer(shape):
        return pl.BlockSpec((None,) + shape, lambda i: (l,) + (0,) * len(shape), pipeline_mode=pl.Buffered(1))

    return pl.pallas_call(
        functools.partial(_sffn_kernel, nb=nb),
        grid=(1,),
        in_specs=[pl.BlockSpec((M, D_MODEL), lambda i: (0, 0)), layer((nb, CONV_W - 1, 2 * D_FF)),
                  layer((D_MODEL, 2 * D_FF)), layer((8, 2 * D_FF)), layer((D_FF, D_MODEL)), layer((2, D_MODEL))],
        out_specs=[pl.BlockSpec((M, D_MODEL), lambda i: (0, 0)),
                   pl.BlockSpec((nb, CONV_W - 1, 2 * D_FF), lambda i: (0, 0, 0))],
        out_shape=[jax.ShapeDtypeStruct((M, D_MODEL), F32),
                   jax.ShapeDtypeStruct((nb, CONV_W - 1, 2 * D_FF), F32)],
        compiler_params=pltpu.CompilerParams(dimension_semantics=("arbitrary",), vmem_limit_bytes=VMEM_LIMIT),
        name="ffn_sample",
    )(x, conv2, wup, cwb, wdn, ln2)


def _run_sample(x_sample, s_rw, s_sh, s_gl, s_rt, s_cv, p, ffn_w):
    bs, ts_, _ = x_sample.shape
    L = DEPTH

    def lanes(a):
        return jnp.broadcast_to(a[..., None], a.shape + (bs,))

    half = HEAD_DIM // 2
    inv = 1.0 / (ROPE_BASE ** jnp.linspace(0.0, 1.0, half, dtype=F32))
    ang = (PAST_LEN + jnp.arange(ts_)).astype(F32)[:, None] * inv[None]
    rot = lanes(jnp.stack([jnp.cos(ang), jnp.sin(ang)]))
    log_gamma = jnp.log(1.0 - jnp.exp2(-5.0 - jnp.arange(N_RET, dtype=F32)))
    gam = jnp.broadcast_to(jnp.exp(log_gamma)[:, None, None], (N_RET, 8, bs))
    vec = lanes(jnp.stack([p["rwkv_w0"], p["rwkv_a0"], p["rwkv_kk"], p["rwkv_ka"],
                           p["rwkv_rk"].reshape(L, RWKV_W), p["rwkv_lnw"], p["rwkv_lnb"]], axis=1))
    wts = (_bf(jnp.swapaxes(p["w_in"], 1, 2)), lanes(p["rwkv_mu"]), vec,
           _bf(jnp.swapaxes(p["rwkv_bw"], 1, 2)), _bf(jnp.swapaxes(p["rwkv_ba"], 1, 2)),
           _bf(jnp.swapaxes(p["rwkv_bg"], 1, 2)), _bf(jnp.swapaxes(p["gla_bgk"], 1, 2)),
           lanes(p["gla_bgk_b"]), lanes(jnp.tile(p["gla_norm_w"], (1, N_GLA))), rot, gam,
           _bf(p["w_out"]), jnp.stack([p["ln1_g"], p["ln1_b"]], axis=1))
    rw_t = jnp.transpose(s_rw, (0, 2, 3, 4, 1))
    gl_t = jnp.transpose(s_gl, (0, 2, 3, 4, 1))
    rt_t = jnp.transpose(s_rt, (0, 2, 3, 4, 1))
    sh_t = jnp.swapaxes(s_sh, 1, 2)
    x = jnp.swapaxes(x_sample, 0, 1).reshape(ts_ * bs, D_MODEL)
    n_rw, n_sh, n_gl, n_rt, n_cv = [], [], [], [], []
    for l in range(L):
        x, sh, rw, gl, rt = _smix_call(l, x, sh_t, rw_t, gl_t, rt_t, wts)
        x, cv = _sffn_call(l, x, s_cv, ffn_w)
        n_sh.append(sh)
        n_rw.append(rw)
        n_gl.append(gl)
        n_rt.append(rt)
        n_cv.append(cv)
    y = jnp.swapaxes(x.reshape(ts_, bs, D_MODEL), 0, 1)
    back = (0, 4, 1, 2, 3)
    return y, (jnp.transpose(jnp.stack(n_rw), back), jnp.swapaxes(jnp.stack(n_sh), 1, 2),
               jnp.transpose(jnp.stack(n_gl), back), jnp.transpose(jnp.stack(n_rt), back),
               jnp.stack(n_cv))


def _pad_to(a, axis, n):
    pad = [(0, 0)] * a.ndim
    pad[axis] = (0, n - a.shape[axis])
    return jnp.pad(a, pad)


def _place(a, axis, segs, total):
    out = []
    pos = 0
    for src, w, dst in segs:
        if dst > pos:
            shp = list(a.shape)
            shp[axis] = dst - pos
            out.append(jnp.zeros(shp, a.dtype))
        out.append(lax.slice_in_dim(a, src, src + w, axis=axis))
        pos = dst + w
    if total > pos:
        shp = list(a.shape)
        shp[axis] = total - pos
        out.append(jnp.zeros(shp, a.dtype))
    return jnp.concatenate(out, axis=axis)


def _in_col_segments():
    g0 = RWKV_COLS
    t0 = RWKV_COLS + GLA_COLS
    segs = [(0, RWKV_COLS, 0)]
    segs += [(g0, GLA_K, RW_P), (g0 + GLA_K, GLA_K, RW_P + GQ_P),
             (g0 + 2 * GLA_K, GLA_V, RW_P + 2 * GQ_P), (g0 + 2 * GLA_K + GLA_V, GLA_V, RW_P + 2 * GQ_P + HW),
             (g0 + 2 * GLA_K + 2 * GLA_V, GLA_LORA, RW_P + 2 * GQ_P + 2 * HW)]
    segs += [(t0 + i * RET_W, RET_W, RW_P + GL_P + i * HW) for i in range(4)]
    return segs


def _prep_weights(w_in, rwkv_mu, rwkv_w0, rwkv_bw, rwkv_a0, rwkv_ba, rwkv_bg, rwkv_kk, rwkv_ka,
                  rwkv_rk, rwkv_lnw, rwkv_lnb, gla_bgk, gla_bgk_b, gla_norm_w, w_out,
                  ln1_g, ln1_b, ln2_g, ln2_b, ffn_up, ffn_conv_w, ffn_conv_b, ffn_down):
    L = w_in.shape[0]
    w1 = _place(_bf(w_in), 2, _in_col_segments(), NP)
    mu = _pad_to(rwkv_mu, 1, RW_P)[:, None, :]
    gnw = _pad_to(jnp.tile(gla_norm_w, (1, N_GLA)), 1, HW)
    vec = jnp.stack([rwkv_w0, rwkv_a0, rwkv_kk, rwkv_ka, rwkv_rk.reshape(L, RWKV_W), rwkv_lnw, rwkv_lnb, gnw], axis=1)
    bw = _bf(_pad_to(rwkv_bw, 1, LANE))
    ba = _bf(_place(rwkv_ba, 1, [(0, LORA_A, LORA_W)], LANE))
    bg = _bf(_pad_to(rwkv_bg, 1, 2 * LANE))
    bgk = _bf(_pad_to(_pad_to(gla_bgk, 1, LANE), 2, GQ_P))
    bgkb = _pad_to(gla_bgk_b, 1, GQ_P)[:, None, :]
    wout = _place(_bf(w_out), 1, [(0, RWKV_W + GLA_V, 0), (RWKV_W + GLA_V, RET_W, 2 * HW)], MIX_P)
    ln1 = jnp.stack([ln1_g, ln1_b], axis=1)
    mixer_w = (w1, mu, vec, bw, ba, bg, bgk, bgkb, wout, ln1)
    cwb = _pad_to(jnp.concatenate([ffn_conv_w, ffn_conv_b[:, None, :]], axis=1), 1, 8)
    ln2 = jnp.stack([ln2_g, ln2_b], axis=1)
    ffn_w = (_bf(ffn_up), cwb, _bf(ffn_down), ln2)
    return mixer_w, ffn_w


def _tables(n_pos, chunk):
    half = HEAD_DIM // 2
    inv = 1.0 / (ROPE_BASE ** jnp.linspace(0.0, 1.0, half, dtype=F32))
    ang = jnp.arange(n_pos).astype(F32)[:, None] * inv[None]
    cos, sin = jnp.cos(ang), jnp.sin(ang)
    cos = jnp.tile(jnp.concatenate([cos, cos], -1), (1, HW // HEAD_DIM))
    sin = jnp.tile(jnp.concatenate([-sin, sin], -1), (1, HW // HEAD_DIM))
    log_gamma = jnp.log(1.0 - jnp.exp2(-5.0 - jnp.arange(N_RET, dtype=F32)))
    i = jnp.arange(chunk, dtype=F32)
    diff = i[:, None] - i[None, :]
    causal = diff >= 0
    dmask = jnp.where(causal, jnp.exp(jnp.where(causal, diff, 0.0) * log_gamma[:, None, None]), 0.0)
    qd = jnp.exp((i + 1.0) * log_gamma[:, None])[..., None]
    kd = jnp.exp((chunk - 1.0 - i) * log_gamma[:, None])[..., None]
    cd = jnp.exp(chunk * log_gamma)[:, None, None]
    qd = jnp.broadcast_to(qd, (N_RET, chunk, HEAD_DIM))
    kd = jnp.broadcast_to(kd, (N_RET, chunk, HEAD_DIM))
    cd = jnp.broadcast_to(cd, (N_RET, SLOTS, HEAD_DIM))
    return cos, sin, dmask, qd, kd, cd


def _run_prompt(x, s_rw, s_sh, s_gl, s_rt, s_cv, mixer_w, ffn_w, *, tt, chunk):
    T = x.shape[1]
    tabs = _tables(T, chunk)
    shift_rows = _pad_to(_pad_to(s_sh, 2, RW_P)[:, :, None, :], 2, SLOTS)
    conv_rows = _place(s_cv, 2, [(0, CONV_W - 1, SLOTS - (CONV_W - 1))], SLOTS)
    n_rw, n_sh, n_gl, n_rt, n_cv = [], [], [], [], []
    for l in range(DEPTH):
        x, sh, rw, gl, rt = _mixer_call(l, x, shift_rows, s_rw, s_gl, s_rt, tabs, mixer_w, tt=tt, chunk=chunk)
        x, cv = _ffn_call(l, x, conv_rows, ffn_w, tt=tt)
        n_sh.append(sh[:, SLOTS - 1, :RWKV_COLS])
        n_cv.append(cv[:, SLOTS - (CONV_W - 1):])
        n_rw.append(rw)
        n_gl.append(gl)
        n_rt.append(rt)
    return x, (jnp.stack(n_rw), jnp.stack(n_sh), jnp.stack(n_gl), jnp.stack(n_rt), jnp.stack(n_cv))


def kernel(x_prompt, x_sample, state_rwkv, state_shift, state_gla, state_ret, state_conv, w_in, rwkv_mu, rwkv_w0, rwkv_bw, rwkv_a0, rwkv_ba, rwkv_bg, rwkv_kk, rwkv_ka, rwkv_rk, rwkv_lnw, rwkv_lnb, gla_bgk, gla_bgk_b, gla_norm_w, w_out, ln1_g, ln1_b, ln2_g, ln2_b, ffn_up, ffn_conv_w, ffn_conv_b, ffn_down):
    mixer_w, ffn_w = _prep_weights(w_in, rwkv_mu, rwkv_w0, rwkv_bw, rwkv_a0, rwkv_ba, rwkv_bg, rwkv_kk, rwkv_ka,
                                   rwkv_rk, rwkv_lnw, rwkv_lnb, gla_bgk, gla_bgk_b, gla_norm_w, w_out,
                                   ln1_g, ln1_b, ln2_g, ln2_b, ffn_up, ffn_conv_w, ffn_conv_b, ffn_down)
    bp, tp, _ = x_prompt.shape
    bs, ts, _ = x_sample.shape
    assert ts == DEC_SEQ

    def zeros_like_state(s):
        return jnp.zeros((s.shape[0], bp) + s.shape[2:], F32)

    tt = min(256, tp)
    chunk = math.gcd(tp, CHUNK)
    y_p, st_p = _run_prompt(
        x_prompt, zeros_like_state(state_rwkv), zeros_like_state(state_shift), zeros_like_state(state_gla),
        zeros_like_state(state_ret), zeros_like_state(state_conv), mixer_w, ffn_w, tt=tt, chunk=chunk)

    raw = dict(w_in=w_in, rwkv_mu=rwkv_mu, rwkv_w0=rwkv_w0, rwkv_bw=rwkv_bw, rwkv_a0=rwkv_a0, rwkv_ba=rwkv_ba,
               rwkv_bg=rwkv_bg, rwkv_kk=rwkv_kk, rwkv_ka=rwkv_ka, rwkv_rk=rwkv_rk, rwkv_lnw=rwkv_lnw,
               rwkv_lnb=rwkv_lnb, gla_bgk=gla_bgk, gla_bgk_b=gla_bgk_b, gla_norm_w=gla_norm_w, w_out=w_out,
               ln1_g=ln1_g, ln1_b=ln1_b)
    y_s, st_s = _run_sample(x_sample, state_rwkv, state_shift, state_gla, state_ret, state_conv, raw, ffn_w)
    return (y_p, y_s) + st_p + st_s
```

```python
import functools
import math

import jax
import jax.numpy as jnp
from jax import lax
from jax.experimental import pallas as pl
from jax.experimental.pallas import tpu as pltpu

F32 = jnp.float32
BF16 = jnp.bfloat16

D_MODEL = 1024
DEPTH = 4
PAST_LEN = 16384
DEC_SEQ = 4
HEAD_DIM = 64
N_HEADS = D_MODEL // HEAD_DIM
N_GLA = (5 * N_HEADS) // 16
N_RET = (5 * N_HEADS) // 16
N_RWKV = N_HEADS - N_GLA - N_RET
RWKV_W = N_RWKV * HEAD_DIM
GLA_DK = HEAD_DIM // 2
GLA_K = N_GLA * GLA_DK
GLA_V = N_GLA * HEAD_DIM
RET_W = N_RET * HEAD_DIM
LORA_W = 64
LORA_A = 64
LORA_G = 160
GLA_LORA = 16
GLA_GATE_NORM = 16.0
CHUNK = 64
D_FF = 2816
CONV_W = 3
ALPHA = (2 * DEPTH) ** 0.25
RWKV_GN_EPS = 64e-5
LN_EPS = 1e-5
RMS_EPS = 1e-6
ROPE_BASE = 10000.0
RWKV_COLS = 3 * RWKV_W + LORA_W + LORA_A + LORA_G
GLA_COLS = 2 * GLA_K + 2 * GLA_V + GLA_LORA
RET_COLS = 4 * RET_W

LANE = 128
HW = 384
RW_P = 1536
GQ_P = 256
GL_P = 2 * GQ_P + 2 * HW + LANE
RT_P = 4 * HW
NP = RW_P + GL_P + RT_P
MIX_P = 3 * HW
SLOTS = 8

VMEM_LIMIT = 56 * 1024 * 1024
MIX_ROWS = 512
FFN_ROWS = 256


_NN = (((1,), (0,)), ((), ()))
_NT = (((1,), (1,)), ((), ()))
_TN = (((0,), (0,)), ((), ()))


def _bf(x):
    return x.astype(BF16)


def _dg(a, b, dims):
    return lax.dot_general(a, b, dims, preferred_element_type=F32)


def _dot(a, b, dims=_NN):
    return _dg(_bf(a), _bf(b), dims)


def _split(x, n):
    parts = []
    r = x
    for i in range(n):
        h = _bf(r)
        parts.append(h)
        if i + 1 < n:
            r = r - h.astype(F32)
    return parts


def _dot_sel(a, sel_bf, n=2, dims=_NN):
    out = None
    for h in _split(a, n):
        t = _dg(h, sel_bf, dims)
        out = t if out is None else out + t
    return out


def _sel_dot(sel_bf, b, n=3, dims=_NN):
    out = None
    for h in _split(b, n):
        t = _dg(sel_bf, h, dims)
        out = t if out is None else out + t
    return out


def _layer_norm(h, g, b):
    mu = jnp.mean(h, -1, keepdims=True)
    d = h - mu
    var = jnp.mean(d * d, -1, keepdims=True)
    return d * lax.rsqrt(var + LN_EPS) * g + b


def _swap_halves(x):
    pieces = []
    for i in range(x.shape[1] // LANE):
        p = x[:, i * LANE:(i + 1) * LANE]
        up = pltpu.roll(p, LANE - HEAD_DIM // 2, axis=1)
        dn = pltpu.roll(p, HEAD_DIM // 2, axis=1)
        lane = lax.broadcasted_iota(jnp.int32, p.shape, 1)
        pieces.append(jnp.where((lane % HEAD_DIM) < HEAD_DIM // 2, up, dn))
    return jnp.concatenate(pieces, axis=1)


def _mixer_kernel(x_ref, shift_ref, srw_ref, sgl_ref, srt_ref, cos_ref, sin_ref,
                  dmask_ref, qd_ref, kd_ref, cd_ref,
                  w1_ref, mu_ref, vec_ref, bw_ref, ba_ref, bg_ref, bgk_ref, bgkb_ref,
                  wout_ref, ln_ref,
                  x1_ref, shift_out_ref, srw_out, sgl_out, srt_out,
                  carry_ref, rw_r, rw_k, rw_v, rw_lw, rw_a, rw_b, rw_g, rw_bonus, rw_y,
                  gl_q, gl_k, gl_v, gl_lg, gl_gate, gl_o,
                  rt_q, rt_k, rt_v, rt_gate, rt_o,
                  *, rows, chunk, group):
    R, C = rows, chunk
    G = group * C
    j = pl.program_id(1)

    @pl.when(j == 0)
    def _():
        srw_out[...] = srw_ref[...]
        sgl_out[...] = sgl_ref[...]
        srt_out[...] = srt_ref[...]

    x = x_ref[0]
    xb = _bf(x)
    row = lax.broadcasted_iota(jnp.int32, (R, 1), 0)
    hi_ = lax.broadcasted_iota(jnp.int32, (2 * LANE, 2 * LANE), 0) // HEAD_DIM
    hj_ = lax.broadcasted_iota(jnp.int32, (2 * LANE, 2 * LANE), 1) // HEAD_DIM
    head_ones = (hi_ == hj_).astype(BF16)

    def head_sum(v):
        return jnp.concatenate([_dg(_bf(v[:, 0:2 * LANE]), head_ones, _NN),
                                _dg(_bf(v[:, 2 * LANE:HW]), head_ones[0:LANE, 0:LANE], _NN)], axis=1)

    p = _dg(xb, w1_ref[:, 0:RW_P], _NN)
    first = jnp.where(j == 0, shift_ref[0, 0:1, :], carry_ref[SLOTS - 1:SLOTS, :])
    prev = jnp.where(row == 0, first, pltpu.roll(p, 1, axis=0))
    carry_ref[...] = p[R - SLOTS:R, :]
    shift_out_ref[0] = p[R - SLOTS:R, :]
    pm = p + (prev - p) * mu_ref[...]
    r = pm[:, 0:HW]
    k = pm[:, HW:2 * HW]
    v = pm[:, 2 * HW:3 * HW]
    wa = pm[:, 3 * HW:3 * HW + LANE]
    xg = pm[:, 3 * HW + LANE:RW_P]
    w0 = vec_ref[0:1, :]
    a0 = vec_ref[1:2, :]
    k_k = vec_ref[2:3, :]
    k_a = vec_ref[3:4, :]
    r_k = vec_ref[4:5, :]
    w_log = -jax.nn.softplus(-(w0 + _dg(_bf(jnp.tanh(wa)), bw_ref[...], _NN))) - 0.5
    log_decay = -jnp.exp(w_log)
    a = jax.nn.sigmoid(a0 + _dg(_bf(wa), ba_ref[...], _NN))
    g = _dg(_bf(jax.nn.sigmoid(xg)), bg_ref[...], _NN)
    kk = k * k_k
    kk = kk / jnp.maximum(jnp.sqrt(head_sum(kk * kk)), 1e-12)
    k2 = k * (1.0 + (a - 1.0) * k_a)
    rw_r[...] = r
    rw_k[...] = k2
    rw_v[...] = v
    rw_lw[...] = log_decay
    rw_a[...] = -kk
    rw_b[...] = kk * a
    rw_g[...] = g
    rw_bonus[...] = head_sum(r * k2 * r_k) * v

    p = _dg(xb, w1_ref[:, RW_P:RW_P + GL_P], _NN)
    gk = p[:, 2 * GQ_P + 2 * HW:GL_P]
    lg = jax.nn.log_sigmoid(_dg(_bf(gk), bgk_ref[...], _NN) + bgkb_ref[...]) / GLA_GATE_NORM
    gl_q[...] = p[:, 0:GQ_P] * GLA_DK ** -0.5
    gl_k[...] = p[:, GQ_P:2 * GQ_P]
    gl_v[...] = p[:, 2 * GQ_P:2 * GQ_P + HW]
    gl_lg[...] = lg
    gl_gate[...] = jax.nn.silu(p[:, 2 * GQ_P + HW:2 * GQ_P + 2 * HW])

    p = _dg(xb, w1_ref[:, RW_P + GL_P:NP], _NN)
    cos = cos_ref[...]
    sin = sin_ref[...]
    q = p[:, 0:HW]
    k = p[:, HW:2 * HW]
    rt_q[...] = q * cos + _swap_halves(q) * sin
    rt_k[...] = (k * cos + _swap_halves(k) * sin) * HEAD_DIM ** -0.5
    rt_v[...] = p[:, 2 * HW:3 * HW]
    rt_gate[...] = jax.nn.silu(p[:, 3 * HW:4 * HW])

    gi = lax.broadcasted_iota(jnp.int32, (G, G), 0)
    gj = lax.broadcasted_iota(jnp.int32, (G, G), 1)
    same_chunk = (gi // C) == (gj // C)
    cum_sel = (same_chunk & (gi >= gj)).astype(BF16)

    def chunk_cumsum(v):
        cum = _sel_dot(cum_sel, v, n=2)
        tot = jnp.concatenate([jnp.broadcast_to(cum[(c + 1) * C - 1:(c + 1) * C, :], (C, v.shape[1]))
                               for c in range(group)], axis=0)
        return cum, tot
    ii = lax.broadcasted_iota(jnp.int32, (C, C), 0)
    jj = lax.broadcasted_iota(jnp.int32, (C, C), 1)
    tri_incl = ii >= jj
    tri_strict = ii > jj
    eye = (ii == jj).astype(F32)
    levels = []
    s = 1
    while s < C:
        levels.append(((ii // (2 * s)) == (jj // (2 * s))) & ((ii % (2 * s)) >= s) & ((jj % (2 * s)) < s))
        s *= 2
    ones_cv = jnp.ones((C, HEAD_DIM), BF16)
    chunks = range(group)

    def group_body(gidx, carry):
        g0 = pl.multiple_of(gidx * G, G)
        gs = pl.ds(g0, G)

        def cr(c):
            return slice(c * C, (c + 1) * C)

        lw = rw_lw[gs, :]
        cum, tot = chunk_cumsum(lw)
        w_inv = jnp.exp(-cum)
        w_rem = jnp.exp(tot - cum)
        r_t = rw_r[gs, :] * jnp.exp(cum)
        a_t = rw_a[gs, :] * jnp.exp(cum - lw)
        b_t = rw_b[gs, :] * w_inv
        k_t = rw_k[gs, :] * w_inv
        b_h = rw_b[gs, :] * w_rem
        k_h = rw_k[gs, :] * w_rem
        w_c = jnp.exp(tot)
        v_rw = rw_v[gs, :]
        items = [(c, h) for c in chunks for h in range(N_RWKV)]

        def hs(h):
            return slice(h * HEAD_DIM, (h + 1) * HEAD_DIM)

        ar = [jnp.concatenate([a_t[cr(c), hs(h)], r_t[cr(c), hs(h)]], axis=0) for c, h in items]
        xb_ = [_dot(ar[i], b_t[cr(c), hs(h)], _NT) for i, (c, h) in enumerate(items)]
        xk_ = [_dot(ar[i], k_t[cr(c), hs(h)], _NT) for i, (c, h) in enumerate(items)]
        a_ab = [jnp.where(tri_strict, t[0:C], 0.0) for t in xb_]
        a_ak = [jnp.where(tri_strict, t[0:C], 0.0) for t in xk_]
        p_rb = [jnp.where(tri_incl, t[C:2 * C], 0.0) for t in xb_]
        p_rk = [jnp.where(tri_incl, t[C:2 * C], 0.0) for t in xk_]

        lg = gl_lg[gs, :]
        gcum, gtot = chunk_cumsum(lg)
        q_in = gl_q[gs, :] * jnp.exp(gcum)
        k_in = gl_k[gs, :] * jnp.exp(-gcum)
        k_st = gl_k[gs, :] * jnp.exp(gtot - gcum)
        v_gl = gl_v[gs, :]
        g_col = []
        for c in chunks:
            parts = _split(lg[cr(c), :], 3)
            g_col.append(jnp.exp(_dg(parts[0], ones_cv, _TN) + _dg(parts[1], ones_cv, _TN)
                                 + _dg(parts[2], ones_cv, _TN)))
        gitems = [(c, h) for c in chunks for h in range(N_GLA)]

        def ks(h):
            return slice(h * GLA_DK, (h + 1) * GLA_DK)

        g_att = [jnp.where(tri_incl, _dot(q_in[cr(c), ks(h)], k_in[cr(c), ks(h)], _NT), 0.0) for c, h in gitems]
        g_kv = [_dot(k_st[cr(c), ks(h)], v_gl[cr(c), hs(h)], _TN) for c, h in gitems]

        q_rt = rt_q[gs, :]
        k_rt = rt_k[gs, :]
        v_rt = rt_v[gs, :]
        ritems = [(c, h) for c in chunks for h in range(N_RET)]
        r_att = [_dot(q_rt[cr(c), hs(h)], k_rt[cr(c), hs(h)], _NT) * dmask_ref[h] for c, h in ritems]
        r_kv = [_dot(k_rt[cr(c), hs(h)] * kd_ref[h], v_rt[cr(c), hs(h)], _TN) for c, h in ritems]

        m = [eye + jnp.where(levels[0], t, 0.0) for t in a_ab]
        for lvl in levels[1:]:
            t_ = [_dot(jnp.where(lvl, a_ab[i], 0.0), m[i]) for i in range(len(items))]
            m = [m[i] + _dot(m[i], t_[i]) for i in range(len(items))]

        g_av = [_dot(g_att[i], v_gl[cr(c), hs(h)]) for i, (c, h) in enumerate(gitems)]
        r_av = [_dot(r_att[i], v_rt[cr(c), hs(h)]) for i, (c, h) in enumerate(ritems)]

        vv = [v_rw[cr(c), hs(h)] for c, h in items]
        akv = [_dot(a_ak[i], vv[i]) for i in range(len(items))]
        at2 = [_dot(m[i], a_t[cr(c), hs(h)]) for i, (c, h) in enumerate(items)]
        y0 = [_dot(p_rk[i], vv[i]) for i in range(len(items))]
        u0 = [_dot(m[i], akv[i]) for i in range(len(items))]
        gp = [_dot(at2[i], b_h[cr(c), hs(h)], _TN) for i, (c, h) in enumerate(items)]
        hh = [_dot(jnp.concatenate([u0[i], vv[i]], axis=0),
                   jnp.concatenate([b_h[cr(c), hs(h)], k_h[cr(c), hs(h)]], axis=0), _TN)
              for i, (c, h) in enumerate(items)]

        s0 = [None] * len(items)
        cur = [srw_out[0, h] for h in range(N_RWKV)]
        for c in chunks:
            for h in range(N_RWKV):
                i = c * N_RWKV + h
                s0[i] = cur[h]
                cur[h] = cur[h] * w_c[c * C:c * C + 1, hs(h)] + _dot(cur[h], gp[i]) + hh[i]
        for h in range(N_RWKV):
            srw_out[0, h] = cur[h]

        gs0 = [None] * len(gitems)
        rs0 = [None] * len(ritems)
        for h in range(N_GLA):
            S = sgl_out[0, h]
            for c in chunks:
                gs0[c * N_GLA + h] = S
                S = S * g_col[c][ks(h), :] + g_kv[c * N_GLA + h]
            sgl_out[0, h] = S
        for h in range(N_RET):
            S = srt_out[0, h]
            for c in chunks:
                rs0[c * N_RET + h] = S
                S = S * cd_ref[h, 0:1, :] + r_kv[c * N_RET + h]
            srt_out[0, h] = S

        us = [_dot(jnp.concatenate([at2[i], r_t[cr(c), hs(h)]], axis=0), s0[i], _NT)
              for i, (c, h) in enumerate(items)]
        g_o = [g_av[i] + _dot(q_in[cr(c), ks(h)], gs0[i]) for i, (c, h) in enumerate(gitems)]
        r_o = [r_av[i] + _dot(q_rt[cr(c), hs(h)] * qd_ref[h], rs0[i]) for i, (c, h) in enumerate(ritems)]
        y = [us[i][C:2 * C] + _dot(p_rb[i], us[i][0:C] + u0[i]) + y0[i] for i in range(len(items))]
        for c in chunks:
            rows_c = pl.ds(g0 + c * C, C)
            rw_y[rows_c, :] = jnp.concatenate(y[c * N_RWKV:(c + 1) * N_RWKV], axis=1)
            gl_o[rows_c, :] = jnp.concatenate(g_o[c * N_GLA:(c + 1) * N_GLA]
                                              + [jnp.zeros((C, HW - GLA_V), F32)], axis=1)
            rt_o[rows_c, :] = jnp.concatenate(r_o[c * N_RET:(c + 1) * N_RET]
                                              + [jnp.zeros((C, HW - RET_W), F32)], axis=1)
        return carry

    if R == G:
        group_body(0, 0)
    else:
        lax.fori_loop(0, R // G, group_body, 0)

    ln_w = vec_ref[5:6, :]
    ln_b = vec_ref[6:7, :]
    gn_w = vec_ref[7:8, :]
    inv_hd = 1.0 / HEAD_DIM
    y = rw_y[...]
    ym = head_sum(y) * inv_hd
    d = y - ym
    yv = head_sum(d * d) * inv_hd
    ya = (d * lax.rsqrt(yv + RWKV_GN_EPS) * ln_w + ln_b + rw_bonus[...]) * rw_g[...]
    o = gl_o[...]
    ob = o * lax.rsqrt(head_sum(o * o) * inv_hd + RMS_EPS) * gn_w * gl_gate[...]
    o = rt_o[...]
    oc = o * lax.rsqrt(head_sum(o * o) * inv_hd + RMS_EPS) * rt_gate[...]
    mix = _dg(_bf(jnp.concatenate([ya, ob, oc], axis=1)), wout_ref[...], _NN)
    out = _layer_norm(ALPHA * x + mix, ln_ref[0:1, :], ln_ref[1:2, :])
    x1_ref[0] = out


def _const_spec(shape):
    nd = len(shape)
    return pl.BlockSpec(shape, lambda i, j: (0,) * nd, pipeline_mode=pl.Buffered(1))


def _layer_spec(l, shape):
    return pl.BlockSpec((None,) + shape, lambda i, j: (l,) + (0,) * len(shape), pipeline_mode=pl.Buffered(1))


def _mixer_call(l, x, shift_rows, s_rw, s_gl, s_rt, tabs, wts, *, tt, chunk):
    B, T, _ = x.shape
    bb, R = 1, tt
    nb, nt = B, T // tt
    group = tt // chunk
    cos, sin, dmask, qd, kd, cd = tabs
    (w1, mu, vec, bw, ba, bg, bgk, bgkb, wout, ln1) = wts
    C = chunk
    lw = functools.partial(_layer_spec, l)

    in_specs = [
        pl.BlockSpec((bb, tt, D_MODEL), lambda i, j: (i, j, 0)),
        pl.BlockSpec((None, bb, SLOTS, RW_P), lambda i, j: (l, i, 0, 0)),
        pl.BlockSpec((None, bb, N_RWKV, HEAD_DIM, HEAD_DIM), lambda i, j: (l, i, 0, 0, 0)),
        pl.BlockSpec((None, bb, N_GLA, GLA_DK, HEAD_DIM), lambda i, j: (l, i, 0, 0, 0)),
        pl.BlockSpec((None, bb, N_RET, HEAD_DIM, HEAD_DIM), lambda i, j: (l, i, 0, 0, 0)),
        pl.BlockSpec((R, HW), lambda i, j: (j, 0)),
        pl.BlockSpec((R, HW), lambda i, j: (j, 0)),
        _const_spec((N_RET, C, C)),
        _const_spec((N_RET, C, HEAD_DIM)),
        _const_spec((N_RET, C, HEAD_DIM)),
        _const_spec((N_RET, SLOTS, HEAD_DIM)),
        lw((D_MODEL, NP)), lw((1, RW_P)), lw((8, HW)), lw((LANE, HW)), lw((LANE, HW)), lw((2 * LANE, HW)),
        lw((LANE, GQ_P)), lw((1, GQ_P)), lw((MIX_P, D_MODEL)), lw((2, D_MODEL)),
    ]
    out_specs = [
        pl.BlockSpec((bb, tt, D_MODEL), lambda i, j: (i, j, 0)),
        pl.BlockSpec((bb, SLOTS, RW_P), lambda i, j: (i, 0, 0)),
        pl.BlockSpec((bb, N_RWKV, HEAD_DIM, HEAD_DIM), lambda i, j: (i, 0, 0, 0)),
        pl.BlockSpec((bb, N_GLA, GLA_DK, HEAD_DIM), lambda i, j: (i, 0, 0, 0)),
        pl.BlockSpec((bb, N_RET, HEAD_DIM, HEAD_DIM), lambda i, j: (i, 0, 0, 0)),
    ]
    out_shape = [
        jax.ShapeDtypeStruct((B, T, D_MODEL), F32),
        jax.ShapeDtypeStruct((B, SLOTS, RW_P), F32),
        jax.ShapeDtypeStruct(s_rw.shape[1:], F32),
        jax.ShapeDtypeStruct(s_gl.shape[1:], F32),
        jax.ShapeDtypeStruct(s_rt.shape[1:], F32),
    ]
    scratch = [pltpu.VMEM((SLOTS, RW_P), F32)]
    scratch += [pltpu.VMEM((R, HW), F32)] * 9
    scratch += [pltpu.VMEM((R, GQ_P), F32), pltpu.VMEM((R, GQ_P), F32), pltpu.VMEM((R, HW), F32),
                pltpu.VMEM((R, GQ_P), F32), pltpu.VMEM((R, HW), F32), pltpu.VMEM((R, HW), F32)]
    scratch += [pltpu.VMEM((R, HW), F32)] * 5
    return pl.pallas_call(
        functools.partial(_mixer_kernel, rows=R, chunk=C, group=group),
        grid=(nb, nt),
        in_specs=in_specs, out_specs=out_specs, out_shape=out_shape,
        scratch_shapes=scratch,
        compiler_params=pltpu.CompilerParams(dimension_semantics=("arbitrary", "arbitrary"),
                                             vmem_limit_bytes=VMEM_LIMIT),
        name="mixer_prompt",
    )(x, shift_rows, s_rw, s_gl, s_rt, cos, sin, dmask, qd, kd, cd,
      w1, mu, vec, bw, ba, bg, bgk, bgkb, wout, ln1)


FFN_CHUNK = 256
FFN_AHEAD = 3


def _ffn_kernel(x_ref, conv_ref, wup_ref, cwb_ref, wdn_ref, ln_ref, y_ref, cv_out_ref, carry_ref, *, rows):
    R = rows
    j = pl.program_id(1)
    x = x_ref[0]
    xb = _bf(x)
    row8 = lax.broadcasted_iota(jnp.int32, (SLOTS, 1), 0)
    n_f = D_FF // FFN_CHUNK

    def cols(f, base):
        return slice(base + f * FFN_CHUNK, base + (f + 1) * FFN_CHUNK)

    def up(f):
        return [_dg(xb, wup_ref[:, cols(f, base)], _NN) for base in (0, D_FF)]

    def conv(u, cs):
        prev = jnp.where(j == 0, conv_ref[0, :, cs], carry_ref[:, cs])
        u1 = pltpu.roll(u, 1, axis=0)
        u2 = pltpu.roll(u, 2, axis=0)
        h1 = jnp.where(row8 == 0, prev[SLOTS - 1:SLOTS], u1[0:SLOTS])
        h2 = jnp.where(row8 == 0, prev[SLOTS - 2:SLOTS - 1],
                       jnp.where(row8 == 1, prev[SLOTS - 1:SLOTS], u2[0:SLOTS]))
        u1 = jnp.concatenate([h1, u1[SLOTS:]], axis=0)
        u2 = jnp.concatenate([h2, u2[SLOTS:]], axis=0)
        carry_ref[:, cs] = u[R - SLOTS:R]
        cv_out_ref[0, :, cs] = u[R - SLOTS:R]
        return cwb_ref[3:4, cs] + (cwb_ref[0:1, cs] * u2 + cwb_ref[1:2, cs] * u1 + cwb_ref[2:3, cs] * u)

    acc = jnp.zeros((R, D_MODEL), F32)
    u_queue = [up(f) for f in range(min(FFN_AHEAD, n_f))]
    for f in range(n_f):
        if f + FFN_AHEAD < n_f:
            u_queue.append(up(f + FFN_AHEAD))
        u_cur = u_queue.pop(0)
        hid = jax.nn.gelu(conv(u_cur[0], cols(f, 0))) * conv(u_cur[1], cols(f, D_FF))
        acc = acc + _dg(_bf(hid), wdn_ref[f * FFN_CHUNK:(f + 1) * FFN_CHUNK, :], _NN)
    y_ref[0] = _layer_norm(ALPHA * x + acc, ln_ref[0:1, :], ln_ref[1:2, :])


def _ffn_call(l, x, conv_rows, wts, *, tt):
    B, T, _ = x.shape
    bb, nb, nt = 1, B, T // tt
    wup, cwb, wdn, ln2 = wts
    lw = functools.partial(_layer_spec, l)
    return pl.pallas_call(
        functools.partial(_ffn_kernel, rows=tt),
        grid=(nb, nt),
        in_specs=[
            pl.BlockSpec((bb, tt, D_MODEL), lambda i, j: (i, j, 0)),
            pl.BlockSpec((None, bb, SLOTS, 2 * D_FF), lambda i, j: (l, i, 0, 0)),
            lw((D_MODEL, 2 * D_FF)), lw((8, 2 * D_FF)), lw((D_FF, D_MODEL)), lw((2, D_MODEL)),
        ],
        out_specs=[
            pl.BlockSpec((bb, tt, D_MODEL), lambda i, j: (i, j, 0)),
            pl.BlockSpec((bb, SLOTS, 2 * D_FF), lambda i, j: (i, 0, 0)),
        ],
        out_shape=[jax.ShapeDtypeStruct((B, T, D_MODEL), F32),
                   jax.ShapeDtypeStruct((B, SLOTS, 2 * D_FF), F32)],
        scratch_shapes=[pltpu.VMEM((SLOTS, 2 * D_FF), F32)],
        compiler_params=pltpu.CompilerParams(dimension_semantics=("arbitrary", "arbitrary"),
                                             vmem_limit_bytes=VMEM_LIMIT),
        name="ffn_prompt",
    )(x, conv_rows, wup, cwb, wdn, ln2)


N_STEPS = N_RWKV + N_GLA + N_RET
G0 = RWKV_COLS
T0 = RWKV_COLS + GLA_COLS


def _smix_kernel(x_ref, shift_ref, srw_ref, sgl_ref, srt_ref, w1t_ref, mu_ref, vec_ref,
                 bwt_ref, bat_ref, bgt_ref, bgkt_ref, gb_ref, gn_ref, rot_ref, gam_ref, wout_ref, ln_ref,
                 x1_ref, shift_out_ref, srw_out, sgl_out, srt_out,
                 rw_r, rw_k, rw_v, rw_w, rw_a, rw_b, rw_g, rw_bonus, rw_y,
                 gl_q, gl_k, gl_v, gl_g, gl_gate, gl_o, rt_q, rt_k, rt_v, rt_gate, rt_o, *, nb):
    s = pl.program_id(0)
    M = DEC_SEQ * nb

    def ts(t):
        return slice(t * nb, (t + 1) * nb)

    def tile_t(c):
        return jnp.concatenate([c] * DEC_SEQ, axis=1)

    def hrows(h):
        return slice(h * HEAD_DIM, (h + 1) * HEAD_DIM)

    @pl.when(s == 0)
    def _():
        xb = _bf(x_ref[...])

        def proj(r0, r1):
            return _dg(w1t_ref[r0:r1, :], xb, _NT)

        p = proj(0, RWKV_COLS)
        shift_out_ref[...] = p[:, (DEC_SEQ - 1) * nb:]
        prev = jnp.concatenate([shift_ref[...], p[:, :(DEC_SEQ - 1) * nb]], axis=1)
        pm = p + (prev - p) * tile_t(mu_ref[...])
        r = pm[0:RWKV_W]
        k = pm[RWKV_W:2 * RWKV_W]
        v = pm[2 * RWKV_W:3 * RWKV_W]
        xw = pm[3 * RWKV_W:3 * RWKV_W + LORA_W]
        xa = pm[3 * RWKV_W + LORA_W:3 * RWKV_W + LORA_W + LORA_A]
        xg = pm[3 * RWKV_W + LORA_W + LORA_A:RWKV_COLS]
        w0, a0, k_k, k_a, r_k = (tile_t(vec_ref[i]) for i in range(5))
        w_log = -jax.nn.softplus(-(w0 + _dg(bwt_ref[...], _bf(jnp.tanh(xw)), _NN))) - 0.5
        a = jax.nn.sigmoid(a0 + _dg(bat_ref[...], _bf(xa), _NN))
        kk = k * k_k
        k2 = k * (1.0 + (a - 1.0) * k_a)
        rk2 = r * k2 * r_k
        for h in range(N_RWKV):
            hs = hrows(h)
            kh = kk[hs]
            kh = kh / jnp.maximum(jnp.sqrt(jnp.sum(kh * kh, axis=0, keepdims=True)), 1e-12)
            rw_a[hs, :] = -kh
            rw_b[hs, :] = kh * a[hs]
            rw_bonus[hs, :] = jnp.sum(rk2[hs], axis=0, keepdims=True) * v[hs]
        rw_r[...] = r
        rw_k[...] = k2
        rw_v[...] = v
        rw_w[...] = jnp.exp(-jnp.exp(w_log))
        rw_g[...] = _dg(bgt_ref[...], _bf(jax.nn.sigmoid(xg)), _NN)

        p = proj(G0, T0)
        gk = p[2 * GLA_K + 2 * GLA_V:GLA_COLS]
        lg = jax.nn.log_sigmoid(_dg(bgkt_ref[...], _bf(gk), _NN) + tile_t(gb_ref[...])) / GLA_GATE_NORM
        gl_q[...] = p[0:GLA_K] * GLA_DK ** -0.5
        gl_k[...] = p[GLA_K:2 * GLA_K]
        gl_v[...] = p[2 * GLA_K:2 * GLA_K + GLA_V]
        gl_g[...] = jnp.exp(lg)
        gl_gate[...] = jax.nn.silu(p[2 * GLA_K + GLA_V:2 * GLA_K + 2 * GLA_V])

        p = proj(T0, T0 + RET_COLS)
        cos = jnp.concatenate([rot_ref[0, t] for t in range(DEC_SEQ)], axis=1)
        sin = jnp.concatenate([rot_ref[1, t] for t in range(DEC_SEQ)], axis=1)
        half = HEAD_DIM // 2

        def rot(xh):
            x1, x2 = xh[0:half], xh[half:HEAD_DIM]
            return jnp.concatenate([x1 * cos - x2 * sin, x1 * sin + x2 * cos], axis=0)

        for h in range(N_RET):
            hs = hrows(h)
            rt_q[hs, :] = rot(p[hs])
            rt_k[hs, :] = rot(p[RET_W + h * HEAD_DIM:RET_W + (h + 1) * HEAD_DIM]) * HEAD_DIM ** -0.5
        rt_v[...] = p[2 * RET_W:3 * RET_W]
        rt_gate[...] = jax.nn.silu(p[3 * RET_W:4 * RET_W])

    @pl.when(s < N_RWKV)
    def _():
        r0 = pl.multiple_of(s * HEAD_DIM, HEAD_DIM)
        hs = pl.ds(r0, HEAD_DIM)

        def v_group(i, carry):
            v0 = pl.multiple_of(i * 8, 8)
            vt = [rw_v[pl.ds(r0 + v0, 8), ts(t)] for t in range(DEC_SEQ)]
            ys = [[] for _ in range(DEC_SEQ)]
            for j in range(8):
                S = srw_ref[v0 + j]
                for t in range(DEC_SEQ):
                    sa = jnp.sum(S * rw_a[hs, ts(t)], axis=0, keepdims=True)
                    S = S * rw_w[hs, ts(t)] + sa * rw_b[hs, ts(t)] + vt[t][j:j + 1, :] * rw_k[hs, ts(t)]
                    ys[t].append(jnp.sum(S * rw_r[hs, ts(t)], axis=0, keepdims=True))
                srw_out[v0 + j] = S
            for t in range(DEC_SEQ):
                rw_y[pl.ds(r0 + v0, 8), ts(t)] = jnp.concatenate(ys[t], axis=0)
            return carry

        lax.fori_loop(0, HEAD_DIM // 8, v_group, 0)

    def kv_head(s_in, s_out, q_ref, k_ref, v_ref, o_ref, decay_rows, k0, v0, nk):
        for t in range(DEC_SEQ):
            src = s_in if t == 0 else s_out
            v_t = v_ref[pl.ds(v0, HEAD_DIM), ts(t)]

            def k_group(i, o, t=t, src=src, v_t=v_t):
                kg = pl.multiple_of(i * 8, 8)
                q8 = q_ref[pl.ds(k0 + kg, 8), ts(t)]
                k8 = k_ref[pl.ds(k0 + kg, 8), ts(t)]
                d8 = decay_rows(kg, t)
                for j in range(8):
                    S = src[kg + j] * d8[j:j + 1, :] + k8[j:j + 1, :] * v_t
                    s_out[kg + j] = S
                    o = o + q8[j:j + 1, :] * S
                return o

            o_ref[pl.ds(v0, HEAD_DIM), ts(t)] = lax.fori_loop(0, nk // 8, k_group, jnp.zeros((HEAD_DIM, nb), F32))

    @pl.when((s >= N_RWKV) & (s < N_RWKV + N_GLA))
    def _():
        h = s - N_RWKV
        k0 = pl.multiple_of(h * GLA_DK, GLA_DK)
        v0 = pl.multiple_of(h * HEAD_DIM, HEAD_DIM)
        kv_head(sgl_ref, sgl_out, gl_q, gl_k, gl_v, gl_o,
                lambda kg, t: gl_g[pl.ds(k0 + kg, 8), ts(t)], k0, v0, GLA_DK)

    @pl.when(s >= N_RWKV + N_GLA)
    def _():
        h = s - (N_RWKV + N_GLA)
        v0 = pl.multiple_of(h * HEAD_DIM, HEAD_DIM)
        gamma = gam_ref[h]
        kv_head(srt_ref, srt_out, rt_q, rt_k, rt_v, rt_o, lambda kg, t: gamma, v0, v0, HEAD_DIM)

    @pl.when(s == N_STEPS - 1)
    def _():
        ln_w, ln_b = tile_t(vec_ref[5]), tile_t(vec_ref[6])
        gn_w = tile_t(gn_ref[...])
        parts = []
        for h in range(N_RWKV):
            hs = hrows(h)
            y = rw_y[hs, :]
            d = y - jnp.mean(y, axis=0, keepdims=True)
            yv = jnp.mean(d * d, axis=0, keepdims=True)
            parts.append((d * lax.rsqrt(yv + RWKV_GN_EPS) * ln_w[hs] + ln_b[hs] + rw_bonus[hs, :]) * rw_g[hs, :])
        for h in range(N_GLA):
            hs = hrows(h)
            o = gl_o[hs, :]
            parts.append(o * lax.rsqrt(jnp.mean(o * o, axis=0, keepdims=True) + RMS_EPS) * gn_w[hs] * gl_gate[hs, :])
        for h in range(N_RET):
            hs = hrows(h)
            o = rt_o[hs, :]
            parts.append(o * lax.rsqrt(jnp.mean(o * o, axis=0, keepdims=True) + RMS_EPS) * rt_gate[hs, :])
        mix_t = _bf(jnp.concatenate(parts, axis=0))
        mix = _dg(mix_t, wout_ref[...], _TN)
        x1_ref[...] = _layer_norm(ALPHA * x_ref[...] + mix, ln_ref[0:1, :], ln_ref[1:2, :])


def _smix_call(l, x, shift_t, s_rw, s_gl, s_rt, wts):
    M = x.shape[0]
    nb = M // DEC_SEQ
    (w1t, mu, vec, bwt, bat, bgt, bgkt, gb, gn, rot, gam, wout, ln1) = wts

    def once(shape, idx=()):
        nd = len(shape)
        return pl.BlockSpec(shape, lambda s: idx + (0,) * (nd - len(idx)), pipeline_mode=pl.Buffered(1))

    def layer(shape):
        return pl.BlockSpec((None,) + shape, lambda s: (l,) + (0,) * len(shape), pipeline_mode=pl.Buffered(1))

    def head_in(shape, first, n):
        return pl.BlockSpec((None, None) + shape,
                            lambda s: (l, jnp.clip(s - first, 0, n - 1)) + (0,) * len(shape))

    def head_out(shape, first, n):
        return pl.BlockSpec((None,) + shape, lambda s: (jnp.clip(s - first, 0, n - 1),) + (0,) * len(shape))

    rw_blk = (HEAD_DIM, HEAD_DIM, nb)
    gl_blk = (GLA_DK, HEAD_DIM, nb)
    in_specs = [
        once((M, D_MODEL)), layer((RWKV_COLS, nb)),
        head_in(rw_blk, 0, N_RWKV), head_in(gl_blk, N_RWKV, N_GLA), head_in(rw_blk, N_RWKV + N_GLA, N_RET),
        layer((RWKV_COLS + GLA_COLS + RET_COLS, D_MODEL)), layer((RWKV_COLS, nb)), layer((7, RWKV_W, nb)),
        layer((RWKV_W, LORA_W)), layer((RWKV_W, LORA_A)), layer((RWKV_W, LORA_G)), layer((GLA_K, GLA_LORA)),
        layer((GLA_K, nb)), layer((GLA_V, nb)), once((2, DEC_SEQ, HEAD_DIM // 2, nb)), once((N_RET, 8, nb)),
        layer((D_MODEL, D_MODEL)), layer((2, D_MODEL)),
    ]
    out_specs = [
        once((M, D_MODEL)), once((RWKV_COLS, nb)),
        head_out(rw_blk, 0, N_RWKV), head_out(gl_blk, N_RWKV, N_GLA), head_out(rw_blk, N_RWKV + N_GLA, N_RET),
    ]
    out_shape = [
        jax.ShapeDtypeStruct((M, D_MODEL), F32), jax.ShapeDtypeStruct((RWKV_COLS, nb), F32),
        jax.ShapeDtypeStruct((N_RWKV,) + rw_blk, F32), jax.ShapeDtypeStruct((N_GLA,) + gl_blk, F32),
        jax.ShapeDtypeStruct((N_RET,) + rw_blk, F32),
    ]
    scratch = ([pltpu.VMEM((RWKV_W, M), F32)] * 9
               + [pltpu.VMEM((GLA_K, M), F32), pltpu.VMEM((GLA_K, M), F32), pltpu.VMEM((GLA_V, M), F32),
                  pltpu.VMEM((GLA_K, M), F32), pltpu.VMEM((GLA_V, M), F32), pltpu.VMEM((GLA_V, M), F32)]
               + [pltpu.VMEM((RET_W, M), F32)] * 5)
    return pl.pallas_call(
        functools.partial(_smix_kernel, nb=nb),
        grid=(N_STEPS,),
        in_specs=in_specs, out_specs=out_specs, out_shape=out_shape, scratch_shapes=scratch,
        compiler_params=pltpu.CompilerParams(dimension_semantics=("arbitrary",), vmem_limit_bytes=VMEM_LIMIT),
        name="mixer_sample",
    )(x, shift_t, s_rw, s_gl, s_rt, w1t, mu, vec, bwt, bat, bgt, bgkt, gb, gn, rot, gam, wout, ln1)


def _sffn_kernel(x_ref, conv_ref, wup_ref, cwb_ref, wdn_ref, ln_ref, y_ref, cv_out_ref, *, nb):
    M = DEC_SEQ * nb
    x = x_ref[...]
    xb = _bf(x)
    n_f = D_FF // FFN_CHUNK

    def cols(f, base):
        return slice(base + f * FFN_CHUNK, base + (f + 1) * FFN_CHUNK)

    def up(f):
        return [_dg(xb, wup_ref[:, cols(f, base)], _NN) for base in (0, D_FF)]

    def conv(u, cs):
        c0 = conv_ref[:, 0, cs]
        c1 = conv_ref[:, 1, cs]
        u1 = jnp.concatenate([c1, u[0:M - nb]], axis=0)
        u2 = jnp.concatenate([c0, c1, u[0:M - 2 * nb]], axis=0)
        cv_out_ref[:, 0, cs] = u[M - 2 * nb:M - nb]
        cv_out_ref[:, 1, cs] = u[M - nb:M]
        return cwb_ref[3:4, cs] + (cwb_ref[0:1, cs] * u2 + cwb_ref[1:2, cs] * u1 + cwb_ref[2:3, cs] * u)

    acc = jnp.zeros((M, D_MODEL), F32)
    u_queue = [up(f) for f in range(min(FFN_AHEAD, n_f))]
    for f in range(n_f):
        if f + FFN_AHEAD < n_f:
            u_queue.append(up(f + FFN_AHEAD))
        u_cur = u_queue.pop(0)
        hid = jax.nn.gelu(conv(u_cur[0], cols(f, 0))) * conv(u_cur[1], cols(f, D_FF))
        acc = acc + _dg(_bf(hid), wdn_ref[f * FFN_CHUNK:(f + 1) * FFN_CHUNK, :], _NN)
    y_ref[...] = _layer_norm(ALPHA * x + acc, ln_ref[0:1, :], ln_ref[1:2, :])


def _sffn_call(l, x, conv2, wts):
    M = x.shape[0]
    nb = M // DEC_SEQ
    wup, cwb, wdn, ln2 = wts

    def layer(shape):
        return pl.BlockSpec((None,) + shape, lambda i: (l,) + (0,) * len(shape), pipeline_mode=pl.Buffered(1))

    return pl.pallas_call(
        functools.partial(_sffn_kernel, nb=nb),
        grid=(1,),
        in_specs=[pl.BlockSpec((M, D_MODEL), lambda i: (0, 0)), layer((nb, CONV_W - 1, 2 * D_FF)),
                  layer((D_MODEL, 2 * D_FF)), layer((8, 2 * D_FF)), layer((D_FF, D_MODEL)), layer((2, D_MODEL))],
        out_specs=[pl.BlockSpec((M, D_MODEL), lambda i: (0, 0)),
                   pl.BlockSpec((nb, CONV_W - 1, 2 * D_FF), lambda i: (0, 0, 0))],
        out_shape=[jax.ShapeDtypeStruct((M, D_MODEL), F32),
                   jax.ShapeDtypeStruct((nb, CONV_W - 1, 2 * D_FF), F32)],
        compiler_params=pltpu.CompilerParams(dimension_semantics=("arbitrary",), vmem_limit_bytes=VMEM_LIMIT),
        name="ffn_sample",
    )(x, conv2, wup, cwb, wdn, ln2)


def _run_sample(x_sample, s_rw, s_sh, s_gl, s_rt, s_cv, p, ffn_w):
    bs, ts_, _ = x_sample.shape
    L = DEPTH

    def lanes(a):
        return jnp.broadcast_to(a[..., None], a.shape + (bs,))

    half = HEAD_DIM // 2
    inv = 1.0 / (ROPE_BASE ** jnp.linspace(0.0, 1.0, half, dtype=F32))
    ang = (PAST_LEN + jnp.arange(ts_)).astype(F32)[:, None] * inv[None]
    rot = lanes(jnp.stack([jnp.cos(ang), jnp.sin(ang)]))
    log_gamma = jnp.log(1.0 - jnp.exp2(-5.0 - jnp.arange(N_RET, dtype=F32)))
    gam = jnp.broadcast_to(jnp.exp(log_gamma)[:, None, None], (N_RET, 8, bs))
    vec = lanes(jnp.stack([p["rwkv_w0"], p["rwkv_a0"], p["rwkv_kk"], p["rwkv_ka"],
                           p["rwkv_rk"].reshape(L, RWKV_W), p["rwkv_lnw"], p["rwkv_lnb"]], axis=1))
    wts = (p["w_in_t"], lanes(p["rwkv_mu"]), vec,
           _bf(jnp.swapaxes(p["rwkv_bw"], 1, 2)), _bf(jnp.swapaxes(p["rwkv_ba"], 1, 2)),
           _bf(jnp.swapaxes(p["rwkv_bg"], 1, 2)), _bf(jnp.swapaxes(p["gla_bgk"], 1, 2)),
           lanes(p["gla_bgk_b"]), lanes(jnp.tile(p["gla_norm_w"], (1, N_GLA))), rot, gam,
           _bf(p["w_out"]), jnp.stack([p["ln1_g"], p["ln1_b"]], axis=1))
    rw_t = jnp.transpose(s_rw, (0, 2, 3, 4, 1))
    gl_t = jnp.transpose(s_gl, (0, 2, 3, 4, 1))
    rt_t = jnp.transpose(s_rt, (0, 2, 3, 4, 1))
    sh_t = jnp.swapaxes(s_sh, 1, 2)
    x = jnp.swapaxes(x_sample, 0, 1).reshape(ts_ * bs, D_MODEL)
    n_rw, n_sh, n_gl, n_rt, n_cv = [], [], [], [], []
    for l in range(L):
        x, sh, rw, gl, rt = _smix_call(l, x, sh_t, rw_t, gl_t, rt_t, wts)
        x, cv = _sffn_call(l, x, s_cv, ffn_w)
        n_sh.append(sh)
        n_rw.append(rw)
        n_gl.append(gl)
        n_rt.append(rt)
        n_cv.append(cv)
    y = jnp.swapaxes(x.reshape(ts_, bs, D_MODEL), 0, 1)
    back = (0, 4, 1, 2, 3)
    return y, (jnp.transpose(jnp.stack(n_rw), back), jnp.swapaxes(jnp.stack(n_sh), 1, 2),
               jnp.transpose(jnp.stack(n_gl), back), jnp.transpose(jnp.stack(n_rt), back),
               jnp.stack(n_cv))


def _pad_to(a, axis, n):
    pad = [(0, 0)] * a.ndim
    pad[axis] = (0, n - a.shape[axis])
    return jnp.pad(a, pad)


def _place(a, axis, segs, total):
    out = []
    pos = 0
    for src, w, dst in segs:
        if dst > pos:
            shp = list(a.shape)
            shp[axis] = dst - pos
            out.append(jnp.zeros(shp, a.dtype))
        out.append(lax.slice_in_dim(a, src, src + w, axis=axis))
        pos = dst + w
    if total > pos:
        shp = list(a.shape)
        shp[axis] = total - pos
        out.append(jnp.zeros(shp, a.dtype))
    return jnp.concatenate(out, axis=axis)


def _in_col_segments():
    g0 = RWKV_COLS
    t0 = RWKV_COLS + GLA_COLS
    segs = [(0, RWKV_COLS, 0)]
    segs += [(g0, GLA_K, RW_P), (g0 + GLA_K, GLA_K, RW_P + GQ_P),
             (g0 + 2 * GLA_K, GLA_V, RW_P + 2 * GQ_P), (g0 + 2 * GLA_K + GLA_V, GLA_V, RW_P + 2 * GQ_P + HW),
             (g0 + 2 * GLA_K + 2 * GLA_V, GLA_LORA, RW_P + 2 * GQ_P + 2 * HW)]
    segs += [(t0 + i * RET_W, RET_W, RW_P + GL_P + i * HW) for i in range(4)]
    return segs


def _prep_weights(w_in, rwkv_mu, rwkv_w0, rwkv_bw, rwkv_a0, rwkv_ba, rwkv_bg, rwkv_kk, rwkv_ka,
                  rwkv_rk, rwkv_lnw, rwkv_lnb, gla_bgk, gla_bgk_b, gla_norm_w, w_out,
                  ln1_g, ln1_b, ln2_g, ln2_b, ffn_up, ffn_conv_w, ffn_conv_b, ffn_down):
    L = w_in.shape[0]
    w1 = _place(_bf(w_in), 2, _in_col_segments(), NP)
    mu = _pad_to(rwkv_mu, 1, RW_P)[:, None, :]
    gnw = _pad_to(jnp.tile(gla_norm_w, (1, N_GLA)), 1, HW)
    vec = jnp.stack([rwkv_w0, rwkv_a0, rwkv_kk, rwkv_ka, rwkv_rk.reshape(L, RWKV_W), rwkv_lnw, rwkv_lnb, gnw], axis=1)
    bw = _bf(_pad_to(rwkv_bw, 1, LANE))
    ba = _bf(_place(rwkv_ba, 1, [(0, LORA_A, LORA_W)], LANE))
    bg = _bf(_pad_to(rwkv_bg, 1, 2 * LANE))
    bgk = _bf(_pad_to(_pad_to(gla_bgk, 1, LANE), 2, GQ_P))
    bgkb = _pad_to(gla_bgk_b, 1, GQ_P)[:, None, :]
    wout = _place(_bf(w_out), 1, [(0, RWKV_W + GLA_V, 0), (RWKV_W + GLA_V, RET_W, 2 * HW)], MIX_P)
    ln1 = jnp.stack([ln1_g, ln1_b], axis=1)
    mixer_w = (w1, mu, vec, bw, ba, bg, bgk, bgkb, wout, ln1)
    cwb = _pad_to(jnp.concatenate([ffn_conv_w, ffn_conv_b[:, None, :]], axis=1), 1, 8)
    ln2 = jnp.stack([ln2_g, ln2_b], axis=1)
    ffn_w = (_bf(ffn_up), cwb, _bf(ffn_down), ln2)
    return mixer_w, ffn_w


def _tables(n_pos, chunk):
    half = HEAD_DIM // 2
    inv = 1.0 / (ROPE_BASE ** jnp.linspace(0.0, 1.0, half, dtype=F32))
    ang = jnp.arange(n_pos).astype(F32)[:, None] * inv[None]
    cos, sin = jnp.cos(ang), jnp.sin(ang)
    cos = jnp.tile(jnp.concatenate([cos, cos], -1), (1, HW // HEAD_DIM))
    sin = jnp.tile(jnp.concatenate([-sin, sin], -1), (1, HW // HEAD_DIM))
    log_gamma = jnp.log(1.0 - jnp.exp2(-5.0 - jnp.arange(N_RET, dtype=F32)))
    i = jnp.arange(chunk, dtype=F32)
    diff = i[:, None] - i[None, :]
    causal = diff >= 0
    dmask = jnp.where(causal, jnp.exp(jnp.where(causal, diff, 0.0) * log_gamma[:, None, None]), 0.0)
    qd = jnp.exp((i + 1.0) * log_gamma[:, None])[..., None]
    kd = jnp.exp((chunk - 1.0 - i) * log_gamma[:, None])[..., None]
    cd = jnp.exp(chunk * log_gamma)[:, None, None]
    qd = jnp.broadcast_to(qd, (N_RET, chunk, HEAD_DIM))
    kd = jnp.broadcast_to(kd, (N_RET, chunk, HEAD_DIM))
    cd = jnp.broadcast_to(cd, (N_RET, SLOTS, HEAD_DIM))
    return cos, sin, dmask, qd, kd, cd


def _run_prompt(x, s_rw, s_sh, s_gl, s_rt, s_cv, mixer_w, ffn_w, *, tt_mix, tt_ffn, chunk):
    T = x.shape[1]
    tabs = _tables(T, chunk)
    shift_rows = _pad_to(_pad_to(s_sh, 2, RW_P)[:, :, None, :], 2, SLOTS)
    conv_rows = _place(s_cv, 2, [(0, CONV_W - 1, SLOTS - (CONV_W - 1))], SLOTS)
    n_rw, n_sh, n_gl, n_rt, n_cv = [], [], [], [], []
    for l in range(DEPTH):
        x, sh, rw, gl, rt = _mixer_call(l, x, shift_rows, s_rw, s_gl, s_rt, tabs, mixer_w, tt=tt_mix, chunk=chunk)
        x, cv = _ffn_call(l, x, conv_rows, ffn_w, tt=tt_ffn)
        n_sh.append(sh[:, SLOTS - 1, :RWKV_COLS])
        n_cv.append(cv[:, SLOTS - (CONV_W - 1):])
        n_rw.append(rw)
        n_gl.append(gl)
        n_rt.append(rt)
    return x, (jnp.stack(n_rw), jnp.stack(n_sh), jnp.stack(n_gl), jnp.stack(n_rt), jnp.stack(n_cv))


def kernel(x_prompt, x_sample, state_rwkv, state_shift, state_gla, state_ret, state_conv, w_in, rwkv_mu, rwkv_w0, rwkv_bw, rwkv_a0, rwkv_ba, rwkv_bg, rwkv_kk, rwkv_ka, rwkv_rk, rwkv_lnw, rwkv_lnb, gla_bgk, gla_bgk_b, gla_norm_w, w_out, ln1_g, ln1_b, ln2_g, ln2_b, ffn_up, ffn_conv_w, ffn_conv_b, ffn_down):
    w_in_t = _bf(jnp.swapaxes(w_in, 1, 2))
    mixer_w, ffn_w = _prep_weights(jnp.swapaxes(w_in_t, 1, 2), rwkv_mu, rwkv_w0, rwkv_bw, rwkv_a0, rwkv_ba, rwkv_bg, rwkv_kk, rwkv_ka,
                                   rwkv_rk, rwkv_lnw, rwkv_lnb, gla_bgk, gla_bgk_b, gla_norm_w, w_out,
                                   ln1_g, ln1_b, ln2_g, ln2_b, ffn_up, ffn_conv_w, ffn_conv_b, ffn_down)
    bp, tp, _ = x_prompt.shape
    bs, ts, _ = x_sample.shape
    assert ts == DEC_SEQ

    def zeros_like_state(s):
        return jnp.zeros((s.shape[0], bp) + s.shape[2:], F32)

    chunk = math.gcd(tp, CHUNK)
    y_p, st_p = _run_prompt(
        x_prompt, zeros_like_state(state_rwkv), zeros_like_state(state_shift), zeros_like_state(state_gla),
        zeros_like_state(state_ret), zeros_like_state(state_conv), mixer_w, ffn_w,
        tt_mix=min(MIX_ROWS, tp), tt_ffn=min(FFN_ROWS, tp), chunk=chunk)

    raw = dict(w_in_t=w_in_t, rwkv_mu=rwkv_mu, rwkv_w0=rwkv_w0, rwkv_bw=rwkv_bw, rwkv_a0=rwkv_a0, rwkv_ba=rwkv_ba,
               rwkv_bg=rwkv_bg, rwkv_kk=rwkv_kk, rwkv_ka=rwkv_ka, rwkv_rk=rwkv_rk, rwkv_lnw=rwkv_lnw,
               rwkv_lnb=rwkv_lnb, gla_bgk=gla_bgk, gla_bgk_b=gla_bgk_b, gla_norm_w=gla_norm_w, w_out=w_out,
               ln1_g=ln1_g, ln1_b=ln1_b)
    y_s, st_s = _run_sample(x_sample, state_rwkv, state_shift, state_gla, state_ret, state_conv, raw, ffn_w)
    return (y_p, y_s) + st_p + st_s
```

```python
import functools
import math

import jax
import jax.numpy as jnp
from jax import lax
from jax.experimental import pallas as pl
from jax.experimental.pallas import tpu as pltpu

F32 = jnp.float32
BF16 = jnp.bfloat16

D_MODEL = 1024
DEPTH = 4
PAST_LEN = 16384
DEC_SEQ = 4
HEAD_DIM = 64
N_HEADS = D_MODEL // HEAD_DIM
N_GLA = (5 * N_HEADS) // 16
N_RET = (5 * N_HEADS) // 16
N_RWKV = N_HEADS - N_GLA - N_RET
RWKV_W = N_RWKV * HEAD_DIM
GLA_DK = HEAD_DIM // 2
GLA_K = N_GLA * GLA_DK
GLA_V = N_GLA * HEAD_DIM
RET_W = N_RET * HEAD_DIM
LORA_W = 64
LORA_A = 64
LORA_G = 160
GLA_LORA = 16
GLA_GATE_NORM = 16.0
CHUNK = 64
D_FF = 2816
CONV_W = 3
ALPHA = (2 * DEPTH) ** 0.25
RWKV_GN_EPS = 64e-5
LN_EPS = 1e-5
RMS_EPS = 1e-6
ROPE_BASE = 10000.0
RWKV_COLS = 3 * RWKV_W + LORA_W + LORA_A + LORA_G
GLA_COLS = 2 * GLA_K + 2 * GLA_V + GLA_LORA
RET_COLS = 4 * RET_W

LANE = 128
HW = 384
RW_P = 1536
GQ_P = 256
GL_P = 2 * GQ_P + 2 * HW + LANE
RT_P = 4 * HW
NP = RW_P + GL_P + RT_P
MIX_P = 3 * HW
SLOTS = 8

VMEM_LIMIT = 56 * 1024 * 1024
MIX_ROWS = 512
FFN_ROWS = 256


_NN = (((1,), (0,)), ((), ()))
_NT = (((1,), (1,)), ((), ()))
_TN = (((0,), (0,)), ((), ()))


def _bf(x):
    return x.astype(BF16)


def _dg(a, b, dims):
    return lax.dot_general(a, b, dims, preferred_element_type=F32)


def _dot(a, b, dims=_NN):
    return _dg(_bf(a), _bf(b), dims)


def _split(x, n):
    parts = []
    r = x
    for i in range(n):
        h = _bf(r)
        parts.append(h)
        if i + 1 < n:
            r = r - h.astype(F32)
    return parts


def _dot_sel(a, sel_bf, n=2, dims=_NN):
    out = None
    for h in _split(a, n):
        t = _dg(h, sel_bf, dims)
        out = t if out is None else out + t
    return out


def _sel_dot(sel_bf, b, n=3, dims=_NN):
    out = None
    for h in _split(b, n):
        t = _dg(sel_bf, h, dims)
        out = t if out is None else out + t
    return out


def _layer_norm(h, g, b):
    mu = jnp.mean(h, -1, keepdims=True)
    d = h - mu
    var = jnp.mean(d * d, -1, keepdims=True)
    return d * lax.rsqrt(var + LN_EPS) * g + b


def _swap_halves(x):
    pieces = []
    for i in range(x.shape[1] // LANE):
        p = x[:, i * LANE:(i + 1) * LANE]
        up = pltpu.roll(p, LANE - HEAD_DIM // 2, axis=1)
        dn = pltpu.roll(p, HEAD_DIM // 2, axis=1)
        lane = lax.broadcasted_iota(jnp.int32, p.shape, 1)
        pieces.append(jnp.where((lane % HEAD_DIM) < HEAD_DIM // 2, up, dn))
    return jnp.concatenate(pieces, axis=1)


def _mixer_kernel(x_ref, shift_ref, srw_ref, sgl_ref, srt_ref, cos_ref, sin_ref,
                  dmask_ref, qd_ref, kd_ref, cd_ref,
                  w1_ref, mu_ref, vec_ref, bw_ref, ba_ref, bg_ref, bgk_ref, bgkb_ref,
                  wout_ref, ln_ref,
                  x1_ref, shift_out_ref, srw_out, sgl_out, srt_out,
                  carry_ref, rw_r, rw_k, rw_v, rw_lw, rw_a, rw_b, rw_g, rw_bonus, rw_y,
                  gl_q, gl_k, gl_v, gl_lg, gl_gate, gl_o,
                  rt_q, rt_k, rt_v, rt_gate, rt_o,
                  *, rows, chunk, group):
    R, C = rows, chunk
    G = group * C
    j = pl.program_id(1)

    @pl.when(j == 0)
    def _():
        srw_out[...] = srw_ref[...]
        sgl_out[...] = sgl_ref[...]
        srt_out[...] = srt_ref[...]

    x = x_ref[0]
    xb = _bf(x)
    row = lax.broadcasted_iota(jnp.int32, (R, 1), 0)
    hi_ = lax.broadcasted_iota(jnp.int32, (2 * LANE, 2 * LANE), 0) // HEAD_DIM
    hj_ = lax.broadcasted_iota(jnp.int32, (2 * LANE, 2 * LANE), 1) // HEAD_DIM
    head_ones = (hi_ == hj_).astype(BF16)

    def head_sum(v):
        return jnp.concatenate([_dg(_bf(v[:, 0:2 * LANE]), head_ones, _NN),
                                _dg(_bf(v[:, 2 * LANE:HW]), head_ones[0:LANE, 0:LANE], _NN)], axis=1)

    p = _dg(xb, w1_ref[:, 0:RW_P], _NN)
    first = jnp.where(j == 0, shift_ref[0, 0:1, :], carry_ref[SLOTS - 1:SLOTS, :])
    prev = jnp.where(row == 0, first, pltpu.roll(p, 1, axis=0))
    carry_ref[...] = p[R - SLOTS:R, :]
    shift_out_ref[0] = p[R - SLOTS:R, :]
    pm = p + (prev - p) * mu_ref[...]
    r = pm[:, 0:HW]
    k = pm[:, HW:2 * HW]
    v = pm[:, 2 * HW:3 * HW]
    wa = pm[:, 3 * HW:3 * HW + LANE]
    xg = pm[:, 3 * HW + LANE:RW_P]
    w0 = vec_ref[0:1, :]
    a0 = vec_ref[1:2, :]
    k_k = vec_ref[2:3, :]
    k_a = vec_ref[3:4, :]
    r_k = vec_ref[4:5, :]
    w_log = -jax.nn.softplus(-(w0 + _dg(_bf(jnp.tanh(wa)), bw_ref[...], _NN))) - 0.5
    log_decay = -jnp.exp(w_log)
    a = jax.nn.sigmoid(a0 + _dg(_bf(wa), ba_ref[...], _NN))
    g = _dg(_bf(jax.nn.sigmoid(xg)), bg_ref[...], _NN)
    kk = k * k_k
    kk = kk / jnp.maximum(jnp.sqrt(head_sum(kk * kk)), 1e-12)
    k2 = k * (1.0 + (a - 1.0) * k_a)
    rw_r[...] = r
    rw_k[...] = k2
    rw_v[...] = v
    rw_lw[...] = log_decay
    rw_a[...] = -kk
    rw_b[...] = kk * a
    rw_g[...] = g
    rw_bonus[...] = head_sum(r * k2 * r_k) * v

    p = _dg(xb, w1_ref[:, RW_P:RW_P + GL_P], _NN)
    gk = p[:, 2 * GQ_P + 2 * HW:GL_P]
    lg = jax.nn.log_sigmoid(_dg(_bf(gk), bgk_ref[...], _NN) + bgkb_ref[...]) / GLA_GATE_NORM
    gl_q[...] = p[:, 0:GQ_P] * GLA_DK ** -0.5
    gl_k[...] = p[:, GQ_P:2 * GQ_P]
    gl_v[...] = p[:, 2 * GQ_P:2 * GQ_P + HW]
    gl_lg[...] = lg
    gl_gate[...] = jax.nn.silu(p[:, 2 * GQ_P + HW:2 * GQ_P + 2 * HW])

    p = _dg(xb, w1_ref[:, RW_P + GL_P:NP], _NN)
    cos = cos_ref[...]
    sin = sin_ref[...]
    q = p[:, 0:HW]
    k = p[:, HW:2 * HW]
    rt_q[...] = q * cos + _swap_halves(q) * sin
    rt_k[...] = (k * cos + _swap_halves(k) * sin) * HEAD_DIM ** -0.5
    rt_v[...] = p[:, 2 * HW:3 * HW]
    rt_gate[...] = jax.nn.silu(p[:, 3 * HW:4 * HW])

    gi = lax.broadcasted_iota(jnp.int32, (G, G), 0)
    gj = lax.broadcasted_iota(jnp.int32, (G, G), 1)
    same_chunk = (gi // C) == (gj // C)
    cum_sel = (same_chunk & (gi >= gj)).astype(BF16)

    def chunk_cumsum(v):
        cum = _sel_dot(cum_sel, v, n=2)
        tot = jnp.concatenate([jnp.broadcast_to(cum[(c + 1) * C - 1:(c + 1) * C, :], (C, v.shape[1]))
                               for c in range(group)], axis=0)
        return cum, tot
    ii = lax.broadcasted_iota(jnp.int32, (C, C), 0)
    jj = lax.broadcasted_iota(jnp.int32, (C, C), 1)
    tri_incl = ii >= jj
    tri_strict = ii > jj
    eye = (ii == jj).astype(F32)
    levels = []
    s = 1
    while s < C:
        levels.append(((ii // (2 * s)) == (jj // (2 * s))) & ((ii % (2 * s)) >= s) & ((jj % (2 * s)) < s))
        s *= 2
    ones_cv = jnp.ones((C, HEAD_DIM), BF16)
    chunks = range(group)

    def group_body(gidx, carry):
        g0 = pl.multiple_of(gidx * G, G)
        gs = pl.ds(g0, G)

        def cr(c):
            return slice(c * C, (c + 1) * C)

        lw = rw_lw[gs, :]
        cum, tot = chunk_cumsum(lw)
        w_inv = jnp.exp(-cum)
        w_rem = jnp.exp(tot - cum)
        r_t = _bf(rw_r[gs, :] * jnp.exp(cum))
        a_t = _bf(rw_a[gs, :] * jnp.exp(cum - lw))
        b_t = _bf(rw_b[gs, :] * w_inv)
        k_t = _bf(rw_k[gs, :] * w_inv)
        b_h = _bf(rw_b[gs, :] * w_rem)
        k_h = _bf(rw_k[gs, :] * w_rem)
        w_c = jnp.exp(tot)
        v_rw = _bf(rw_v[gs, :])
        items = [(c, h) for c in chunks for h in range(N_RWKV)]

        def hs(h):
            return slice(h * HEAD_DIM, (h + 1) * HEAD_DIM)

        ar = [jnp.concatenate([a_t[cr(c), hs(h)], r_t[cr(c), hs(h)]], axis=0) for c, h in items]
        xb_ = [_dot(ar[i], b_t[cr(c), hs(h)], _NT) for i, (c, h) in enumerate(items)]
        xk_ = [_dot(ar[i], k_t[cr(c), hs(h)], _NT) for i, (c, h) in enumerate(items)]
        a_ab = [jnp.where(tri_strict, t[0:C], 0.0) for t in xb_]
        a_ak = [jnp.where(tri_strict, t[0:C], 0.0) for t in xk_]
        p_rb = [jnp.where(tri_incl, t[C:2 * C], 0.0) for t in xb_]
        p_rk = [jnp.where(tri_incl, t[C:2 * C], 0.0) for t in xk_]

        lg = gl_lg[gs, :]
        gcum, gtot = chunk_cumsum(lg)
        q_in = _bf(gl_q[gs, :] * jnp.exp(gcum))
        k_in = _bf(gl_k[gs, :] * jnp.exp(-gcum))
        k_st = _bf(gl_k[gs, :] * jnp.exp(gtot - gcum))
        v_gl = _bf(gl_v[gs, :])
        g_col = []
        for c in chunks:
            parts = _split(lg[cr(c), :], 3)
            g_col.append(jnp.exp(_dg(parts[0], ones_cv, _TN) + _dg(parts[1], ones_cv, _TN)
                                 + _dg(parts[2], ones_cv, _TN)))
        gitems = [(c, h) for c in chunks for h in range(N_GLA)]

        def ks(h):
            return slice(h * GLA_DK, (h + 1) * GLA_DK)

        g_att = [jnp.where(tri_incl, _dot(q_in[cr(c), ks(h)], k_in[cr(c), ks(h)], _NT), 0.0) for c, h in gitems]
        g_kv = [_dot(k_st[cr(c), ks(h)], v_gl[cr(c), hs(h)], _TN) for c, h in gitems]

        q_rt = rt_q[gs, :]
        k_rt = rt_k[gs, :]
        v_rt = _bf(rt_v[gs, :])
        q_rb, k_rb = _bf(q_rt), _bf(k_rt)
        ritems = [(c, h) for c in chunks for h in range(N_RET)]
        r_att = [_dot(q_rb[cr(c), hs(h)], k_rb[cr(c), hs(h)], _NT) * dmask_ref[h] for c, h in ritems]
        r_kv = [_dot(k_rt[cr(c), hs(h)] * kd_ref[h], v_rt[cr(c), hs(h)], _TN) for c, h in ritems]

        m = [eye + jnp.where(levels[0], t, 0.0) for t in a_ab]
        for lvl in levels[1:]:
            t_ = [_dot(jnp.where(lvl, a_ab[i], 0.0), m[i]) for i in range(len(items))]
            m = [m[i] + _dot(m[i], t_[i]) for i in range(len(items))]

        g_av = [_dot(g_att[i], v_gl[cr(c), hs(h)]) for i, (c, h) in enumerate(gitems)]
        r_av = [_dot(r_att[i], v_rt[cr(c), hs(h)]) for i, (c, h) in enumerate(ritems)]

        vv = [v_rw[cr(c), hs(h)] for c, h in items]
        akv = [_dot(a_ak[i], vv[i]) for i in range(len(items))]
        at2 = [_dot(m[i], a_t[cr(c), hs(h)]) for i, (c, h) in enumerate(items)]
        y0 = [_dot(p_rk[i], vv[i]) for i in range(len(items))]
        u0 = [_dot(m[i], akv[i]) for i in range(len(items))]
        gp = [_dot(at2[i], b_h[cr(c), hs(h)], _TN) for i, (c, h) in enumerate(items)]
        hh = [_dot(jnp.concatenate([_bf(u0[i]), vv[i]], axis=0),
                   jnp.concatenate([b_h[cr(c), hs(h)], k_h[cr(c), hs(h)]], axis=0), _TN)
              for i, (c, h) in enumerate(items)]

        cur = [srw_out[0, h] for h in range(N_RWKV)]
        g_cur = [sgl_out[0, h] for h in range(N_GLA)]
        r_cur = [srt_out[0, h] for h in range(N_RET)]
        for c in chunks:
            s0 = list(cur)
            for h in range(N_RWKV):
                i = c * N_RWKV + h
                cur[h] = cur[h] * w_c[c * C:c * C + 1, hs(h)] + _dot(cur[h], gp[i]) + hh[i]
            us = [_dot(jnp.concatenate([_bf(at2[c * N_RWKV + h]), r_t[cr(c), hs(h)]], axis=0), s0[h], _NT)
                  for h in range(N_RWKV)]
            g_o = [g_av[c * N_GLA + h] + _dot(q_in[cr(c), ks(h)], g_cur[h]) for h in range(N_GLA)]
            r_o = [r_av[c * N_RET + h] + _dot(q_rt[cr(c), hs(h)] * qd_ref[h], r_cur[h]) for h in range(N_RET)]
            y = [us[h][C:2 * C] + _dot(p_rb[c * N_RWKV + h], us[h][0:C] + u0[c * N_RWKV + h]) + y0[c * N_RWKV + h]
                 for h in range(N_RWKV)]
            g_cur = [g_cur[h] * g_col[c][ks(h), :] + g_kv[c * N_GLA + h] for h in range(N_GLA)]
            r_cur = [r_cur[h] * cd_ref[h, 0:1, :] + r_kv[c * N_RET + h] for h in range(N_RET)]
            rows_c = pl.ds(g0 + c * C, C)
            rw_y[rows_c, :] = jnp.concatenate(y, axis=1)
            gl_o[rows_c, :] = jnp.concatenate(g_o + [jnp.zeros((C, HW - GLA_V), F32)], axis=1)
            rt_o[rows_c, :] = jnp.concatenate(r_o + [jnp.zeros((C, HW - RET_W), F32)], axis=1)
        for h in range(N_RWKV):
            srw_out[0, h] = cur[h]
        for h in range(N_GLA):
            sgl_out[0, h] = g_cur[h]
        for h in range(N_RET):
            srt_out[0, h] = r_cur[h]
        return carry

    if R == G:
        group_body(0, 0)
    else:
        lax.fori_loop(0, R // G, group_body, 0)

    ln_w = vec_ref[5:6, :]
    ln_b = vec_ref[6:7, :]
    gn_w = vec_ref[7:8, :]
    inv_hd = 1.0 / HEAD_DIM
    y = rw_y[...]
    ym = head_sum(y) * inv_hd
    d = y - ym
    yv = head_sum(d * d) * inv_hd
    ya = (d * lax.rsqrt(yv + RWKV_GN_EPS) * ln_w + ln_b + rw_bonus[...]) * rw_g[...]
    o = gl_o[...]
    ob = o * lax.rsqrt(head_sum(o * o) * inv_hd + RMS_EPS) * gn_w * gl_gate[...]
    o = rt_o[...]
    oc = o * lax.rsqrt(head_sum(o * o) * inv_hd + RMS_EPS) * rt_gate[...]
    mix = _dg(_bf(jnp.concatenate([ya, ob, oc], axis=1)), wout_ref[...], _NN)
    out = _layer_norm(ALPHA * x + mix, ln_ref[0:1, :], ln_ref[1:2, :])
    x1_ref[0] = out


def _const_spec(shape):
    nd = len(shape)
    return pl.BlockSpec(shape, lambda i, j: (0,) * nd, pipeline_mode=pl.Buffered(1))


def _layer_spec(l, shape):
    return pl.BlockSpec((None,) + shape, lambda i, j: (l,) + (0,) * len(shape), pipeline_mode=pl.Buffered(1))


def _mixer_call(l, x, shift_rows, s_rw, s_gl, s_rt, tabs, wts, *, tt, chunk):
    B, T, _ = x.shape
    bb, R = 1, tt
    nb, nt = B, T // tt
    group = tt // chunk
    cos, sin, dmask, qd, kd, cd = tabs
    (w1, mu, vec, bw, ba, bg, bgk, bgkb, wout, ln1) = wts
    C = chunk
    lw = functools.partial(_layer_spec, l)

    in_specs = [
        pl.BlockSpec((bb, tt, D_MODEL), lambda i, j: (i, j, 0)),
        pl.BlockSpec((None, bb, SLOTS, RW_P), lambda i, j: (l, i, 0, 0)),
        pl.BlockSpec((None, bb, N_RWKV, HEAD_DIM, HEAD_DIM), lambda i, j: (l, i, 0, 0, 0)),
        pl.BlockSpec((None, bb, N_GLA, GLA_DK, HEAD_DIM), lambda i, j: (l, i, 0, 0, 0)),
        pl.BlockSpec((None, bb, N_RET, HEAD_DIM, HEAD_DIM), lambda i, j: (l, i, 0, 0, 0)),
        pl.BlockSpec((R, HW), lambda i, j: (j, 0)),
        pl.BlockSpec((R, HW), lambda i, j: (j, 0)),
        _const_spec((N_RET, C, C)),
        _const_spec((N_RET, C, HEAD_DIM)),
        _const_spec((N_RET, C, HEAD_DIM)),
        _const_spec((N_RET, SLOTS, HEAD_DIM)),
        lw((D_MODEL, NP)), lw((1, RW_P)), lw((8, HW)), lw((LANE, HW)), lw((LANE, HW)), lw((2 * LANE, HW)),
        lw((LANE, GQ_P)), lw((1, GQ_P)), lw((MIX_P, D_MODEL)), lw((2, D_MODEL)),
    ]
    out_specs = [
        pl.BlockSpec((bb, tt, D_MODEL), lambda i, j: (i, j, 0)),
        pl.BlockSpec((bb, SLOTS, RW_P), lambda i, j: (i, 0, 0)),
        pl.BlockSpec((bb, N_RWKV, HEAD_DIM, HEAD_DIM), lambda i, j: (i, 0, 0, 0)),
        pl.BlockSpec((bb, N_GLA, GLA_DK, HEAD_DIM), lambda i, j: (i, 0, 0, 0)),
        pl.BlockSpec((bb, N_RET, HEAD_DIM, HEAD_DIM), lambda i, j: (i, 0, 0, 0)),
    ]
    out_shape = [
        jax.ShapeDtypeStruct((B, T, D_MODEL), F32),
        jax.ShapeDtypeStruct((B, SLOTS, RW_P), F32),
        jax.ShapeDtypeStruct(s_rw.shape[1:], F32),
        jax.ShapeDtypeStruct(s_gl.shape[1:], F32),
        jax.ShapeDtypeStruct(s_rt.shape[1:], F32),
    ]
    scratch = [pltpu.VMEM((SLOTS, RW_P), F32)]
    scratch += [pltpu.VMEM((R, HW), F32)] * 9
    scratch += [pltpu.VMEM((R, GQ_P), F32), pltpu.VMEM((R, GQ_P), F32), pltpu.VMEM((R, HW), F32),
                pltpu.VMEM((R, GQ_P), F32), pltpu.VMEM((R, HW), F32), pltpu.VMEM((R, HW), F32)]
    scratch += [pltpu.VMEM((R, HW), F32)] * 5
    return pl.pallas_call(
        functools.partial(_mixer_kernel, rows=R, chunk=C, group=group),
        grid=(nb, nt),
        in_specs=in_specs, out_specs=out_specs, out_shape=out_shape,
        scratch_shapes=scratch,
        compiler_params=pltpu.CompilerParams(dimension_semantics=("arbitrary", "arbitrary"),
                                             vmem_limit_bytes=VMEM_LIMIT),
        name="mixer_prompt",
    )(x, shift_rows, s_rw, s_gl, s_rt, cos, sin, dmask, qd, kd, cd,
      w1, mu, vec, bw, ba, bg, bgk, bgkb, wout, ln1)


FFN_CHUNK = 256
FFN_AHEAD = 3


def _ffn_kernel(x_ref, conv_ref, wup_ref, cwb_ref, wdn_ref, ln_ref, y_ref, cv_out_ref, carry_ref, *, rows):
    R = rows
    j = pl.program_id(1)
    x = x_ref[0]
    xb = _bf(x)
    row8 = lax.broadcasted_iota(jnp.int32, (SLOTS, 1), 0)
    n_f = D_FF // FFN_CHUNK

    def cols(f, base):
        return slice(base + f * FFN_CHUNK, base + (f + 1) * FFN_CHUNK)

    def up(f):
        return [_dg(xb, wup_ref[:, cols(f, base)], _NN) for base in (0, D_FF)]

    def conv(u, cs):
        prev = jnp.where(j == 0, conv_ref[0, :, cs], carry_ref[:, cs])
        u1 = pltpu.roll(u, 1, axis=0)
        u2 = pltpu.roll(u, 2, axis=0)
        h1 = jnp.where(row8 == 0, prev[SLOTS - 1:SLOTS], u1[0:SLOTS])
        h2 = jnp.where(row8 == 0, prev[SLOTS - 2:SLOTS - 1],
                       jnp.where(row8 == 1, prev[SLOTS - 1:SLOTS], u2[0:SLOTS]))
        u1 = jnp.concatenate([h1, u1[SLOTS:]], axis=0)
        u2 = jnp.concatenate([h2, u2[SLOTS:]], axis=0)
        carry_ref[:, cs] = u[R - SLOTS:R]
        cv_out_ref[0, :, cs] = u[R - SLOTS:R]
        return cwb_ref[3:4, cs] + (cwb_ref[0:1, cs] * u2 + cwb_ref[1:2, cs] * u1 + cwb_ref[2:3, cs] * u)

    acc = jnp.zeros((R, D_MODEL), F32)
    u_queue = [up(f) for f in range(min(FFN_AHEAD, n_f))]
    for f in range(n_f):
        if f + FFN_AHEAD < n_f:
            u_queue.append(up(f + FFN_AHEAD))
        u_cur = u_queue.pop(0)
        hid = jax.nn.gelu(conv(u_cur[0], cols(f, 0))) * conv(u_cur[1], cols(f, D_FF))
        acc = acc + _dg(_bf(hid), wdn_ref[f * FFN_CHUNK:(f + 1) * FFN_CHUNK, :], _NN)
    y_ref[0] = _layer_norm(ALPHA * x + acc, ln_ref[0:1, :], ln_ref[1:2, :])


def _ffn_call(l, x, conv_rows, wts, *, tt):
    B, T, _ = x.shape
    bb, nb, nt = 1, B, T // tt
    wup, cwb, wdn, ln2 = wts
    lw = functools.partial(_layer_spec, l)
    return pl.pallas_call(
        functools.partial(_ffn_kernel, rows=tt),
        grid=(nb, nt),
        in_specs=[
            pl.BlockSpec((bb, tt, D_MODEL), lambda i, j: (i, j, 0)),
            pl.BlockSpec((None, bb, SLOTS, 2 * D_FF), lambda i, j: (l, i, 0, 0)),
            lw((D_MODEL, 2 * D_FF)), lw((8, 2 * D_FF)), lw((D_FF, D_MODEL)), lw((2, D_MODEL)),
        ],
        out_specs=[
            pl.BlockSpec((bb, tt, D_MODEL), lambda i, j: (i, j, 0)),
            pl.BlockSpec((bb, SLOTS, 2 * D_FF), lambda i, j: (i, 0, 0)),
        ],
        out_shape=[jax.ShapeDtypeStruct((B, T, D_MODEL), F32),
                   jax.ShapeDtypeStruct((B, SLOTS, 2 * D_FF), F32)],
        scratch_shapes=[pltpu.VMEM((SLOTS, 2 * D_FF), F32)],
        compiler_params=pltpu.CompilerParams(dimension_semantics=("arbitrary", "arbitrary"),
                                             vmem_limit_bytes=VMEM_LIMIT),
        name="ffn_prompt",
    )(x, conv_rows, wup, cwb, wdn, ln2)


N_STEPS = N_RWKV + N_GLA + N_RET
G0 = RWKV_COLS
T0 = RWKV_COLS + GLA_COLS


def _smix_kernel(x_ref, shift_ref, srw_ref, sgl_ref, srt_ref, w1t_ref, mu_ref, vec_ref,
                 bwt_ref, bat_ref, bgt_ref, bgkt_ref, gb_ref, gn_ref, rot_ref, gam_ref, wout_ref, ln_ref,
                 x1_ref, shift_out_ref, srw_out, sgl_out, srt_out,
                 rw_r, rw_k, rw_v, rw_w, rw_a, rw_b, rw_g, rw_bonus, rw_y,
                 gl_q, gl_k, gl_v, gl_g, gl_gate, gl_o, rt_q, rt_k, rt_v, rt_gate, rt_o, *, nb):
    s = pl.program_id(0)
    M = DEC_SEQ * nb

    def ts(t):
        return slice(t * nb, (t + 1) * nb)

    def tile_t(c):
        return jnp.concatenate([c] * DEC_SEQ, axis=1)

    def hrows(h):
        return slice(h * HEAD_DIM, (h + 1) * HEAD_DIM)

    @pl.when(s == 0)
    def _():
        xb = _bf(x_ref[...])

        def proj(r0, r1):
            return _dg(w1t_ref[r0:r1, :], xb, _NT)

        p = proj(0, RWKV_COLS)
        shift_out_ref[...] = p[:, (DEC_SEQ - 1) * nb:]
        prev = jnp.concatenate([shift_ref[...], p[:, :(DEC_SEQ - 1) * nb]], axis=1)
        pm = p + (prev - p) * tile_t(mu_ref[...])
        r = pm[0:RWKV_W]
        k = pm[RWKV_W:2 * RWKV_W]
        v = pm[2 * RWKV_W:3 * RWKV_W]
        xw = pm[3 * RWKV_W:3 * RWKV_W + LORA_W]
        xa = pm[3 * RWKV_W + LORA_W:3 * RWKV_W + LORA_W + LORA_A]
        xg = pm[3 * RWKV_W + LORA_W + LORA_A:RWKV_COLS]
        w0, a0, k_k, k_a, r_k = (tile_t(vec_ref[i]) for i in range(5))
        w_log = -jax.nn.softplus(-(w0 + _dg(bwt_ref[...], _bf(jnp.tanh(xw)), _NN))) - 0.5
        a = jax.nn.sigmoid(a0 + _dg(bat_ref[...], _bf(xa), _NN))
        kk = k * k_k
        k2 = k * (1.0 + (a - 1.0) * k_a)
        rk2 = r * k2 * r_k
        for h in range(N_RWKV):
            hs = hrows(h)
            kh = kk[hs]
            kh = kh / jnp.maximum(jnp.sqrt(jnp.sum(kh * kh, axis=0, keepdims=True)), 1e-12)
            rw_a[hs, :] = -kh
            rw_b[hs, :] = kh * a[hs]
            rw_bonus[hs, :] = jnp.sum(rk2[hs], axis=0, keepdims=True) * v[hs]
        rw_r[...] = r
        rw_k[...] = k2
        rw_v[...] = v
        rw_w[...] = jnp.exp(-jnp.exp(w_log))
        rw_g[...] = _dg(bgt_ref[...], _bf(jax.nn.sigmoid(xg)), _NN)

        p = proj(G0, T0)
        gk = p[2 * GLA_K + 2 * GLA_V:GLA_COLS]
        lg = jax.nn.log_sigmoid(_dg(bgkt_ref[...], _bf(gk), _NN) + tile_t(gb_ref[...])) / GLA_GATE_NORM
        gl_q[...] = p[0:GLA_K] * GLA_DK ** -0.5
        gl_k[...] = p[GLA_K:2 * GLA_K]
        gl_v[...] = p[2 * GLA_K:2 * GLA_K + GLA_V]
        gl_g[...] = jnp.exp(lg)
        gl_gate[...] = jax.nn.silu(p[2 * GLA_K + GLA_V:2 * GLA_K + 2 * GLA_V])

        p = proj(T0, T0 + RET_COLS)
        cos = jnp.concatenate([rot_ref[0, t] for t in range(DEC_SEQ)], axis=1)
        sin = jnp.concatenate([rot_ref[1, t] for t in range(DEC_SEQ)], axis=1)
        half = HEAD_DIM // 2

        def rot(xh):
            x1, x2 = xh[0:half], xh[half:HEAD_DIM]
            return jnp.concatenate([x1 * cos - x2 * sin, x1 * sin + x2 * cos], axis=0)

        for h in range(N_RET):
            hs = hrows(h)
            rt_q[hs, :] = rot(p[hs])
            rt_k[hs, :] = rot(p[RET_W + h * HEAD_DIM:RET_W + (h + 1) * HEAD_DIM]) * HEAD_DIM ** -0.5
        rt_v[...] = p[2 * RET_W:3 * RET_W]
        rt_gate[...] = jax.nn.silu(p[3 * RET_W:4 * RET_W])

    @pl.when(s < N_RWKV)
    def _():
        r0 = pl.multiple_of(s * HEAD_DIM, HEAD_DIM)
        hs = pl.ds(r0, HEAD_DIM)

        def v_group(i, carry):
            v0 = pl.multiple_of(i * 8, 8)
            vt = [rw_v[pl.ds(r0 + v0, 8), ts(t)] for t in range(DEC_SEQ)]
            ys = [[] for _ in range(DEC_SEQ)]
            for j in range(8):
                S = srw_ref[v0 + j]
                for t in range(DEC_SEQ):
                    sa = jnp.sum(S * rw_a[hs, ts(t)], axis=0, keepdims=True)
                    S = S * rw_w[hs, ts(t)] + sa * rw_b[hs, ts(t)] + vt[t][j:j + 1, :] * rw_k[hs, ts(t)]
                    ys[t].append(jnp.sum(S * rw_r[hs, ts(t)], axis=0, keepdims=True))
                srw_out[v0 + j] = S
            for t in range(DEC_SEQ):
                rw_y[pl.ds(r0 + v0, 8), ts(t)] = jnp.concatenate(ys[t], axis=0)
            return carry

        lax.fori_loop(0, HEAD_DIM // 8, v_group, 0)

    def kv_head(s_in, s_out, q_ref, k_ref, v_ref, o_ref, decay_rows, k0, v0, nk):
        for t in range(DEC_SEQ):
            src = s_in if t == 0 else s_out
            v_t = v_ref[pl.ds(v0, HEAD_DIM), ts(t)]

            def k_group(i, o, t=t, src=src, v_t=v_t):
                kg = pl.multiple_of(i * 8, 8)
                q8 = q_ref[pl.ds(k0 + kg, 8), ts(t)]
                k8 = k_ref[pl.ds(k0 + kg, 8), ts(t)]
                d8 = decay_rows(kg, t)
                for j in range(8):
                    S = src[kg + j] * d8[j:j + 1, :] + k8[j:j + 1, :] * v_t
                    s_out[kg + j] = S
                    o = o + q8[j:j + 1, :] * S
                return o

            o_ref[pl.ds(v0, HEAD_DIM), ts(t)] = lax.fori_loop(0, nk // 8, k_group, jnp.zeros((HEAD_DIM, nb), F32))

    @pl.when((s >= N_RWKV) & (s < N_RWKV + N_GLA))
    def _():
        h = s - N_RWKV
        k0 = pl.multiple_of(h * GLA_DK, GLA_DK)
        v0 = pl.multiple_of(h * HEAD_DIM, HEAD_DIM)
        kv_head(sgl_ref, sgl_out, gl_q, gl_k, gl_v, gl_o,
                lambda kg, t: gl_g[pl.ds(k0 + kg, 8), ts(t)], k0, v0, GLA_DK)

    @pl.when(s >= N_RWKV + N_GLA)
    def _():
        h = s - (N_RWKV + N_GLA)
        v0 = pl.multiple_of(h * HEAD_DIM, HEAD_DIM)
        gamma = gam_ref[h]
        kv_head(srt_ref, srt_out, rt_q, rt_k, rt_v, rt_o, lambda kg, t: gamma, v0, v0, HEAD_DIM)

    @pl.when(s == N_STEPS - 1)
    def _():
        ln_w, ln_b = tile_t(vec_ref[5]), tile_t(vec_ref[6])
        gn_w = tile_t(gn_ref[...])
        parts = []
        for h in range(N_RWKV):
            hs = hrows(h)
            y = rw_y[hs, :]
            d = y - jnp.mean(y, axis=0, keepdims=True)
            yv = jnp.mean(d * d, axis=0, keepdims=True)
            parts.append((d * lax.rsqrt(yv + RWKV_GN_EPS) * ln_w[hs] + ln_b[hs] + rw_bonus[hs, :]) * rw_g[hs, :])
        for h in range(N_GLA):
            hs = hrows(h)
            o = gl_o[hs, :]
            parts.append(o * lax.rsqrt(jnp.mean(o * o, axis=0, keepdims=True) + RMS_EPS) * gn_w[hs] * gl_gate[hs, :])
        for h in range(N_RET):
            hs = hrows(h)
            o = rt_o[hs, :]
            parts.append(o * lax.rsqrt(jnp.mean(o * o, axis=0, keepdims=True) + RMS_EPS) * rt_gate[hs, :])
        mix_t = _bf(jnp.concatenate(parts, axis=0))
        mix = _dg(mix_t, wout_ref[...], _TN)
        x1_ref[...] = _layer_norm(ALPHA * x_ref[...] + mix, ln_ref[0:1, :], ln_ref[1:2, :])


def _smix_call(l, x, shift_t, s_rw, s_gl, s_rt, wts):
    M = x.shape[0]
    nb = M // DEC_SEQ
    (w1t, mu, vec, bwt, bat, bgt, bgkt, gb, gn, rot, gam, wout, ln1) = wts

    def once(shape, idx=()):
        nd = len(shape)
        return pl.BlockSpec(shape, lambda s: idx + (0,) * (nd - len(idx)), pipeline_mode=pl.Buffered(1))

    def layer(shape):
        return pl.BlockSpec((None,) + shape, lambda s: (l,) + (0,) * len(shape), pipeline_mode=pl.Buffered(1))

    def head_in(shape, first, n):
        return pl.BlockSpec((None, None) + shape,
                            lambda s: (l, jnp.clip(s - first, 0, n - 1)) + (0,) * len(shape))

    def head_out(shape, first, n):
        return pl.BlockSpec((None,) + shape, lambda s: (jnp.clip(s - first, 0, n - 1),) + (0,) * len(shape))

    rw_blk = (HEAD_DIM, HEAD_DIM, nb)
    gl_blk = (GLA_DK, HEAD_DIM, nb)
    in_specs = [
        once((M, D_MODEL)), layer((RWKV_COLS, nb)),
        head_in(rw_blk, 0, N_RWKV), head_in(gl_blk, N_RWKV, N_GLA), head_in(rw_blk, N_RWKV + N_GLA, N_RET),
        layer((RWKV_COLS + GLA_COLS + RET_COLS, D_MODEL)), layer((RWKV_COLS, nb)), layer((7, RWKV_W, nb)),
        layer((RWKV_W, LORA_W)), layer((RWKV_W, LORA_A)), layer((RWKV_W, LORA_G)), layer((GLA_K, GLA_LORA)),
        layer((GLA_K, nb)), layer((GLA_V, nb)), once((2, DEC_SEQ, HEAD_DIM // 2, nb)), once((N_RET, 8, nb)),
        layer((D_MODEL, D_MODEL)), layer((2, D_MODEL)),
    ]
    out_specs = [
        once((M, D_MODEL)), once((RWKV_COLS, nb)),
        head_out(rw_blk, 0, N_RWKV), head_out(gl_blk, N_RWKV, N_GLA), head_out(rw_blk, N_RWKV + N_GLA, N_RET),
    ]
    out_shape = [
        jax.ShapeDtypeStruct((M, D_MODEL), F32), jax.ShapeDtypeStruct((RWKV_COLS, nb), F32),
        jax.ShapeDtypeStruct((N_RWKV,) + rw_blk, F32), jax.ShapeDtypeStruct((N_GLA,) + gl_blk, F32),
        jax.ShapeDtypeStruct((N_RET,) + rw_blk, F32),
    ]
    scratch = ([pltpu.VMEM((RWKV_W, M), F32)] * 9
               + [pltpu.VMEM((GLA_K, M), F32), pltpu.VMEM((GLA_K, M), F32), pltpu.VMEM((GLA_V, M), F32),
                  pltpu.VMEM((GLA_K, M), F32), pltpu.VMEM((GLA_V, M), F32), pltpu.VMEM((GLA_V, M), F32)]
               + [pltpu.VMEM((RET_W, M), F32)] * 5)
    return pl.pallas_call(
        functools.partial(_smix_kernel, nb=nb),
        grid=(N_STEPS,),
        in_specs=in_specs, out_specs=out_specs, out_shape=out_shape, scratch_shapes=scratch,
        compiler_params=pltpu.CompilerParams(dimension_semantics=("arbitrary",), vmem_limit_bytes=VMEM_LIMIT),
        name="mixer_sample",
    )(x, shift_t, s_rw, s_gl, s_rt, w1t, mu, vec, bwt, bat, bgt, bgkt, gb, gn, rot, gam, wout, ln1)


def _sffn_kernel(x_ref, conv_ref, wup_ref, cwb_ref, wdn_ref, ln_ref, y_ref, cv_out_ref, *, nb):
    M = DEC_SEQ * nb
    x = x_ref[...]
    xb = _bf(x)
    n_f = D_FF // FFN_CHUNK

    def cols(f, base):
        return slice(base + f * FFN_CHUNK, base + (f + 1) * FFN_CHUNK)

    def up(f):
        return [_dg(xb, wup_ref[:, cols(f, base)], _NN) for base in (0, D_FF)]

    def conv(u, cs):
        c0 = conv_ref[:, 0, cs]
        c1 = conv_ref[:, 1, cs]
        u1 = jnp.concatenate([c1, u[0:M - nb]], axis=0)
        u2 = jnp.concatenate([c0, c1, u[0:M - 2 * nb]], axis=0)
        cv_out_ref[:, 0, cs] = u[M - 2 * nb:M - nb]
        cv_out_ref[:, 1, cs] = u[M - nb:M]
        return cwb_ref[3:4, cs] + (cwb_ref[0:1, cs] * u2 + cwb_ref[1:2, cs] * u1 + cwb_ref[2:3, cs] * u)

    acc = jnp.zeros((M, D_MODEL), F32)
    u_queue = [up(f) for f in range(min(FFN_AHEAD, n_f))]
    for f in range(n_f):
        if f + FFN_AHEAD < n_f:
            u_queue.append(up(f + FFN_AHEAD))
        u_cur = u_queue.pop(0)
        hid = jax.nn.gelu(conv(u_cur[0], cols(f, 0))) * conv(u_cur[1], cols(f, D_FF))
        acc = acc + _dg(_bf(hid), wdn_ref[f * FFN_CHUNK:(f + 1) * FFN_CHUNK, :], _NN)
    y_ref[...] = _layer_norm(ALPHA * x + acc, ln_ref[0:1, :], ln_ref[1:2, :])


def _sffn_call(l, x, conv2, wts):
    M = x.shape[0]
    nb = M // DEC_SEQ
    wup, cwb, wdn, ln2 = wts

    def layer(shape):
        return pl.BlockSpec((None,) + shape, lambda i: (l,) + (0,) * len(shape), pipeline_mode=pl.Buffered(1))

    return pl.pallas_call(
        functools.partial(_sffn_kernel, nb=nb),
        grid=(1,),
        in_specs=[pl.BlockSpec((M, D_MODEL), lambda i: (0, 0)), layer((nb, CONV_W - 1, 2 * D_FF)),
                  layer((D_MODEL, 2 * D_FF)), layer((8, 2 * D_FF)), layer((D_FF, D_MODEL)), layer((2, D_MODEL))],
        out_specs=[pl.BlockSpec((M, D_MODEL), lambda i: (0, 0)),
                   pl.BlockSpec((nb, CONV_W - 1, 2 * D_FF), lambda i: (0, 0, 0))],
        out_shape=[jax.ShapeDtypeStruct((M, D_MODEL), F32),
                   jax.ShapeDtypeStruct((nb, CONV_W - 1, 2 * D_FF), F32)],
        compiler_params=pltpu.CompilerParams(dimension_semantics=("arbitrary",), vmem_limit_bytes=VMEM_LIMIT),
        name="ffn_sample",
    )(x, conv2, wup, cwb, wdn, ln2)


def _run_sample(x_sample, s_rw, s_sh, s_gl, s_rt, s_cv, p, ffn_w):
    bs, ts_, _ = x_sample.shape
    L = DEPTH

    def lanes(a):
        return jnp.broadcast_to(a[..., None], a.shape + (bs,))

    half = HEAD_DIM // 2
    inv = 1.0 / (ROPE_BASE ** jnp.linspace(0.0, 1.0, half, dtype=F32))
    ang = (PAST_LEN + jnp.arange(ts_)).astype(F32)[:, None] * inv[None]
    rot = lanes(jnp.stack([jnp.cos(ang), jnp.sin(ang)]))
    log_gamma = jnp.log(1.0 - jnp.exp2(-5.0 - jnp.arange(N_RET, dtype=F32)))
    gam = jnp.broadcast_to(jnp.exp(log_gamma)[:, None, None], (N_RET, 8, bs))
    vec = lanes(jnp.stack([p["rwkv_w0"], p["rwkv_a0"], p["rwkv_kk"], p["rwkv_ka"],
                           p["rwkv_rk"].reshape(L, RWKV_W), p["rwkv_lnw"], p["rwkv_lnb"]], axis=1))
    wts = (p["w_in_t"], lanes(p["rwkv_mu"]), vec,
           _bf(jnp.swapaxes(p["rwkv_bw"], 1, 2)), _bf(jnp.swapaxes(p["rwkv_ba"], 1, 2)),
           _bf(jnp.swapaxes(p["rwkv_bg"], 1, 2)), _bf(jnp.swapaxes(p["gla_bgk"], 1, 2)),
           lanes(p["gla_bgk_b"]), lanes(jnp.tile(p["gla_norm_w"], (1, N_GLA))), rot, gam,
           _bf(p["w_out"]), jnp.stack([p["ln1_g"], p["ln1_b"]], axis=1))
    rw_t = jnp.transpose(s_rw, (0, 2, 3, 4, 1))
    gl_t = jnp.transpose(s_gl, (0, 2, 3, 4, 1))
    rt_t = jnp.transpose(s_rt, (0, 2, 3, 4, 1))
    sh_t = jnp.swapaxes(s_sh, 1, 2)
    x = jnp.swapaxes(x_sample, 0, 1).reshape(ts_ * bs, D_MODEL)
    n_rw, n_sh, n_gl, n_rt, n_cv = [], [], [], [], []
    for l in range(L):
        x, sh, rw, gl, rt = _smix_call(l, x, sh_t, rw_t, gl_t, rt_t, wts)
        x, cv = _sffn_call(l, x, s_cv, ffn_w)
        n_sh.append(sh)
        n_rw.append(rw)
        n_gl.append(gl)
        n_rt.append(rt)
        n_cv.append(cv)
    y = jnp.swapaxes(x.reshape(ts_, bs, D_MODEL), 0, 1)
    back = (0, 4, 1, 2, 3)
    return y, (jnp.transpose(jnp.stack(n_rw), back), jnp.swapaxes(jnp.stack(n_sh), 1, 2),
               jnp.transpose(jnp.stack(n_gl), back), jnp.transpose(jnp.stack(n_rt), back),
               jnp.stack(n_cv))


def _pad_to(a, axis, n):
    pad = [(0, 0)] * a.ndim
    pad[axis] = (0, n - a.shape[axis])
    return jnp.pad(a, pad)


def _place(a, axis, segs, total):
    out = []
    pos = 0
    for src, w, dst in segs:
        if dst > pos:
            shp = list(a.shape)
            shp[axis] = dst - pos
            out.append(jnp.zeros(shp, a.dtype))
        out.append(lax.slice_in_dim(a, src, src + w, axis=axis))
        pos = dst + w
    if total > pos:
        shp = list(a.shape)
        shp[axis] = total - pos
        out.append(jnp.zeros(shp, a.dtype))
    return jnp.concatenate(out, axis=axis)


def _in_col_segments():
    g0 = RWKV_COLS
    t0 = RWKV_COLS + GLA_COLS
    segs = [(0, RWKV_COLS, 0)]
    segs += [(g0, GLA_K, RW_P), (g0 + GLA_K, GLA_K, RW_P + GQ_P),
             (g0 + 2 * GLA_K, GLA_V, RW_P + 2 * GQ_P), (g0 + 2 * GLA_K + GLA_V, GLA_V, RW_P + 2 * GQ_P + HW),
             (g0 + 2 * GLA_K + 2 * GLA_V, GLA_LORA, RW_P + 2 * GQ_P + 2 * HW)]
    segs += [(t0 + i * RET_W, RET_W, RW_P + GL_P + i * HW) for i in range(4)]
    return segs


def _prep_weights(w_in, rwkv_mu, rwkv_w0, rwkv_bw, rwkv_a0, rwkv_ba, rwkv_bg, rwkv_kk, rwkv_ka,
                  rwkv_rk, rwkv_lnw, rwkv_lnb, gla_bgk, gla_bgk_b, gla_norm_w, w_out,
                  ln1_g, ln1_b, ln2_g, ln2_b, ffn_up, ffn_conv_w, ffn_conv_b, ffn_down):
    L = w_in.shape[0]
    w1 = _place(_bf(w_in), 2, _in_col_segments(), NP)
    mu = _pad_to(rwkv_mu, 1, RW_P)[:, None, :]
    gnw = _pad_to(jnp.tile(gla_norm_w, (1, N_GLA)), 1, HW)
    vec = jnp.stack([rwkv_w0, rwkv_a0, rwkv_kk, rwkv_ka, rwkv_rk.reshape(L, RWKV_W), rwkv_lnw, rwkv_lnb, gnw], axis=1)
    bw = _bf(_pad_to(rwkv_bw, 1, LANE))
    ba = _bf(_place(rwkv_ba, 1, [(0, LORA_A, LORA_W)], LANE))
    bg = _bf(_pad_to(rwkv_bg, 1, 2 * LANE))
    bgk = _bf(_pad_to(_pad_to(gla_bgk, 1, LANE), 2, GQ_P))
    bgkb = _pad_to(gla_bgk_b, 1, GQ_P)[:, None, :]
    wout = _place(_bf(w_out), 1, [(0, RWKV_W + GLA_V, 0), (RWKV_W + GLA_V, RET_W, 2 * HW)], MIX_P)
    ln1 = jnp.stack([ln1_g, ln1_b], axis=1)
    mixer_w = (w1, mu, vec, bw, ba, bg, bgk, bgkb, wout, ln1)
    cwb = _pad_to(jnp.concatenate([ffn_conv_w, ffn_conv_b[:, None, :]], axis=1), 1, 8)
    ln2 = jnp.stack([ln2_g, ln2_b], axis=1)
    ffn_w = (_bf(ffn_up), cwb, _bf(ffn_down), ln2)
    return mixer_w, ffn_w


def _tables(n_pos, chunk):
    half = HEAD_DIM // 2
    inv = 1.0 / (ROPE_BASE ** jnp.linspace(0.0, 1.0, half, dtype=F32))
    ang = jnp.arange(n_pos).astype(F32)[:, None] * inv[None]
    cos, sin = jnp.cos(ang), jnp.sin(ang)
    cos = jnp.tile(jnp.concatenate([cos, cos], -1), (1, HW // HEAD_DIM))
    sin = jnp.tile(jnp.concatenate([-sin, sin], -1), (1, HW // HEAD_DIM))
    log_gamma = jnp.log(1.0 - jnp.exp2(-5.0 - jnp.arange(N_RET, dtype=F32)))
    i = jnp.arange(chunk, dtype=F32)
    diff = i[:, None] - i[None, :]
    causal = diff >= 0
    dmask = jnp.where(causal, jnp.exp(jnp.where(causal, diff, 0.0) * log_gamma[:, None, None]), 0.0)
    qd = jnp.exp((i + 1.0) * log_gamma[:, None])[..., None]
    kd = jnp.exp((chunk - 1.0 - i) * log_gamma[:, None])[..., None]
    cd = jnp.exp(chunk * log_gamma)[:, None, None]
    qd = jnp.broadcast_to(qd, (N_RET, chunk, HEAD_DIM))
    kd = jnp.broadcast_to(kd, (N_RET, chunk, HEAD_DIM))
    cd = jnp.broadcast_to(cd, (N_RET, SLOTS, HEAD_DIM))
    return cos, sin, dmask, qd, kd, cd


def _run_prompt(x, s_rw, s_sh, s_gl, s_rt, s_cv, mixer_w, ffn_w, *, tt_mix, tt_ffn, chunk):
    T = x.shape[1]
    tabs = _tables(T, chunk)
    shift_rows = _pad_to(_pad_to(s_sh, 2, RW_P)[:, :, None, :], 2, SLOTS)
    conv_rows = _place(s_cv, 2, [(0, CONV_W - 1, SLOTS - (CONV_W - 1))], SLOTS)
    n_rw, n_sh, n_gl, n_rt, n_cv = [], [], [], [], []
    for l in range(DEPTH):
        x, sh, rw, gl, rt = _mixer_call(l, x, shift_rows, s_rw, s_gl, s_rt, tabs, mixer_w, tt=tt_mix, chunk=chunk)
        x, cv = _ffn_call(l, x, conv_rows, ffn_w, tt=tt_ffn)
        n_sh.append(sh[:, SLOTS - 1, :RWKV_COLS])
        n_cv.append(cv[:, SLOTS - (CONV_W - 1):])
        n_rw.append(rw)
        n_gl.append(gl)
        n_rt.append(rt)
    return x, (jnp.stack(n_rw), jnp.stack(n_sh), jnp.stack(n_gl), jnp.stack(n_rt), jnp.stack(n_cv))


def kernel(x_prompt, x_sample, state_rwkv, state_shift, state_gla, state_ret, state_conv, w_in, rwkv_mu, rwkv_w0, rwkv_bw, rwkv_a0, rwkv_ba, rwkv_bg, rwkv_kk, rwkv_ka, rwkv_rk, rwkv_lnw, rwkv_lnb, gla_bgk, gla_bgk_b, gla_norm_w, w_out, ln1_g, ln1_b, ln2_g, ln2_b, ffn_up, ffn_conv_w, ffn_conv_b, ffn_down):
    w_in_t = _bf(jnp.swapaxes(w_in, 1, 2))
    mixer_w, ffn_w = _prep_weights(jnp.swapaxes(w_in_t, 1, 2), rwkv_mu, rwkv_w0, rwkv_bw, rwkv_a0, rwkv_ba, rwkv_bg, rwkv_kk, rwkv_ka,
                                   rwkv_rk, rwkv_lnw, rwkv_lnb, gla_bgk, gla_bgk_b, gla_norm_w, w_out,
                                   ln1_g, ln1_b, ln2_g, ln2_b, ffn_up, ffn_conv_w, ffn_conv_b, ffn_down)
    bp, tp, _ = x_prompt.shape
    bs, ts, _ = x_sample.shape
    assert ts == DEC_SEQ

    def zeros_like_state(s):
        return jnp.zeros((s.shape[0], bp) + s.shape[2:], F32)

    chunk = math.gcd(tp, CHUNK)
    y_p, st_p = _run_prompt(
        x_prompt, zeros_like_state(state_rwkv), zeros_like_state(state_shift), zeros_like_state(state_gla),
        zeros_like_state(state_ret), zeros_like_state(state_conv), mixer_w, ffn_w,
        tt_mix=min(MIX_ROWS, tp), tt_ffn=min(FFN_ROWS, tp), chunk=chunk)

    raw = dict(w_in_t=w_in_t, rwkv_mu=rwkv_mu, rwkv_w0=rwkv_w0, rwkv_bw=rwkv_bw, rwkv_a0=rwkv_a0, rwkv_ba=rwkv_ba,
               rwkv_bg=rwkv_bg, rwkv_kk=rwkv_kk, rwkv_ka=rwkv_ka, rwkv_rk=rwkv_rk, rwkv_lnw=rwkv_lnw,
               rwkv_lnb=rwkv_lnb, gla_bgk=gla_bgk, gla_bgk_b=gla_bgk_b, gla_norm_w=gla_norm_w, w_out=w_out,
               ln1_g=ln1_g, ln1_b=ln1_b)
    y_s, st_s = _run_sample(x_sample, state_rwkv, state_shift, state_gla, state_ret, state_conv, raw, ffn_w)
    return (y_p, y_s) + st_p + st_s
```

```python
import functools
import math

import jax
import jax.numpy as jnp
from jax import lax
from jax.experimental import pallas as pl
from jax.experimental.pallas import tpu as pltpu

F32 = jnp.float32
BF16 = jnp.bfloat16

D_MODEL = 1024
DEPTH = 4
PAST_LEN = 16384
DEC_SEQ = 4
HEAD_DIM = 64
N_HEADS = D_MODEL // HEAD_DIM
N_GLA = (5 * N_HEADS) // 16
N_RET = (5 * N_HEADS) // 16
N_RWKV = N_HEADS - N_GLA - N_RET
RWKV_W = N_RWKV * HEAD_DIM
GLA_DK = HEAD_DIM // 2
GLA_K = N_GLA * GLA_DK
GLA_V = N_GLA * HEAD_DIM
RET_W = N_RET * HEAD_DIM
LORA_W = 64
LORA_A = 64
LORA_G = 160
GLA_LORA = 16
GLA_GATE_NORM = 16.0
CHUNK = 64
D_FF = 2816
CONV_W = 3
ALPHA = (2 * DEPTH) ** 0.25
RWKV_GN_EPS = 64e-5
LN_EPS = 1e-5
RMS_EPS = 1e-6
ROPE_BASE = 10000.0
RWKV_COLS = 3 * RWKV_W + LORA_W + LORA_A + LORA_G
GLA_COLS = 2 * GLA_K + 2 * GLA_V + GLA_LORA
RET_COLS = 4 * RET_W

LANE = 128
HW = 384
RW_P = 1536
GQ_P = 256
GL_P = 2 * GQ_P + 2 * HW + LANE
RT_P = 4 * HW
NP = RW_P + GL_P + RT_P
MIX_P = 3 * HW
SLOTS = 8

VMEM_LIMIT = 56 * 1024 * 1024
MIX_ROWS = 512
FFN_ROWS = 256


_NN = (((1,), (0,)), ((), ()))
_NT = (((1,), (1,)), ((), ()))
_TN = (((0,), (0,)), ((), ()))


def _bf(x):
    return x.astype(BF16)


def _dg(a, b, dims):
    return lax.dot_general(a, b, dims, preferred_element_type=F32)


def _dot(a, b, dims=_NN):
    return _dg(_bf(a), _bf(b), dims)


def _split(x, n):
    parts = []
    r = x
    for i in range(n):
        h = _bf(r)
        parts.append(h)
        if i + 1 < n:
            r = r - h.astype(F32)
    return parts


def _dot_sel(a, sel_bf, n=2, dims=_NN):
    out = None
    for h in _split(a, n):
        t = _dg(h, sel_bf, dims)
        out = t if out is None else out + t
    return out


def _sel_dot(sel_bf, b, n=3, dims=_NN):
    out = None
    for h in _split(b, n):
        t = _dg(sel_bf, h, dims)
        out = t if out is None else out + t
    return out


def _layer_norm(h, g, b):
    mu = jnp.mean(h, -1, keepdims=True)
    d = h - mu
    var = jnp.mean(d * d, -1, keepdims=True)
    return d * lax.rsqrt(var + LN_EPS) * g + b


def _swap_halves(x):
    pieces = []
    for i in range(x.shape[1] // LANE):
        p = x[:, i * LANE:(i + 1) * LANE]
        up = pltpu.roll(p, LANE - HEAD_DIM // 2, axis=1)
        dn = pltpu.roll(p, HEAD_DIM // 2, axis=1)
        lane = lax.broadcasted_iota(jnp.int32, p.shape, 1)
        pieces.append(jnp.where((lane % HEAD_DIM) < HEAD_DIM // 2, up, dn))
    return jnp.concatenate(pieces, axis=1)


def _mixer_kernel(x_ref, shift_ref, srw_ref, sgl_ref, srt_ref, cos_ref, sin_ref,
                  dmask_ref, qd_ref, kd_ref, cd_ref,
                  w1_ref, mu_ref, vec_ref, bw_ref, ba_ref, bg_ref, bgk_ref, bgkb_ref,
                  wout_ref, ln_ref,
                  x1_ref, shift_out_ref, srw_out, sgl_out, srt_out,
                  carry_ref, rw_r, rw_k, rw_v, rw_lw, rw_a, rw_b, rw_g, rw_bonus, rw_y,
                  gl_q, gl_k, gl_v, gl_lg, gl_gate, gl_o,
                  rt_q, rt_k, rt_v, rt_gate, rt_o,
                  *, rows, chunk, group):
    R, C = rows, chunk
    G = group * C
    j = pl.program_id(1)

    @pl.when(j == 0)
    def _():
        srw_out[...] = srw_ref[...]
        sgl_out[...] = sgl_ref[...]
        srt_out[...] = srt_ref[...]

    x = x_ref[0]
    xb = _bf(x)
    row = lax.broadcasted_iota(jnp.int32, (R, 1), 0)
    hi_ = lax.broadcasted_iota(jnp.int32, (2 * LANE, 2 * LANE), 0) // HEAD_DIM
    hj_ = lax.broadcasted_iota(jnp.int32, (2 * LANE, 2 * LANE), 1) // HEAD_DIM
    head_ones = (hi_ == hj_).astype(BF16)

    def head_sum(v):
        return jnp.concatenate([_dg(_bf(v[:, 0:2 * LANE]), head_ones, _NN),
                                _dg(_bf(v[:, 2 * LANE:HW]), head_ones[0:LANE, 0:LANE], _NN)], axis=1)

    p = _dg(xb, w1_ref[:, 0:RW_P], _NN)
    first = jnp.where(j == 0, shift_ref[0, 0:1, :], carry_ref[SLOTS - 1:SLOTS, :])
    prev = jnp.where(row == 0, first, pltpu.roll(p, 1, axis=0))
    carry_ref[...] = p[R - SLOTS:R, :]
    shift_out_ref[0] = p[R - SLOTS:R, :]
    pm = p + (prev - p) * mu_ref[...]
    r = pm[:, 0:HW]
    k = pm[:, HW:2 * HW]
    v = pm[:, 2 * HW:3 * HW]
    wa = pm[:, 3 * HW:3 * HW + LANE]
    xg = pm[:, 3 * HW + LANE:RW_P]
    w0 = vec_ref[0:1, :]
    a0 = vec_ref[1:2, :]
    k_k = vec_ref[2:3, :]
    k_a = vec_ref[3:4, :]
    r_k = vec_ref[4:5, :]
    w_log = -jax.nn.softplus(-(w0 + _dg(_bf(jnp.tanh(wa)), bw_ref[...], _NN))) - 0.5
    log_decay = -jnp.exp(w_log)
    a = jax.nn.sigmoid(a0 + _dg(_bf(wa), ba_ref[...], _NN))
    g = _dg(_bf(jax.nn.sigmoid(xg)), bg_ref[...], _NN)
    kk = k * k_k
    kk = kk / jnp.maximum(jnp.sqrt(head_sum(kk * kk)), 1e-12)
    k2 = k * (1.0 + (a - 1.0) * k_a)
    rw_r[...] = r
    rw_k[...] = k2
    rw_v[...] = v
    rw_lw[...] = log_decay
    rw_a[...] = -kk
    rw_b[...] = kk * a
    rw_g[...] = g
    rw_bonus[...] = head_sum(r * k2 * r_k) * v

    p = _dg(xb, w1_ref[:, RW_P:RW_P + GL_P], _NN)
    gk = p[:, 2 * GQ_P + 2 * HW:GL_P]
    lg = jax.nn.log_sigmoid(_dg(_bf(gk), bgk_ref[...], _NN) + bgkb_ref[...]) / GLA_GATE_NORM
    gl_q[...] = p[:, 0:GQ_P] * GLA_DK ** -0.5
    gl_k[...] = p[:, GQ_P:2 * GQ_P]
    gl_v[...] = p[:, 2 * GQ_P:2 * GQ_P + HW]
    gl_lg[...] = lg
    gl_gate[...] = jax.nn.silu(p[:, 2 * GQ_P + HW:2 * GQ_P + 2 * HW])

    p = _dg(xb, w1_ref[:, RW_P + GL_P:NP], _NN)
    cos = cos_ref[...]
    sin = sin_ref[...]
    q = p[:, 0:HW]
    k = p[:, HW:2 * HW]
    rt_q[...] = q * cos + _swap_halves(q) * sin
    rt_k[...] = (k * cos + _swap_halves(k) * sin) * HEAD_DIM ** -0.5
    rt_v[...] = p[:, 2 * HW:3 * HW]
    rt_gate[...] = jax.nn.silu(p[:, 3 * HW:4 * HW])

    gi = lax.broadcasted_iota(jnp.int32, (G, G), 0)
    gj = lax.broadcasted_iota(jnp.int32, (G, G), 1)
    same_chunk = (gi // C) == (gj // C)
    cum_sel = (same_chunk & (gi >= gj)).astype(BF16)

    def chunk_cumsum(v):
        cum = _sel_dot(cum_sel, v, n=2)
        tot = jnp.concatenate([jnp.broadcast_to(cum[(c + 1) * C - 1:(c + 1) * C, :], (C, v.shape[1]))
                               for c in range(group)], axis=0)
        return cum, tot
    ii = lax.broadcasted_iota(jnp.int32, (C, C), 0)
    jj = lax.broadcasted_iota(jnp.int32, (C, C), 1)
    tri_incl = ii >= jj
    tri_strict = ii > jj
    eye = (ii == jj).astype(F32)
    levels = []
    s = 1
    while s < C:
        levels.append(((ii // (2 * s)) == (jj // (2 * s))) & ((ii % (2 * s)) >= s) & ((jj % (2 * s)) < s))
        s *= 2
    ones_cv = jnp.ones((C, HEAD_DIM), BF16)
    chunks = range(group)

    def group_body(gidx, carry):
        g0 = pl.multiple_of(gidx * G, G)
        gs = pl.ds(g0, G)

        def cr(c):
            return slice(c * C, (c + 1) * C)

        lw = rw_lw[gs, :]
        cum, tot = chunk_cumsum(lw)
        w_inv = jnp.exp(-cum)
        w_rem = jnp.exp(tot - cum)
        r_t = _bf(rw_r[gs, :] * jnp.exp(cum))
        a_t = _bf(rw_a[gs, :] * jnp.exp(cum - lw))
        b_t = _bf(rw_b[gs, :] * w_inv)
        k_t = _bf(rw_k[gs, :] * w_inv)
        b_h = _bf(rw_b[gs, :] * w_rem)
        k_h = _bf(rw_k[gs, :] * w_rem)
        w_c = jnp.exp(tot)
        v_rw = _bf(rw_v[gs, :])
        items = [(c, h) for c in chunks for h in range(N_RWKV)]

        def hs(h):
            return slice(h * HEAD_DIM, (h + 1) * HEAD_DIM)

        ar = [jnp.concatenate([a_t[cr(c), hs(h)], r_t[cr(c), hs(h)]], axis=0) for c, h in items]
        xb_ = [_dot(ar[i], b_t[cr(c), hs(h)], _NT) for i, (c, h) in enumerate(items)]
        xk_ = [_dot(ar[i], k_t[cr(c), hs(h)], _NT) for i, (c, h) in enumerate(items)]
        a_ab = [jnp.where(tri_strict, t[0:C], 0.0) for t in xb_]
        a_ak = [jnp.where(tri_strict, t[0:C], 0.0) for t in xk_]
        p_rb = [jnp.where(tri_incl, t[C:2 * C], 0.0) for t in xb_]
        p_rk = [jnp.where(tri_incl, t[C:2 * C], 0.0) for t in xk_]

        lg = gl_lg[gs, :]
        gcum, gtot = chunk_cumsum(lg)
        q_in = _bf(gl_q[gs, :] * jnp.exp(gcum))
        k_in = _bf(gl_k[gs, :] * jnp.exp(-gcum))
        k_st = _bf(gl_k[gs, :] * jnp.exp(gtot - gcum))
        v_gl = _bf(gl_v[gs, :])
        g_col = []
        for c in chunks:
            parts = _split(lg[cr(c), :], 3)
            g_col.append(jnp.exp(_dg(parts[0], ones_cv, _TN) + _dg(parts[1], ones_cv, _TN)
                                 + _dg(parts[2], ones_cv, _TN)))
        gitems = [(c, h) for c in chunks for h in range(N_GLA)]

        def ks(h):
            return slice(h * GLA_DK, (h + 1) * GLA_DK)

        g_att = [jnp.where(tri_incl, _dot(q_in[cr(c), ks(h)], k_in[cr(c), ks(h)], _NT), 0.0) for c, h in gitems]
        g_kv = [_dot(k_st[cr(c), ks(h)], v_gl[cr(c), hs(h)], _TN) for c, h in gitems]

        q_rt = rt_q[gs, :]
        k_rt = rt_k[gs, :]
        v_rt = _bf(rt_v[gs, :])
        q_rb, k_rb = _bf(q_rt), _bf(k_rt)
        ritems = [(c, h) for c in chunks for h in range(N_RET)]
        r_att = [_dot(q_rb[cr(c), hs(h)], k_rb[cr(c), hs(h)], _NT) * dmask_ref[h] for c, h in ritems]
        r_kv = [_dot(k_rt[cr(c), hs(h)] * kd_ref[h], v_rt[cr(c), hs(h)], _TN) for c, h in ritems]

        m = [eye + jnp.where(levels[0], t, 0.0) for t in a_ab]
        for lvl in levels[1:]:
            t_ = [_dot(jnp.where(lvl, a_ab[i], 0.0), m[i]) for i in range(len(items))]
            m = [m[i] + _dot(m[i], t_[i]) for i in range(len(items))]

        g_av = [_dot(g_att[i], v_gl[cr(c), hs(h)]) for i, (c, h) in enumerate(gitems)]
        r_av = [_dot(r_att[i], v_rt[cr(c), hs(h)]) for i, (c, h) in enumerate(ritems)]

        vv = [v_rw[cr(c), hs(h)] for c, h in items]
        akv = [_dot(a_ak[i], vv[i]) for i in range(len(items))]
        at2 = [_dot(m[i], a_t[cr(c), hs(h)]) for i, (c, h) in enumerate(items)]
        y0 = [_dot(p_rk[i], vv[i]) for i in range(len(items))]
        u0 = [_dot(m[i], akv[i]) for i in range(len(items))]
        gp = [_dot(at2[i], b_h[cr(c), hs(h)], _TN) for i, (c, h) in enumerate(items)]
        hh = [_dot(jnp.concatenate([_bf(u0[i]), vv[i]], axis=0),
                   jnp.concatenate([b_h[cr(c), hs(h)], k_h[cr(c), hs(h)]], axis=0), _TN)
              for i, (c, h) in enumerate(items)]

        cur = [srw_out[0, h] for h in range(N_RWKV)]
        g_cur = [sgl_out[0, h] for h in range(N_GLA)]
        r_cur = [srt_out[0, h] for h in range(N_RET)]
        for c in chunks:
            s0 = list(cur)
            for h in range(N_RWKV):
                i = c * N_RWKV + h
                cur[h] = cur[h] * w_c[c * C:c * C + 1, hs(h)] + _dot(cur[h], gp[i]) + hh[i]
            us = [_dot(jnp.concatenate([_bf(at2[c * N_RWKV + h]), r_t[cr(c), hs(h)]], axis=0), s0[h], _NT)
                  for h in range(N_RWKV)]
            g_o = [g_av[c * N_GLA + h] + _dot(q_in[cr(c), ks(h)], g_cur[h]) for h in range(N_GLA)]
            r_o = [r_av[c * N_RET + h] + _dot(q_rt[cr(c), hs(h)] * qd_ref[h], r_cur[h]) for h in range(N_RET)]
            y = [us[h][C:2 * C] + _dot(p_rb[c * N_RWKV + h], us[h][0:C] + u0[c * N_RWKV + h]) + y0[c * N_RWKV + h]
                 for h in range(N_RWKV)]
            g_cur = [g_cur[h] * g_col[c][ks(h), :] + g_kv[c * N_GLA + h] for h in range(N_GLA)]
            r_cur = [r_cur[h] * cd_ref[h, 0:1, :] + r_kv[c * N_RET + h] for h in range(N_RET)]
            rows_c = pl.ds(g0 + c * C, C)
            rw_y[rows_c, :] = jnp.concatenate(y, axis=1)
            gl_o[rows_c, :] = jnp.concatenate(g_o + [jnp.zeros((C, HW - GLA_V), F32)], axis=1)
            rt_o[rows_c, :] = jnp.concatenate(r_o + [jnp.zeros((C, HW - RET_W), F32)], axis=1)
        for h in range(N_RWKV):
            srw_out[0, h] = cur[h]
        for h in range(N_GLA):
            sgl_out[0, h] = g_cur[h]
        for h in range(N_RET):
            srt_out[0, h] = r_cur[h]
        return carry

    if R == G:
        group_body(0, 0)
    else:
        lax.fori_loop(0, R // G, group_body, 0)

    ln_w = vec_ref[5:6, :]
    ln_b = vec_ref[6:7, :]
    gn_w = vec_ref[7:8, :]
    inv_hd = 1.0 / HEAD_DIM
    y = rw_y[...]
    ym = head_sum(y) * inv_hd
    d = y - ym
    yv = head_sum(d * d) * inv_hd
    ya = (d * lax.rsqrt(yv + RWKV_GN_EPS) * ln_w + ln_b + rw_bonus[...]) * rw_g[...]
    o = gl_o[...]
    ob = o * lax.rsqrt(head_sum(o * o) * inv_hd + RMS_EPS) * gn_w * gl_gate[...]
    o = rt_o[...]
    oc = o * lax.rsqrt(head_sum(o * o) * inv_hd + RMS_EPS) * rt_gate[...]
    mix = _dg(_bf(jnp.concatenate([ya, ob, oc], axis=1)), wout_ref[...], _NN)
    out = _layer_norm(ALPHA * x + mix, ln_ref[0:1, :], ln_ref[1:2, :])
    x1_ref[0] = out


def _const_spec(shape):
    nd = len(shape)
    return pl.BlockSpec(shape, lambda i, j: (0,) * nd, pipeline_mode=pl.Buffered(1))


def _layer_spec(l, shape):
    return pl.BlockSpec((None,) + shape, lambda i, j: (l,) + (0,) * len(shape), pipeline_mode=pl.Buffered(1))


def _mixer_call(l, x, shift_rows, s_rw, s_gl, s_rt, tabs, wts, *, tt, chunk):
    B, T, _ = x.shape
    bb, R = 1, tt
    nb, nt = B, T // tt
    group = tt // chunk
    cos, sin, dmask, qd, kd, cd = tabs
    (w1, mu, vec, bw, ba, bg, bgk, bgkb, wout, ln1) = wts
    C = chunk
    lw = functools.partial(_layer_spec, l)

    in_specs = [
        pl.BlockSpec((bb, tt, D_MODEL), lambda i, j: (i, j, 0)),
        pl.BlockSpec((None, bb, SLOTS, RW_P), lambda i, j: (l, i, 0, 0)),
        pl.BlockSpec((None, bb, N_RWKV, HEAD_DIM, HEAD_DIM), lambda i, j: (l, i, 0, 0, 0)),
        pl.BlockSpec((None, bb, N_GLA, GLA_DK, HEAD_DIM), lambda i, j: (l, i, 0, 0, 0)),
        pl.BlockSpec((None, bb, N_RET, HEAD_DIM, HEAD_DIM), lambda i, j: (l, i, 0, 0, 0)),
        pl.BlockSpec((R, HW), lambda i, j: (j, 0)),
        pl.BlockSpec((R, HW), lambda i, j: (j, 0)),
        _const_spec((N_RET, C, C)),
        _const_spec((N_RET, C, HEAD_DIM)),
        _const_spec((N_RET, C, HEAD_DIM)),
        _const_spec((N_RET, SLOTS, HEAD_DIM)),
        lw((D_MODEL, NP)), lw((1, RW_P)), lw((8, HW)), lw((LANE, HW)), lw((LANE, HW)), lw((2 * LANE, HW)),
        lw((LANE, GQ_P)), lw((1, GQ_P)), lw((MIX_P, D_MODEL)), lw((2, D_MODEL)),
    ]
    out_specs = [
        pl.BlockSpec((bb, tt, D_MODEL), lambda i, j: (i, j, 0)),
        pl.BlockSpec((bb, SLOTS, RW_P), lambda i, j: (i, 0, 0)),
        pl.BlockSpec((bb, N_RWKV, HEAD_DIM, HEAD_DIM), lambda i, j: (i, 0, 0, 0)),
        pl.BlockSpec((bb, N_GLA, GLA_DK, HEAD_DIM), lambda i, j: (i, 0, 0, 0)),
        pl.BlockSpec((bb, N_RET, HEAD_DIM, HEAD_DIM), lambda i, j: (i, 0, 0, 0)),
    ]
    out_shape = [
        jax.ShapeDtypeStruct((B, T, D_MODEL), F32),
        jax.ShapeDtypeStruct((B, SLOTS, RW_P), F32),
        jax.ShapeDtypeStruct(s_rw.shape[1:], F32),
        jax.ShapeDtypeStruct(s_gl.shape[1:], F32),
        jax.ShapeDtypeStruct(s_rt.shape[1:], F32),
    ]
    scratch = [pltpu.VMEM((SLOTS, RW_P), F32)]
    scratch += [pltpu.VMEM((R, HW), F32)] * 9
    scratch += [pltpu.VMEM((R, GQ_P), F32), pltpu.VMEM((R, GQ_P), F32), pltpu.VMEM((R, HW), F32),
                pltpu.VMEM((R, GQ_P), F32), pltpu.VMEM((R, HW), F32), pltpu.VMEM((R, HW), F32)]
    scratch += [pltpu.VMEM((R, HW), F32)] * 5
    return pl.pallas_call(
        functools.partial(_mixer_kernel, rows=R, chunk=C, group=group),
        grid=(nb, nt),
        in_specs=in_specs, out_specs=out_specs, out_shape=out_shape,
        scratch_shapes=scratch,
        compiler_params=pltpu.CompilerParams(dimension_semantics=("arbitrary", "arbitrary"),
                                             vmem_limit_bytes=VMEM_LIMIT),
        name="mixer_prompt",
    )(x, shift_rows, s_rw, s_gl, s_rt, cos, sin, dmask, qd, kd, cd,
      w1, mu, vec, bw, ba, bg, bgk, bgkb, wout, ln1)


FFN_CHUNK = 256
FFN_AHEAD = 3


def _ffn_kernel(x_ref, conv_ref, wup_ref, cwb_ref, wdn_ref, ln_ref, y_ref, cv_out_ref, ubuf, *, rows):
    R = rows
    j = pl.program_id(1)

    @pl.when(j == 0)
    def _():
        ubuf[0:SLOTS, :] = conv_ref[0]

    x = x_ref[0]
    xb = _bf(x)
    n_f = D_FF // FFN_CHUNK

    def cols(f, base):
        return slice(base + f * FFN_CHUNK, base + (f + 1) * FFN_CHUNK)

    def up(f):
        return [_dg(xb, wup_ref[:, cols(f, base)], _NN) for base in (0, D_FF)]

    def conv(u, cs):
        ubuf[SLOTS:SLOTS + R, cs] = u
        u1 = ubuf[SLOTS - 1:SLOTS - 1 + R, cs]
        u2 = ubuf[SLOTS - 2:SLOTS - 2 + R, cs]
        ubuf[0:SLOTS, cs] = u[R - SLOTS:R]
        cv_out_ref[0, :, cs] = u[R - SLOTS:R]
        return cwb_ref[3:4, cs] + (cwb_ref[0:1, cs] * u2 + cwb_ref[1:2, cs] * u1 + cwb_ref[2:3, cs] * u)

    acc = jnp.zeros((R, D_MODEL), F32)
    u_queue = [up(f) for f in range(min(FFN_AHEAD, n_f))]
    for f in range(n_f):
        if f + FFN_AHEAD < n_f:
            u_queue.append(up(f + FFN_AHEAD))
        u_cur = u_queue.pop(0)
        hid = jax.nn.gelu(conv(u_cur[0], cols(f, 0))) * conv(u_cur[1], cols(f, D_FF))
        acc = acc + _dg(_bf(hid), wdn_ref[f * FFN_CHUNK:(f + 1) * FFN_CHUNK, :], _NN)
    y_ref[0] = _layer_norm(ALPHA * x + acc, ln_ref[0:1, :], ln_ref[1:2, :])


def _ffn_call(l, x, conv_rows, wts, *, tt):
    B, T, _ = x.shape
    bb, nb, nt = 1, B, T // tt
    wup, cwb, wdn, ln2 = wts
    lw = functools.partial(_layer_spec, l)
    return pl.pallas_call(
        functools.partial(_ffn_kernel, rows=tt),
        grid=(nb, nt),
        in_specs=[
            pl.BlockSpec((bb, tt, D_MODEL), lambda i, j: (i, j, 0)),
            pl.BlockSpec((None, bb, SLOTS, 2 * D_FF), lambda i, j: (l, i, 0, 0)),
            lw((D_MODEL, 2 * D_FF)), lw((8, 2 * D_FF)), lw((D_FF, D_MODEL)), lw((2, D_MODEL)),
        ],
        out_specs=[
            pl.BlockSpec((bb, tt, D_MODEL), lambda i, j: (i, j, 0)),
            pl.BlockSpec((bb, SLOTS, 2 * D_FF), lambda i, j: (i, 0, 0)),
        ],
        out_shape=[jax.ShapeDtypeStruct((B, T, D_MODEL), F32),
                   jax.ShapeDtypeStruct((B, SLOTS, 2 * D_FF), F32)],
        scratch_shapes=[pltpu.VMEM((SLOTS + tt, 2 * D_FF), F32)],
        compiler_params=pltpu.CompilerParams(dimension_semantics=("arbitrary", "arbitrary"),
                                             vmem_limit_bytes=VMEM_LIMIT),
        name="ffn_prompt",
    )(x, conv_rows, wup, cwb, wdn, ln2)


N_STEPS = N_RWKV + N_GLA + N_RET
G0 = RWKV_COLS
T0 = RWKV_COLS + GLA_COLS


def _smix_kernel(x_ref, shift_ref, srw_ref, sgl_ref, srt_ref, w1t_ref, mu_ref, vec_ref,
                 bwt_ref, bat_ref, bgt_ref, bgkt_ref, gb_ref, gn_ref, rot_ref, gam_ref, wout_ref, ln_ref,
                 x1_ref, shift_out_ref, srw_out, sgl_out, srt_out,
                 rw_r, rw_k, rw_v, rw_w, rw_a, rw_b, rw_g, rw_bonus, rw_y,
                 gl_q, gl_k, gl_v, gl_g, gl_gate, gl_o, rt_q, rt_k, rt_v, rt_gate, rt_o, *, nb):
    s = pl.program_id(0)
    M = DEC_SEQ * nb

    def ts(t):
        return slice(t * nb, (t + 1) * nb)

    def tile_t(c):
        return jnp.concatenate([c] * DEC_SEQ, axis=1)

    def hrows(h):
        return slice(h * HEAD_DIM, (h + 1) * HEAD_DIM)

    @pl.when(s == 0)
    def _():
        xb = _bf(x_ref[...])

        def proj(r0, r1):
            return _dg(w1t_ref[r0:r1, :], xb, _NT)

        p = proj(0, RWKV_COLS)
        shift_out_ref[...] = p[:, (DEC_SEQ - 1) * nb:]
        prev = jnp.concatenate([shift_ref[...], p[:, :(DEC_SEQ - 1) * nb]], axis=1)
        pm = p + (prev - p) * tile_t(mu_ref[...])
        r = pm[0:RWKV_W]
        k = pm[RWKV_W:2 * RWKV_W]
        v = pm[2 * RWKV_W:3 * RWKV_W]
        xw = pm[3 * RWKV_W:3 * RWKV_W + LORA_W]
        xa = pm[3 * RWKV_W + LORA_W:3 * RWKV_W + LORA_W + LORA_A]
        xg = pm[3 * RWKV_W + LORA_W + LORA_A:RWKV_COLS]
        w0, a0, k_k, k_a, r_k = (tile_t(vec_ref[i]) for i in range(5))
        w_log = -jax.nn.softplus(-(w0 + _dg(bwt_ref[...], _bf(jnp.tanh(xw)), _NN))) - 0.5
        a = jax.nn.sigmoid(a0 + _dg(bat_ref[...], _bf(xa), _NN))
        kk = k * k_k
        k2 = k * (1.0 + (a - 1.0) * k_a)
        rk2 = r * k2 * r_k
        for h in range(N_RWKV):
            hs = hrows(h)
            kh = kk[hs]
            kh = kh / jnp.maximum(jnp.sqrt(jnp.sum(kh * kh, axis=0, keepdims=True)), 1e-12)
            rw_a[hs, :] = -kh
            rw_b[hs, :] = kh * a[hs]
            rw_bonus[hs, :] = jnp.sum(rk2[hs], axis=0, keepdims=True) * v[hs]
        rw_r[...] = r
        rw_k[...] = k2
        rw_v[...] = v
        rw_w[...] = jnp.exp(-jnp.exp(w_log))
        rw_g[...] = _dg(bgt_ref[...], _bf(jax.nn.sigmoid(xg)), _NN)

        p = proj(G0, T0)
        gk = p[2 * GLA_K + 2 * GLA_V:GLA_COLS]
        lg = jax.nn.log_sigmoid(_dg(bgkt_ref[...], _bf(gk), _NN) + tile_t(gb_ref[...])) / GLA_GATE_NORM
        gl_q[...] = p[0:GLA_K] * GLA_DK ** -0.5
        gl_k[...] = p[GLA_K:2 * GLA_K]
        gl_v[...] = p[2 * GLA_K:2 * GLA_K + GLA_V]
        gl_g[...] = jnp.exp(lg)
        gl_gate[...] = jax.nn.silu(p[2 * GLA_K + GLA_V:2 * GLA_K + 2 * GLA_V])

        p = proj(T0, T0 + RET_COLS)
        cos = jnp.concatenate([rot_ref[0, t] for t in range(DEC_SEQ)], axis=1)
        sin = jnp.concatenate([rot_ref[1, t] for t in range(DEC_SEQ)], axis=1)
        half = HEAD_DIM // 2

        def rot(xh):
            x1, x2 = xh[0:half], xh[half:HEAD_DIM]
            return jnp.concatenate([x1 * cos - x2 * sin, x1 * sin + x2 * cos], axis=0)

        for h in range(N_RET):
            hs = hrows(h)
            rt_q[hs, :] = rot(p[hs])
            rt_k[hs, :] = rot(p[RET_W + h * HEAD_DIM:RET_W + (h + 1) * HEAD_DIM]) * HEAD_DIM ** -0.5
        rt_v[...] = p[2 * RET_W:3 * RET_W]
        rt_gate[...] = jax.nn.silu(p[3 * RET_W:4 * RET_W])

    @pl.when(s < N_RWKV)
    def _():
        r0 = pl.multiple_of(s * HEAD_DIM, HEAD_DIM)
        hs = pl.ds(r0, HEAD_DIM)

        def v_group(i, carry):
            v0 = pl.multiple_of(i * 8, 8)
            vt = [rw_v[pl.ds(r0 + v0, 8), ts(t)] for t in range(DEC_SEQ)]
            ys = [[] for _ in range(DEC_SEQ)]
            for j in range(8):
                S = srw_ref[v0 + j]
                for t in range(DEC_SEQ):
                    sa = jnp.sum(S * rw_a[hs, ts(t)], axis=0, keepdims=True)
                    S = S * rw_w[hs, ts(t)] + sa * rw_b[hs, ts(t)] + vt[t][j:j + 1, :] * rw_k[hs, ts(t)]
                    ys[t].append(jnp.sum(S * rw_r[hs, ts(t)], axis=0, keepdims=True))
                srw_out[v0 + j] = S
            for t in range(DEC_SEQ):
                rw_y[pl.ds(r0 + v0, 8), ts(t)] = jnp.concatenate(ys[t], axis=0)
            return carry

        lax.fori_loop(0, HEAD_DIM // 8, v_group, 0)

    def kv_head(s_in, s_out, q_ref, k_ref, v_ref, o_ref, decay_rows, k0, v0, nk):
        for t in range(DEC_SEQ):
            src = s_in if t == 0 else s_out
            v_t = v_ref[pl.ds(v0, HEAD_DIM), ts(t)]

            def k_group(i, o, t=t, src=src, v_t=v_t):
                kg = pl.multiple_of(i * 8, 8)
                q8 = q_ref[pl.ds(k0 + kg, 8), ts(t)]
                k8 = k_ref[pl.ds(k0 + kg, 8), ts(t)]
                d8 = decay_rows(kg, t)
                for j in range(8):
                    S = src[kg + j] * d8[j:j + 1, :] + k8[j:j + 1, :] * v_t
                    s_out[kg + j] = S
                    o = o + q8[j:j + 1, :] * S
                return o

            o_ref[pl.ds(v0, HEAD_DIM), ts(t)] = lax.fori_loop(0, nk // 8, k_group, jnp.zeros((HEAD_DIM, nb), F32))

    @pl.when((s >= N_RWKV) & (s < N_RWKV + N_GLA))
    def _():
        h = s - N_RWKV
        k0 = pl.multiple_of(h * GLA_DK, GLA_DK)
        v0 = pl.multiple_of(h * HEAD_DIM, HEAD_DIM)
        kv_head(sgl_ref, sgl_out, gl_q, gl_k, gl_v, gl_o,
                lambda kg, t: gl_g[pl.ds(k0 + kg, 8), ts(t)], k0, v0, GLA_DK)

    @pl.when(s >= N_RWKV + N_GLA)
    def _():
        h = s - (N_RWKV + N_GLA)
        v0 = pl.multiple_of(h * HEAD_DIM, HEAD_DIM)
        gamma = gam_ref[h]
        kv_head(srt_ref, srt_out, rt_q, rt_k, rt_v, rt_o, lambda kg, t: gamma, v0, v0, HEAD_DIM)

    @pl.when(s == N_STEPS - 1)
    def _():
        ln_w, ln_b = tile_t(vec_ref[5]), tile_t(vec_ref[6])
        gn_w = tile_t(gn_ref[...])
        parts = []
        for h in range(N_RWKV):
            hs = hrows(h)
            y = rw_y[hs, :]
            d = y - jnp.mean(y, axis=0, keepdims=True)
            yv = jnp.mean(d * d, axis=0, keepdims=True)
            parts.append((d * lax.rsqrt(yv + RWKV_GN_EPS) * ln_w[hs] + ln_b[hs] + rw_bonus[hs, :]) * rw_g[hs, :])
        for h in range(N_GLA):
            hs = hrows(h)
            o = gl_o[hs, :]
            parts.append(o * lax.rsqrt(jnp.mean(o * o, axis=0, keepdims=True) + RMS_EPS) * gn_w[hs] * gl_gate[hs, :])
        for h in range(N_RET):
            hs = hrows(h)
            o = rt_o[hs, :]
            parts.append(o * lax.rsqrt(jnp.mean(o * o, axis=0, keepdims=True) + RMS_EPS) * rt_gate[hs, :])
        mix_t = _bf(jnp.concatenate(parts, axis=0))
        mix = _dg(mix_t, wout_ref[...], _TN)
        x1_ref[...] = _layer_norm(ALPHA * x_ref[...] + mix, ln_ref[0:1, :], ln_ref[1:2, :])


def _smix_call(l, x, shift_t, s_rw, s_gl, s_rt, wts):
    M = x.shape[0]
    nb = M // DEC_SEQ
    (w1t, mu, vec, bwt, bat, bgt, bgkt, gb, gn, rot, gam, wout, ln1) = wts

    def once(shape, idx=()):
        nd = len(shape)
        return pl.BlockSpec(shape, lambda s: idx + (0,) * (nd - len(idx)), pipeline_mode=pl.Buffered(1))

    def layer(shape):
        return pl.BlockSpec((None,) + shape, lambda s: (l,) + (0,) * len(shape), pipeline_mode=pl.Buffered(1))

    def head_in(shape, first, n):
        return pl.BlockSpec((None, None) + shape,
                            lambda s: (l, jnp.clip(s - first, 0, n - 1)) + (0,) * len(shape))

    def head_out(shape, first, n):
        return pl.BlockSpec((None,) + shape, lambda s: (jnp.clip(s - first, 0, n - 1),) + (0,) * len(shape))

    rw_blk = (HEAD_DIM, HEAD_DIM, nb)
    gl_blk = (GLA_DK, HEAD_DIM, nb)
    in_specs = [
        once((M, D_MODEL)), layer((RWKV_COLS, nb)),
        head_in(rw_blk, 0, N_RWKV), head_in(gl_blk, N_RWKV, N_GLA), head_in(rw_blk, N_RWKV + N_GLA, N_RET),
        layer((RWKV_COLS + GLA_COLS + RET_COLS, D_MODEL)), layer((RWKV_COLS, nb)), layer((7, RWKV_W, nb)),
        layer((RWKV_W, LORA_W)), layer((RWKV_W, LORA_A)), layer((RWKV_W, LORA_G)), layer((GLA_K, GLA_LORA)),
        layer((GLA_K, nb)), layer((GLA_V, nb)), once((2, DEC_SEQ, HEAD_DIM // 2, nb)), once((N_RET, 8, nb)),
        layer((D_MODEL, D_MODEL)), layer((2, D_MODEL)),
    ]
    out_specs = [
        once((M, D_MODEL)), once((RWKV_COLS, nb)),
        head_out(rw_blk, 0, N_RWKV), head_out(gl_blk, N_RWKV, N_GLA), head_out(rw_blk, N_RWKV + N_GLA, N_RET),
    ]
    out_shape = [
        jax.ShapeDtypeStruct((M, D_MODEL), F32), jax.ShapeDtypeStruct((RWKV_COLS, nb), F32),
        jax.ShapeDtypeStruct((N_RWKV,) + rw_blk, F32), jax.ShapeDtypeStruct((N_GLA,) + gl_blk, F32),
        jax.ShapeDtypeStruct((N_RET,) + rw_blk, F32),
    ]
    scratch = ([pltpu.VMEM((RWKV_W, M), F32)] * 9
               + [pltpu.VMEM((GLA_K, M), F32), pltpu.VMEM((GLA_K, M), F32), pltpu.VMEM((GLA_V, M), F32),
                  pltpu.VMEM((GLA_K, M), F32), pltpu.VMEM((GLA_V, M), F32), pltpu.VMEM((GLA_V, M), F32)]
               + [pltpu.VMEM((RET_W, M), F32)] * 5)
    return pl.pallas_call(
        functools.partial(_smix_kernel, nb=nb),
        grid=(N_STEPS,),
        in_specs=in_specs, out_specs=out_specs, out_shape=out_shape, scratch_shapes=scratch,
        compiler_params=pltpu.CompilerParams(dimension_semantics=("arbitrary",), vmem_limit_bytes=VMEM_LIMIT),
        name="mixer_sample",
    )(x, shift_t, s_rw, s_gl, s_rt, w1t, mu, vec, bwt, bat, bgt, bgkt, gb, gn, rot, gam, wout, ln1)


def _sffn_kernel(x_ref, conv_ref, wup_ref, cwb_ref, wdn_ref, ln_ref, y_ref, cv_out_ref, *, nb):
    M = DEC_SEQ * nb
    x = x_ref[...]
    xb = _bf(x)
    n_f = D_FF // FFN_CHUNK

    def cols(f, base):
        return slice(base + f * FFN_CHUNK, base + (f + 1) * FFN_CHUNK)

    def up(f):
        return [_dg(xb, wup_ref[:, cols(f, base)], _NN) for base in (0, D_FF)]

    def conv(u, cs):
        c0 = conv_ref[:, 0, cs]
        c1 = conv_ref[:, 1, cs]
        u1 = jnp.concatenate([c1, u[0:M - nb]], axis=0)
        u2 = jnp.concatenate([c0, c1, u[0:M - 2 * nb]], axis=0)
        cv_out_ref[:, 0, cs] = u[M - 2 * nb:M - nb]
        cv_out_ref[:, 1, cs] = u[M - nb:M]
        return cwb_ref[3:4, cs] + (cwb_ref[0:1, cs] * u2 + cwb_ref[1:2, cs] * u1 + cwb_ref[2:3, cs] * u)

    acc = jnp.zeros((M, D_MODEL), F32)
    u_queue = [up(f) for f in range(min(FFN_AHEAD, n_f))]
    for f in range(n_f):
        if f + FFN_AHEAD < n_f:
            u_queue.append(up(f + FFN_AHEAD))
        u_cur = u_queue.pop(0)
        hid = jax.nn.gelu(conv(u_cur[0], cols(f, 0))) * conv(u_cur[1], cols(f, D_FF))
        acc = acc + _dg(_bf(hid), wdn_ref[f * FFN_CHUNK:(f + 1) * FFN_CHUNK, :], _NN)
    y_ref[...] = _layer_norm(ALPHA * x + acc, ln_ref[0:1, :], ln_ref[1:2, :])


def _sffn_call(l, x, conv2, wts):
    M = x.shape[0]
    nb = M // DEC_SEQ
    wup, cwb, wdn, ln2 = wts

    def layer(shape):
        return pl.BlockSpec((None,) + shape, lambda i: (l,) + (0,) * len(shape), pipeline_mode=pl.Buffered(1))

    return pl.pallas_call(
        functools.partial(_sffn_kernel, nb=nb),
        grid=(1,),
        in_specs=[pl.BlockSpec((M, D_MODEL), lambda i: (0, 0)), layer((nb, CONV_W - 1, 2 * D_FF)),
                  layer((D_MODEL, 2 * D_FF)), layer((8, 2 * D_FF)), layer((D_FF, D_MODEL)), layer((2, D_MODEL))],
        out_specs=[pl.BlockSpec((M, D_MODEL), lambda i: (0, 0)),
                   pl.BlockSpec((nb, CONV_W - 1, 2 * D_FF), lambda i: (0, 0, 0))],
        out_shape=[jax.ShapeDtypeStruct((M, D_MODEL), F32),
                   jax.ShapeDtypeStruct((nb, CONV_W - 1, 2 * D_FF), F32)],
        compiler_params=pltpu.CompilerParams(dimension_semantics=("arbitrary",), vmem_limit_bytes=VMEM_LIMIT),
        name="ffn_sample",
    )(x, conv2, wup, cwb, wdn, ln2)


def _run_sample(x_sample, s_rw, s_sh, s_gl, s_rt, s_cv, p, ffn_w):
    bs, ts_, _ = x_sample.shape
    L = DEPTH

    def lanes(a):
        return jnp.broadcast_to(a[..., None], a.shape + (bs,))

    half = HEAD_DIM // 2
    inv = 1.0 / (ROPE_BASE ** jnp.linspace(0.0, 1.0, half, dtype=F32))
    ang = (PAST_LEN + jnp.arange(ts_)).astype(F32)[:, None] * inv[None]
    rot = lanes(jnp.stack([jnp.cos(ang), jnp.sin(ang)]))
    log_gamma = jnp.log(1.0 - jnp.exp2(-5.0 - jnp.arange(N_RET, dtype=F32)))
    gam = jnp.broadcast_to(jnp.exp(log_gamma)[:, None, None], (N_RET, 8, bs))
    vec = lanes(jnp.stack([p["rwkv_w0"], p["rwkv_a0"], p["rwkv_kk"], p["rwkv_ka"],
                           p["rwkv_rk"].reshape(L, RWKV_W), p["rwkv_lnw"], p["rwkv_lnb"]], axis=1))
    wts = (p["w_in_t"], lanes(p["rwkv_mu"]), vec,
           _bf(jnp.swapaxes(p["rwkv_bw"], 1, 2)), _bf(jnp.swapaxes(p["rwkv_ba"], 1, 2)),
           _bf(jnp.swapaxes(p["rwkv_bg"], 1, 2)), _bf(jnp.swapaxes(p["gla_bgk"], 1, 2)),
           lanes(p["gla_bgk_b"]), lanes(jnp.tile(p["gla_norm_w"], (1, N_GLA))), rot, gam,
           _bf(p["w_out"]), jnp.stack([p["ln1_g"], p["ln1_b"]], axis=1))
    rw_t = jnp.transpose(s_rw, (0, 2, 3, 4, 1))
    gl_t = jnp.transpose(s_gl, (0, 2, 3, 4, 1))
    rt_t = jnp.transpose(s_rt, (0, 2, 3, 4, 1))
    sh_t = jnp.swapaxes(s_sh, 1, 2)
    x = jnp.swapaxes(x_sample, 0, 1).reshape(ts_ * bs, D_MODEL)
    n_rw, n_sh, n_gl, n_rt, n_cv = [], [], [], [], []
    for l in range(L):
        x, sh, rw, gl, rt = _smix_call(l, x, sh_t, rw_t, gl_t, rt_t, wts)
        x, cv = _sffn_call(l, x, s_cv, ffn_w)
        n_sh.append(sh)
        n_rw.append(rw)
        n_gl.append(gl)
        n_rt.append(rt)
        n_cv.append(cv)
    y = jnp.swapaxes(x.reshape(ts_, bs, D_MODEL), 0, 1)
    back = (0, 4, 1, 2, 3)
    return y, (jnp.transpose(jnp.stack(n_rw), back), jnp.swapaxes(jnp.stack(n_sh), 1, 2),
               jnp.transpose(jnp.stack(n_gl), back), jnp.transpose(jnp.stack(n_rt), back),
               jnp.stack(n_cv))


def _pad_to(a, axis, n):
    pad = [(0, 0)] * a.ndim
    pad[axis] = (0, n - a.shape[axis])
    return jnp.pad(a, pad)


def _place(a, axis, segs, total):
    out = []
    pos = 0
    for src, w, dst in segs:
        if dst > pos:
            shp = list(a.shape)
            shp[axis] = dst - pos
            out.append(jnp.zeros(shp, a.dtype))
        out.append(lax.slice_in_dim(a, src, src + w, axis=axis))
        pos = dst + w
    if total > pos:
        shp = list(a.shape)
        shp[axis] = total - pos
        out.append(jnp.zeros(shp, a.dtype))
    return jnp.concatenate(out, axis=axis)


def _in_col_segments():
    g0 = RWKV_COLS
    t0 = RWKV_COLS + GLA_COLS
    segs = [(0, RWKV_COLS, 0)]
    segs += [(g0, GLA_K, RW_P), (g0 + GLA_K, GLA_K, RW_P + GQ_P),
             (g0 + 2 * GLA_K, GLA_V, RW_P + 2 * GQ_P), (g0 + 2 * GLA_K + GLA_V, GLA_V, RW_P + 2 * GQ_P + HW),
             (g0 + 2 * GLA_K + 2 * GLA_V, GLA_LORA, RW_P + 2 * GQ_P + 2 * HW)]
    segs += [(t0 + i * RET_W, RET_W, RW_P + GL_P + i * HW) for i in range(4)]
    return segs


def _prep_weights(w_in, rwkv_mu, rwkv_w0, rwkv_bw, rwkv_a0, rwkv_ba, rwkv_bg, rwkv_kk, rwkv_ka,
                  rwkv_rk, rwkv_lnw, rwkv_lnb, gla_bgk, gla_bgk_b, gla_norm_w, w_out,
                  ln1_g, ln1_b, ln2_g, ln2_b, ffn_up, ffn_conv_w, ffn_conv_b, ffn_down):
    L = w_in.shape[0]
    w1 = _place(_bf(w_in), 2, _in_col_segments(), NP)
    mu = _pad_to(rwkv_mu, 1, RW_P)[:, None, :]
    gnw = _pad_to(jnp.tile(gla_norm_w, (1, N_GLA)), 1, HW)
    vec = jnp.stack([rwkv_w0, rwkv_a0, rwkv_kk, rwkv_ka, rwkv_rk.reshape(L, RWKV_W), rwkv_lnw, rwkv_lnb, gnw], axis=1)
    bw = _bf(_pad_to(rwkv_bw, 1, LANE))
    ba = _bf(_place(rwkv_ba, 1, [(0, LORA_A, LORA_W)], LANE))
    bg = _bf(_pad_to(rwkv_bg, 1, 2 * LANE))
    bgk = _bf(_pad_to(_pad_to(gla_bgk, 1, LANE), 2, GQ_P))
    bgkb = _pad_to(gla_bgk_b, 1, GQ_P)[:, None, :]
    wout = _place(_bf(w_out), 1, [(0, RWKV_W + GLA_V, 0), (RWKV_W + GLA_V, RET_W, 2 * HW)], MIX_P)
    ln1 = jnp.stack([ln1_g, ln1_b], axis=1)
    mixer_w = (w1, mu, vec, bw, ba, bg, bgk, bgkb, wout, ln1)
    cwb = _pad_to(jnp.concatenate([ffn_conv_w, ffn_conv_b[:, None, :]], axis=1), 1, 8)
    ln2 = jnp.stack([ln2_g, ln2_b], axis=1)
    ffn_w = (_bf(ffn_up), cwb, _bf(ffn_down), ln2)
    return mixer_w, ffn_w


def _tables(n_pos, chunk):
    half = HEAD_DIM // 2
    inv = 1.0 / (ROPE_BASE ** jnp.linspace(0.0, 1.0, half, dtype=F32))
    ang = jnp.arange(n_pos).astype(F32)[:, None] * inv[None]
    cos, sin = jnp.cos(ang), jnp.sin(ang)
    cos = jnp.tile(jnp.concatenate([cos, cos], -1), (1, HW // HEAD_DIM))
    sin = jnp.tile(jnp.concatenate([-sin, sin], -1), (1, HW // HEAD_DIM))
    log_gamma = jnp.log(1.0 - jnp.exp2(-5.0 - jnp.arange(N_RET, dtype=F32)))
    i = jnp.arange(chunk, dtype=F32)
    diff = i[:, None] - i[None, :]
    causal = diff >= 0
    dmask = jnp.where(causal, jnp.exp(jnp.where(causal, diff, 0.0) * log_gamma[:, None, None]), 0.0)
    qd = jnp.exp((i + 1.0) * log_gamma[:, None])[..., None]
    kd = jnp.exp((chunk - 1.0 - i) * log_gamma[:, None])[..., None]
    cd = jnp.exp(chunk * log_gamma)[:, None, None]
    qd = jnp.broadcast_to(qd, (N_RET, chunk, HEAD_DIM))
    kd = jnp.broadcast_to(kd, (N_RET, chunk, HEAD_DIM))
    cd = jnp.broadcast_to(cd, (N_RET, SLOTS, HEAD_DIM))
    return cos, sin, dmask, qd, kd, cd


def _run_prompt(x, s_rw, s_sh, s_gl, s_rt, s_cv, mixer_w, ffn_w, *, tt_mix, tt_ffn, chunk):
    T = x.shape[1]
    tabs = _tables(T, chunk)
    shift_rows = _pad_to(_pad_to(s_sh, 2, RW_P)[:, :, None, :], 2, SLOTS)
    conv_rows = _place(s_cv, 2, [(0, CONV_W - 1, SLOTS - (CONV_W - 1))], SLOTS)
    n_rw, n_sh, n_gl, n_rt, n_cv = [], [], [], [], []
    for l in range(DEPTH):
        x, sh, rw, gl, rt = _mixer_call(l, x, shift_rows, s_rw, s_gl, s_rt, tabs, mixer_w, tt=tt_mix, chunk=chunk)
        x, cv = _ffn_call(l, x, conv_rows, ffn_w, tt=tt_ffn)
        n_sh.append(sh[:, SLOTS - 1, :RWKV_COLS])
        n_cv.append(cv[:, SLOTS - (CONV_W - 1):])
        n_rw.append(rw)
        n_gl.append(gl)
        n_rt.append(rt)
    return x, (jnp.stack(n_rw), jnp.stack(n_sh), jnp.stack(n_gl), jnp.stack(n_rt), jnp.stack(n_cv))


def kernel(x_prompt, x_sample, state_rwkv, state_shift, state_gla, state_ret, state_conv, w_in, rwkv_mu, rwkv_w0, rwkv_bw, rwkv_a0, rwkv_ba, rwkv_bg, rwkv_kk, rwkv_ka, rwkv_rk, rwkv_lnw, rwkv_lnb, gla_bgk, gla_bgk_b, gla_norm_w, w_out, ln1_g, ln1_b, ln2_g, ln2_b, ffn_up, ffn_conv_w, ffn_conv_b, ffn_down):
    w_in_t = _bf(jnp.swapaxes(w_in, 1, 2))
    mixer_w, ffn_w = _prep_weights(jnp.swapaxes(w_in_t, 1, 2), rwkv_mu, rwkv_w0, rwkv_bw, rwkv_a0, rwkv_ba, rwkv_bg, rwkv_kk, rwkv_ka,
                                   rwkv_rk, rwkv_lnw, rwkv_lnb, gla_bgk, gla_bgk_b, gla_norm_w, w_out,
                                   ln1_g, ln1_b, ln2_g, ln2_b, ffn_up, ffn_conv_w, ffn_conv_b, ffn_down)
    bp, tp, _ = x_prompt.shape
    bs, ts, _ = x_sample.shape
    assert ts == DEC_SEQ

    def zeros_like_state(s):
        return jnp.zeros((s.shape[0], bp) + s.shape[2:], F32)

    chunk = math.gcd(tp, CHUNK)
    y_p, st_p = _run_prompt(
        x_prompt, zeros_like_state(state_rwkv), zeros_like_state(state_shift), zeros_like_state(state_gla),
        zeros_like_state(state_ret), zeros_like_state(state_conv), mixer_w, ffn_w,
        tt_mix=min(MIX_ROWS, tp), tt_ffn=min(FFN_ROWS, tp), chunk=chunk)

    raw = dict(w_in_t=w_in_t, rwkv_mu=rwkv_mu, rwkv_w0=rwkv_w0, rwkv_bw=rwkv_bw, rwkv_a0=rwkv_a0, rwkv_ba=rwkv_ba,
               rwkv_bg=rwkv_bg, rwkv_kk=rwkv_kk, rwkv_ka=rwkv_ka, rwkv_rk=rwkv_rk, rwkv_lnw=rwkv_lnw,
               rwkv_lnb=rwkv_lnb, gla_bgk=gla_bgk, gla_bgk_b=gla_bgk_b, gla_norm_w=gla_norm_w, w_out=w_out,
               ln1_g=ln1_g, ln1_b=ln1_b)
    y_s, st_s = _run_sample(x_sample, state_rwkv, state_shift, state_gla, state_ret, state_conv, raw, ffn_w)
    return (y_p, y_s) + st_p + st_s
```

```python
import functools
import math

import jax
import jax.numpy as jnp
from jax import lax
from jax.experimental import pallas as pl
from jax.experimental.pallas import tpu as pltpu

F32 = jnp.float32
BF16 = jnp.bfloat16

D_MODEL = 1024
DEPTH = 4
PAST_LEN = 16384
DEC_SEQ = 4
HEAD_DIM = 64
N_HEADS = D_MODEL // HEAD_DIM
N_GLA = (5 * N_HEADS) // 16
N_RET = (5 * N_HEADS) // 16
N_RWKV = N_HEADS - N_GLA - N_RET
RWKV_W = N_RWKV * HEAD_DIM
GLA_DK = HEAD_DIM // 2
GLA_K = N_GLA * GLA_DK
GLA_V = N_GLA * HEAD_DIM
RET_W = N_RET * HEAD_DIM
LORA_W = 64
LORA_A = 64
LORA_G = 160
GLA_LORA = 16
GLA_GATE_NORM = 16.0
CHUNK = 64
D_FF = 2816
CONV_W = 3
ALPHA = (2 * DEPTH) ** 0.25
RWKV_GN_EPS = 64e-5
LN_EPS = 1e-5
RMS_EPS = 1e-6
ROPE_BASE = 10000.0
RWKV_COLS = 3 * RWKV_W + LORA_W + LORA_A + LORA_G
GLA_COLS = 2 * GLA_K + 2 * GLA_V + GLA_LORA
RET_COLS = 4 * RET_W

LANE = 128
HW = 384
RW_P = 1536
GQ_P = 256
GL_P = 2 * GQ_P + 2 * HW + LANE
RT_P = 4 * HW
NP = RW_P + GL_P + RT_P
MIX_P = 3 * HW
SLOTS = 8

VMEM_LIMIT = 56 * 1024 * 1024
MIX_ROWS = 512
FFN_ROWS = 256


_NN = (((1,), (0,)), ((), ()))
_NT = (((1,), (1,)), ((), ()))
_TN = (((0,), (0,)), ((), ()))


def _bf(x):
    return x.astype(BF16)


def _dg(a, b, dims):
    return lax.dot_general(a, b, dims, preferred_element_type=F32)


def _dot(a, b, dims=_NN):
    return _dg(_bf(a), _bf(b), dims)


def _split(x, n):
    parts = []
    r = x
    for i in range(n):
        h = _bf(r)
        parts.append(h)
        if i + 1 < n:
            r = r - h.astype(F32)
    return parts


def _dot_sel(a, sel_bf, n=2, dims=_NN):
    out = None
    for h in _split(a, n):
        t = _dg(h, sel_bf, dims)
        out = t if out is None else out + t
    return out


def _sel_dot(sel_bf, b, n=3, dims=_NN):
    out = None
    for h in _split(b, n):
        t = _dg(sel_bf, h, dims)
        out = t if out is None else out + t
    return out


def _layer_norm(h, g, b):
    mu = jnp.mean(h, -1, keepdims=True)
    d = h - mu
    var = jnp.mean(d * d, -1, keepdims=True)
    return d * lax.rsqrt(var + LN_EPS) * g + b


def _swap_halves(x):
    pieces = []
    for i in range(x.shape[1] // LANE):
        p = x[:, i * LANE:(i + 1) * LANE]
        up = pltpu.roll(p, LANE - HEAD_DIM // 2, axis=1)
        dn = pltpu.roll(p, HEAD_DIM // 2, axis=1)
        lane = lax.broadcasted_iota(jnp.int32, p.shape, 1)
        pieces.append(jnp.where((lane % HEAD_DIM) < HEAD_DIM // 2, up, dn))
    return jnp.concatenate(pieces, axis=1)


def _mixer_kernel(x_ref, shift_ref, srw_ref, sgl_ref, srt_ref, cos_ref, sin_ref,
                  dmask_ref, qd_ref, kd_ref, cd_ref,
                  w1_ref, mu_ref, vec_ref, bw_ref, ba_ref, bg_ref, bgk_ref, bgkb_ref,
                  wout_ref, ln_ref,
                  x1_ref, shift_out_ref, srw_out, sgl_out, srt_out,
                  carry_ref, rw_r, rw_k, rw_v, rw_lw, rw_a, rw_b, rw_g, rw_bonus, rw_y,
                  gl_q, gl_k, gl_v, gl_lg, gl_gate, gl_o,
                  rt_q, rt_k, rt_v, rt_gate, rt_o,
                  *, rows, chunk, group):
    R, C = rows, chunk
    G = group * C
    j = pl.program_id(1)

    @pl.when(j == 0)
    def _():
        srw_out[...] = srw_ref[...]
        sgl_out[...] = sgl_ref[...]
        srt_out[...] = srt_ref[...]

    x = x_ref[0]
    xb = _bf(x)
    row = lax.broadcasted_iota(jnp.int32, (R, 1), 0)
    hi_ = lax.broadcasted_iota(jnp.int32, (2 * LANE, 2 * LANE), 0) // HEAD_DIM
    hj_ = lax.broadcasted_iota(jnp.int32, (2 * LANE, 2 * LANE), 1) // HEAD_DIM
    head_ones = (hi_ == hj_).astype(BF16)

    def head_sum(v):
        return jnp.concatenate([_dg(_bf(v[:, 0:2 * LANE]), head_ones, _NN),
                                _dg(_bf(v[:, 2 * LANE:HW]), head_ones[0:LANE, 0:LANE], _NN)], axis=1)

    p = _dg(xb, w1_ref[:, 0:RW_P], _NN)
    first = jnp.where(j == 0, shift_ref[0, 0:1, :], carry_ref[SLOTS - 1:SLOTS, :])
    prev = jnp.where(row == 0, first, pltpu.roll(p, 1, axis=0))
    carry_ref[...] = p[R - SLOTS:R, :]
    shift_out_ref[0] = p[R - SLOTS:R, :]
    pm = p + (prev - p) * mu_ref[...]
    r = pm[:, 0:HW]
    k = pm[:, HW:2 * HW]
    v = pm[:, 2 * HW:3 * HW]
    wa = pm[:, 3 * HW:3 * HW + LANE]
    xg = pm[:, 3 * HW + LANE:RW_P]
    w0 = vec_ref[0:1, :]
    a0 = vec_ref[1:2, :]
    k_k = vec_ref[2:3, :]
    k_a = vec_ref[3:4, :]
    r_k = vec_ref[4:5, :]
    w_log = -jax.nn.softplus(-(w0 + _dg(_bf(jnp.tanh(wa)), bw_ref[...], _NN))) - 0.5
    log_decay = -jnp.exp(w_log)
    a = jax.nn.sigmoid(a0 + _dg(_bf(wa), ba_ref[...], _NN))
    g = _dg(_bf(jax.nn.sigmoid(xg)), bg_ref[...], _NN)
    kk = k * k_k
    kk = kk / jnp.maximum(jnp.sqrt(head_sum(kk * kk)), 1e-12)
    k2 = k * (1.0 + (a - 1.0) * k_a)
    rw_r[...] = r
    rw_k[...] = k2
    rw_v[...] = v
    rw_lw[...] = log_decay
    rw_a[...] = -kk
    rw_b[...] = kk * a
    rw_g[...] = g
    rw_bonus[...] = head_sum(r * k2 * r_k) * v

    p = _dg(xb, w1_ref[:, RW_P:RW_P + GL_P], _NN)
    gk = p[:, 2 * GQ_P + 2 * HW:GL_P]
    lg = jax.nn.log_sigmoid(_dg(_bf(gk), bgk_ref[...], _NN) + bgkb_ref[...]) / GLA_GATE_NORM
    gl_q[...] = p[:, 0:GQ_P] * GLA_DK ** -0.5
    gl_k[...] = p[:, GQ_P:2 * GQ_P]
    gl_v[...] = p[:, 2 * GQ_P:2 * GQ_P + HW]
    gl_lg[...] = lg
    gl_gate[...] = jax.nn.silu(p[:, 2 * GQ_P + HW:2 * GQ_P + 2 * HW])

    p = _dg(xb, w1_ref[:, RW_P + GL_P:NP], _NN)
    cos = cos_ref[...]
    sin = sin_ref[...]
    q = p[:, 0:HW]
    k = p[:, HW:2 * HW]
    rt_q[...] = q * cos + _swap_halves(q) * sin
    rt_k[...] = (k * cos + _swap_halves(k) * sin) * HEAD_DIM ** -0.5
    rt_v[...] = p[:, 2 * HW:3 * HW]
    rt_gate[...] = jax.nn.silu(p[:, 3 * HW:4 * HW])

    CS = min(G, 2 * LANE)
    gi = lax.broadcasted_iota(jnp.int32, (CS, CS), 0)
    gj = lax.broadcasted_iota(jnp.int32, (CS, CS), 1)
    cum_sel = (((gi // C) == (gj // C)) & (gi >= gj)).astype(BF16)

    def chunk_cumsum(v):
        cum = jnp.concatenate([_sel_dot(cum_sel, v[i * CS:(i + 1) * CS], n=2) for i in range(G // CS)], axis=0)
        tot = jnp.concatenate([jnp.broadcast_to(cum[(c + 1) * C - 1:(c + 1) * C, :], (C, v.shape[1]))
                               for c in range(group)], axis=0)
        return cum, tot
    ii = lax.broadcasted_iota(jnp.int32, (C, C), 0)
    jj = lax.broadcasted_iota(jnp.int32, (C, C), 1)
    tri_incl = ii >= jj
    tri_strict = ii > jj
    eye = (ii == jj).astype(F32)
    levels = []
    s = 1
    while s < C:
        levels.append(((ii // (2 * s)) == (jj // (2 * s))) & ((ii % (2 * s)) >= s) & ((jj % (2 * s)) < s))
        s *= 2
    ones_cv = jnp.ones((C, HEAD_DIM), BF16)
    chunks = range(group)

    def group_body(gidx, carry):
        g0 = pl.multiple_of(gidx * G, G)
        gs = pl.ds(g0, G)

        def cr(c):
            return slice(c * C, (c + 1) * C)

        lw = rw_lw[gs, :]
        cum, tot = chunk_cumsum(lw)
        w_inv = jnp.exp(-cum)
        w_rem = jnp.exp(tot - cum)
        r_t = _bf(rw_r[gs, :] * jnp.exp(cum))
        a_t = _bf(rw_a[gs, :] * jnp.exp(cum - lw))
        b_t = _bf(rw_b[gs, :] * w_inv)
        k_t = _bf(rw_k[gs, :] * w_inv)
        b_h = _bf(rw_b[gs, :] * w_rem)
        k_h = _bf(rw_k[gs, :] * w_rem)
        w_c = jnp.exp(tot)
        v_rw = _bf(rw_v[gs, :])
        items = [(c, h) for c in chunks for h in range(N_RWKV)]

        def hs(h):
            return slice(h * HEAD_DIM, (h + 1) * HEAD_DIM)

        ar = [jnp.concatenate([a_t[cr(c), hs(h)], r_t[cr(c), hs(h)]], axis=0) for c, h in items]
        xb_ = [_dot(ar[i], b_t[cr(c), hs(h)], _NT) for i, (c, h) in enumerate(items)]
        xk_ = [_dot(ar[i], k_t[cr(c), hs(h)], _NT) for i, (c, h) in enumerate(items)]
        a_ab = [jnp.where(tri_strict, t[0:C], 0.0) for t in xb_]
        a_ak = [jnp.where(tri_strict, t[0:C], 0.0) for t in xk_]
        p_rb = [jnp.where(tri_incl, t[C:2 * C], 0.0) for t in xb_]
        p_rk = [jnp.where(tri_incl, t[C:2 * C], 0.0) for t in xk_]

        lg = gl_lg[gs, :]
        gcum, gtot = chunk_cumsum(lg)
        q_in = _bf(gl_q[gs, :] * jnp.exp(gcum))
        k_in = _bf(gl_k[gs, :] * jnp.exp(-gcum))
        k_st = _bf(gl_k[gs, :] * jnp.exp(gtot - gcum))
        v_gl = _bf(gl_v[gs, :])
        g_col = []
        for c in chunks:
            parts = _split(lg[cr(c), :], 2)
            g_col.append(jnp.exp(_dg(parts[0], ones_cv, _TN) + _dg(parts[1], ones_cv, _TN)))
        gitems = [(c, h) for c in chunks for h in range(N_GLA)]

        def ks(h):
            return slice(h * GLA_DK, (h + 1) * GLA_DK)

        g_att = [jnp.where(tri_incl, _dot(q_in[cr(c), ks(h)], k_in[cr(c), ks(h)], _NT), 0.0) for c, h in gitems]
        g_kv = [_dot(k_st[cr(c), ks(h)], v_gl[cr(c), hs(h)], _TN) for c, h in gitems]

        q_rt = rt_q[gs, :]
        k_rt = rt_k[gs, :]
        v_rt = _bf(rt_v[gs, :])
        q_rb, k_rb = _bf(q_rt), _bf(k_rt)
        ritems = [(c, h) for c in chunks for h in range(N_RET)]
        r_att = [_dot(q_rb[cr(c), hs(h)], k_rb[cr(c), hs(h)], _NT) * dmask_ref[h] for c, h in ritems]
        r_kv = [_dot(k_rt[cr(c), hs(h)] * kd_ref[h], v_rt[cr(c), hs(h)], _TN) for c, h in ritems]

        m = [eye + jnp.where(levels[0], t, 0.0) for t in a_ab]
        for lvl in levels[1:]:
            t_ = [_dot(jnp.where(lvl, a_ab[i], 0.0), m[i]) for i in range(len(items))]
            m = [m[i] + _dot(m[i], t_[i]) for i in range(len(items))]

        g_av = [_dot(g_att[i], v_gl[cr(c), hs(h)]) for i, (c, h) in enumerate(gitems)]
        r_av = [_dot(r_att[i], v_rt[cr(c), hs(h)]) for i, (c, h) in enumerate(ritems)]

        vv = [v_rw[cr(c), hs(h)] for c, h in items]
        akv = [_dot(a_ak[i], vv[i]) for i in range(len(items))]
        at2 = [_dot(m[i], a_t[cr(c), hs(h)]) for i, (c, h) in enumerate(items)]
        y0 = [_dot(p_rk[i], vv[i]) for i in range(len(items))]
        u0 = [_dot(m[i], akv[i]) for i in range(len(items))]
        gp = [_dot(at2[i], b_h[cr(c), hs(h)], _TN) for i, (c, h) in enumerate(items)]
        hh = [_dot(jnp.concatenate([_bf(u0[i]), vv[i]], axis=0),
                   jnp.concatenate([b_h[cr(c), hs(h)], k_h[cr(c), hs(h)]], axis=0), _TN)
              for i, (c, h) in enumerate(items)]

        cur = [srw_out[0, h] for h in range(N_RWKV)]
        g_cur = [sgl_out[0, h] for h in range(N_GLA)]
        r_cur = [srt_out[0, h] for h in range(N_RET)]
        for c in chunks:
            s0 = list(cur)
            for h in range(N_RWKV):
                i = c * N_RWKV + h
                cur[h] = cur[h] * w_c[c * C:c * C + 1, hs(h)] + _dot(cur[h], gp[i]) + hh[i]
            us = [_dot(jnp.concatenate([_bf(at2[c * N_RWKV + h]), r_t[cr(c), hs(h)]], axis=0), s0[h], _NT)
                  for h in range(N_RWKV)]
            g_o = [g_av[c * N_GLA + h] + _dot(q_in[cr(c), ks(h)], g_cur[h]) for h in range(N_GLA)]
            r_o = [r_av[c * N_RET + h] + _dot(q_rt[cr(c), hs(h)] * qd_ref[h], r_cur[h]) for h in range(N_RET)]
            y = [us[h][C:2 * C] + _dot(p_rb[c * N_RWKV + h], us[h][0:C] + u0[c * N_RWKV + h]) + y0[c * N_RWKV + h]
                 for h in range(N_RWKV)]
            g_cur = [g_cur[h] * g_col[c][ks(h), :] + g_kv[c * N_GLA + h] for h in range(N_GLA)]
            r_cur = [r_cur[h] * cd_ref[h, 0:1, :] + r_kv[c * N_RET + h] for h in range(N_RET)]
            rows_c = pl.ds(g0 + c * C, C)
            rw_y[rows_c, :] = jnp.concatenate(y, axis=1)
            gl_o[rows_c, :] = jnp.concatenate(g_o + [jnp.zeros((C, HW - GLA_V), F32)], axis=1)
            rt_o[rows_c, :] = jnp.concatenate(r_o + [jnp.zeros((C, HW - RET_W), F32)], axis=1)
        for h in range(N_RWKV):
            srw_out[0, h] = cur[h]
        for h in range(N_GLA):
            sgl_out[0, h] = g_cur[h]
        for h in range(N_RET):
            srt_out[0, h] = r_cur[h]
        return carry

    if R == G:
        group_body(0, 0)
    else:
        lax.fori_loop(0, R // G, group_body, 0)

    ln_w = vec_ref[5:6, :]
    ln_b = vec_ref[6:7, :]
    gn_w = vec_ref[7:8, :]
    inv_hd = 1.0 / HEAD_DIM
    y = rw_y[...]
    ym = head_sum(y) * inv_hd
    d = y - ym
    yv = head_sum(d * d) * inv_hd
    ya = (d * lax.rsqrt(yv + RWKV_GN_EPS) * ln_w + ln_b + rw_bonus[...]) * rw_g[...]
    o = gl_o[...]
    ob = o * lax.rsqrt(head_sum(o * o) * inv_hd + RMS_EPS) * gn_w * gl_gate[...]
    o = rt_o[...]
    oc = o * lax.rsqrt(head_sum(o * o) * inv_hd + RMS_EPS) * rt_gate[...]
    mix = _dg(_bf(jnp.concatenate([ya, ob, oc], axis=1)), wout_ref[...], _NN)
    out = _layer_norm(ALPHA * x + mix, ln_ref[0:1, :], ln_ref[1:2, :])
    x1_ref[0] = out


def _const_spec(shape):
    nd = len(shape)
    return pl.BlockSpec(shape, lambda i, j: (0,) * nd, pipeline_mode=pl.Buffered(1))


def _layer_spec(l, shape):
    return pl.BlockSpec((None,) + shape, lambda i, j: (l,) + (0,) * len(shape), pipeline_mode=pl.Buffered(1))


def _mixer_call(l, x, shift_rows, s_rw, s_gl, s_rt, tabs, wts, *, tt, chunk):
    B, T, _ = x.shape
    bb, R = 1, tt
    nb, nt = B, T // tt
    group = tt // chunk
    cos, sin, dmask, qd, kd, cd = tabs
    (w1, mu, vec, bw, ba, bg, bgk, bgkb, wout, ln1) = wts
    C = chunk
    lw = functools.partial(_layer_spec, l)

    in_specs = [
        pl.BlockSpec((bb, tt, D_MODEL), lambda i, j: (i, j, 0)),
        pl.BlockSpec((None, bb, SLOTS, RW_P), lambda i, j: (l, i, 0, 0)),
        pl.BlockSpec((None, bb, N_RWKV, HEAD_DIM, HEAD_DIM), lambda i, j: (l, i, 0, 0, 0)),
        pl.BlockSpec((None, bb, N_GLA, GLA_DK, HEAD_DIM), lambda i, j: (l, i, 0, 0, 0)),
        pl.BlockSpec((None, bb, N_RET, HEAD_DIM, HEAD_DIM), lambda i, j: (l, i, 0, 0, 0)),
        pl.BlockSpec((R, HW), lambda i, j: (j, 0)),
        pl.BlockSpec((R, HW), lambda i, j: (j, 0)),
        _const_spec((N_RET, C, C)),
        _const_spec((N_RET, C, HEAD_DIM)),
        _const_spec((N_RET, C, HEAD_DIM)),
        _const_spec((N_RET, SLOTS, HEAD_DIM)),
        lw((D_MODEL, NP)), lw((1, RW_P)), lw((8, HW)), lw((LANE, HW)), lw((LANE, HW)), lw((2 * LANE, HW)),
        lw((LANE, GQ_P)), lw((1, GQ_P)), lw((MIX_P, D_MODEL)), lw((2, D_MODEL)),
    ]
    out_specs = [
        pl.BlockSpec((bb, tt, D_MODEL), lambda i, j: (i, j, 0)),
        pl.BlockSpec((bb, SLOTS, RW_P), lambda i, j: (i, 0, 0)),
        pl.BlockSpec((bb, N_RWKV, HEAD_DIM, HEAD_DIM), lambda i, j: (i, 0, 0, 0)),
        pl.BlockSpec((bb, N_GLA, GLA_DK, HEAD_DIM), lambda i, j: (i, 0, 0, 0)),
        pl.BlockSpec((bb, N_RET, HEAD_DIM, HEAD_DIM), lambda i, j: (i, 0, 0, 0)),
    ]
    out_shape = [
        jax.ShapeDtypeStruct((B, T, D_MODEL), F32),
        jax.ShapeDtypeStruct((B, SLOTS, RW_P), F32),
        jax.ShapeDtypeStruct(s_rw.shape[1:], F32),
        jax.ShapeDtypeStruct(s_gl.shape[1:], F32),
        jax.ShapeDtypeStruct(s_rt.shape[1:], F32),
    ]
    scratch = [pltpu.VMEM((SLOTS, RW_P), F32)]
    scratch += [pltpu.VMEM((R, HW), F32)] * 9
    scratch += [pltpu.VMEM((R, GQ_P), F32), pltpu.VMEM((R, GQ_P), F32), pltpu.VMEM((R, HW), F32),
                pltpu.VMEM((R, GQ_P), F32), pltpu.VMEM((R, HW), F32), pltpu.VMEM((R, HW), F32)]
    scratch += [pltpu.VMEM((R, HW), F32)] * 5
    return pl.pallas_call(
        functools.partial(_mixer_kernel, rows=R, chunk=C, group=group),
        grid=(nb, nt),
        in_specs=in_specs, out_specs=out_specs, out_shape=out_shape,
        scratch_shapes=scratch,
        compiler_params=pltpu.CompilerParams(dimension_semantics=("arbitrary", "arbitrary"),
                                             vmem_limit_bytes=VMEM_LIMIT),
        name="mixer_prompt",
    )(x, shift_rows, s_rw, s_gl, s_rt, cos, sin, dmask, qd, kd, cd,
      w1, mu, vec, bw, ba, bg, bgk, bgkb, wout, ln1)


FFN_CHUNK = 256
FFN_AHEAD = 3


def _ffn_kernel(x_ref, conv_ref, wup_ref, cwb_ref, wdn_ref, ln_ref, y_ref, cv_out_ref, ubuf, *, rows):
    R = rows
    j = pl.program_id(1)

    @pl.when(j == 0)
    def _():
        ubuf[0:SLOTS, :] = conv_ref[0]

    x = x_ref[0]
    xb = _bf(x)
    n_f = D_FF // FFN_CHUNK

    def cols(f, base):
        return slice(base + f * FFN_CHUNK, base + (f + 1) * FFN_CHUNK)

    def up(f):
        return [_dg(xb, wup_ref[:, cols(f, base)], _NN) for base in (0, D_FF)]

    def conv(u, cs):
        ubuf[SLOTS:SLOTS + R, cs] = u
        u1 = ubuf[SLOTS - 1:SLOTS - 1 + R, cs]
        u2 = ubuf[SLOTS - 2:SLOTS - 2 + R, cs]
        ubuf[0:SLOTS, cs] = u[R - SLOTS:R]
        cv_out_ref[0, :, cs] = u[R - SLOTS:R]
        return cwb_ref[3:4, cs] + (cwb_ref[0:1, cs] * u2 + cwb_ref[1:2, cs] * u1 + cwb_ref[2:3, cs] * u)

    acc = jnp.zeros((R, D_MODEL), F32)
    u_queue = [up(f) for f in range(min(FFN_AHEAD, n_f))]
    for f in range(n_f):
        if f + FFN_AHEAD < n_f:
            u_queue.append(up(f + FFN_AHEAD))
        u_cur = u_queue.pop(0)
        hid = jax.nn.gelu(conv(u_cur[0], cols(f, 0))) * conv(u_cur[1], cols(f, D_FF))
        acc = acc + _dg(_bf(hid), wdn_ref[f * FFN_CHUNK:(f + 1) * FFN_CHUNK, :], _NN)
    y_ref[0] = _layer_norm(ALPHA * x + acc, ln_ref[0:1, :], ln_ref[1:2, :])


def _ffn_call(l, x, conv_rows, wts, *, tt):
    B, T, _ = x.shape
    bb, nb, nt = 1, B, T // tt
    wup, cwb, wdn, ln2 = wts
    lw = functools.partial(_layer_spec, l)
    return pl.pallas_call(
        functools.partial(_ffn_kernel, rows=tt),
        grid=(nb, nt),
        in_specs=[
            pl.BlockSpec((bb, tt, D_MODEL), lambda i, j: (i, j, 0)),
            pl.BlockSpec((None, bb, SLOTS, 2 * D_FF), lambda i, j: (l, i, 0, 0)),
            lw((D_MODEL, 2 * D_FF)), lw((8, 2 * D_FF)), lw((D_FF, D_MODEL)), lw((2, D_MODEL)),
        ],
        out_specs=[
            pl.BlockSpec((bb, tt, D_MODEL), lambda i, j: (i, j, 0)),
            pl.BlockSpec((bb, SLOTS, 2 * D_FF), lambda i, j: (i, 0, 0)),
        ],
        out_shape=[jax.ShapeDtypeStruct((B, T, D_MODEL), F32),
                   jax.ShapeDtypeStruct((B, SLOTS, 2 * D_FF), F32)],
        scratch_shapes=[pltpu.VMEM((SLOTS + tt, 2 * D_FF), F32)],
        compiler_params=pltpu.CompilerParams(dimension_semantics=("arbitrary", "arbitrary"),
                                             vmem_limit_bytes=VMEM_LIMIT),
        name="ffn_prompt",
    )(x, conv_rows, wup, cwb, wdn, ln2)


N_STEPS = N_RWKV + N_GLA + N_RET
G0 = RWKV_COLS
T0 = RWKV_COLS + GLA_COLS


def _smix_kernel(x_ref, shift_ref, srw_ref, sgl_ref, srt_ref, w1t_ref, mu_ref, vec_ref,
                 bwt_ref, bat_ref, bgt_ref, bgkt_ref, gb_ref, gn_ref, rot_ref, gam_ref, wout_ref, ln_ref,
                 x1_ref, shift_out_ref, srw_out, sgl_out, srt_out,
                 rw_r, rw_k, rw_v, rw_w, rw_a, rw_b, rw_g, rw_bonus, rw_y,
                 gl_q, gl_k, gl_v, gl_g, gl_gate, gl_o, rt_q, rt_k, rt_v, rt_gate, rt_o, *, nb):
    s = pl.program_id(0)
    M = DEC_SEQ * nb

    def ts(t):
        return slice(t * nb, (t + 1) * nb)

    def tile_t(c):
        return jnp.concatenate([c] * DEC_SEQ, axis=1)

    def hrows(h):
        return slice(h * HEAD_DIM, (h + 1) * HEAD_DIM)

    @pl.when(s == 0)
    def _():
        xb = _bf(x_ref[...])

        def proj(r0, r1):
            return _dg(w1t_ref[r0:r1, :], xb, _NT)

        p = proj(0, RWKV_COLS)
        shift_out_ref[...] = p[:, (DEC_SEQ - 1) * nb:]
        prev = jnp.concatenate([shift_ref[...], p[:, :(DEC_SEQ - 1) * nb]], axis=1)
        pm = p + (prev - p) * tile_t(mu_ref[...])
        r = pm[0:RWKV_W]
        k = pm[RWKV_W:2 * RWKV_W]
        v = pm[2 * RWKV_W:3 * RWKV_W]
        xw = pm[3 * RWKV_W:3 * RWKV_W + LORA_W]
        xa = pm[3 * RWKV_W + LORA_W:3 * RWKV_W + LORA_W + LORA_A]
        xg = pm[3 * RWKV_W + LORA_W + LORA_A:RWKV_COLS]
        w0, a0, k_k, k_a, r_k = (tile_t(vec_ref[i]) for i in range(5))
        w_log = -jax.nn.softplus(-(w0 + _dg(bwt_ref[...], _bf(jnp.tanh(xw)), _NN))) - 0.5
        a = jax.nn.sigmoid(a0 + _dg(bat_ref[...], _bf(xa), _NN))
        kk = k * k_k
        k2 = k * (1.0 + (a - 1.0) * k_a)
        rk2 = r * k2 * r_k
        for h in range(N_RWKV):
            hs = hrows(h)
            kh = kk[hs]
            kh = kh / jnp.maximum(jnp.sqrt(jnp.sum(kh * kh, axis=0, keepdims=True)), 1e-12)
            rw_a[hs, :] = -kh
            rw_b[hs, :] = kh * a[hs]
            rw_bonus[hs, :] = jnp.sum(rk2[hs], axis=0, keepdims=True) * v[hs]
        rw_r[...] = r
        rw_k[...] = k2
        rw_v[...] = v
        rw_w[...] = jnp.exp(-jnp.exp(w_log))
        rw_g[...] = _dg(bgt_ref[...], _bf(jax.nn.sigmoid(xg)), _NN)

        p = proj(G0, T0)
        gk = p[2 * GLA_K + 2 * GLA_V:GLA_COLS]
        lg = jax.nn.log_sigmoid(_dg(bgkt_ref[...], _bf(gk), _NN) + tile_t(gb_ref[...])) / GLA_GATE_NORM
        gl_q[...] = p[0:GLA_K] * GLA_DK ** -0.5
        gl_k[...] = p[GLA_K:2 * GLA_K]
        gl_v[...] = p[2 * GLA_K:2 * GLA_K + GLA_V]
        gl_g[...] = jnp.exp(lg)
        gl_gate[...] = jax.nn.silu(p[2 * GLA_K + GLA_V:2 * GLA_K + 2 * GLA_V])

        p = proj(T0, T0 + RET_COLS)
        cos = jnp.concatenate([rot_ref[0, t] for t in range(DEC_SEQ)], axis=1)
        sin = jnp.concatenate([rot_ref[1, t] for t in range(DEC_SEQ)], axis=1)
        half = HEAD_DIM // 2

        def rot(xh):
            x1, x2 = xh[0:half], xh[half:HEAD_DIM]
            return jnp.concatenate([x1 * cos - x2 * sin, x1 * sin + x2 * cos], axis=0)

        for h in range(N_RET):
            hs = hrows(h)
            rt_q[hs, :] = rot(p[hs])
            rt_k[hs, :] = rot(p[RET_W + h * HEAD_DIM:RET_W + (h + 1) * HEAD_DIM]) * HEAD_DIM ** -0.5
        rt_v[...] = p[2 * RET_W:3 * RET_W]
        rt_gate[...] = jax.nn.silu(p[3 * RET_W:4 * RET_W])

    @pl.when(s < N_RWKV)
    def _():
        r0 = pl.multiple_of(s * HEAD_DIM, HEAD_DIM)
        hs = pl.ds(r0, HEAD_DIM)

        def v_group(i, carry):
            v0 = pl.multiple_of(i * 8, 8)
            vt = [rw_v[pl.ds(r0 + v0, 8), ts(t)] for t in range(DEC_SEQ)]
            ys = [[] for _ in range(DEC_SEQ)]
            for j in range(8):
                S = srw_ref[v0 + j]
                for t in range(DEC_SEQ):
                    sa = jnp.sum(S * rw_a[hs, ts(t)], axis=0, keepdims=True)
                    S = S * rw_w[hs, ts(t)] + sa * rw_b[hs, ts(t)] + vt[t][j:j + 1, :] * rw_k[hs, ts(t)]
                    ys[t].append(jnp.sum(S * rw_r[hs, ts(t)], axis=0, keepdims=True))
                srw_out[v0 + j] = S
            for t in range(DEC_SEQ):
                rw_y[pl.ds(r0 + v0, 8), ts(t)] = jnp.concatenate(ys[t], axis=0)
            return carry

        lax.fori_loop(0, HEAD_DIM // 8, v_group, 0)

    def kv_head(s_in, s_out, q_ref, k_ref, v_ref, o_ref, decay_rows, k0, v0, nk):
        for t in range(DEC_SEQ):
            src = s_in if t == 0 else s_out
            v_t = v_ref[pl.ds(v0, HEAD_DIM), ts(t)]

            def k_group(i, o, t=t, src=src, v_t=v_t):
                kg = pl.multiple_of(i * 8, 8)
                q8 = q_ref[pl.ds(k0 + kg, 8), ts(t)]
                k8 = k_ref[pl.ds(k0 + kg, 8), ts(t)]
                d8 = decay_rows(kg, t)
                for j in range(8):
                    S = src[kg + j] * d8[j:j + 1, :] + k8[j:j + 1, :] * v_t
                    s_out[kg + j] = S
                    o = o + q8[j:j + 1, :] * S
                return o

            o_ref[pl.ds(v0, HEAD_DIM), ts(t)] = lax.fori_loop(0, nk // 8, k_group, jnp.zeros((HEAD_DIM, nb), F32))

    @pl.when((s >= N_RWKV) & (s < N_RWKV + N_GLA))
    def _():
        h = s - N_RWKV
        k0 = pl.multiple_of(h * GLA_DK, GLA_DK)
        v0 = pl.multiple_of(h * HEAD_DIM, HEAD_DIM)
        kv_head(sgl_ref, sgl_out, gl_q, gl_k, gl_v, gl_o,
                lambda kg, t: gl_g[pl.ds(k0 + kg, 8), ts(t)], k0, v0, GLA_DK)

    @pl.when(s >= N_RWKV + N_GLA)
    def _():
        h = s - (N_RWKV + N_GLA)
        v0 = pl.multiple_of(h * HEAD_DIM, HEAD_DIM)
        gamma = gam_ref[h]
        kv_head(srt_ref, srt_out, rt_q, rt_k, rt_v, rt_o, lambda kg, t: gamma, v0, v0, HEAD_DIM)

    @pl.when(s == N_STEPS - 1)
    def _():
        ln_w, ln_b = tile_t(vec_ref[5]), tile_t(vec_ref[6])
        gn_w = tile_t(gn_ref[...])
        parts = []
        for h in range(N_RWKV):
            hs = hrows(h)
            y = rw_y[hs, :]
            d = y - jnp.mean(y, axis=0, keepdims=True)
            yv = jnp.mean(d * d, axis=0, keepdims=True)
            parts.append((d * lax.rsqrt(yv + RWKV_GN_EPS) * ln_w[hs] + ln_b[hs] + rw_bonus[hs, :]) * rw_g[hs, :])
        for h in range(N_GLA):
            hs = hrows(h)
            o = gl_o[hs, :]
            parts.append(o * lax.rsqrt(jnp.mean(o * o, axis=0, keepdims=True) + RMS_EPS) * gn_w[hs] * gl_gate[hs, :])
        for h in range(N_RET):
            hs = hrows(h)
            o = rt_o[hs, :]
            parts.append(o * lax.rsqrt(jnp.mean(o * o, axis=0, keepdims=True) + RMS_EPS) * rt_gate[hs, :])
        mix_t = _bf(jnp.concatenate(parts, axis=0))
        mix = _dg(mix_t, wout_ref[...], _TN)
        x1_ref[...] = _layer_norm(ALPHA * x_ref[...] + mix, ln_ref[0:1, :], ln_ref[1:2, :])


def _smix_call(l, x, shift_t, s_rw, s_gl, s_rt, wts):
    M = x.shape[0]
    nb = M // DEC_SEQ
    (w1t, mu, vec, bwt, bat, bgt, bgkt, gb, gn, rot, gam, wout, ln1) = wts

    def once(shape, idx=()):
        nd = len(shape)
        return pl.BlockSpec(shape, lambda s: idx + (0,) * (nd - len(idx)), pipeline_mode=pl.Buffered(1))

    def layer(shape):
        return pl.BlockSpec((None,) + shape, lambda s: (l,) + (0,) * len(shape), pipeline_mode=pl.Buffered(1))

    def head_in(shape, first, n):
        return pl.BlockSpec((None, None) + shape,
                            lambda s: (l, jnp.clip(s - first, 0, n - 1)) + (0,) * len(shape))

    def head_out(shape, first, n):
        return pl.BlockSpec((None,) + shape, lambda s: (jnp.clip(s - first, 0, n - 1),) + (0,) * len(shape))

    rw_blk = (HEAD_DIM, HEAD_DIM, nb)
    gl_blk = (GLA_DK, HEAD_DIM, nb)
    in_specs = [
        once((M, D_MODEL)), layer((RWKV_COLS, nb)),
        head_in(rw_blk, 0, N_RWKV), head_in(gl_blk, N_RWKV, N_GLA), head_in(rw_blk, N_RWKV + N_GLA, N_RET),
        layer((RWKV_COLS + GLA_COLS + RET_COLS, D_MODEL)), layer((RWKV_COLS, nb)), layer((7, RWKV_W, nb)),
        layer((RWKV_W, LORA_W)), layer((RWKV_W, LORA_A)), layer((RWKV_W, LORA_G)), layer((GLA_K, GLA_LORA)),
        layer((GLA_K, nb)), layer((GLA_V, nb)), once((2, DEC_SEQ, HEAD_DIM // 2, nb)), once((N_RET, 8, nb)),
        layer((D_MODEL, D_MODEL)), layer((2, D_MODEL)),
    ]
    out_specs = [
        once((M, D_MODEL)), once((RWKV_COLS, nb)),
        head_out(rw_blk, 0, N_RWKV), head_out(gl_blk, N_RWKV, N_GLA), head_out(rw_blk, N_RWKV + N_GLA, N_RET),
    ]
    out_shape = [
        jax.ShapeDtypeStruct((M, D_MODEL), F32), jax.ShapeDtypeStruct((RWKV_COLS, nb), F32),
        jax.ShapeDtypeStruct((N_RWKV,) + rw_blk, F32), jax.ShapeDtypeStruct((N_GLA,) + gl_blk, F32),
        jax.ShapeDtypeStruct((N_RET,) + rw_blk, F32),
    ]
    scratch = ([pltpu.VMEM((RWKV_W, M), F32)] * 9
               + [pltpu.VMEM((GLA_K, M), F32), pltpu.VMEM((GLA_K, M), F32), pltpu.VMEM((GLA_V, M), F32),
                  pltpu.VMEM((GLA_K, M), F32), pltpu.VMEM((GLA_V, M), F32), pltpu.VMEM((GLA_V, M), F32)]
               + [pltpu.VMEM((RET_W, M), F32)] * 5)
    return pl.pallas_call(
        functools.partial(_smix_kernel, nb=nb),
        grid=(N_STEPS,),
        in_specs=in_specs, out_specs=out_specs, out_shape=out_shape, scratch_shapes=scratch,
        compiler_params=pltpu.CompilerParams(dimension_semantics=("arbitrary",), vmem_limit_bytes=VMEM_LIMIT),
        name="mixer_sample",
    )(x, shift_t, s_rw, s_gl, s_rt, w1t, mu, vec, bwt, bat, bgt, bgkt, gb, gn, rot, gam, wout, ln1)


def _sffn_kernel(x_ref, conv_ref, wup_ref, cwb_ref, wdn_ref, ln_ref, y_ref, cv_out_ref, *, nb):
    M = DEC_SEQ * nb
    x = x_ref[...]
    xb = _bf(x)
    n_f = D_FF // FFN_CHUNK

    def cols(f, base):
        return slice(base + f * FFN_CHUNK, base + (f + 1) * FFN_CHUNK)

    def up(f):
        return [_dg(xb, wup_ref[:, cols(f, base)], _NN) for base in (0, D_FF)]

    def conv(u, cs):
        c0 = conv_ref[:, 0, cs]
        c1 = conv_ref[:, 1, cs]
        u1 = jnp.concatenate([c1, u[0:M - nb]], axis=0)
        u2 = jnp.concatenate([c0, c1, u[0:M - 2 * nb]], axis=0)
        cv_out_ref[:, 0, cs] = u[M - 2 * nb:M - nb]
        cv_out_ref[:, 1, cs] = u[M - nb:M]
        return cwb_ref[3:4, cs] + (cwb_ref[0:1, cs] * u2 + cwb_ref[1:2, cs] * u1 + cwb_ref[2:3, cs] * u)

    acc = jnp.zeros((M, D_MODEL), F32)
    u_queue = [up(f) for f in range(min(FFN_AHEAD, n_f))]
    for f in range(n_f):
        if f + FFN_AHEAD < n_f:
            u_queue.append(up(f + FFN_AHEAD))
        u_cur = u_queue.pop(0)
        hid = jax.nn.gelu(conv(u_cur[0], cols(f, 0))) * conv(u_cur[1], cols(f, D_FF))
        acc = acc + _dg(_bf(hid), wdn_ref[f * FFN_CHUNK:(f + 1) * FFN_CHUNK, :], _NN)
    y_ref[...] = _layer_norm(ALPHA * x + acc, ln_ref[0:1, :], ln_ref[1:2, :])


def _sffn_call(l, x, conv2, wts):
    M = x.shape[0]
    nb = M // DEC_SEQ
    wup, cwb, wdn, ln2 = wts

    def layer(shape):
        return pl.BlockSpec((None,) + shape, lambda i: (l,) + (0,) * len(shape), pipeline_mode=pl.Buffered(1))

    return pl.pallas_call(
        functools.partial(_sffn_kernel, nb=nb),
        grid=(1,),
        in_specs=[pl.BlockSpec((M, D_MODEL), lambda i: (0, 0)), layer((nb, CONV_W - 1, 2 * D_FF)),
                  layer((D_MODEL, 2 * D_FF)), layer((8, 2 * D_FF)), layer((D_FF, D_MODEL)), layer((2, D_MODEL))],
        out_specs=[pl.BlockSpec((M, D_MODEL), lambda i: (0, 0)),
                   pl.BlockSpec((nb, CONV_W - 1, 2 * D_FF), lambda i: (0, 0, 0))],
        out_shape=[jax.ShapeDtypeStruct((M, D_MODEL), F32),
                   jax.ShapeDtypeStruct((nb, CONV_W - 1, 2 * D_FF), F32)],
        compiler_params=pltpu.CompilerParams(dimension_semantics=("arbitrary",), vmem_limit_bytes=VMEM_LIMIT),
        name="ffn_sample",
    )(x, conv2, wup, cwb, wdn, ln2)


def _run_sample(x_sample, s_rw, s_sh, s_gl, s_rt, s_cv, p, ffn_w):
    bs, ts_, _ = x_sample.shape
    L = DEPTH

    def lanes(a):
        return jnp.broadcast_to(a[..., None], a.shape + (bs,))

    half = HEAD_DIM // 2
    inv = 1.0 / (ROPE_BASE ** jnp.linspace(0.0, 1.0, half, dtype=F32))
    ang = (PAST_LEN + jnp.arange(ts_)).astype(F32)[:, None] * inv[None]
    rot = lanes(jnp.stack([jnp.cos(ang), jnp.sin(ang)]))
    log_gamma = jnp.log(1.0 - jnp.exp2(-5.0 - jnp.arange(N_RET, dtype=F32)))
    gam = jnp.broadcast_to(jnp.exp(log_gamma)[:, None, None], (N_RET, 8, bs))
    vec = lanes(jnp.stack([p["rwkv_w0"], p["rwkv_a0"], p["rwkv_kk"], p["rwkv_ka"],
                           p["rwkv_rk"].reshape(L, RWKV_W), p["rwkv_lnw"], p["rwkv_lnb"]], axis=1))
    wts = (p["w_in_t"], lanes(p["rwkv_mu"]), vec,
           _bf(jnp.swapaxes(p["rwkv_bw"], 1, 2)), _bf(jnp.swapaxes(p["rwkv_ba"], 1, 2)),
           _bf(jnp.swapaxes(p["rwkv_bg"], 1, 2)), _bf(jnp.swapaxes(p["gla_bgk"], 1, 2)),
           lanes(p["gla_bgk_b"]), lanes(jnp.tile(p["gla_norm_w"], (1, N_GLA))), rot, gam,
           _bf(p["w_out"]), jnp.stack([p["ln1_g"], p["ln1_b"]], axis=1))
    rw_t = jnp.transpose(s_rw, (0, 2, 3, 4, 1))
    gl_t = jnp.transpose(s_gl, (0, 2, 3, 4, 1))
    rt_t = jnp.transpose(s_rt, (0, 2, 3, 4, 1))
    sh_t = jnp.swapaxes(s_sh, 1, 2)
    x = jnp.swapaxes(x_sample, 0, 1).reshape(ts_ * bs, D_MODEL)
    n_rw, n_sh, n_gl, n_rt, n_cv = [], [], [], [], []
    for l in range(L):
        x, sh, rw, gl, rt = _smix_call(l, x, sh_t, rw_t, gl_t, rt_t, wts)
        x, cv = _sffn_call(l, x, s_cv, ffn_w)
        n_sh.append(sh)
        n_rw.append(rw)
        n_gl.append(gl)
        n_rt.append(rt)
        n_cv.append(cv)
    y = jnp.swapaxes(x.reshape(ts_, bs, D_MODEL), 0, 1)
    back = (0, 4, 1, 2, 3)
    return y, (jnp.transpose(jnp.stack(n_rw), back), jnp.swapaxes(jnp.stack(n_sh), 1, 2),
               jnp.transpose(jnp.stack(n_gl), back), jnp.transpose(jnp.stack(n_rt), back),
               jnp.stack(n_cv))


def _pad_to(a, axis, n):
    pad = [(0, 0)] * a.ndim
    pad[axis] = (0, n - a.shape[axis])
    return jnp.pad(a, pad)


def _place(a, axis, segs, total):
    out = []
    pos = 0
    for src, w, dst in segs:
        if dst > pos:
            shp = list(a.shape)
            shp[axis] = dst - pos
            out.append(jnp.zeros(shp, a.dtype))
        out.append(lax.slice_in_dim(a, src, src + w, axis=axis))
        pos = dst + w
    if total > pos:
        shp = list(a.shape)
        shp[axis] = total - pos
        out.append(jnp.zeros(shp, a.dtype))
    return jnp.concatenate(out, axis=axis)


def _in_col_segments():
    g0 = RWKV_COLS
    t0 = RWKV_COLS + GLA_COLS
    segs = [(0, RWKV_COLS, 0)]
    segs += [(g0, GLA_K, RW_P), (g0 + GLA_K, GLA_K, RW_P + GQ_P),
             (g0 + 2 * GLA_K, GLA_V, RW_P + 2 * GQ_P), (g0 + 2 * GLA_K + GLA_V, GLA_V, RW_P + 2 * GQ_P + HW),
             (g0 + 2 * GLA_K + 2 * GLA_V, GLA_LORA, RW_P + 2 * GQ_P + 2 * HW)]
    segs += [(t0 + i * RET_W, RET_W, RW_P + GL_P + i * HW) for i in range(4)]
    return segs


def _prep_weights(w_in, rwkv_mu, rwkv_w0, rwkv_bw, rwkv_a0, rwkv_ba, rwkv_bg, rwkv_kk, rwkv_ka,
                  rwkv_rk, rwkv_lnw, rwkv_lnb, gla_bgk, gla_bgk_b, gla_norm_w, w_out,
                  ln1_g, ln1_b, ln2_g, ln2_b, ffn_up, ffn_conv_w, ffn_conv_b, ffn_down):
    L = w_in.shape[0]
    w1 = _place(_bf(w_in), 2, _in_col_segments(), NP)
    mu = _pad_to(rwkv_mu, 1, RW_P)[:, None, :]
    gnw = _pad_to(jnp.tile(gla_norm_w, (1, N_GLA)), 1, HW)
    vec = jnp.stack([rwkv_w0, rwkv_a0, rwkv_kk, rwkv_ka, rwkv_rk.reshape(L, RWKV_W), rwkv_lnw, rwkv_lnb, gnw], axis=1)
    bw = _bf(_pad_to(rwkv_bw, 1, LANE))
    ba = _bf(_place(rwkv_ba, 1, [(0, LORA_A, LORA_W)], LANE))
    bg = _bf(_pad_to(rwkv_bg, 1, 2 * LANE))
    bgk = _bf(_pad_to(_pad_to(gla_bgk, 1, LANE), 2, GQ_P))
    bgkb = _pad_to(gla_bgk_b, 1, GQ_P)[:, None, :]
    wout = _place(_bf(w_out), 1, [(0, RWKV_W + GLA_V, 0), (RWKV_W + GLA_V, RET_W, 2 * HW)], MIX_P)
    ln1 = jnp.stack([ln1_g, ln1_b], axis=1)
    mixer_w = (w1, mu, vec, bw, ba, bg, bgk, bgkb, wout, ln1)
    cwb = _pad_to(jnp.concatenate([ffn_conv_w, ffn_conv_b[:, None, :]], axis=1), 1, 8)
    ln2 = jnp.stack([ln2_g, ln2_b], axis=1)
    ffn_w = (_bf(ffn_up), cwb, _bf(ffn_down), ln2)
    return mixer_w, ffn_w


def _tables(n_pos, chunk):
    half = HEAD_DIM // 2
    inv = 1.0 / (ROPE_BASE ** jnp.linspace(0.0, 1.0, half, dtype=F32))
    ang = jnp.arange(n_pos).astype(F32)[:, None] * inv[None]
    cos, sin = jnp.cos(ang), jnp.sin(ang)
    cos = jnp.tile(jnp.concatenate([cos, cos], -1), (1, HW // HEAD_DIM))
    sin = jnp.tile(jnp.concatenate([-sin, sin], -1), (1, HW // HEAD_DIM))
    log_gamma = jnp.log(1.0 - jnp.exp2(-5.0 - jnp.arange(N_RET, dtype=F32)))
    i = jnp.arange(chunk, dtype=F32)
    diff = i[:, None] - i[None, :]
    causal = diff >= 0
    dmask = jnp.where(causal, jnp.exp(jnp.where(causal, diff, 0.0) * log_gamma[:, None, None]), 0.0)
    qd = jnp.exp((i + 1.0) * log_gamma[:, None])[..., None]
    kd = jnp.exp((chunk - 1.0 - i) * log_gamma[:, None])[..., None]
    cd = jnp.exp(chunk * log_gamma)[:, None, None]
    qd = jnp.broadcast_to(qd, (N_RET, chunk, HEAD_DIM))
    kd = jnp.broadcast_to(kd, (N_RET, chunk, HEAD_DIM))
    cd = jnp.broadcast_to(cd, (N_RET, SLOTS, HEAD_DIM))
    return cos, sin, dmask, qd, kd, cd


def _run_prompt(x, s_rw, s_sh, s_gl, s_rt, s_cv, mixer_w, ffn_w, *, tt_mix, tt_ffn, chunk):
    T = x.shape[1]
    tabs = _tables(T, chunk)
    shift_rows = _pad_to(_pad_to(s_sh, 2, RW_P)[:, :, None, :], 2, SLOTS)
    conv_rows = _place(s_cv, 2, [(0, CONV_W - 1, SLOTS - (CONV_W - 1))], SLOTS)
    n_rw, n_sh, n_gl, n_rt, n_cv = [], [], [], [], []
    for l in range(DEPTH):
        x, sh, rw, gl, rt = _mixer_call(l, x, shift_rows, s_rw, s_gl, s_rt, tabs, mixer_w, tt=tt_mix, chunk=chunk)
        x, cv = _ffn_call(l, x, conv_rows, ffn_w, tt=tt_ffn)
        n_sh.append(sh[:, SLOTS - 1, :RWKV_COLS])
        n_cv.append(cv[:, SLOTS - (CONV_W - 1):])
        n_rw.append(rw)
        n_gl.append(gl)
        n_rt.append(rt)
    return x, (jnp.stack(n_rw), jnp.stack(n_sh), jnp.stack(n_gl), jnp.stack(n_rt), jnp.stack(n_cv))


def kernel(x_prompt, x_sample, state_rwkv, state_shift, state_gla, state_ret, state_conv, w_in, rwkv_mu, rwkv_w0, rwkv_bw, rwkv_a0, rwkv_ba, rwkv_bg, rwkv_kk, rwkv_ka, rwkv_rk, rwkv_lnw, rwkv_lnb, gla_bgk, gla_bgk_b, gla_norm_w, w_out, ln1_g, ln1_b, ln2_g, ln2_b, ffn_up, ffn_conv_w, ffn_conv_b, ffn_down):
    w_in_t = _bf(jnp.swapaxes(w_in, 1, 2))
    mixer_w, ffn_w = _prep_weights(jnp.swapaxes(w_in_t, 1, 2), rwkv_mu, rwkv_w0, rwkv_bw, rwkv_a0, rwkv_ba, rwkv_bg, rwkv_kk, rwkv_ka,
                                   rwkv_rk, rwkv_lnw, rwkv_lnb, gla_bgk, gla_bgk_b, gla_norm_w, w_out,
                                   ln1_g, ln1_b, ln2_g, ln2_b, ffn_up, ffn_conv_w, ffn_conv_b, ffn_down)
    bp, tp, _ = x_prompt.shape
    bs, ts, _ = x_sample.shape
    assert ts == DEC_SEQ

    def zeros_like_state(s):
        return jnp.zeros((s.shape[0], bp) + s.shape[2:], F32)

    chunk = math.gcd(tp, CHUNK)
    y_p, st_p = _run_prompt(
        x_prompt, zeros_like_state(state_rwkv), zeros_like_state(state_shift), zeros_like_state(state_gla),
        zeros_like_state(state_ret), zeros_like_state(state_conv), mixer_w, ffn_w,
        tt_mix=min(MIX_ROWS, tp), tt_ffn=min(FFN_ROWS, tp), chunk=chunk)

    raw = dict(w_in_t=w_in_t, rwkv_mu=rwkv_mu, rwkv_w0=rwkv_w0, rwkv_bw=rwkv_bw, rwkv_a0=rwkv_a0, rwkv_ba=rwkv_ba,
               rwkv_bg=rwkv_bg, rwkv_kk=rwkv_kk, rwkv_ka=rwkv_ka, rwkv_rk=rwkv_rk, rwkv_lnw=rwkv_lnw,
               rwkv_lnb=rwkv_lnb, gla_bgk=gla_bgk, gla_bgk_b=gla_bgk_b, gla_norm_w=gla_norm_w, w_out=w_out,
               ln1_g=ln1_g, ln1_b=ln1_b)
    y_s, st_s = _run_sample(x_sample, state_rwkv, state_shift, state_gla, state_ret, state_conv, raw, ffn_w)
    return (y_p, y_s) + st_p + st_s
```

```python
import functools
import math

import jax
import jax.numpy as jnp
from jax import lax
from jax.experimental import pallas as pl
from jax.experimental.pallas import tpu as pltpu

F32 = jnp.float32
BF16 = jnp.bfloat16

D_MODEL = 1024
DEPTH = 4
PAST_LEN = 16384
DEC_SEQ = 4
HEAD_DIM = 64
N_HEADS = D_MODEL // HEAD_DIM
N_GLA = (5 * N_HEADS) // 16
N_RET = (5 * N_HEADS) // 16
N_RWKV = N_HEADS - N_GLA - N_RET
RWKV_W = N_RWKV * HEAD_DIM
GLA_DK = HEAD_DIM // 2
GLA_K = N_GLA * GLA_DK
GLA_V = N_GLA * HEAD_DIM
RET_W = N_RET * HEAD_DIM
LORA_W = 64
LORA_A = 64
LORA_G = 160
GLA_LORA = 16
GLA_GATE_NORM = 16.0
CHUNK = 64
D_FF = 2816
CONV_W = 3
ALPHA = (2 * DEPTH) ** 0.25
RWKV_GN_EPS = 64e-5
LN_EPS = 1e-5
RMS_EPS = 1e-6
ROPE_BASE = 10000.0
RWKV_COLS = 3 * RWKV_W + LORA_W + LORA_A + LORA_G
GLA_COLS = 2 * GLA_K + 2 * GLA_V + GLA_LORA
RET_COLS = 4 * RET_W

LANE = 128
HW = 384
RW_P = 1536
GQ_P = 256
GL_P = 2 * GQ_P + 2 * HW
RT_P = 4 * HW
NP = RW_P + GL_P + RT_P
MIX_P = 3 * HW
SLOTS = 8

VMEM_LIMIT = 56 * 1024 * 1024
MIX_ROWS = 512
FFN_ROWS = 256


_NN = (((1,), (0,)), ((), ()))
_NT = (((1,), (1,)), ((), ()))
_TN = (((0,), (0,)), ((), ()))


def _bf(x):
    return x.astype(BF16)


def _dg(a, b, dims):
    return lax.dot_general(a, b, dims, preferred_element_type=F32)


def _dot(a, b, dims=_NN):
    return _dg(_bf(a), _bf(b), dims)


def _split(x, n):
    parts = []
    r = x
    for i in range(n):
        h = _bf(r)
        parts.append(h)
        if i + 1 < n:
            r = r - h.astype(F32)
    return parts


def _sel_dot(sel_bf, b, n=3, dims=_NN):
    out = None
    for h in _split(b, n):
        t = _dg(sel_bf, h, dims)
        out = t if out is None else out + t
    return out


def _layer_norm(h, g, b):
    mu = jnp.mean(h, -1, keepdims=True)
    d = h - mu
    var = jnp.mean(d * d, -1, keepdims=True)
    return d * lax.rsqrt(var + LN_EPS) * g + b


def _swap_halves(x):
    pieces = []
    for i in range(x.shape[1] // LANE):
        p = x[:, i * LANE:(i + 1) * LANE]
        up = pltpu.roll(p, LANE - HEAD_DIM // 2, axis=1)
        dn = pltpu.roll(p, HEAD_DIM // 2, axis=1)
        lane = lax.broadcasted_iota(jnp.int32, p.shape, 1)
        pieces.append(jnp.where((lane % HEAD_DIM) < HEAD_DIM // 2, up, dn))
    return jnp.concatenate(pieces, axis=1)


def _mixer_kernel(x_ref, shift_ref, srw_ref, sgl_ref, srt_ref, cos_ref, sin_ref,
                  dmask_ref, qd_ref, kd_ref, cd_ref,
                  w1_ref, mu_ref, vec_ref, bw_ref, ba_ref, bg_ref, bgk_ref, bgkb_ref,
                  wout_ref, ln_ref,
                  x1_ref, shift_out_ref, srw_out, sgl_out, srt_out,
                  carry_ref, rw_r, rw_k, rw_v, rw_lw, rw_a, rw_b, rw_g, rw_bonus, rw_y,
                  gl_q, gl_k, gl_v, gl_lg, gl_gate, gl_o,
                  rt_q, rt_k, rt_v, rt_gate, rt_o,
                  *, rows, chunk, group):
    R, C = rows, chunk
    G = group * C
    j = pl.program_id(1)

    @pl.when(j == 0)
    def _():
        srw_out[...] = srw_ref[...]
        sgl_out[...] = sgl_ref[...]
        srt_out[...] = srt_ref[...]

    x = x_ref[0]
    xb = _bf(x)
    row = lax.broadcasted_iota(jnp.int32, (R, 1), 0)
    hi_ = lax.broadcasted_iota(jnp.int32, (2 * LANE, 2 * LANE), 0) // HEAD_DIM
    hj_ = lax.broadcasted_iota(jnp.int32, (2 * LANE, 2 * LANE), 1) // HEAD_DIM
    head_ones = (hi_ == hj_).astype(BF16)

    def head_sum(v):
        return jnp.concatenate([_dg(_bf(v[:, 0:2 * LANE]), head_ones, _NN),
                                _dg(_bf(v[:, 2 * LANE:HW]), head_ones[0:LANE, 0:LANE], _NN)], axis=1)

    p = _dg(xb, w1_ref[:, 0:RW_P], _NN)
    first = jnp.where(j == 0, shift_ref[0, 0:1, :], carry_ref[SLOTS - 1:SLOTS, :])
    prev = jnp.where(row == 0, first, pltpu.roll(p, 1, axis=0))
    carry_ref[...] = p[R - SLOTS:R, :]
    shift_out_ref[0] = p[R - SLOTS:R, :]
    pm = p + (prev - p) * mu_ref[...]
    r = pm[:, 0:HW]
    k = pm[:, HW:2 * HW]
    v = pm[:, 2 * HW:3 * HW]
    wa = pm[:, 3 * HW:3 * HW + LANE]
    xg = pm[:, 3 * HW + LANE:RW_P]
    w0 = vec_ref[0:1, :]
    a0 = vec_ref[1:2, :]
    k_k = vec_ref[2:3, :]
    k_a = vec_ref[3:4, :]
    r_k = vec_ref[4:5, :]
    w_log = -jax.nn.softplus(-(w0 + _dg(_bf(jnp.tanh(wa)), bw_ref[...], _NN))) - 0.5
    log_decay = -jnp.exp(w_log)
    a = jax.nn.sigmoid(a0 + _dg(_bf(wa), ba_ref[...], _NN))
    g = _dg(_bf(jax.nn.sigmoid(xg)), bg_ref[...], _NN)
    kk = k * k_k
    kk = kk / jnp.maximum(jnp.sqrt(head_sum(kk * kk)), 1e-12)
    k2 = k * (1.0 + (a - 1.0) * k_a)
    rw_r[...] = r
    rw_k[...] = k2
    rw_v[...] = v
    rw_lw[...] = log_decay
    rw_a[...] = -kk
    rw_b[...] = kk * a
    rw_g[...] = g
    rw_bonus[...] = head_sum(r * k2 * r_k) * v

    p = _dg(xb, w1_ref[:, RW_P:RW_P + GL_P], _NN)
    lg = jax.nn.log_sigmoid(_dg(_bf(xg), bgk_ref[...], _NN) + bgkb_ref[...]) / GLA_GATE_NORM
    gl_q[...] = p[:, 0:GQ_P] * GLA_DK ** -0.5
    gl_k[...] = p[:, GQ_P:2 * GQ_P]
    gl_v[...] = p[:, 2 * GQ_P:2 * GQ_P + HW]
    gl_lg[...] = lg
    gl_gate[...] = jax.nn.silu(p[:, 2 * GQ_P + HW:2 * GQ_P + 2 * HW])

    p = _dg(xb, w1_ref[:, RW_P + GL_P:NP], _NN)
    cos = cos_ref[...]
    sin = sin_ref[...]
    q = p[:, 0:HW]
    k = p[:, HW:2 * HW]
    rt_q[...] = q * cos + _swap_halves(q) * sin
    rt_k[...] = (k * cos + _swap_halves(k) * sin) * HEAD_DIM ** -0.5
    rt_v[...] = p[:, 2 * HW:3 * HW]
    rt_gate[...] = jax.nn.silu(p[:, 3 * HW:4 * HW])

    CS = min(G, 2 * LANE)
    gi = lax.broadcasted_iota(jnp.int32, (CS, CS), 0)
    gj = lax.broadcasted_iota(jnp.int32, (CS, CS), 1)
    cum_sel = (((gi // C) == (gj // C)) & (gi >= gj)).astype(BF16)

    def chunk_cumsum(v):
        cum = jnp.concatenate([_sel_dot(cum_sel, v[i * CS:(i + 1) * CS], n=2) for i in range(G // CS)], axis=0)
        tot = jnp.concatenate([jnp.broadcast_to(cum[(c + 1) * C - 1:(c + 1) * C, :], (C, v.shape[1]))
                               for c in range(group)], axis=0)
        return cum, tot
    ii = lax.broadcasted_iota(jnp.int32, (C, C), 0)
    jj = lax.broadcasted_iota(jnp.int32, (C, C), 1)
    tri_incl = ii >= jj
    tri_strict = ii > jj
    eye = (ii == jj).astype(F32)
    levels = []
    s = 1
    while s < C:
        levels.append(((ii // (2 * s)) == (jj // (2 * s))) & ((ii % (2 * s)) >= s) & ((jj % (2 * s)) < s))
        s *= 2
    ones_cv = jnp.ones((C, HEAD_DIM), BF16)
    chunks = range(group)

    def group_body(gidx, carry):
        g0 = pl.multiple_of(gidx * G, G)
        gs = pl.ds(g0, G)

        def cr(c):
            return slice(c * C, (c + 1) * C)

        lw = rw_lw[gs, :]
        cum, tot = chunk_cumsum(lw)
        w_inv = jnp.exp(-cum)
        w_rem = jnp.exp(tot - cum)
        r_t = _bf(rw_r[gs, :] * jnp.exp(cum))
        a_t = _bf(rw_a[gs, :] * jnp.exp(cum - lw))
        b_t = _bf(rw_b[gs, :] * w_inv)
        k_t = _bf(rw_k[gs, :] * w_inv)
        b_h = _bf(rw_b[gs, :] * w_rem)
        k_h = _bf(rw_k[gs, :] * w_rem)
        w_c = jnp.exp(tot)
        v_rw = _bf(rw_v[gs, :])
        items = [(c, h) for c in chunks for h in range(N_RWKV)]

        def hs(h):
            return slice(h * HEAD_DIM, (h + 1) * HEAD_DIM)

        ar = [jnp.concatenate([a_t[cr(c), hs(h)], r_t[cr(c), hs(h)]], axis=0) for c, h in items]
        xb_ = [_dot(ar[i], b_t[cr(c), hs(h)], _NT) for i, (c, h) in enumerate(items)]
        xk_ = [_dot(ar[i], k_t[cr(c), hs(h)], _NT) for i, (c, h) in enumerate(items)]
        a_ab = [jnp.where(tri_strict, t[0:C], 0.0) for t in xb_]
        a_ak = [jnp.where(tri_strict, t[0:C], 0.0) for t in xk_]
        p_rb = [jnp.where(tri_incl, t[C:2 * C], 0.0) for t in xb_]
        p_rk = [jnp.where(tri_incl, t[C:2 * C], 0.0) for t in xk_]

        lg = gl_lg[gs, :]
        gcum, gtot = chunk_cumsum(lg)
        q_in = _bf(gl_q[gs, :] * jnp.exp(gcum))
        k_in = _bf(gl_k[gs, :] * jnp.exp(-gcum))
        k_st = _bf(gl_k[gs, :] * jnp.exp(gtot - gcum))
        v_gl = _bf(gl_v[gs, :])
        g_col = []
        for c in chunks:
            parts = _split(lg[cr(c), :], 2)
            g_col.append(jnp.exp(_dg(parts[0], ones_cv, _TN) + _dg(parts[1], ones_cv, _TN)))
        gitems = [(c, h) for c in chunks for h in range(N_GLA)]

        def ks(h):
            return slice(h * GLA_DK, (h + 1) * GLA_DK)

        g_att = [jnp.where(tri_incl, _dot(q_in[cr(c), ks(h)], k_in[cr(c), ks(h)], _NT), 0.0) for c, h in gitems]
        g_kv = [_dot(k_st[cr(c), ks(h)], v_gl[cr(c), hs(h)], _TN) for c, h in gitems]

        q_rt = rt_q[gs, :]
        k_rt = rt_k[gs, :]
        v_rt = _bf(rt_v[gs, :])
        q_rb, k_rb = _bf(q_rt), _bf(k_rt)
        ritems = [(c, h) for c in chunks for h in range(N_RET)]
        r_att = [_dot(q_rb[cr(c), hs(h)], k_rb[cr(c), hs(h)], _NT) * dmask_ref[h] for c, h in ritems]
        r_kv = [_dot(k_rt[cr(c), hs(h)] * kd_ref[h], v_rt[cr(c), hs(h)], _TN) for c, h in ritems]

        m = [eye + jnp.where(levels[0], t, 0.0) for t in a_ab]
        for lvl in levels[1:]:
            t_ = [_dot(jnp.where(lvl, a_ab[i], 0.0), m[i]) for i in range(len(items))]
            m = [m[i] + _dot(m[i], t_[i]) for i in range(len(items))]

        g_av = [_dot(g_att[i], v_gl[cr(c), hs(h)]) for i, (c, h) in enumerate(gitems)]
        r_av = [_dot(r_att[i], v_rt[cr(c), hs(h)]) for i, (c, h) in enumerate(ritems)]

        vv = [v_rw[cr(c), hs(h)] for c, h in items]
        akv = [_dot(a_ak[i], vv[i]) for i in range(len(items))]
        at2 = [_dot(m[i], a_t[cr(c), hs(h)]) for i, (c, h) in enumerate(items)]
        y0 = [_dot(p_rk[i], vv[i]) for i in range(len(items))]
        u0 = [_dot(m[i], akv[i]) for i in range(len(items))]
        gp = [_dot(at2[i], b_h[cr(c), hs(h)], _TN) for i, (c, h) in enumerate(items)]
        hh = [_dot(jnp.concatenate([_bf(u0[i]), vv[i]], axis=0),
                   jnp.concatenate([b_h[cr(c), hs(h)], k_h[cr(c), hs(h)]], axis=0), _TN)
              for i, (c, h) in enumerate(items)]

        cur = [srw_out[0, h] for h in range(N_RWKV)]
        g_cur = [sgl_out[0, h] for h in range(N_GLA)]
        r_cur = [srt_out[0, h] for h in range(N_RET)]
        for c in chunks:
            s0 = list(cur)
            for h in range(N_RWKV):
                i = c * N_RWKV + h
                cur[h] = cur[h] * w_c[c * C:c * C + 1, hs(h)] + _dot(cur[h], gp[i]) + hh[i]
            us = [_dot(jnp.concatenate([_bf(at2[c * N_RWKV + h]), r_t[cr(c), hs(h)]], axis=0), s0[h], _NT)
                  for h in range(N_RWKV)]
            g_o = [g_av[c * N_GLA + h] + _dot(q_in[cr(c), ks(h)], g_cur[h]) for h in range(N_GLA)]
            r_o = [r_av[c * N_RET + h] + _dot(q_rt[cr(c), hs(h)] * qd_ref[h], r_cur[h]) for h in range(N_RET)]
            y = [us[h][C:2 * C] + _dot(p_rb[c * N_RWKV + h], us[h][0:C] + u0[c * N_RWKV + h]) + y0[c * N_RWKV + h]
                 for h in range(N_RWKV)]
            g_cur = [g_cur[h] * g_col[c][ks(h), :] + g_kv[c * N_GLA + h] for h in range(N_GLA)]
            r_cur = [r_cur[h] * cd_ref[h, 0:1, :] + r_kv[c * N_RET + h] for h in range(N_RET)]
            rows_c = pl.ds(g0 + c * C, C)
            rw_y[rows_c, :] = jnp.concatenate(y, axis=1)
            gl_o[rows_c, :] = jnp.concatenate(g_o + [jnp.zeros((C, HW - GLA_V), F32)], axis=1)
            rt_o[rows_c, :] = jnp.concatenate(r_o + [jnp.zeros((C, HW - RET_W), F32)], axis=1)
        for h in range(N_RWKV):
            srw_out[0, h] = cur[h]
        for h in range(N_GLA):
            sgl_out[0, h] = g_cur[h]
        for h in range(N_RET):
            srt_out[0, h] = r_cur[h]
        return carry

    if R == G:
        group_body(0, 0)
    else:
        lax.fori_loop(0, R // G, group_body, 0)

    ln_w = vec_ref[5:6, :]
    ln_b = vec_ref[6:7, :]
    gn_w = vec_ref[7:8, :]
    inv_hd = 1.0 / HEAD_DIM
    y = rw_y[...]
    ym = head_sum(y) * inv_hd
    d = y - ym
    yv = head_sum(d * d) * inv_hd
    ya = (d * lax.rsqrt(yv + RWKV_GN_EPS) * ln_w + ln_b + rw_bonus[...]) * rw_g[...]
    o = gl_o[...]
    ob = o * lax.rsqrt(head_sum(o * o) * inv_hd + RMS_EPS) * gn_w * gl_gate[...]
    o = rt_o[...]
    oc = o * lax.rsqrt(head_sum(o * o) * inv_hd + RMS_EPS) * rt_gate[...]
    mix = _dg(_bf(jnp.concatenate([ya, ob, oc], axis=1)), wout_ref[...], _NN)
    out = _layer_norm(ALPHA * x + mix, ln_ref[0:1, :], ln_ref[1:2, :])
    x1_ref[0] = out


def _const_spec(shape):
    nd = len(shape)
    return pl.BlockSpec(shape, lambda i, j: (0,) * nd, pipeline_mode=pl.Buffered(1))


def _layer_spec(l, shape):
    return pl.BlockSpec((None,) + shape, lambda i, j: (l,) + (0,) * len(shape), pipeline_mode=pl.Buffered(1))


def _mixer_call(l, x, shift_rows, s_rw, s_gl, s_rt, tabs, wts, *, tt, chunk):
    B, T, _ = x.shape
    bb, R = 1, tt
    nb, nt = B, T // tt
    group = tt // chunk
    cos, sin, dmask, qd, kd, cd = tabs
    (w1, mu, vec, bw, ba, bg, bgk, bgkb, wout, ln1) = wts
    C = chunk
    lw = functools.partial(_layer_spec, l)

    in_specs = [
        pl.BlockSpec((bb, tt, D_MODEL), lambda i, j: (i, j, 0)),
        pl.BlockSpec((None, bb, SLOTS, RW_P), lambda i, j: (l, i, 0, 0)),
        pl.BlockSpec((None, bb, N_RWKV, HEAD_DIM, HEAD_DIM), lambda i, j: (l, i, 0, 0, 0)),
        pl.BlockSpec((None, bb, N_GLA, GLA_DK, HEAD_DIM), lambda i, j: (l, i, 0, 0, 0)),
        pl.BlockSpec((None, bb, N_RET, HEAD_DIM, HEAD_DIM), lambda i, j: (l, i, 0, 0, 0)),
        pl.BlockSpec((R, HW), lambda i, j: (j, 0)),
        pl.BlockSpec((R, HW), lambda i, j: (j, 0)),
        _const_spec((N_RET, C, C)),
        _const_spec((N_RET, C, HEAD_DIM)),
        _const_spec((N_RET, C, HEAD_DIM)),
        _const_spec((N_RET, SLOTS, HEAD_DIM)),
        lw((D_MODEL, NP)), lw((1, RW_P)), lw((8, HW)), lw((LANE, HW)), lw((LANE, HW)), lw((2 * LANE, HW)),
        lw((2 * LANE, GQ_P)), lw((1, GQ_P)), lw((MIX_P, D_MODEL)), lw((2, D_MODEL)),
    ]
    out_specs = [
        pl.BlockSpec((bb, tt, D_MODEL), lambda i, j: (i, j, 0)),
        pl.BlockSpec((bb, SLOTS, RW_P), lambda i, j: (i, 0, 0)),
        pl.BlockSpec((bb, N_RWKV, HEAD_DIM, HEAD_DIM), lambda i, j: (i, 0, 0, 0)),
        pl.BlockSpec((bb, N_GLA, GLA_DK, HEAD_DIM), lambda i, j: (i, 0, 0, 0)),
        pl.BlockSpec((bb, N_RET, HEAD_DIM, HEAD_DIM), lambda i, j: (i, 0, 0, 0)),
    ]
    out_shape = [
        jax.ShapeDtypeStruct((B, T, D_MODEL), F32),
        jax.ShapeDtypeStruct((B, SLOTS, RW_P), F32),
        jax.ShapeDtypeStruct(s_rw.shape[1:], F32),
        jax.ShapeDtypeStruct(s_gl.shape[1:], F32),
        jax.ShapeDtypeStruct(s_rt.shape[1:], F32),
    ]
    scratch = [pltpu.VMEM((SLOTS, RW_P), F32)]
    scratch += [pltpu.VMEM((R, HW), F32)] * 9
    scratch += [pltpu.VMEM((R, GQ_P), F32), pltpu.VMEM((R, GQ_P), F32), pltpu.VMEM((R, HW), F32),
                pltpu.VMEM((R, GQ_P), F32), pltpu.VMEM((R, HW), F32), pltpu.VMEM((R, HW), F32)]
    scratch += [pltpu.VMEM((R, HW), F32)] * 5
    return pl.pallas_call(
        functools.partial(_mixer_kernel, rows=R, chunk=C, group=group),
        grid=(nb, nt),
        in_specs=in_specs, out_specs=out_specs, out_shape=out_shape,
        scratch_shapes=scratch,
        compiler_params=pltpu.CompilerParams(dimension_semantics=("arbitrary", "arbitrary"),
                                             vmem_limit_bytes=VMEM_LIMIT),
        name="mixer_prompt",
    )(x, shift_rows, s_rw, s_gl, s_rt, cos, sin, dmask, qd, kd, cd,
      w1, mu, vec, bw, ba, bg, bgk, bgkb, wout, ln1)


FFN_CHUNK = 256
FFN_AHEAD = 3


def _ffn_kernel(x_ref, conv_ref, wup_ref, cwb_ref, wdn_ref, ln_ref, y_ref, cv_out_ref, ubuf, *, rows):
    R = rows
    j = pl.program_id(1)

    @pl.when(j == 0)
    def _():
        ubuf[0:SLOTS, :] = conv_ref[0]

    x = x_ref[0]
    xb = _bf(x)
    n_f = D_FF // FFN_CHUNK

    def cols(f, base):
        return slice(base + f * FFN_CHUNK, base + (f + 1) * FFN_CHUNK)

    def up(f):
        return [_dg(xb, wup_ref[:, cols(f, base)], _NN) for base in (0, D_FF)]

    def conv(u, cs):
        ubuf[SLOTS:SLOTS + R, cs] = u
        u1 = ubuf[SLOTS - 1:SLOTS - 1 + R, cs]
        u2 = ubuf[SLOTS - 2:SLOTS - 2 + R, cs]
        ubuf[0:SLOTS, cs] = u[R - SLOTS:R]
        cv_out_ref[0, :, cs] = u[R - SLOTS:R]
        return cwb_ref[3:4, cs] + (cwb_ref[0:1, cs] * u2 + cwb_ref[1:2, cs] * u1 + cwb_ref[2:3, cs] * u)

    acc = jnp.zeros((R, D_MODEL), F32)
    u_queue = [up(f) for f in range(min(FFN_AHEAD, n_f))]
    for f in range(n_f):
        if f + FFN_AHEAD < n_f:
            u_queue.append(up(f + FFN_AHEAD))
        u_cur = u_queue.pop(0)
        hid = jax.nn.gelu(conv(u_cur[0], cols(f, 0))) * conv(u_cur[1], cols(f, D_FF))
        acc = acc + _dg(_bf(hid), wdn_ref[f * FFN_CHUNK:(f + 1) * FFN_CHUNK, :], _NN)
    y_ref[0] = _layer_norm(ALPHA * x + acc, ln_ref[0:1, :], ln_ref[1:2, :])


def _ffn_call(l, x, conv_rows, wts, *, tt):
    B, T, _ = x.shape
    bb, nb, nt = 1, B, T // tt
    wup, cwb, wdn, ln2 = wts
    lw = functools.partial(_layer_spec, l)
    return pl.pallas_call(
        functools.partial(_ffn_kernel, rows=tt),
        grid=(nb, nt),
        in_specs=[
            pl.BlockSpec((bb, tt, D_MODEL), lambda i, j: (i, j, 0)),
            pl.BlockSpec((None, bb, SLOTS, 2 * D_FF), lambda i, j: (l, i, 0, 0)),
            lw((D_MODEL, 2 * D_FF)), lw((8, 2 * D_FF)), lw((D_FF, D_MODEL)), lw((2, D_MODEL)),
        ],
        out_specs=[
            pl.BlockSpec((bb, tt, D_MODEL), lambda i, j: (i, j, 0)),
            pl.BlockSpec((bb, SLOTS, 2 * D_FF), lambda i, j: (i, 0, 0)),
        ],
        out_shape=[jax.ShapeDtypeStruct((B, T, D_MODEL), F32),
                   jax.ShapeDtypeStruct((B, SLOTS, 2 * D_FF), F32)],
        scratch_shapes=[pltpu.VMEM((SLOTS + tt, 2 * D_FF), F32)],
        compiler_params=pltpu.CompilerParams(dimension_semantics=("arbitrary", "arbitrary"),
                                             vmem_limit_bytes=VMEM_LIMIT),
        name="ffn_prompt",
    )(x, conv_rows, wup, cwb, wdn, ln2)


N_STEPS = N_RWKV + N_GLA + N_RET
G0 = RWKV_COLS
T0 = RWKV_COLS + GLA_COLS


def _smix_kernel(x_ref, shift_ref, srw_ref, sgl_ref, srt_ref, w1t_ref, mu_ref, vec_ref,
                 bwt_ref, bat_ref, bgt_ref, bgkt_ref, gb_ref, gn_ref, rot_ref, gam_ref, wout_ref, ln_ref,
                 x1_ref, shift_out_ref, srw_out, sgl_out, srt_out,
                 rw_r, rw_k, rw_v, rw_w, rw_a, rw_b, rw_g, rw_bonus, rw_y,
                 gl_q, gl_k, gl_v, gl_g, gl_gate, gl_o, rt_q, rt_k, rt_v, rt_gate, rt_o, *, nb):
    s = pl.program_id(0)
    M = DEC_SEQ * nb

    def ts(t):
        return slice(t * nb, (t + 1) * nb)

    def tile_t(c):
        return jnp.concatenate([c] * DEC_SEQ, axis=1)

    def hrows(h):
        return slice(h * HEAD_DIM, (h + 1) * HEAD_DIM)

    @pl.when(s == 0)
    def _():
        xb = _bf(x_ref[...])

        def proj(r0, r1):
            return _dg(w1t_ref[r0:r1, :], xb, _NT)

        p = proj(0, RWKV_COLS)
        shift_out_ref[...] = p[:, (DEC_SEQ - 1) * nb:]
        prev = jnp.concatenate([shift_ref[...], p[:, :(DEC_SEQ - 1) * nb]], axis=1)
        pm = p + (prev - p) * tile_t(mu_ref[...])
        r = pm[0:RWKV_W]
        k = pm[RWKV_W:2 * RWKV_W]
        v = pm[2 * RWKV_W:3 * RWKV_W]
        xw = pm[3 * RWKV_W:3 * RWKV_W + LORA_W]
        xa = pm[3 * RWKV_W + LORA_W:3 * RWKV_W + LORA_W + LORA_A]
        xg = pm[3 * RWKV_W + LORA_W + LORA_A:RWKV_COLS]
        w0, a0, k_k, k_a, r_k = (tile_t(vec_ref[i]) for i in range(5))
        w_log = -jax.nn.softplus(-(w0 + _dg(bwt_ref[...], _bf(jnp.tanh(xw)), _NN))) - 0.5
        a = jax.nn.sigmoid(a0 + _dg(bat_ref[...], _bf(xa), _NN))
        kk = k * k_k
        k2 = k * (1.0 + (a - 1.0) * k_a)
        rk2 = r * k2 * r_k
        for h in range(N_RWKV):
            hs = hrows(h)
            kh = kk[hs]
            kh = kh / jnp.maximum(jnp.sqrt(jnp.sum(kh * kh, axis=0, keepdims=True)), 1e-12)
            rw_a[hs, :] = -kh
            rw_b[hs, :] = kh * a[hs]
            rw_bonus[hs, :] = jnp.sum(rk2[hs], axis=0, keepdims=True) * v[hs]
        rw_r[...] = r
        rw_k[...] = k2
        rw_v[...] = v
        rw_w[...] = jnp.exp(-jnp.exp(w_log))
        rw_g[...] = _dg(bgt_ref[...], _bf(jax.nn.sigmoid(xg)), _NN)

        p = proj(G0, T0)
        gk = p[2 * GLA_K + 2 * GLA_V:GLA_COLS]
        lg = jax.nn.log_sigmoid(_dg(bgkt_ref[...], _bf(gk), _NN) + tile_t(gb_ref[...])) / GLA_GATE_NORM
        gl_q[...] = p[0:GLA_K] * GLA_DK ** -0.5
        gl_k[...] = p[GLA_K:2 * GLA_K]
        gl_v[...] = p[2 * GLA_K:2 * GLA_K + GLA_V]
        gl_g[...] = jnp.exp(lg)
        gl_gate[...] = jax.nn.silu(p[2 * GLA_K + GLA_V:2 * GLA_K + 2 * GLA_V])

        p = proj(T0, T0 + RET_COLS)
        cos = jnp.concatenate([rot_ref[0, t] for t in range(DEC_SEQ)], axis=1)
        sin = jnp.concatenate([rot_ref[1, t] for t in range(DEC_SEQ)], axis=1)
        half = HEAD_DIM // 2

        def rot(xh):
            x1, x2 = xh[0:half], xh[half:HEAD_DIM]
            return jnp.concatenate([x1 * cos - x2 * sin, x1 * sin + x2 * cos], axis=0)

        for h in range(N_RET):
            hs = hrows(h)
            rt_q[hs, :] = rot(p[hs])
            rt_k[hs, :] = rot(p[RET_W + h * HEAD_DIM:RET_W + (h + 1) * HEAD_DIM]) * HEAD_DIM ** -0.5
        rt_v[...] = p[2 * RET_W:3 * RET_W]
        rt_gate[...] = jax.nn.silu(p[3 * RET_W:4 * RET_W])

    @pl.when(s < N_RWKV)
    def _():
        r0 = pl.multiple_of(s * HEAD_DIM, HEAD_DIM)
        hs = pl.ds(r0, HEAD_DIM)

        def v_group(i, carry):
            v0 = pl.multiple_of(i * 8, 8)
            vt = [rw_v[pl.ds(r0 + v0, 8), ts(t)] for t in range(DEC_SEQ)]
            ys = [[] for _ in range(DEC_SEQ)]
            for j in range(8):
                S = srw_ref[v0 + j]
                for t in range(DEC_SEQ):
                    sa = jnp.sum(S * rw_a[hs, ts(t)], axis=0, keepdims=True)
                    S = S * rw_w[hs, ts(t)] + sa * rw_b[hs, ts(t)] + vt[t][j:j + 1, :] * rw_k[hs, ts(t)]
                    ys[t].append(jnp.sum(S * rw_r[hs, ts(t)], axis=0, keepdims=True))
                srw_out[v0 + j] = S
            for t in range(DEC_SEQ):
                rw_y[pl.ds(r0 + v0, 8), ts(t)] = jnp.concatenate(ys[t], axis=0)
            return carry

        lax.fori_loop(0, HEAD_DIM // 8, v_group, 0)

    def kv_head(s_in, s_out, q_ref, k_ref, v_ref, o_ref, decay_rows, k0, v0, nk):
        for t in range(DEC_SEQ):
            src = s_in if t == 0 else s_out
            v_t = v_ref[pl.ds(v0, HEAD_DIM), ts(t)]

            def k_group(i, o, t=t, src=src, v_t=v_t):
                kg = pl.multiple_of(i * 8, 8)
                q8 = q_ref[pl.ds(k0 + kg, 8), ts(t)]
                k8 = k_ref[pl.ds(k0 + kg, 8), ts(t)]
                d8 = decay_rows(kg, t)
                for j in range(8):
                    S = src[kg + j] * d8[j:j + 1, :] + k8[j:j + 1, :] * v_t
                    s_out[kg + j] = S
                    o = o + q8[j:j + 1, :] * S
                return o

            o_ref[pl.ds(v0, HEAD_DIM), ts(t)] = lax.fori_loop(0, nk // 8, k_group, jnp.zeros((HEAD_DIM, nb), F32))

    @pl.when((s >= N_RWKV) & (s < N_RWKV + N_GLA))
    def _():
        h = s - N_RWKV
        k0 = pl.multiple_of(h * GLA_DK, GLA_DK)
        v0 = pl.multiple_of(h * HEAD_DIM, HEAD_DIM)
        kv_head(sgl_ref, sgl_out, gl_q, gl_k, gl_v, gl_o,
                lambda kg, t: gl_g[pl.ds(k0 + kg, 8), ts(t)], k0, v0, GLA_DK)

    @pl.when(s >= N_RWKV + N_GLA)
    def _():
        h = s - (N_RWKV + N_GLA)
        v0 = pl.multiple_of(h * HEAD_DIM, HEAD_DIM)
        gamma = gam_ref[h]
        kv_head(srt_ref, srt_out, rt_q, rt_k, rt_v, rt_o, lambda kg, t: gamma, v0, v0, HEAD_DIM)

    @pl.when(s == N_STEPS - 1)
    def _():
        ln_w, ln_b = tile_t(vec_ref[5]), tile_t(vec_ref[6])
        gn_w = tile_t(gn_ref[...])
        parts = []
        for h in range(N_RWKV):
            hs = hrows(h)
            y = rw_y[hs, :]
            d = y - jnp.mean(y, axis=0, keepdims=True)
            yv = jnp.mean(d * d, axis=0, keepdims=True)
            parts.append((d * lax.rsqrt(yv + RWKV_GN_EPS) * ln_w[hs] + ln_b[hs] + rw_bonus[hs, :]) * rw_g[hs, :])
        for h in range(N_GLA):
            hs = hrows(h)
            o = gl_o[hs, :]
            parts.append(o * lax.rsqrt(jnp.mean(o * o, axis=0, keepdims=True) + RMS_EPS) * gn_w[hs] * gl_gate[hs, :])
        for h in range(N_RET):
            hs = hrows(h)
            o = rt_o[hs, :]
            parts.append(o * lax.rsqrt(jnp.mean(o * o, axis=0, keepdims=True) + RMS_EPS) * rt_gate[hs, :])
        mix_t = _bf(jnp.concatenate(parts, axis=0))
        mix = _dg(mix_t, wout_ref[...], _TN)
        x1_ref[...] = _layer_norm(ALPHA * x_ref[...] + mix, ln_ref[0:1, :], ln_ref[1:2, :])


def _smix_call(l, x, shift_t, s_rw, s_gl, s_rt, wts):
    M = x.shape[0]
    nb = M // DEC_SEQ
    (w1t, mu, vec, bwt, bat, bgt, bgkt, gb, gn, rot, gam, wout, ln1) = wts

    def once(shape, idx=()):
        nd = len(shape)
        return pl.BlockSpec(shape, lambda s: idx + (0,) * (nd - len(idx)), pipeline_mode=pl.Buffered(1))

    def layer(shape):
        return pl.BlockSpec((None,) + shape, lambda s: (l,) + (0,) * len(shape), pipeline_mode=pl.Buffered(1))

    def head_in(shape, first, n):
        return pl.BlockSpec((None, None) + shape,
                            lambda s: (l, jnp.clip(s - first, 0, n - 1)) + (0,) * len(shape))

    def head_out(shape, first, n):
        return pl.BlockSpec((None,) + shape, lambda s: (jnp.clip(s - first, 0, n - 1),) + (0,) * len(shape))

    rw_blk = (HEAD_DIM, HEAD_DIM, nb)
    gl_blk = (GLA_DK, HEAD_DIM, nb)
    in_specs = [
        once((M, D_MODEL)), layer((RWKV_COLS, nb)),
        head_in(rw_blk, 0, N_RWKV), head_in(gl_blk, N_RWKV, N_GLA), head_in(rw_blk, N_RWKV + N_GLA, N_RET),
        layer((RWKV_COLS + GLA_COLS + RET_COLS, D_MODEL)), layer((RWKV_COLS, nb)), layer((7, RWKV_W, nb)),
        layer((RWKV_W, LORA_W)), layer((RWKV_W, LORA_A)), layer((RWKV_W, LORA_G)), layer((GLA_K, GLA_LORA)),
        layer((GLA_K, nb)), layer((GLA_V, nb)), once((2, DEC_SEQ, HEAD_DIM // 2, nb)), once((N_RET, 8, nb)),
        layer((D_MODEL, D_MODEL)), layer((2, D_MODEL)),
    ]
    out_specs = [
        once((M, D_MODEL)), once((RWKV_COLS, nb)),
        head_out(rw_blk, 0, N_RWKV), head_out(gl_blk, N_RWKV, N_GLA), head_out(rw_blk, N_RWKV + N_GLA, N_RET),
    ]
    out_shape = [
        jax.ShapeDtypeStruct((M, D_MODEL), F32), jax.ShapeDtypeStruct((RWKV_COLS, nb), F32),
        jax.ShapeDtypeStruct((N_RWKV,) + rw_blk, F32), jax.ShapeDtypeStruct((N_GLA,) + gl_blk, F32),
        jax.ShapeDtypeStruct((N_RET,) + rw_blk, F32),
    ]
    scratch = ([pltpu.VMEM((RWKV_W, M), F32)] * 9
               + [pltpu.VMEM((GLA_K, M), F32), pltpu.VMEM((GLA_K, M), F32), pltpu.VMEM((GLA_V, M), F32),
                  pltpu.VMEM((GLA_K, M), F32), pltpu.VMEM((GLA_V, M), F32), pltpu.VMEM((GLA_V, M), F32)]
               + [pltpu.VMEM((RET_W, M), F32)] * 5)
    return pl.pallas_call(
        functools.partial(_smix_kernel, nb=nb),
        grid=(N_STEPS,),
        in_specs=in_specs, out_specs=out_specs, out_shape=out_shape, scratch_shapes=scratch,
        compiler_params=pltpu.CompilerParams(dimension_semantics=("arbitrary",), vmem_limit_bytes=VMEM_LIMIT),
        name="mixer_sample",
    )(x, shift_t, s_rw, s_gl, s_rt, w1t, mu, vec, bwt, bat, bgt, bgkt, gb, gn, rot, gam, wout, ln1)


def _sffn_kernel(x_ref, conv_ref, wup_ref, cwb_ref, wdn_ref, ln_ref, y_ref, cv_out_ref, *, nb):
    M = DEC_SEQ * nb
    x = x_ref[...]
    xb = _bf(x)
    n_f = D_FF // FFN_CHUNK

    def cols(f, base):
        return slice(base + f * FFN_CHUNK, base + (f + 1) * FFN_CHUNK)

    def up(f):
        return [_dg(xb, wup_ref[:, cols(f, base)], _NN) for base in (0, D_FF)]

    def conv(u, cs):
        c0 = conv_ref[:, 0, cs]
        c1 = conv_ref[:, 1, cs]
        u1 = jnp.concatenate([c1, u[0:M - nb]], axis=0)
        u2 = jnp.concatenate([c0, c1, u[0:M - 2 * nb]], axis=0)
        cv_out_ref[:, 0, cs] = u[M - 2 * nb:M - nb]
        cv_out_ref[:, 1, cs] = u[M - nb:M]
        return cwb_ref[3:4, cs] + (cwb_ref[0:1, cs] * u2 + cwb_ref[1:2, cs] * u1 + cwb_ref[2:3, cs] * u)

    acc = jnp.zeros((M, D_MODEL), F32)
    u_queue = [up(f) for f in range(min(FFN_AHEAD, n_f))]
    for f in range(n_f):
        if f + FFN_AHEAD < n_f:
            u_queue.append(up(f + FFN_AHEAD))
        u_cur = u_queue.pop(0)
        hid = jax.nn.gelu(conv(u_cur[0], cols(f, 0))) * conv(u_cur[1], cols(f, D_FF))
        acc = acc + _dg(_bf(hid), wdn_ref[f * FFN_CHUNK:(f + 1) * FFN_CHUNK, :], _NN)
    y_ref[...] = _layer_norm(ALPHA * x + acc, ln_ref[0:1, :], ln_ref[1:2, :])


def _sffn_call(l, x, conv2, wts):
    M = x.shape[0]
    nb = M // DEC_SEQ
    wup, cwb, wdn, ln2 = wts

    def layer(shape):
        return pl.BlockSpec((None,) + shape, lambda i: (l,) + (0,) * len(shape), pipeline_mode=pl.Buffered(1))

    return pl.pallas_call(
        functools.partial(_sffn_kernel, nb=nb),
        grid=(1,),
        in_specs=[pl.BlockSpec((M, D_MODEL), lambda i: (0, 0)), layer((nb, CONV_W - 1, 2 * D_FF)),
                  layer((D_MODEL, 2 * D_FF)), layer((8, 2 * D_FF)), layer((D_FF, D_MODEL)), layer((2, D_MODEL))],
        out_specs=[pl.BlockSpec((M, D_MODEL), lambda i: (0, 0)),
                   pl.BlockSpec((nb, CONV_W - 1, 2 * D_FF), lambda i: (0, 0, 0))],
        out_shape=[jax.ShapeDtypeStruct((M, D_MODEL), F32),
                   jax.ShapeDtypeStruct((nb, CONV_W - 1, 2 * D_FF), F32)],
        compiler_params=pltpu.CompilerParams(dimension_semantics=("arbitrary",), vmem_limit_bytes=VMEM_LIMIT),
        name="ffn_sample",
    )(x, conv2, wup, cwb, wdn, ln2)


def _run_sample(x_sample, s_rw, s_sh, s_gl, s_rt, s_cv, p, ffn_w):
    bs, ts_, _ = x_sample.shape
    L = DEPTH

    def lanes(a):
        return jnp.broadcast_to(a[..., None], a.shape + (bs,))

    half = HEAD_DIM // 2
    inv = 1.0 / (ROPE_BASE ** jnp.linspace(0.0, 1.0, half, dtype=F32))
    ang = (PAST_LEN + jnp.arange(ts_)).astype(F32)[:, None] * inv[None]
    rot = lanes(jnp.stack([jnp.cos(ang), jnp.sin(ang)]))
    log_gamma = jnp.log(1.0 - jnp.exp2(-5.0 - jnp.arange(N_RET, dtype=F32)))
    gam = jnp.broadcast_to(jnp.exp(log_gamma)[:, None, None], (N_RET, 8, bs))
    vec = lanes(jnp.stack([p["rwkv_w0"], p["rwkv_a0"], p["rwkv_kk"], p["rwkv_ka"],
                           p["rwkv_rk"].reshape(L, RWKV_W), p["rwkv_lnw"], p["rwkv_lnb"]], axis=1))
    wts = (p["w_in_t"], lanes(p["rwkv_mu"]), vec,
           _bf(jnp.swapaxes(p["rwkv_bw"], 1, 2)), _bf(jnp.swapaxes(p["rwkv_ba"], 1, 2)),
           _bf(jnp.swapaxes(p["rwkv_bg"], 1, 2)), _bf(jnp.swapaxes(p["gla_bgk"], 1, 2)),
           lanes(p["gla_bgk_b"]), lanes(jnp.tile(p["gla_norm_w"], (1, N_GLA))), rot, gam,
           _bf(p["w_out"]), jnp.stack([p["ln1_g"], p["ln1_b"]], axis=1))
    rw_t = jnp.transpose(s_rw, (0, 2, 3, 4, 1))
    gl_t = jnp.transpose(s_gl, (0, 2, 3, 4, 1))
    rt_t = jnp.transpose(s_rt, (0, 2, 3, 4, 1))
    sh_t = jnp.swapaxes(s_sh, 1, 2)
    x = jnp.swapaxes(x_sample, 0, 1).reshape(ts_ * bs, D_MODEL)
    n_rw, n_sh, n_gl, n_rt, n_cv = [], [], [], [], []
    for l in range(L):
        x, sh, rw, gl, rt = _smix_call(l, x, sh_t, rw_t, gl_t, rt_t, wts)
        x, cv = _sffn_call(l, x, s_cv, ffn_w)
        n_sh.append(sh)
        n_rw.append(rw)
        n_gl.append(gl)
        n_rt.append(rt)
        n_cv.append(cv)
    y = jnp.swapaxes(x.reshape(ts_, bs, D_MODEL), 0, 1)
    back = (0, 4, 1, 2, 3)
    return y, (jnp.transpose(jnp.stack(n_rw), back), jnp.swapaxes(jnp.stack(n_sh), 1, 2),
               jnp.transpose(jnp.stack(n_gl), back), jnp.transpose(jnp.stack(n_rt), back),
               jnp.stack(n_cv))


def _pad_to(a, axis, n):
    pad = [(0, 0)] * a.ndim
    pad[axis] = (0, n - a.shape[axis])
    return jnp.pad(a, pad)


def _place(a, axis, segs, total):
    out = []
    pos = 0
    for src, w, dst in segs:
        if dst > pos:
            shp = list(a.shape)
            shp[axis] = dst - pos
            out.append(jnp.zeros(shp, a.dtype))
        out.append(lax.slice_in_dim(a, src, src + w, axis=axis))
        pos = dst + w
    if total > pos:
        shp = list(a.shape)
        shp[axis] = total - pos
        out.append(jnp.zeros(shp, a.dtype))
    return jnp.concatenate(out, axis=axis)


def _in_col_segments():
    g0 = RWKV_COLS
    t0 = RWKV_COLS + GLA_COLS
    segs = [(0, RWKV_COLS, 0), (g0 + 2 * GLA_K + 2 * GLA_V, GLA_LORA, RWKV_COLS)]
    segs += [(g0, GLA_K, RW_P), (g0 + GLA_K, GLA_K, RW_P + GQ_P),
             (g0 + 2 * GLA_K, GLA_V, RW_P + 2 * GQ_P), (g0 + 2 * GLA_K + GLA_V, GLA_V, RW_P + 2 * GQ_P + HW)]
    segs += [(t0 + i * RET_W, RET_W, RW_P + GL_P + i * HW) for i in range(4)]
    return segs


def _prep_weights(w_in, rwkv_mu, rwkv_w0, rwkv_bw, rwkv_a0, rwkv_ba, rwkv_bg, rwkv_kk, rwkv_ka,
                  rwkv_rk, rwkv_lnw, rwkv_lnb, gla_bgk, gla_bgk_b, gla_norm_w, w_out,
                  ln1_g, ln1_b, ln2_g, ln2_b, ffn_up, ffn_conv_w, ffn_conv_b, ffn_down):
    L = w_in.shape[0]
    w1 = _place(_bf(w_in), 2, _in_col_segments(), NP)
    mu = _pad_to(rwkv_mu, 1, RW_P)[:, None, :]
    gnw = _pad_to(jnp.tile(gla_norm_w, (1, N_GLA)), 1, HW)
    vec = jnp.stack([rwkv_w0, rwkv_a0, rwkv_kk, rwkv_ka, rwkv_rk.reshape(L, RWKV_W), rwkv_lnw, rwkv_lnb, gnw], axis=1)
    bw = _bf(_pad_to(rwkv_bw, 1, LANE))
    ba = _bf(_place(rwkv_ba, 1, [(0, LORA_A, LORA_W)], LANE))
    bg = _bf(_pad_to(rwkv_bg, 1, 2 * LANE))
    bgk = _bf(_pad_to(_place(gla_bgk, 1, [(0, GLA_LORA, LORA_G)], 2 * LANE), 2, GQ_P))
    bgkb = _pad_to(gla_bgk_b, 1, GQ_P)[:, None, :]
    wout = _place(_bf(w_out), 1, [(0, RWKV_W + GLA_V, 0), (RWKV_W + GLA_V, RET_W, 2 * HW)], MIX_P)
    ln1 = jnp.stack([ln1_g, ln1_b], axis=1)
    mixer_w = (w1, mu, vec, bw, ba, bg, bgk, bgkb, wout, ln1)
    cwb = _pad_to(jnp.concatenate([ffn_conv_w, ffn_conv_b[:, None, :]], axis=1), 1, 8)
    ln2 = jnp.stack([ln2_g, ln2_b], axis=1)
    ffn_w = (_bf(ffn_up), cwb, _bf(ffn_down), ln2)
    return mixer_w, ffn_w


def _tables(n_pos, chunk):
    half = HEAD_DIM // 2
    inv = 1.0 / (ROPE_BASE ** jnp.linspace(0.0, 1.0, half, dtype=F32))
    ang = jnp.arange(n_pos).astype(F32)[:, None] * inv[None]
    cos, sin = jnp.cos(ang), jnp.sin(ang)
    cos = jnp.tile(jnp.concatenate([cos, cos], -1), (1, HW // HEAD_DIM))
    sin = jnp.tile(jnp.concatenate([-sin, sin], -1), (1, HW // HEAD_DIM))
    log_gamma = jnp.log(1.0 - jnp.exp2(-5.0 - jnp.arange(N_RET, dtype=F32)))
    i = jnp.arange(chunk, dtype=F32)
    diff = i[:, None] - i[None, :]
    causal = diff >= 0
    dmask = jnp.where(causal, jnp.exp(jnp.where(causal, diff, 0.0) * log_gamma[:, None, None]), 0.0)
    qd = jnp.exp((i + 1.0) * log_gamma[:, None])[..., None]
    kd = jnp.exp((chunk - 1.0 - i) * log_gamma[:, None])[..., None]
    cd = jnp.exp(chunk * log_gamma)[:, None, None]
    qd = jnp.broadcast_to(qd, (N_RET, chunk, HEAD_DIM))
    kd = jnp.broadcast_to(kd, (N_RET, chunk, HEAD_DIM))
    cd = jnp.broadcast_to(cd, (N_RET, SLOTS, HEAD_DIM))
    return cos, sin, dmask, qd, kd, cd


def _run_prompt(x, s_rw, s_sh, s_gl, s_rt, s_cv, mixer_w, ffn_w, *, tt_mix, tt_ffn, chunk):
    T = x.shape[1]
    tabs = _tables(T, chunk)
    shift_rows = _pad_to(_pad_to(s_sh, 2, RW_P)[:, :, None, :], 2, SLOTS)
    conv_rows = _place(s_cv, 2, [(0, CONV_W - 1, SLOTS - (CONV_W - 1))], SLOTS)
    n_rw, n_sh, n_gl, n_rt, n_cv = [], [], [], [], []
    for l in range(DEPTH):
        x, sh, rw, gl, rt = _mixer_call(l, x, shift_rows, s_rw, s_gl, s_rt, tabs, mixer_w, tt=tt_mix, chunk=chunk)
        x, cv = _ffn_call(l, x, conv_rows, ffn_w, tt=tt_ffn)
        n_sh.append(sh[:, SLOTS - 1, :RWKV_COLS])
        n_cv.append(cv[:, SLOTS - (CONV_W - 1):])
        n_rw.append(rw)
        n_gl.append(gl)
        n_rt.append(rt)
    return x, (jnp.stack(n_rw), jnp.stack(n_sh), jnp.stack(n_gl), jnp.stack(n_rt), jnp.stack(n_cv))


def kernel(x_prompt, x_sample, state_rwkv, state_shift, state_gla, state_ret, state_conv, w_in, rwkv_mu, rwkv_w0, rwkv_bw, rwkv_a0, rwkv_ba, rwkv_bg, rwkv_kk, rwkv_ka, rwkv_rk, rwkv_lnw, rwkv_lnb, gla_bgk, gla_bgk_b, gla_norm_w, w_out, ln1_g, ln1_b, ln2_g, ln2_b, ffn_up, ffn_conv_w, ffn_conv_b, ffn_down):
    w_in_t = _bf(jnp.swapaxes(w_in, 1, 2))
    mixer_w, ffn_w = _prep_weights(jnp.swapaxes(w_in_t, 1, 2), rwkv_mu, rwkv_w0, rwkv_bw, rwkv_a0, rwkv_ba, rwkv_bg, rwkv_kk, rwkv_ka,
                                   rwkv_rk, rwkv_lnw, rwkv_lnb, gla_bgk, gla_bgk_b, gla_norm_w, w_out,
                                   ln1_g, ln1_b, ln2_g, ln2_b, ffn_up, ffn_conv_w, ffn_conv_b, ffn_down)
    bp, tp, _ = x_prompt.shape
    bs, ts, _ = x_sample.shape
    assert ts == DEC_SEQ

    def zeros_like_state(s):
        return jnp.zeros((s.shape[0], bp) + s.shape[2:], F32)

    chunk = math.gcd(tp, CHUNK)
    y_p, st_p = _run_prompt(
        x_prompt, zeros_like_state(state_rwkv), zeros_like_state(state_shift), zeros_like_state(state_gla),
        zeros_like_state(state_ret), zeros_like_state(state_conv), mixer_w, ffn_w,
        tt_mix=min(MIX_ROWS, tp), tt_ffn=min(FFN_ROWS, tp), chunk=chunk)

    raw = dict(w_in_t=w_in_t, rwkv_mu=rwkv_mu, rwkv_w0=rwkv_w0, rwkv_bw=rwkv_bw, rwkv_a0=rwkv_a0, rwkv_ba=rwkv_ba,
               rwkv_bg=rwkv_bg, rwkv_kk=rwkv_kk, rwkv_ka=rwkv_ka, rwkv_rk=rwkv_rk, rwkv_lnw=rwkv_lnw,
               rwkv_lnb=rwkv_lnb, gla_bgk=gla_bgk, gla_bgk_b=gla_bgk_b, gla_norm_w=gla_norm_w, w_out=w_out,
               ln1_g=ln1_g, ln1_b=ln1_b)
    y_s, st_s = _run_sample(x_sample, state_rwkv, state_shift, state_gla, state_ret, state_conv, raw, ffn_w)
    return (y_p, y_s) + st_p + st_s
```
